```python
import math
import jax, jax.numpy as jnp
from jax import lax
import numpy as np

D_MODEL = 2048
BATCH = 4
SEQ = 2048
DEPTH = 2
DEC_BATCH = 128
DEC_SEQ = 4
PAST_LEN = 16384
PAGE_SIZE = 128

N_EVEN = (DEPTH + 1) // 2
N_ODD = DEPTH // 2
S5_WIDTH = D_MODEL // 2
S5_GROUP = 16
S5_GROUPS = S5_WIDTH // S5_GROUP
S5_STATE = 64
GLA_HEADS = 4
GLA_DV = D_MODEL - S5_WIDTH
GLA_DK = GLA_DV // 2
GLA_HK = GLA_DK // GLA_HEADS
GLA_HV = GLA_DV // GLA_HEADS
GLA_LOWRANK = 16
GLA_TAU = 16.0
GLA_CHUNK = 64
EVEN_IN = S5_WIDTH + 2 * GLA_DK + 2 * GLA_DV + GLA_LOWRANK
ML_HEADS = 4
ML_WIDTH = D_MODEL
ML_DH = ML_WIDTH // ML_HEADS
ML_CHUNK = 64
ODD_IN = 4 * ML_WIDTH + 2 * ML_HEADS
D_FF = 4 * D_MODEL
EPS = 1e-6
F32 = jnp.float32

kernel_name = 'hybrid_s5_gla_mlstm_decode_step'


def rmsnorm(x, g):
    x32 = x.astype(F32)
    y = x32 * lax.rsqrt(jnp.mean(x32 * x32, axis=-1, keepdims=True) + EPS)
    return (y * g.astype(F32)).astype(x.dtype)


def head_rmsnorm(x, g):
    return x * lax.rsqrt(jnp.mean(x * x, axis=-1, keepdims=True) + EPS) * g.astype(F32)


def _complex_affine_combine(e1, e2):
    a1r, a1i, b1r, b1i = e1
    a2r, a2i, b2r, b2i = e2
    return (a2r * a1r - a2i * a1i, a2r * a1i + a2i * a1r,
            a2r * b1r - a2i * b1i + b2r, a2r * b1i + a2i * b1r + b2i)


def s5_mixer(u, h_re, h_im, a_re, a_im, log_step, b_re, b_im, c_re, c_im, d_skip, glu_w, glu_b):
    bsz, seq, _ = u.shape
    u32 = u.astype(F32).reshape(bsz, seq, S5_GROUPS, S5_GROUP)
    ar = a_re.astype(F32)
    ai = a_im.astype(F32)
    dt = jnp.exp(log_step.astype(F32))[:, None]
    mag = jnp.exp(ar * dt)
    abar_re = mag * jnp.cos(ai * dt)
    abar_im = mag * jnp.sin(ai * dt)
    lam2 = ar * ar + ai * ai
    zr = abar_re - 1.0
    cr = (zr * ar + abar_im * ai) / lam2
    ci = (abar_im * ar - zr * ai) / lam2
    br = b_re.astype(F32)
    bi = b_im.astype(F32)
    bbar_re = cr[..., None] * br - ci[..., None] * bi
    bbar_im = cr[..., None] * bi + ci[..., None] * br
    bu_re = jnp.einsum('blgq,gpq->blgp', u32, bbar_re)
    bu_im = jnp.einsum('blgq,gpq->blgp', u32, bbar_im)
    h_re = h_re.astype(F32)
    h_im = h_im.astype(F32)
    bu_re = bu_re.at[:, 0].add(abar_re * h_re - abar_im * h_im)
    bu_im = bu_im.at[:, 0].add(abar_re * h_im + abar_im * h_re)
    a_r = jnp.broadcast_to(abar_re, bu_re.shape)
    a_i = jnp.broadcast_to(abar_im, bu_im.shape)
    _, _, x_re, x_im = lax.associative_scan(_complex_affine_combine, (a_r, a_i, bu_re, bu_im), axis=1)
    y = (jnp.einsum('blgp,gqp->blgq', x_re, c_re.astype(F32))
         - jnp.einsum('blgp,gqp->blgq', x_im, c_im.astype(F32)))
    y = y + d_skip.astype(F32).reshape(S5_GROUPS, S5_GROUP) * u32
    y = jax.nn.gelu(y.reshape(bsz, seq, S5_WIDTH))
    out = y * jax.nn.sigmoid(y @ glu_w.astype(F32) + glu_b.astype(F32))
    return out, x_re[:, -1], x_im[:, -1]


def gla_mixer(q, k, v, g, gk_low, s0, gk_up, gk_b, norm_w):
    bsz, seq, _ = q.shape
    H = GLA_HEADS
    q = q.astype(F32).reshape(bsz, seq, H, GLA_HK) * (GLA_HK ** -0.5)
    k = k.astype(F32).reshape(bsz, seq, H, GLA_HK)
    v = v.astype(F32).reshape(bsz, seq, H, GLA_HV)
    log_a = jax.nn.log_sigmoid(gk_low.astype(F32) @ gk_up.astype(F32) + gk_b.astype(F32)) / GLA_TAU
    log_a = log_a.reshape(bsz, seq, H, GLA_HK)
    csz = math.gcd(seq, GLA_CHUNK)
    nch = seq // csz

    def to_chunks(t):
        return t.reshape(bsz, nch, csz, H, -1).transpose(1, 0, 3, 2, 4)

    mask = jnp.tril(jnp.ones((csz, csz), dtype=bool))

    def step(s, inp):
        qc, kc, vc, lac = inp
        bcum = jnp.cumsum(lac, axis=2)
        blast = bcum[:, :, -1:]
        qd = qc * jnp.exp(bcum)
        kd = kc * jnp.exp(-bcum)
        att = jnp.where(mask, jnp.einsum('bhtd,bhsd->bhts', qd, kd), 0.0)
        o = jnp.einsum('bhts,bhse->bhte', att, vc) + jnp.einsum('bhtd,bhde->bhte', qd, s)
        s_new = (jnp.exp(blast[:, :, 0])[..., None] * s
                 + jnp.einsum('bhsd,bhse->bhde', kc * jnp.exp(blast - bcum), vc))
        return s_new, o

    s_fin, o = lax.scan(step, s0.astype(F32), (to_chunks(q), to_chunks(k), to_chunks(v), to_chunks(log_a)))
    o = o.transpose(1, 0, 3, 2, 4).reshape(bsz, seq, H, GLA_HV)
    o = head_rmsnorm(o, norm_w).reshape(bsz, seq, GLA_DV)
    return o * jax.nn.silu(g.astype(F32)), s_fin


def mlstm_mixer(q, k, v, o_gate, ig, fg, c0, n0, m0, norm_w):
    bsz, seq, _ = q.shape
    H = ML_HEADS
    q = q.astype(F32).reshape(bsz, seq, H, ML_DH)
    k = k.astype(F32).reshape(bsz, seq, H, ML_DH) * (ML_DH ** -0.5)
    v = v.astype(F32).reshape(bsz, seq, H, ML_DH)
    logf = jax.nn.log_sigmoid(fg.astype(F32))
    ig = ig.astype(F32)
    csz = math.gcd(seq, ML_CHUNK)
    nch = seq // csz

    def to_chunks(t):
        return t.reshape(bsz, nch, csz, H, -1).transpose(1, 0, 3, 2, 4)

    def gate_chunks(t):
        return t.reshape(bsz, nch, csz, H).transpose(1, 0, 3, 2)

    mask = jnp.tril(jnp.ones((csz, csz), dtype=bool))

    def step(carry, inp):
        cm, nm, mm = carry
        qc, kc, vc, ic, lfc = inp
        fcum = jnp.cumsum(lfc, axis=-1)
        dmat = jnp.where(mask, fcum[..., :, None] - fcum[..., None, :] + ic[..., None, :], -jnp.inf)
        dprev = fcum + mm[..., None]
        m = jnp.maximum(jnp.max(dmat, axis=-1), dprev)
        w = jnp.exp(dmat - m[..., None])
        wp = jnp.exp(dprev - m)
        sc = jnp.einsum('bhtd,bhsd->bhts', qc, kc) * w
        num = jnp.einsum('bhts,bhse->bhte', sc, vc) + wp[..., None] * jnp.einsum('bhtd,bhde->bhte', qc, cm)
        den = jnp.sum(sc, axis=-1) + wp * jnp.einsum('bhtd,bhd->bht', qc, nm)
        h = num / jnp.maximum(jnp.abs(den), jnp.exp(-m))[..., None]
        m_new = m[..., -1]
        decay = jnp.exp(fcum[..., -1] + mm - m_new)
        wk = jnp.exp(fcum[..., -1:] - fcum + ic - m_new[..., None])
        c_new = decay[..., None, None] * cm + jnp.einsum('bhs,bhsd,bhse->bhde', wk, kc, vc)
        n_new = decay[..., None] * nm + jnp.einsum('bhs,bhsd->bhd', wk, kc)
        return (c_new, n_new, m_new), h

    (c_f, n_f, m_f), h = lax.scan(
        step, (c0.astype(F32), n0.astype(F32), m0.astype(F32)),
        (to_chunks(q), to_chunks(k), to_chunks(v), gate_chunks(ig), gate_chunks(logf)))
    h = h.transpose(1, 0, 3, 2, 4).reshape(bsz, seq, H, ML_DH)
    h = h * jax.nn.sigmoid(o_gate.astype(F32)).reshape(bsz, seq, H, ML_DH)
    h = head_rmsnorm(h, norm_w).reshape(bsz, seq, ML_WIDTH)
    return h, c_f, n_f, m_f


def sq_relu_mlp(x, w_up, w_down):
    hid = jnp.square(jax.nn.relu(x @ w_up))
    return hid @ w_down


def trunk(x, s5_re, s5_im, gla_s, ml_c, ml_n, ml_m, W):
    h = x
    out_s5_re, out_s5_im, out_gla, out_c, out_n, out_m = [], [], [], [], [], []
    for layer in range(DEPTH):
        xn = rmsnorm(h, W['norm_mix'][layer])
        if layer % 2 == 0:
            e = layer // 2
            proj = xn @ W['w_in_even'][e]
            o1 = S5_WIDTH
            o2 = o1 + GLA_DK
            o3 = o2 + GLA_DK
            o4 = o3 + GLA_DV
            o5 = o4 + GLA_DV
            u, q, k, v, g, gk_low = (proj[..., :o1], proj[..., o1:o2], proj[..., o2:o3],
                                     proj[..., o3:o4], proj[..., o4:o5], proj[..., o5:])
            y_s5, hr, hi = s5_mixer(u, s5_re[e], s5_im[e], W['s5_a_re'][e], W['s5_a_im'][e],
                                    W['s5_log_step'][e], W['s5_b_re'][e], W['s5_b_im'][e],
                                    W['s5_c_re'][e], W['s5_c_im'][e], W['s5_d'][e],
                                    W['s5_glu_w'][e], W['s5_glu_b'][e])
            y_gla, sg = gla_mixer(q, k, v, g, gk_low, gla_s[e], W['gla_gk_up'][e],
                                  W['gla_gk_b'][e], W['gla_norm'][e])
            mix = jnp.concatenate([y_s5, y_gla], axis=-1).astype(h.dtype) @ W['w_out_even'][e]
            out_s5_re.append(hr)
            out_s5_im.append(hi)
            out_gla.append(sg)
        else:
            o = layer // 2
            proj = xn @ W['w_in_odd'][o]
            q = proj[..., 0:ML_WIDTH]
            k = proj[..., ML_WIDTH:2 * ML_WIDTH]
            v = proj[..., 2 * ML_WIDTH:3 * ML_WIDTH]
            og = proj[..., 3 * ML_WIDTH:4 * ML_WIDTH]
            ig = proj[..., 4 * ML_WIDTH:4 * ML_WIDTH + ML_HEADS].astype(F32) + W['mlstm_b_i'][o].astype(F32)
            fg = proj[..., 4 * ML_WIDTH + ML_HEADS:].astype(F32) + W['mlstm_b_f'][o].astype(F32)
            y_ml, cf, nf, mf = mlstm_mixer(q, k, v, og, ig, fg, ml_c[o], ml_n[o], ml_m[o], W['mlstm_norm'][o])
            mix = y_ml.astype(h.dtype) @ W['w_out_odd'][o]
            out_c.append(cf)
            out_n.append(nf)
            out_m.append(mf)
        h = h + mix.astype(h.dtype)
        h = h + sq_relu_mlp(rmsnorm(h, W['norm_mlp'][layer]), W['w_mlp_up'][layer], W['w_mlp_down'][layer]).astype(h.dtype)
    y = rmsnorm(h, W['norm_final'])
    return (y, jnp.stack(out_s5_re), jnp.stack(out_s5_im), jnp.stack(out_gla),
            jnp.stack(out_c), jnp.stack(out_n), jnp.stack(out_m))


def setup_inputs(seed: int = 0) -> dict:
    key = jax.random.key(seed)
    keys = list(jax.random.split(key, 64))

    def nk():
        return keys.pop()

    def normal(shape, scale):
        return jax.random.normal(nk(), shape, F32) * scale

    G, P, Q = S5_GROUPS, S5_STATE, S5_GROUP
    D = D_MODEL
    inputs = {}
    inputs['x_prompt'] = normal((BATCH, SEQ, D), 1.0)
    inputs['x_sample'] = normal((DEC_BATCH, DEC_SEQ, D), 1.0)
    inputs['state_s5_re'] = normal((N_EVEN, DEC_BATCH, G, P), 0.5)
    inputs['state_s5_im'] = normal((N_EVEN, DEC_BATCH, G, P), 0.5)
    inputs['state_gla'] = normal((N_EVEN, DEC_BATCH, GLA_HEADS, GLA_HK, GLA_HV), 0.3)
    inputs['state_mlstm_c'] = normal((N_ODD, DEC_BATCH, ML_HEADS, ML_DH, ML_DH), 0.1)
    inputs['state_mlstm_n'] = normal((N_ODD, DEC_BATCH, ML_HEADS, ML_DH), 0.5)
    inputs['state_mlstm_m'] = normal((N_ODD, DEC_BATCH, ML_HEADS), 1.0)
    inputs['norm_mix'] = 1.0 + normal((DEPTH, D), 0.02)
    inputs['norm_mlp'] = 1.0 + normal((DEPTH, D), 0.02)
    inputs['norm_final'] = 1.0 + normal((D,), 0.02)
    inputs['w_in_even'] = normal((N_EVEN, D, EVEN_IN), D ** -0.5)
    inputs['s5_a_re'] = -0.5 + normal((N_EVEN, G, P), 0.01)
    inputs['s5_a_im'] = jnp.pi * jnp.arange(P, dtype=F32) + normal((N_EVEN, G, P), 0.01)
    inputs['s5_log_step'] = jax.random.uniform(nk(), (N_EVEN, G), F32, math.log(1e-3), math.log(1e-1))
    inputs['s5_b_re'] = normal((N_EVEN, G, P, Q), (2 * Q) ** -0.5)
    inputs['s5_b_im'] = normal((N_EVEN, G, P, Q), (2 * Q) ** -0.5)
    inputs['s5_c_re'] = normal((N_EVEN, G, Q, P), P ** -0.5)
    inputs['s5_c_im'] = normal((N_EVEN, G, Q, P), P ** -0.5)
    inputs['s5_d'] = normal((N_EVEN, S5_WIDTH), 1.0)
    inputs['s5_glu_w'] = normal((N_EVEN, S5_WIDTH, S5_WIDTH), S5_WIDTH ** -0.5)
    inputs['s5_glu_b'] = normal((N_EVEN, S5_WIDTH), 0.02)
    inputs['gla_gk_up'] = normal((N_EVEN, GLA_LOWRANK, GLA_DK), GLA_LOWRANK ** -0.5)
    inputs['gla_gk_b'] = normal((N_EVEN, GLA_DK), 0.02)
    inputs['gla_norm'] = 1.0 + normal((N_EVEN, GLA_HV), 0.02)
    inputs['w_out_even'] = normal((N_EVEN, D, D), D ** -0.5)
    inputs['w_in_odd'] = normal((N_ODD, D, ODD_IN), D ** -0.5)
    inputs['mlstm_b_i'] = normal((N_ODD, ML_HEADS), 0.1)
    inputs['mlstm_b_f'] = jax.random.uniform(nk(), (N_ODD, ML_HEADS), F32, 3.0, 6.0)
    inputs['mlstm_norm'] = 1.0 + normal((N_ODD, ML_DH), 0.02)
    inputs['w_out_odd'] = normal((N_ODD, D, D), D ** -0.5)
    inputs['w_mlp_up'] = normal((DEPTH, D, D_FF), D ** -0.5)
    inputs['w_mlp_down'] = normal((DEPTH, D_FF, D), D_FF ** -0.5)
    return inputs


def reference(x_prompt, x_sample, state_s5_re, state_s5_im, state_gla, state_mlstm_c, state_mlstm_n,
              state_mlstm_m, norm_mix, norm_mlp, norm_final, w_in_even, s5_a_re, s5_a_im, s5_log_step,
              s5_b_re, s5_b_im, s5_c_re, s5_c_im, s5_d, s5_glu_w, s5_glu_b, gla_gk_up, gla_gk_b,
              gla_norm, w_out_even, w_in_odd, mlstm_b_i, mlstm_b_f, mlstm_norm, w_out_odd,
              w_mlp_up, w_mlp_down):
    W = dict(norm_mix=norm_mix, norm_mlp=norm_mlp, norm_final=norm_final, w_in_even=w_in_even,
             s5_a_re=s5_a_re, s5_a_im=s5_a_im, s5_log_step=s5_log_step, s5_b_re=s5_b_re,
             s5_b_im=s5_b_im, s5_c_re=s5_c_re, s5_c_im=s5_c_im, s5_d=s5_d, s5_glu_w=s5_glu_w,
             s5_glu_b=s5_glu_b, gla_gk_up=gla_gk_up, gla_gk_b=gla_gk_b, gla_norm=gla_norm,
             w_out_even=w_out_even, w_in_odd=w_in_odd, mlstm_b_i=mlstm_b_i, mlstm_b_f=mlstm_b_f,
             mlstm_norm=mlstm_norm, w_out_odd=w_out_odd, w_mlp_up=w_mlp_up, w_mlp_down=w_mlp_down)
    bp = x_prompt.shape[0]
    z_s5 = jnp.zeros((N_EVEN, bp, S5_GROUPS, S5_STATE), F32)
    z_gla = jnp.zeros((N_EVEN, bp, GLA_HEADS, GLA_HK, GLA_HV), F32)
    z_c = jnp.zeros((N_ODD, bp, ML_HEADS, ML_DH, ML_DH), F32)
    z_n = jnp.zeros((N_ODD, bp, ML_HEADS, ML_DH), F32)
    z_m = jnp.full((N_ODD, bp, ML_HEADS), -jnp.inf, F32)
    y_prompt, p_s5_re, p_s5_im, p_gla, p_c, p_n, p_m = trunk(x_prompt, z_s5, z_s5, z_gla, z_c, z_n, z_m, W)
    y_sample, s_s5_re, s_s5_im, s_gla, s_c, s_n, s_m = trunk(
        x_sample, state_s5_re, state_s5_im, state_gla, state_mlstm_c, state_mlstm_n, state_mlstm_m, W)
    return (y_prompt, y_sample, p_s5_re, p_s5_im, p_gla, p_c, p_n, p_m,
            s_s5_re, s_s5_im, s_gla, s_c, s_n, s_m)
```

```python
import functools
import math

import jax
import jax.numpy as jnp
from jax import lax
from jax.experimental import pallas as pl
from jax.experimental.pallas import tpu as pltpu

F32 = jnp.float32
BF16 = jnp.bfloat16
EPS = 1e-6

D_MODEL = 2048
D_FF = 4 * D_MODEL
S5_WIDTH = 1024
S5_GROUPS = 64
S5_GROUP = 16
S5_STATE = 64
S5_CH = S5_GROUPS * S5_STATE
S5_NBLK = 4
GLA_HEADS = 4
GLA_HK = 128
GLA_HV = 256
GLA_TAU = 16.0
GLA_CHUNK = 64
ML_HEADS = 4
ML_DH = 512
ML_CHUNK = 256

SUBLANES = 8
LANES = 128
VMEM_LIMIT = 56 * 1024 * 1024


def _cparams(sem):
    return pltpu.CompilerParams(dimension_semantics=sem, vmem_limit_bytes=VMEM_LIMIT)


def _dot(a, b):
    return jnp.dot(a, b, preferred_element_type=F32)


def _dot_nt(a, b):
    return lax.dot_general(a, b, (((1,), (1,)), ((), ())), preferred_element_type=F32)


def _split3(x):
    p1 = x.astype(BF16)
    r1 = x - p1.astype(F32)
    p2 = r1.astype(BF16)
    r2 = r1 - p2.astype(F32)
    p3 = r2.astype(BF16)
    return p1, p2, p3


def _dot_exact_lhs(m, x):
    p1, p2, p3 = _split3(x)
    return _dot(m, p1) + _dot(m, p2) + _dot(m, p3)


def _dot_nt_exact_lhs(m, x):
    p1, p2, p3 = _split3(x)
    return _dot_nt(m, p1) + _dot_nt(m, p2) + _dot_nt(m, p3)


def _log_sigmoid(x):
    return jnp.minimum(x, 0.0) - jnp.log1p(jnp.exp(-jnp.abs(x)))


def _sigmoid(x):
    return 1.0 / (1.0 + jnp.exp(-x))


def _norm_mm_body(*refs, act, has_tail, tm, rchunk):
    if has_tail:
        x_ref, g_ref, w_ref, wt_ref, o_ref, ot_ref, xn_ref = refs
    else:
        x_ref, g_ref, w_ref, o_ref, xn_ref = refs

    @pl.when(pl.program_id(1) == 0)
    def _():
        g = g_ref[...]
        for r in range(0, tm, rchunk):
            x = x_ref[r:r + rchunk, :]
            ms = jnp.mean(x * x, axis=-1, keepdims=True)
            xn = (x * lax.rsqrt(ms + EPS) * g).astype(BF16)
            xn_ref[r:r + rchunk, :] = xn
        if has_tail:
            ot_ref[...] = _dot(xn_ref[...], wt_ref[...].astype(BF16))

    acc = _dot(xn_ref[...], w_ref[...].astype(BF16))
    if act == "relu2":
        acc = jnp.square(jnp.maximum(acc, 0.0))
    o_ref[...] = acc.astype(o_ref.dtype)


def _norm_mm(x, g, w3, layer, n_main, *, tail=None, act=None, out_dtype=F32, tm=1088, tn=512):
    m, kdim = x.shape
    grid = (m // tm, n_main // tn)
    in_specs = [
        pl.BlockSpec((tm, kdim), lambda i, j: (i, 0)),
        pl.BlockSpec((1, kdim), lambda i, j: (0, 0)),
        pl.BlockSpec((None, kdim, tn), lambda i, j: (layer, 0, j)),
    ]
    args = [x, g.reshape(1, kdim), w3]
    out_shape = [jax.ShapeDtypeStruct((m, n_main), out_dtype)]
    out_specs = [pl.BlockSpec((tm, tn), lambda i, j: (i, j))]
    if tail is not None:
        in_specs.append(pl.BlockSpec((kdim, LANES), lambda i, j: (0, 0)))
        args.append(tail)
        out_shape.append(jax.ShapeDtypeStruct((m, LANES), F32))
        out_specs.append(pl.BlockSpec((tm, LANES), lambda i, j: (i, 0)))
    body = functools.partial(_norm_mm_body, act=act, has_tail=tail is not None, tm=tm, rchunk=272)
    res = pl.pallas_call(
        body, grid=grid, in_specs=in_specs, out_specs=out_specs, out_shape=out_shape,
        scratch_shapes=[pltpu.VMEM((tm, kdim), BF16)],
        compiler_params=_cparams(("arbitrary", "arbitrary")),
        name="norm_mm",
    )(*args)
    return res if tail is not None else res[0]


def _mm_res_body(a_ref, w_ref, r_ref, o_ref):
    p = _dot(a_ref[...].astype(BF16), w_ref[...].astype(BF16))

    @pl.when(pl.program_id(2) == 0)
    def _():
        o_ref[...] = r_ref[...] + p

    @pl.when(pl.program_id(2) > 0)
    def _():
        o_ref[...] += p


def _mm_res(a, w3, layer, res, *, tm=1088, tn=1024, tk=1024):
    m, kdim = a.shape
    n = w3.shape[2]
    grid = (m // tm, n // tn, kdim // tk)
    return pl.pallas_call(
        _mm_res_body, grid=grid,
        in_specs=[
            pl.BlockSpec((tm, tk), lambda i, j, k: (i, k)),
            pl.BlockSpec((None, tk, tn), lambda i, j, k: (layer, k, j)),
            pl.BlockSpec((tm, tn), lambda i, j, k: (i, j)),
        ],
        out_specs=pl.BlockSpec((tm, tn), lambda i, j, k: (i, j)),
        out_shape=jax.ShapeDtypeStruct((m, n), F32),
        compiler_params=_cparams(("arbitrary", "arbitrary", "arbitrary")),
        name="mm_res",
    )(a, w3, res)


def _rmsnorm_body(x_ref, g_ref, o_ref):
    x = x_ref[...]
    ms = jnp.mean(x * x, axis=-1, keepdims=True)
    o_ref[...] = x * lax.rsqrt(ms + EPS) * g_ref[...]


def _rmsnorm(x, g, *, tm=272):
    m, d = x.shape
    return pl.pallas_call(
        _rmsnorm_body, grid=(m // tm,),
        in_specs=[pl.BlockSpec((tm, d), lambda i: (i, 0)), pl.BlockSpec((1, d), lambda i: (0, 0))],
        out_specs=pl.BlockSpec((tm, d), lambda i: (i, 0)),
        out_shape=jax.ShapeDtypeStruct((m, d), F32),
        compiler_params=_cparams(("arbitrary",)),
        name="final_norm",
    )(x, g.reshape(1, d))


def _s5_disc_body(ar_ref, ai_ref, ls_ref, br_ref, bi_ref, abr_ref, abi_ref, bbr_ref, bbi_ref):
    ar = ar_ref[...]
    ai = ai_ref[...]
    dt = jnp.exp(ls_ref[...])
    mag = jnp.exp(ar * dt)
    abr = mag * jnp.cos(ai * dt)
    abi = mag * jnp.sin(ai * dt)
    lam2 = ar * ar + ai * ai
    zr = abr - 1.0
    cr = (zr * ar + abi * ai) / lam2
    ci = (abi * ar - zr * ai) / lam2
    br = br_ref[...]
    bi = bi_ref[...]
    abr_ref[...] = abr
    abi_ref[...] = abi
    bbr_ref[...] = cr * br - ci * bi
    bbi_ref[...] = cr * bi + ci * br


def _s5_discretise(a_re, a_im, log_step, b_re, b_im):
    g, p, q = S5_GROUPS, S5_STATE, S5_GROUP
    rep = lambda t: jnp.repeat(t, q, axis=0)
    ls = jnp.broadcast_to(log_step[:, None], (g, p))
    brt = jnp.transpose(b_re, (0, 2, 1)).reshape(g * q, p)
    bit = jnp.transpose(b_im, (0, 2, 1)).reshape(g * q, p)
    shp = jax.ShapeDtypeStruct((g * q, p), F32)
    spec = pl.BlockSpec((g * q, p), lambda: (0, 0))
    abr, abi, bbr, bbi = pl.pallas_call(
        _s5_disc_body, in_specs=[spec] * 5, out_specs=[spec] * 4, out_shape=[shp] * 4,
        name="s5_disc",
    )(rep(a_re), rep(a_im), rep(ls), brt, bit)
    abar_re = abr[::q].reshape(1, S5_CH)
    abar_im = abi[::q].reshape(1, S5_CH)
    return abar_re, abar_im, bbr.reshape(g, q, p), bbi.reshape(g, q, p)


def _block_diag(t, nblk):
    g, a, b = t.shape
    gb = g // nblk
    t = t.reshape(nblk, gb, a, b)
    eye = jnp.eye(gb, dtype=t.dtype)
    out = t[:, :, :, None, :] * eye[None, :, None, :, None]
    return out.reshape(nblk, gb * a, gb * b)


def _s5_body(*refs, tl, nb, has_init):
    if has_init:
        (u_ref, bre_ref, bim_ref, cre_ref, cim_ref, ar_ref, ai_ref, d_ref, gw_ref, gb_ref,
         h0r_ref, h0i_ref, y_ref, hr_ref, hi_ref, sre, sim, ysc) = refs
    else:
        (u_ref, bre_ref, bim_ref, cre_ref, cim_ref, ar_ref, ai_ref, d_ref, gw_ref, gb_ref,
         y_ref, hr_ref, hi_ref, sre, sim, ysc) = refs
    rows = tl * nb
    cb = S5_CH // S5_NBLK
    ub = S5_WIDTH // S5_NBLK

    @pl.when(pl.program_id(0) == 0)
    def _():
        if has_init:
            hr_ref[...] = h0r_ref[...]
            hi_ref[...] = h0i_ref[...]
        else:
            hr_ref[...] = jnp.zeros_like(hr_ref)
            hi_ref[...] = jnp.zeros_like(hi_ref)

    u = u_ref[...]
    u_b = u.astype(BF16)
    for k in range(S5_NBLK):
        uk = u_b[:, k * ub:(k + 1) * ub]
        sre[...] = _dot(uk, bre_ref[k])
        sim[...] = _dot(uk, bim_ref[k])
        ar8 = jnp.broadcast_to(ar_ref[:, k * cb:(k + 1) * cb], (SUBLANES, cb))
        ai8 = jnp.broadcast_to(ai_ref[:, k * cb:(k + 1) * cb], (SUBLANES, cb))

        def rg_body(rg, carry, k=k, ar8=ar8, ai8=ai8):
            r0 = pl.multiple_of(rg * SUBLANES, SUBLANES)
            xr = hr_ref[pl.ds(r0, SUBLANES), k * cb:(k + 1) * cb]
            xi = hi_ref[pl.ds(r0, SUBLANES), k * cb:(k + 1) * cb]

            def t_body(t, c):
                xr, xi = c
                row = pl.multiple_of(t * nb + r0, SUBLANES)
                br = sre[pl.ds(row, SUBLANES), :]
                bi = sim[pl.ds(row, SUBLANES), :]
                nr = ar8 * xr - ai8 * xi + br
                ni = ar8 * xi + ai8 * xr + bi
                sre[pl.ds(row, SUBLANES), :] = nr
                sim[pl.ds(row, SUBLANES), :] = ni
                return nr, ni

            xr, xi = lax.fori_loop(0, tl, t_body, (xr, xi))
            hr_ref[pl.ds(r0, SUBLANES), k * cb:(k + 1) * cb] = xr
            hi_ref[pl.ds(r0, SUBLANES), k * cb:(k + 1) * cb] = xi
            return carry

        lax.fori_loop(0, nb // SUBLANES, rg_body, 0)
        ysc[:, k * ub:(k + 1) * ub] = (_dot(sre[...].astype(BF16), cre_ref[k])
                                       - _dot(sim[...].astype(BF16), cim_ref[k]))

    y = ysc[...] + d_ref[...] * u
    y = 0.5 * y * (1.0 + jnp.tanh(math.sqrt(2.0 / math.pi) * (y + 0.044715 * (y * y * y))))
    z = _dot(y.astype(BF16), gw_ref[...]) + gb_ref[...]
    out = y * _sigmoid(z)
    y_ref[...] = out.astype(y_ref.dtype)


def _s5_call(u3, bre, bim, cre, cim, abar_re, abar_im, d_skip, glu_w, glu_b, h0, *, tl):
    l, nb, _ = u3.shape
    rows = tl * nb
    u2 = u3.reshape(l * nb, S5_WIDTH)
    has_init = h0 is not None
    const = lambda shape: pl.BlockSpec(shape, lambda t: (0,) * len(shape))
    in_specs = [
        pl.BlockSpec((rows, S5_WIDTH), lambda t: (t, 0)),
        const(bre.shape), const(bim.shape), const(cre.shape), const(cim.shape),
        const((1, S5_CH)), const((1, S5_CH)), const((1, S5_WIDTH)),
        const((S5_WIDTH, S5_WIDTH)), const((1, S5_WIDTH)),
    ]
    args = [u2, bre, bim, cre, cim, abar_re, abar_im, d_skip, glu_w, glu_b]
    if has_init:
        in_specs += [const((nb, S5_CH)), const((nb, S5_CH))]
        args += [h0[0], h0[1]]
    body = functools.partial(_s5_body, tl=tl, nb=nb, has_init=has_init)
    return pl.pallas_call(
        body, grid=(l // tl,), in_specs=in_specs,
        out_specs=[pl.BlockSpec((rows, S5_WIDTH), lambda t: (t, 0)),
                   const((nb, S5_CH)), const((nb, S5_CH))],
        out_shape=[jax.ShapeDtypeStruct((l * nb, S5_WIDTH), BF16),
                   jax.ShapeDtypeStruct((nb, S5_CH), F32),
                   jax.ShapeDtypeStruct((nb, S5_CH), F32)],
        scratch_shapes=[pltpu.VMEM((rows, S5_CH // S5_NBLK), F32),
                        pltpu.VMEM((rows, S5_CH // S5_NBLK), F32),
                        pltpu.VMEM((rows, S5_WIDTH), F32)],
        compiler_params=_cparams(("arbitrary",)),
        name="s5_mixer",
    )(*args)


def _gla_body(*refs, nbb, nch, cs, has_init):
    if has_init:
        (q_ref, k_ref, v_ref, g_ref, gl_ref, up_ref, gb_ref, nw_ref, s0_ref, y_ref, s_ref) = refs
    else:
        (q_ref, k_ref, v_ref, g_ref, gl_ref, up_ref, gb_ref, nw_ref, y_ref, s_ref) = refs
    rows = nbb * nch * cs
    sh = int(math.log2(cs))

    @pl.when(pl.program_id(2) == 0)
    def _():
        if has_init:
            s_ref[...] = s0_ref[...]
        else:
            s_ref[...] = jnp.zeros_like(s_ref)

    q = q_ref[...] * (GLA_HK ** -0.5)
    k = k_ref[...]
    vb = v_ref[...].astype(BF16)
    z = _dot(gl_ref[...].astype(BF16), up_ref[...].astype(BF16)) + gb_ref[...]
    la = _log_sigmoid(z) * (1.0 / GLA_TAU)

    ri = lax.broadcasted_iota(jnp.int32, (rows, rows), 0)
    ci = lax.broadcasted_iota(jnp.int32, (rows, rows), 1)
    same = (ri >> sh) == (ci >> sh)
    causal = jnp.logical_and(same, ci <= ri)
    lmat = jnp.where(causal, 1.0, 0.0).astype(BF16)
    tmat = jnp.where(same, 1.0, 0.0).astype(BF16)
    p1, p2, p3 = _split3(la)
    bcum = _dot(lmat, p1) + _dot(lmat, p2) + _dot(lmat, p3)
    tot = _dot(tmat, p1) + _dot(tmat, p2) + _dot(tmat, p3)

    qd = q * jnp.exp(bcum)
    kd = k * jnp.exp(-bcum)
    kdec = k * jnp.exp(tot - bcum)
    qdb = qd.astype(BF16)
    att = jnp.where(causal, _dot_nt(qdb, kd.astype(BF16)), 0.0)
    o = _dot(att.astype(BF16), vb)

    kdec_t = jnp.transpose(kdec)
    dec_t = jnp.transpose(jnp.exp(tot))
    rchunk = lax.broadcasted_iota(jnp.int32, (rows, 1), 0) >> sh
    cchunk = lax.broadcasted_iota(jnp.int32, (1, rows), 1) >> sh
    single = nbb * nch == 1
    for b in range(nbb):
        s = s_ref[b, 0]
        for c in range(nch):
            idx = b * nch + c
            oi = _dot(qdb, s.astype(BF16))
            o = o + (oi if single else jnp.where(rchunk == idx, oi, 0.0))
            kt = kdec_t if single else jnp.where(cchunk == idx, kdec_t, 0.0)
            s = dec_t[:, idx * cs:idx * cs + 1] * s + _dot(kt.astype(BF16), vb)
        s_ref[b, 0] = s

    o = o * lax.rsqrt(jnp.mean(o * o, axis=-1, keepdims=True) + EPS) * nw_ref[...]
    gg = g_ref[...]
    y_ref[...] = (o * (gg * _sigmoid(gg))).astype(y_ref.dtype)


def _gla_call(projm, projt, up_pad, gkb, nw, s0, *, row0, nseq_blocks, ntiles, nbb, nch, cs, batch):
    rows = nbb * nch * cs
    has_init = s0 is not None
    rb0 = row0 // rows
    rowblk = lambda s, h, t: rb0 + s * ntiles + t
    qoff = S5_WIDTH // GLA_HK
    koff = qoff + GLA_HEADS
    voff = (S5_WIDTH + 2 * GLA_HEADS * GLA_HK) // GLA_HV
    goff = voff + GLA_HEADS
    in_specs = [
        pl.BlockSpec((rows, GLA_HK), lambda s, h, t: (rowblk(s, h, t), qoff + h)),
        pl.BlockSpec((rows, GLA_HK), lambda s, h, t: (rowblk(s, h, t), koff + h)),
        pl.BlockSpec((rows, GLA_HV), lambda s, h, t: (rowblk(s, h, t), voff + h)),
        pl.BlockSpec((rows, GLA_HV), lambda s, h, t: (rowblk(s, h, t), goff + h)),
        pl.BlockSpec((rows, LANES), lambda s, h, t: (rowblk(s, h, t), 0)),
        pl.BlockSpec((LANES, GLA_HK), lambda s, h, t: (0, h)),
        pl.BlockSpec((1, GLA_HK), lambda s, h, t: (0, h)),
        pl.BlockSpec((1, GLA_HV), lambda s, h, t: (0, 0)),
    ]
    args = [projm, projm, projm, projm, projt, up_pad, gkb, nw]
    if has_init:
        in_specs.append(pl.BlockSpec((nbb, 1, GLA_HK, GLA_HV), lambda s, h, t: (s, h, 0, 0)))
        args.append(s0)
    body = functools.partial(_gla_body, nbb=nbb, nch=nch, cs=cs, has_init=has_init)
    return pl.pallas_call(
        body, grid=(nseq_blocks, GLA_HEADS, ntiles), in_specs=in_specs,
        out_specs=[pl.BlockSpec((rows, GLA_HV), lambda s, h, t: (s * ntiles + t, h)),
                   pl.BlockSpec((nbb, 1, GLA_HK, GLA_HV), lambda s, h, t: (s, h, 0, 0))],
        out_shape=[jax.ShapeDtypeStruct((nseq_blocks * ntiles * rows, GLA_HEADS * GLA_HV), BF16),
                   jax.ShapeDtypeStruct((batch, GLA_HEADS, GLA_HK, GLA_HV), F32)],
        compiler_params=_cparams(("arbitrary", "arbitrary", "arbitrary")),
        name="gla_mixer",
    )(*args)


def _mlstm_body(*refs, nbb, cl, rp, has_init):
    if has_init:
        (q_ref, k_ref, v_ref, og_ref, gt_ref, gb_ref, nw_ref, c0_ref, n0_ref, m0_ref,
         y_ref, c_ref, n_ref, m_ref) = refs
    else:
        (q_ref, k_ref, v_ref, og_ref, gt_ref, gb_ref, nw_ref, y_ref, c_ref, n_ref, m_ref) = refs
    rows = nbb * cl
    sh = int(math.log2(cl))
    head = pl.program_id(1)

    @pl.when(pl.program_id(2) == 0)
    def _():
        if has_init:
            c_ref[...] = c0_ref[...]
            n_ref[...] = n0_ref[...]
            m_ref[...] = m0_ref[...]
        else:
            c_ref[...] = jnp.zeros_like(c_ref)
            n_ref[...] = jnp.zeros_like(n_ref)
            m_ref[...] = jnp.full(m_ref.shape, -jnp.inf, F32)

    def pad(x):
        if rp == rows:
            return x
        return jnp.concatenate([x, jnp.zeros((rp - rows, x.shape[1]), x.dtype)], axis=0)

    q = pad(q_ref[...])
    k = pad(k_ref[...]) * (ML_DH ** -0.5)
    v = pad(v_ref[...])
    og = pad(og_ref[...])
    gates = pad(gt_ref[...] + gb_ref[...])
    lane = lax.broadcasted_iota(jnp.int32, (rp, LANES), 1)
    ig = jnp.sum(jnp.where(lane == head, gates, 0.0), axis=1, keepdims=True)
    fg = jnp.sum(jnp.where(lane == head + ML_HEADS, gates, 0.0), axis=1, keepdims=True)
    lf = _log_sigmoid(fg)

    ri = lax.broadcasted_iota(jnp.int32, (rp, rp), 0)
    ci = lax.broadcasted_iota(jnp.int32, (rp, rp), 1)
    causal = jnp.logical_and((ri >> sh) == (ci >> sh), ci <= ri)
    lmat = jnp.where(causal, 1.0, 0.0).astype(BF16)
    fcum = _dot_exact_lhs(lmat, jnp.where(lane == 0, lf, 0.0))[:, 0:1]
    cols = jnp.where(lane == 0, fcum, jnp.where(lane == 1, ig, 0.0))
    er = lax.broadcasted_iota(jnp.int32, (SUBLANES, LANES), 0)
    ec = lax.broadcasted_iota(jnp.int32, (SUBLANES, LANES), 1)
    pick = jnp.where(er == ec, 1.0, 0.0).astype(BF16)
    as_rows = _dot_nt_exact_lhs(pick, cols)
    fcum_row = as_rows[0:1, :]
    ig_row = as_rows[1:2, :]

    rbatch = lax.broadcasted_iota(jnp.int32, (rp, 1), 0) >> sh
    cbatch = lax.broadcasted_iota(jnp.int32, (1, rp), 1) >> sh
    mm = jnp.zeros((rp, 1), F32)
    for b in range(nbb):
        mb = m_ref[b, 0][:, 0:1]
        mm = jnp.where(rbatch == b, mb, mm) if nbb > 1 else jnp.broadcast_to(mb, (rp, 1))

    dmat = jnp.where(causal, fcum - fcum_row + ig_row, -jnp.inf)
    dprev = fcum + mm
    m = jnp.maximum(jnp.max(dmat, axis=1, keepdims=True), dprev)
    w = jnp.exp(dmat - m)
    wp = jnp.exp(dprev - m)
    qb = q.astype(BF16)
    vb = v.astype(BF16)
    sc = _dot_nt(qb, k.astype(BF16)) * w
    num = _dot(sc.astype(BF16), vb)
    den = jnp.sum(sc, axis=1, keepdims=True)

    k_t = jnp.transpose(k)
    for b in range(nbb):
        cm = c_ref[b, 0]
        nm = n_ref[b, 0]
        mb = m_ref[b, 0][:, 0:1]
        qc = wp * _dot(qb, cm.astype(BF16))
        qn = wp * jnp.sum(q * nm, axis=1, keepdims=True)
        if nbb > 1:
            rsel = rbatch == b
            qc = jnp.where(rsel, qc, 0.0)
            qn = jnp.where(rsel, qn, 0.0)
        num = num + qc
        den = den + qn
        last = b * cl + cl - 1
        m_new = m[last:last + 1, :]
        fl = fcum[last:last + 1, :]
        decay = jnp.exp(fl + mb - m_new)
        wk_col = jnp.exp(fl - fcum + ig - m_new)
        wk_row = jnp.exp(fl - fcum_row + ig_row - m_new)
        if nbb > 1:
            wk_col = jnp.where(rbatch == b, wk_col, 0.0)
            wk_row = jnp.where(cbatch == b, wk_row, 0.0)
        c_ref[b, 0] = decay * cm + _dot((k_t * wk_row).astype(BF16), vb)
        n_ref[b, 0] = decay * nm + jnp.sum(wk_col * k, axis=0, keepdims=True)
        m_ref[b, 0] = jnp.broadcast_to(m_new, (1, LANES))

    hh = num / jnp.maximum(jnp.abs(den), jnp.exp(-m))
    hh = hh * _sigmoid(og)
    hh = hh * lax.rsqrt(jnp.mean(hh * hh, axis=-1, keepdims=True) + EPS) * nw_ref[...]
    y_ref[...] = hh[:rows].astype(y_ref.dtype)


def _mlstm_call(projm, projt, gate_b, nw, init, *, row0, nseq_blocks, ntiles, nbb, cl, rp, batch):
    rows = nbb * cl
    has_init = init is not None
    rb0 = row0 // rows
    rowblk = lambda s, h, t: rb0 + s * ntiles + t
    hd = ML_HEADS
    in_specs = [
        pl.BlockSpec((rows, ML_DH), lambda s, h, t: (rowblk(s, h, t), h)),
        pl.BlockSpec((rows, ML_DH), lambda s, h, t: (rowblk(s, h, t), hd + h)),
        pl.BlockSpec((rows, ML_DH), lambda s, h, t: (rowblk(s, h, t), 2 * hd + h)),
        pl.BlockSpec((rows, ML_DH), lambda s, h, t: (rowblk(s, h, t), 3 * hd + h)),
        pl.BlockSpec((rows, LANES), lambda s, h, t: (rowblk(s, h, t), 0)),
        pl.BlockSpec((1, LANES), lambda s, h, t: (0, 0)),
        pl.BlockSpec((1, ML_DH), lambda s, h, t: (0, 0)),
    ]
    args = [projm, projm, projm, projm, projt, gate_b, nw]
    c_spec = pl.BlockSpec((nbb, 1, ML_DH, ML_DH), lambda s, h, t: (s, h, 0, 0))
    n_spec = pl.BlockSpec((nbb, 1, 1, ML_DH), lambda s, h, t: (s, h, 0, 0))
    m_spec = pl.BlockSpec((nbb, 1, 1, LANES), lambda s, h, t: (s, h, 0, 0))
    if has_init:
        in_specs += [c_spec, n_spec, m_spec]
        args += list(init)
    body = functools.partial(_mlstm_body, nbb=nbb, cl=cl, rp=rp, has_init=has_init)
    return pl.pallas_call(
        body, grid=(nseq_blocks, ML_HEADS, ntiles), in_specs=in_specs,
        out_specs=[pl.BlockSpec((rows, ML_DH), lambda s, h, t: (s * ntiles + t, h)),
                   c_spec, n_spec, m_spec],
        out_shape=[jax.ShapeDtypeStruct((nseq_blocks * ntiles * rows, ML_HEADS * ML_DH), BF16),
                   jax.ShapeDtypeStruct((batch, ML_HEADS, ML_DH, ML_DH), F32),
                   jax.ShapeDtypeStruct((batch, ML_HEADS, 1, ML_DH), F32),
                   jax.ShapeDtypeStruct((batch, ML_HEADS, 1, LANES), F32)],
        compiler_params=_cparams(("arbitrary", "arbitrary", "arbitrary")),
        name="mlstm_mixer",
    )(*args)


def kernel(x_prompt, x_sample, state_s5_re, state_s5_im, state_gla, state_mlstm_c, state_mlstm_n,
           state_mlstm_m, norm_mix, norm_mlp, norm_final, w_in_even, s5_a_re, s5_a_im, s5_log_step,
           s5_b_re, s5_b_im, s5_c_re, s5_c_im, s5_d, s5_glu_w, s5_glu_b, gla_gk_up, gla_gk_b,
           gla_norm, w_out_even, w_in_odd, mlstm_b_i, mlstm_b_f, mlstm_norm, w_out_odd,
           w_mlp_up, w_mlp_down):
    bp, lp, d = x_prompt.shape
    bs, ls, _ = x_sample.shape
    mp = bp * lp
    msamp = bs * ls
    x = jnp.concatenate([x_prompt.reshape(mp, d), x_sample.reshape(msamp, d)], axis=0)

    n_even = S5_WIDTH + 2 * GLA_HEADS * GLA_HK + 2 * GLA_HEADS * GLA_HV
    tail_e = jnp.pad(w_in_even[0][:, n_even:], ((0, 0), (0, LANES - (w_in_even.shape[2] - n_even))))
    projm, projt = _norm_mm(x, norm_mix[0], w_in_even, 0, n_even, tail=tail_e)

    abar_re, abar_im, bbr, bbi = _s5_discretise(s5_a_re[0], s5_a_im[0], s5_log_step[0],
                                                s5_b_re[0], s5_b_im[0])
    bre = _block_diag(bbr, S5_NBLK).astype(BF16)
    bim = _block_diag(bbi, S5_NBLK).astype(BF16)
    cre = _block_diag(jnp.transpose(s5_c_re[0], (0, 2, 1)), S5_NBLK).astype(BF16)
    cim = _block_diag(jnp.transpose(s5_c_im[0], (0, 2, 1)), S5_NBLK).astype(BF16)
    s5_consts = (bre, bim, cre, cim, abar_re, abar_im, s5_d[0].reshape(1, S5_WIDTH),
                 s5_glu_w[0].astype(BF16), s5_glu_b[0].reshape(1, S5_WIDTH))

    u_p = jnp.transpose(projm[:mp, :S5_WIDTH].reshape(bp, lp, S5_WIDTH), (1, 0, 2))
    u_p = jnp.pad(u_p, ((0, 0), (0, SUBLANES - bp), (0, 0)))
    ys5_p, hr_p, hi_p = _s5_call(u_p, *s5_consts, None, tl=64)
    ys5_p = jnp.transpose(ys5_p.reshape(lp, SUBLANES, S5_WIDTH)[:, :bp], (1, 0, 2)).reshape(mp, S5_WIDTH)
    u_s = jnp.transpose(projm[mp:, :S5_WIDTH].reshape(bs, ls, S5_WIDTH), (1, 0, 2))
    h0 = (state_s5_re[0].reshape(bs, S5_CH), state_s5_im[0].reshape(bs, S5_CH))
    ys5_s, hr_s, hi_s = _s5_call(u_s, *s5_consts, h0, tl=ls)
    ys5_s = jnp.transpose(ys5_s.reshape(ls, bs, S5_WIDTH), (1, 0, 2)).reshape(msamp, S5_WIDTH)

    up_pad = jnp.pad(gla_gk_up[0], ((0, LANES - gla_gk_up.shape[1]), (0, 0)))
    gkb = gla_gk_b[0].reshape(1, -1)
    gnw = gla_norm[0].reshape(1, GLA_HV)
    ptile = 256
    ygla_p, sg_p = _gla_call(projm, projt, up_pad, gkb, gnw, None, row0=0, nseq_blocks=bp,
                             ntiles=lp // ptile, nbb=1, nch=ptile // GLA_CHUNK, cs=GLA_CHUNK, batch=bp)
    gnbb = 32
    ygla_s, sg_s = _gla_call(projm, projt, up_pad, gkb, gnw, state_gla[0], row0=mp,
                             nseq_blocks=bs // gnbb, ntiles=1, nbb=gnbb, nch=1, cs=ls, batch=bs)

    mix = jnp.concatenate([jnp.concatenate([ys5_p, ygla_p], axis=1),
                           jnp.concatenate([ys5_s, ygla_s], axis=1)], axis=0)
    h = _mm_res(mix, w_out_even, 0, x)
    hid = _norm_mm(h, norm_mlp[0], w_mlp_up, 0, D_FF, act="relu2", out_dtype=BF16)
    h = _mm_res(hid, w_mlp_down, 0, h)

    n_odd = 4 * ML_HEADS * ML_DH
    tail_o = jnp.pad(w_in_odd[0][:, n_odd:], ((0, 0), (0, LANES - (w_in_odd.shape[2] - n_odd))))
    projm, projt = _norm_mm(h, norm_mix[1], w_in_odd, 0, n_odd, tail=tail_o)
    gate_b = jnp.pad(jnp.concatenate([mlstm_b_i[0], mlstm_b_f[0]]), (0, LANES - 2 * ML_HEADS)).reshape(1, LANES)
    mnw = mlstm_norm[0].reshape(1, ML_DH)
    yml_p, c_p, n_p, m_p = _mlstm_call(projm, projt, gate_b, mnw, None, row0=0, nseq_blocks=bp,
                                       ntiles=lp // ML_CHUNK, nbb=1, cl=ML_CHUNK, rp=ML_CHUNK, batch=bp)
    mnbb = 8
    init = (state_mlstm_c[0], state_mlstm_n[0].reshape(bs, ML_HEADS, 1, ML_DH),
            jnp.broadcast_to(state_mlstm_m[0][:, :, None, None], (bs, ML_HEADS, 1, LANES)))
    yml_s, c_s, n_s, m_s = _mlstm_call(projm, projt, gate_b, mnw, init, row0=mp, nseq_blocks=bs // mnbb,
                                       ntiles=1, nbb=mnbb, cl=ls, rp=LANES, batch=bs)
    mix = jnp.concatenate([yml_p, yml_s], axis=0)
    h = _mm_res(mix, w_out_odd, 0, h)
    hid = _norm_mm(h, norm_mlp[1], w_mlp_up, 1, D_FF, act="relu2", out_dtype=BF16)
    h = _mm_res(hid, w_mlp_down, 1, h)

    y = _rmsnorm(h, norm_final)
    g, p = S5_GROUPS, S5_STATE
    return (y[:mp].reshape(bp, lp, d), y[mp:].reshape(bs, ls, d),
            hr_p[:bp].reshape(1, bp, g, p), hi_p[:bp].reshape(1, bp, g, p), sg_p[None],
            c_p[None], n_p.reshape(1, bp, ML_HEADS, ML_DH), m_p[:, :, 0, 0][None],
            hr_s.reshape(1, bs, g, p), hi_s.reshape(1, bs, g, p), sg_s[None],
            c_s[None], n_s.reshape(1, bs, ML_HEADS, ML_DH), m_s[:, :, 0, 0][None])
```

```python
import functools
import math

import jax
import jax.numpy as jnp
import numpy as np
from jax import lax
from jax.experimental import pallas as pl
from jax.experimental.pallas import tpu as pltpu

F32 = jnp.float32
BF16 = jnp.bfloat16
EPS = 1e-6

D_MODEL = 2048
D_FF = 4 * D_MODEL
S5_WIDTH = 1024
S5_GROUPS = 64
S5_GROUP = 16
S5_STATE = 64
S5_CH = S5_GROUPS * S5_STATE
S5_NBLK = 4
GLA_HEADS = 4
GLA_HK = 128
GLA_HV = 256
GLA_TAU = 16.0
GLA_CHUNK = 64
ML_HEADS = 4
ML_DH = 512
ML_CHUNK = 256

SUBLANES = 8
LANES = 128
VMEM_LIMIT = 56 * 1024 * 1024


def _cparams(sem):
    return pltpu.CompilerParams(dimension_semantics=sem, vmem_limit_bytes=VMEM_LIMIT)


def _dot(a, b):
    return jnp.dot(a, b, preferred_element_type=F32)


def _dot_nt(a, b):
    return lax.dot_general(a, b, (((1,), (1,)), ((), ())), preferred_element_type=F32)


def _split3(x):
    p1 = x.astype(BF16)
    r1 = x - p1.astype(F32)
    p2 = r1.astype(BF16)
    r2 = r1 - p2.astype(F32)
    p3 = r2.astype(BF16)
    return p1, p2, p3


def _dot_exact_lhs(m, x):
    p1, p2, p3 = _split3(x)
    return _dot(m, p1) + _dot(m, p2) + _dot(m, p3)


def _dot_nt_exact_lhs(m, x):
    p1, p2, p3 = _split3(x)
    return _dot_nt(m, p1) + _dot_nt(m, p2) + _dot_nt(m, p3)


def _log_sigmoid(x):
    return jnp.minimum(x, 0.0) - jnp.log1p(jnp.exp(-jnp.abs(x)))


def _sigmoid(x):
    return 1.0 / (1.0 + jnp.exp(-x))


def _norm_mm_body(*refs, act, has_tail, n_ptiles, tm, rchunk):
    dual = n_ptiles is not None
    x_ref = refs[0]
    xs_ref = refs[1] if dual else None
    rest = refs[2:] if dual else refs[1:]
    if has_tail:
        g_ref, w_ref, wt_ref, o_ref, ot_ref, xn_ref = rest
    else:
        g_ref, w_ref, o_ref, xn_ref = rest

    def normalise(src_ref, nrows):
        g = g_ref[...]
        for r in range(0, nrows, rchunk):
            x = src_ref[r:r + rchunk, :]
            ms = jnp.mean(x * x, axis=-1, keepdims=True)
            xn_ref[r:r + rchunk, :] = (x * lax.rsqrt(ms + EPS) * g).astype(BF16)

    @pl.when(pl.program_id(1) == 0)
    def _():
        if dual:
            @pl.when(pl.program_id(0) < n_ptiles)
            def _():
                normalise(x_ref, tm)

            @pl.when(pl.program_id(0) >= n_ptiles)
            def _():
                ns = xs_ref.shape[0]
                normalise(xs_ref, ns)
                xn_ref[ns:, :] = jnp.zeros((tm - ns, xn_ref.shape[1]), BF16)
        else:
            normalise(x_ref, tm)
        if has_tail:
            ot_ref[...] = _dot(xn_ref[...], wt_ref[...].astype(BF16))

    acc = _dot(xn_ref[...], w_ref[...].astype(BF16))
    if act == "relu2":
        acc = jnp.square(jnp.maximum(acc, 0.0))
    o_ref[...] = acc.astype(o_ref.dtype)


def _norm_mm(x, g, w3, layer, n_main, *, tail=None, act=None, out_dtype=F32, tm=1088, tn=512):
    dual = isinstance(x, tuple)
    if dual:
        xp, xs = x
        n_ptiles = xp.shape[0] // tm
        m, kdim = (n_ptiles + 1) * tm, xp.shape[1]
        grid = (n_ptiles + 1, n_main // tn)
        in_specs = [
            pl.BlockSpec((tm, kdim), lambda i, j: (jnp.minimum(i, n_ptiles - 1), 0)),
            pl.BlockSpec(xs.shape, lambda i, j: (0, 0)),
        ]
        args = [xp, xs]
    else:
        n_ptiles = None
        m, kdim = x.shape
        grid = (m // tm, n_main // tn)
        in_specs = [pl.BlockSpec((tm, kdim), lambda i, j: (i, 0))]
        args = [x]
    in_specs += [
        pl.BlockSpec((1, kdim), lambda i, j: (0, 0)),
        pl.BlockSpec((None, kdim, tn), lambda i, j: (layer, 0, j)),
    ]
    args += [g.reshape(1, kdim), w3]
    out_shape = [jax.ShapeDtypeStruct((m, n_main), out_dtype)]
    out_specs = [pl.BlockSpec((tm, tn), lambda i, j: (i, j))]
    if tail is not None:
        in_specs.append(pl.BlockSpec((kdim, LANES), lambda i, j: (0, 0)))
        args.append(tail)
        out_shape.append(jax.ShapeDtypeStruct((m, LANES), F32))
        out_specs.append(pl.BlockSpec((tm, LANES), lambda i, j: (i, 0)))
    body = functools.partial(_norm_mm_body, act=act, has_tail=tail is not None, n_ptiles=n_ptiles,
                             tm=tm, rchunk=tm // 4)
    res = pl.pallas_call(
        body, grid=grid, in_specs=in_specs, out_specs=out_specs, out_shape=out_shape,
        scratch_shapes=[pltpu.VMEM((tm, kdim), BF16)],
        compiler_params=_cparams(("arbitrary", "arbitrary")),
        name="norm_mm",
    )(*args)
    return res if tail is not None else res[0]


def _mm_res_body(a_ref, w_ref, r_ref, o_ref):
    p = _dot(a_ref[...].astype(BF16), w_ref[...].astype(BF16))

    @pl.when(pl.program_id(2) == 0)
    def _():
        o_ref[...] = r_ref[...] + p

    @pl.when(pl.program_id(2) > 0)
    def _():
        o_ref[...] += p


def _mm_res(a, w3, layer, res, *, tm=1088, tn=1024, tk=1024):
    m, kdim = a.shape
    n = w3.shape[2]
    grid = (m // tm, n // tn, kdim // tk)
    return pl.pallas_call(
        _mm_res_body, grid=grid,
        in_specs=[
            pl.BlockSpec((tm, tk), lambda i, j, k: (i, k)),
            pl.BlockSpec((None, tk, tn), lambda i, j, k: (layer, k, j)),
            pl.BlockSpec((tm, tn), lambda i, j, k: (i, j)),
        ],
        out_specs=pl.BlockSpec((tm, tn), lambda i, j, k: (i, j)),
        out_shape=jax.ShapeDtypeStruct((m, n), F32),
        compiler_params=_cparams(("arbitrary", "arbitrary", "arbitrary")),
        name="mm_res",
    )(a, w3, res)


def _mm_out_body(*refs, kcs, n_ptiles):
    n = len(kcs)
    ap, asm = refs[:n], refs[n:2 * n]
    w_ref, rp_ref, rs_ref, o_ref, wb_ref = refs[2 * n:]
    i = pl.program_id(1)

    @pl.when(i == 0)
    def _():
        wb_ref[...] = w_ref[...].astype(BF16)

    def compute(srcs, r_ref):
        acc = r_ref[...]
        off = 0
        for a_ref, kc in zip(srcs, kcs):
            acc = acc + _dot(a_ref[...], wb_ref[off:off + kc, :])
            off += kc
        o_ref[...] = acc

    @pl.when(i < n_ptiles)
    def _():
        compute(ap, rp_ref)

    @pl.when(i >= n_ptiles)
    def _():
        compute(asm, rs_ref)


def _mm_out(a_prompt, a_sample, w3, layer, res_prompt, res_sample, *, tm=512, tn=1024):
    kcs = tuple(a.shape[1] for a in a_prompt)
    kdim, n = sum(kcs), w3.shape[2]
    n_ptiles = a_prompt[0].shape[0] // tm
    m = a_prompt[0].shape[0] + a_sample[0].shape[0]
    (rp, rp0), (rs, rs0) = res_prompt, res_sample
    pidx = lambda i: jnp.minimum(i, n_ptiles - 1)
    in_specs = ([pl.BlockSpec((tm, kc), lambda j, i: (pidx(i), 0)) for kc in kcs]
                + [pl.BlockSpec((tm, kc), lambda j, i: (0, 0)) for kc in kcs]
                + [pl.BlockSpec((None, kdim, tn), lambda j, i: (layer, 0, j)),
                   pl.BlockSpec((tm, tn), lambda j, i: (rp0 + pidx(i), j)),
                   pl.BlockSpec((tm, tn), lambda j, i: (rs0, j))])
    body = functools.partial(_mm_out_body, kcs=kcs, n_ptiles=n_ptiles)
    return pl.pallas_call(
        body, grid=(n // tn, n_ptiles + 1), in_specs=in_specs,
        out_specs=pl.BlockSpec((tm, tn), lambda j, i: (i, j)),
        out_shape=jax.ShapeDtypeStruct((m, n), F32),
        scratch_shapes=[pltpu.VMEM((kdim, tn), BF16)],
        compiler_params=_cparams(("arbitrary", "arbitrary")),
        name="mm_out",
    )(*a_prompt, *a_sample, w3, rp, rs)


def _rmsnorm_body(x_ref, g_ref, op_ref, os_ref, *, n_ptiles):
    x = x_ref[...]
    ms = jnp.mean(x * x, axis=-1, keepdims=True)
    y = x * lax.rsqrt(ms + EPS) * g_ref[...]

    @pl.when(pl.program_id(0) < n_ptiles)
    def _():
        op_ref[...] = y

    @pl.when(pl.program_id(0) >= n_ptiles)
    def _():
        os_ref[...] = y


def _rmsnorm(x, g, m_prompt, *, tm=512):
    m, d = x.shape
    n_ptiles = m_prompt // tm
    return pl.pallas_call(
        functools.partial(_rmsnorm_body, n_ptiles=n_ptiles), grid=(n_ptiles + 1,),
        in_specs=[pl.BlockSpec((tm, d), lambda i: (i, 0)), pl.BlockSpec((1, d), lambda i: (0, 0))],
        out_specs=[pl.BlockSpec((tm, d), lambda i: (jnp.minimum(i, n_ptiles - 1), 0)),
                   pl.BlockSpec((tm, d), lambda i: (0, 0))],
        out_shape=[jax.ShapeDtypeStruct((m_prompt, d), F32),
                   jax.ShapeDtypeStruct((m - m_prompt, d), F32)],
        compiler_params=_cparams(("arbitrary",)),
        name="final_norm",
    )(x, g.reshape(1, d))


def _s5_disc_body(ar_ref, ai_ref, ls_ref, br_ref, bi_ref, abr_ref, abi_ref, bbr_ref, bbi_ref):
    ar = ar_ref[...]
    ai = ai_ref[...]
    dt = jnp.exp(ls_ref[...])
    mag = jnp.exp(ar * dt)
    abr = mag * jnp.cos(ai * dt)
    abi = mag * jnp.sin(ai * dt)
    lam2 = ar * ar + ai * ai
    zr = abr - 1.0
    cr = (zr * ar + abi * ai) / lam2
    ci = (abi * ar - zr * ai) / lam2
    br = br_ref[...]
    bi = bi_ref[...]
    abr_ref[...] = abr
    abi_ref[...] = abi
    bbr_ref[...] = cr * br - ci * bi
    bbi_ref[...] = cr * bi + ci * br


def _s5_discretise(a_re, a_im, log_step, b_re, b_im):
    g, p, q = S5_GROUPS, S5_STATE, S5_GROUP
    rep = lambda t: jnp.repeat(t, q, axis=0)
    ls = jnp.broadcast_to(log_step[:, None], (g, p))
    brt = jnp.transpose(b_re, (0, 2, 1)).reshape(g * q, p)
    bit = jnp.transpose(b_im, (0, 2, 1)).reshape(g * q, p)
    shp = jax.ShapeDtypeStruct((g * q, p), F32)
    spec = pl.BlockSpec((g * q, p), lambda: (0, 0))
    abr, abi, bbr, bbi = pl.pallas_call(
        _s5_disc_body, in_specs=[spec] * 5, out_specs=[spec] * 4, out_shape=[shp] * 4,
        name="s5_disc",
    )(rep(a_re), rep(a_im), rep(ls), brt, bit)
    abar_re = abr[::q].reshape(1, S5_CH)
    abar_im = abi[::q].reshape(1, S5_CH)
    return abar_re, abar_im, bbr.reshape(g, q, p), bbi.reshape(g, q, p)


def _block_diag(t, nblk):
    g, a, b = t.shape
    gb = g // nblk
    t = t.reshape(nblk, gb, a, b)
    eye = jnp.eye(gb, dtype=t.dtype)
    out = t[:, :, :, None, :] * eye[None, :, None, :, None]
    return out.reshape(nblk, gb * a, gb * b)


def _s5_perm(nb, tl):
    r = np.arange(nb * tl)
    p = np.zeros((nb * tl, nb * tl), np.float32)
    p[r, (r % nb) * tl + r // nb] = 1.0
    return p


def _s5_body(*refs, nsrc, nb, has_init, permute):
    u_refs, refs = refs[:nsrc], refs[nsrc:]
    if permute:
        pm_ref, pmt_ref = refs[:2]
        refs = refs[2:]
    bre_ref, bim_ref, cre_ref, cim_ref, ar_ref, ai_ref, d_ref, gw_ref, gb_ref = refs[:9]
    refs = refs[9:]
    if has_init:
        h0r_ref, h0i_ref, y_ref, hr_ref, hi_ref, sre, sim, ysc = refs
    else:
        y_ref, hr_ref, hi_ref, sre, sim, ysc = refs
    rows = sre.shape[0]
    tl = rows // nb
    cb = S5_CH // S5_NBLK
    ub = S5_WIDTH // S5_NBLK
    hw = cb // 2

    @pl.when(pl.program_id(0) == 0)
    def _():
        if has_init:
            hr_ref[...] = h0r_ref[...]
            hi_ref[...] = h0i_ref[...]
        else:
            hr_ref[...] = jnp.zeros_like(hr_ref)
            hi_ref[...] = jnp.zeros_like(hi_ref)

    u = u_refs[0][...] if nsrc == 1 else jnp.concatenate([r[...] for r in u_refs], axis=0)
    if permute:
        moved = _dot(pm_ref[...], jnp.concatenate(_split3(u), axis=1))
        t1 = moved[:, :S5_WIDTH]
        u_tm = t1 + moved[:, S5_WIDTH:2 * S5_WIDTH] + moved[:, 2 * S5_WIDTH:]
        u_b = t1.astype(BF16)
    else:
        u_tm = u
        u_b = u.astype(BF16)
    low_rows = lax.broadcasted_iota(jnp.int32, (SUBLANES, hw), 0) < nb

    for k in range(S5_NBLK):
        uk = u_b[:, k * ub:(k + 1) * ub]
        sre[...] = _dot(uk, bre_ref[k])
        sim[...] = _dot(uk, bim_ref[k])
        for half in range(2):
            c0 = k * cb + half * hw
            l0 = half * hw
            ar8 = jnp.broadcast_to(ar_ref[:, c0:c0 + hw], (SUBLANES, hw))
            ai8 = jnp.broadcast_to(ai_ref[:, c0:c0 + hw], (SUBLANES, hw))

            if nb < SUBLANES:
                def j_body(j, c, l0=l0, ar8=ar8, ai8=ai8):
                    sr, si = c
                    row = pl.multiple_of(j * SUBLANES, SUBLANES)
                    dr = sre[pl.ds(row, SUBLANES), l0:l0 + hw]
                    di = sim[pl.ds(row, SUBLANES), l0:l0 + hw]
                    pr = pltpu.roll(sr, nb, 0)
                    pi = pltpu.roll(si, nb, 0)
                    yr = ar8 * pr - ai8 * pi + dr
                    yi = ar8 * pi + ai8 * pr + di
                    qr = pltpu.roll(yr, nb, 0)
                    qi = pltpu.roll(yi, nb, 0)
                    zr = ar8 * qr - ai8 * qi + dr
                    zi = ar8 * qi + ai8 * qr + di
                    sre[pl.ds(row, SUBLANES), l0:l0 + hw] = jnp.where(low_rows, yr, zr)
                    sim[pl.ds(row, SUBLANES), l0:l0 + hw] = jnp.where(low_rows, yi, zi)
                    return zr, zi

                sr, si = lax.fori_loop(0, rows // SUBLANES, j_body,
                                       (hr_ref[:, c0:c0 + hw], hi_ref[:, c0:c0 + hw]), unroll=2)
                hr_ref[:, c0:c0 + hw] = sr
                hi_ref[:, c0:c0 + hw] = si
            else:
                def rg_body(rg, carry, c0=c0, l0=l0, ar8=ar8, ai8=ai8):
                    r0 = pl.multiple_of(rg * SUBLANES, SUBLANES)
                    xr = hr_ref[pl.ds(r0, SUBLANES), c0:c0 + hw]
                    xi = hi_ref[pl.ds(r0, SUBLANES), c0:c0 + hw]
                    for t in range(tl):
                        row = pl.multiple_of(t * nb + r0, SUBLANES)
                        nr = ar8 * xr - ai8 * xi + sre[pl.ds(row, SUBLANES), l0:l0 + hw]
                        ni = ar8 * xi + ai8 * xr + sim[pl.ds(row, SUBLANES), l0:l0 + hw]
                        sre[pl.ds(row, SUBLANES), l0:l0 + hw] = nr
                        sim[pl.ds(row, SUBLANES), l0:l0 + hw] = ni
                        xr, xi = nr, ni
                    hr_ref[pl.ds(r0, SUBLANES), c0:c0 + hw] = xr
                    hi_ref[pl.ds(r0, SUBLANES), c0:c0 + hw] = xi
                    return carry

                lax.fori_loop(0, nb // SUBLANES, rg_body, 0)
        ysc[:, k * ub:(k + 1) * ub] = (_dot(sre[...].astype(BF16), cre_ref[k])
                                       - _dot(sim[...].astype(BF16), cim_ref[k]))

    y = ysc[...] + d_ref[...] * u_tm
    y = 0.5 * y * (1.0 + jnp.tanh(math.sqrt(2.0 / math.pi) * (y + 0.044715 * (y * y * y))))
    z = _dot(y.astype(BF16), gw_ref[...]) + gb_ref[...]
    out = (y * _sigmoid(z)).astype(BF16)
    if permute:
        out = _dot(pmt_ref[...], out).astype(BF16)
    y_ref[...] = out.reshape(y_ref.shape)


def _s5_call(src, bre, bim, cre, cim, abar_re, abar_im, d_skip, glu_w, glu_b, h0, *,
             row0, nsrc, rs, seq_len, nb, tl, permute):
    rows = nsrc * rs
    ntiles = seq_len // rs if nsrc > 1 else 1
    has_init = h0 is not None
    srows = max(nb, SUBLANES)
    const = lambda shape: pl.BlockSpec(shape, lambda t: (0,) * len(shape))
    in_specs = [pl.BlockSpec((rs, S5_WIDTH), lambda t, s=s: ((row0 + s * seq_len) // rs + t, 0))
                for s in range(nsrc)]
    args = [src] * nsrc
    if permute:
        pm = _s5_perm(nb, tl)
        in_specs += [const((rows, rows)), const((rows, rows))]
        args += [jnp.asarray(pm, BF16), jnp.asarray(pm.T, BF16)]
    in_specs += [
        const(bre.shape), const(bim.shape), const(cre.shape), const(cim.shape),
        const((1, S5_CH)), const((1, S5_CH)), const((1, S5_WIDTH)),
        const((S5_WIDTH, S5_WIDTH)), const((1, S5_WIDTH)),
    ]
    args += [bre, bim, cre, cim, abar_re, abar_im, d_skip, glu_w, glu_b]
    if has_init:
        in_specs += [const((srows, S5_CH)), const((srows, S5_CH))]
        args += [h0[0], h0[1]]
    body = functools.partial(_s5_body, nsrc=nsrc, nb=nb, has_init=has_init, permute=permute)
    return pl.pallas_call(
        body, grid=(ntiles,), in_specs=in_specs,
        out_specs=[pl.BlockSpec((nsrc, rs, S5_WIDTH), lambda t: (0, t, 0)),
                   const((srows, S5_CH)), const((srows, S5_CH))],
        out_shape=[jax.ShapeDtypeStruct((nsrc, ntiles * rs, S5_WIDTH), BF16),
                   jax.ShapeDtypeStruct((srows, S5_CH), F32),
                   jax.ShapeDtypeStruct((srows, S5_CH), F32)],
        scratch_shapes=[pltpu.VMEM((rows, S5_CH // S5_NBLK), F32),
                        pltpu.VMEM((rows, S5_CH // S5_NBLK), F32),
                        pltpu.VMEM((rows, S5_WIDTH), F32)],
        compiler_params=_cparams(("arbitrary",)),
        name="s5_mixer",
    )(*args)


def _gla_body(*refs, nbb, nch, cs, has_init, hps):
    if has_init:
        (q_ref, k_ref, v_ref, g_ref, gl_ref, up_ref, gb_ref, nw_ref, s0_ref, y_ref, s_ref) = refs
    else:
        (q_ref, k_ref, v_ref, g_ref, gl_ref, up_ref, gb_ref, nw_ref, y_ref, s_ref) = refs
    rows = nbb * nch * cs
    sh = int(math.log2(cs))

    @pl.when(pl.program_id(2) == 0)
    def _():
        if has_init:
            s_ref[...] = s0_ref[...]
        else:
            s_ref[...] = jnp.zeros_like(s_ref)

    ri = lax.broadcasted_iota(jnp.int32, (rows, rows), 0)
    ci = lax.broadcasted_iota(jnp.int32, (rows, rows), 1)
    same = (ri >> sh) == (ci >> sh)
    causal = jnp.logical_and(same, ci <= ri)
    lmat = jnp.where(causal, 1.0, 0.0).astype(BF16)
    tmat = jnp.where(same, 1.0, 0.0).astype(BF16)
    rchunk = lax.broadcasted_iota(jnp.int32, (rows, 1), 0) >> sh
    cchunk = lax.broadcasted_iota(jnp.int32, (1, rows), 1) >> sh
    single = nbb * nch == 1
    glb = gl_ref[...].astype(BF16)

    for hh in range(hps):
        ks = slice(hh * GLA_HK, (hh + 1) * GLA_HK)
        vs = slice(hh * GLA_HV, (hh + 1) * GLA_HV)
        q = q_ref[:, ks] * (GLA_HK ** -0.5)
        k = k_ref[:, ks]
        vb = v_ref[:, vs].astype(BF16)
        z = _dot(glb, up_ref[:, ks].astype(BF16)) + gb_ref[:, ks]
        la = _log_sigmoid(z) * (1.0 / GLA_TAU)
        p1, p2, p3 = _split3(la)
        bcum = _dot(lmat, p1) + _dot(lmat, p2) + _dot(lmat, p3)
        tot = _dot(tmat, p1) + _dot(tmat, p2) + _dot(tmat, p3)

        qd = q * jnp.exp(bcum)
        kd = k * jnp.exp(-bcum)
        kdec = k * jnp.exp(tot - bcum)
        qdb = qd.astype(BF16)
        att = jnp.where(causal, _dot_nt(qdb, kd.astype(BF16)), 0.0)
        o = _dot(att.astype(BF16), vb)

        kdec_t = jnp.transpose(kdec)
        dec_t = jnp.transpose(jnp.exp(tot))
        for b in range(nbb):
            s = s_ref[b, hh]
            for c in range(nch):
                idx = b * nch + c
                oi = _dot(qdb, s.astype(BF16))
                o = o + (oi if single else jnp.where(rchunk == idx, oi, 0.0))
                kt = kdec_t if single else jnp.where(cchunk == idx, kdec_t, 0.0)
                s = dec_t[:, idx * cs:idx * cs + 1] * s + _dot(kt.astype(BF16), vb)
            s_ref[b, hh] = s

        o = o * lax.rsqrt(jnp.mean(o * o, axis=-1, keepdims=True) + EPS) * nw_ref[...]
        gg = g_ref[:, vs]
        y_ref[:, vs] = (o * (gg * _sigmoid(gg))).astype(y_ref.dtype)


def _gla_call(projm, projt, up_pad, gkb, nw, s0, *, row0, nseq_blocks, ntiles, nbb, nch, cs, batch, hps):
    rows = nbb * nch * cs
    has_init = s0 is not None
    rb0 = row0 // rows
    rowblk = lambda s, h, t: rb0 + s * ntiles + t
    wk, wv = hps * GLA_HK, hps * GLA_HV
    qoff = S5_WIDTH // wk
    koff = qoff + GLA_HEADS // hps
    voff = (S5_WIDTH + 2 * GLA_HEADS * GLA_HK) // wv
    goff = voff + GLA_HEADS // hps
    in_specs = [
        pl.BlockSpec((rows, wk), lambda s, h, t: (rowblk(s, h, t), qoff + h)),
        pl.BlockSpec((rows, wk), lambda s, h, t: (rowblk(s, h, t), koff + h)),
        pl.BlockSpec((rows, wv), lambda s, h, t: (rowblk(s, h, t), voff + h)),
        pl.BlockSpec((rows, wv), lambda s, h, t: (rowblk(s, h, t), goff + h)),
        pl.BlockSpec((rows, LANES), lambda s, h, t: (rowblk(s, h, t), 0)),
        pl.BlockSpec((LANES, wk), lambda s, h, t: (0, h)),
        pl.BlockSpec((1, wk), lambda s, h, t: (0, h)),
        pl.BlockSpec((1, GLA_HV), lambda s, h, t: (0, 0)),
    ]
    args = [projm, projm, projm, projm, projt, up_pad, gkb, nw]
    if has_init:
        in_specs.append(pl.BlockSpec((nbb, hps, GLA_HK, GLA_HV), lambda s, h, t: (s, h, 0, 0)))
        args.append(s0)
    body = functools.partial(_gla_body, nbb=nbb, nch=nch, cs=cs, has_init=has_init, hps=hps)
    return pl.pallas_call(
        body, grid=(nseq_blocks, GLA_HEADS // hps, ntiles), in_specs=in_specs,
        out_specs=[pl.BlockSpec((rows, wv), lambda s, h, t: (s * ntiles + t, h)),
                   pl.BlockSpec((nbb, hps, GLA_HK, GLA_HV), lambda s, h, t: (s, h, 0, 0))],
        out_shape=[jax.ShapeDtypeStruct((nseq_blocks * ntiles * rows, GLA_HEADS * GLA_HV), BF16),
                   jax.ShapeDtypeStruct((batch, GLA_HEADS, GLA_HK, GLA_HV), F32)],
        compiler_params=_cparams(("arbitrary", "arbitrary", "arbitrary")),
        name="gla_mixer",
    )(*args)


def _mlstm_body(*refs, nbb, cl, rp, has_init):
    if has_init:
        (q_ref, k_ref, v_ref, og_ref, gt_ref, gb_ref, nw_ref, c0_ref, n0_ref, m0_ref,
         y_ref, c_ref, n_ref, m_ref) = refs
    else:
        (q_ref, k_ref, v_ref, og_ref, gt_ref, gb_ref, nw_ref, y_ref, c_ref, n_ref, m_ref) = refs
    rows = nbb * cl
    sh = int(math.log2(cl))
    head = pl.program_id(1)

    @pl.when(pl.program_id(2) == 0)
    def _():
        if has_init:
            c_ref[...] = c0_ref[...]
            n_ref[...] = n0_ref[...]
            m_ref[...] = m0_ref[...]
        else:
            c_ref[...] = jnp.zeros_like(c_ref)
            n_ref[...] = jnp.zeros_like(n_ref)
            m_ref[...] = jnp.full(m_ref.shape, -jnp.inf, F32)

    def pad(x):
        if rp == rows:
            return x
        return jnp.concatenate([x, jnp.zeros((rp - rows, x.shape[1]), x.dtype)], axis=0)

    ksc = ML_DH ** -0.5
    q = pad(q_ref[...])
    k = pad(k_ref[...])
    v = pad(v_ref[...])
    og = pad(og_ref[...])
    gates = pad(gt_ref[...] + gb_ref[...])
    lane = lax.broadcasted_iota(jnp.int32, (rp, LANES), 1)
    ig = jnp.sum(jnp.where(lane == head, gates, 0.0), axis=1, keepdims=True)
    fg = jnp.sum(jnp.where(lane == head + ML_HEADS, gates, 0.0), axis=1, keepdims=True)
    lf = _log_sigmoid(fg)

    ri = lax.broadcasted_iota(jnp.int32, (rp, rp), 0)
    ci = lax.broadcasted_iota(jnp.int32, (rp, rp), 1)
    causal = jnp.logical_and((ri >> sh) == (ci >> sh), ci <= ri)
    lmat = jnp.where(causal, 1.0, 0.0).astype(BF16)
    fcum = _dot_exact_lhs(lmat, jnp.where(lane == 0, lf, 0.0))[:, 0:1]
    cols = jnp.where(lane == 0, fcum, jnp.where(lane == 1, ig, 0.0))
    er = lax.broadcasted_iota(jnp.int32, (SUBLANES, LANES), 0)
    ec = lax.broadcasted_iota(jnp.int32, (SUBLANES, LANES), 1)
    pick = jnp.where(er == ec, 1.0, 0.0).astype(BF16)
    as_rows = _dot_nt_exact_lhs(pick, cols)
    fcum_row = as_rows[0:1, :]
    ig_row = as_rows[1:2, :]

    rbatch = lax.broadcasted_iota(jnp.int32, (rp, 1), 0) >> sh
    cbatch = lax.broadcasted_iota(jnp.int32, (1, rp), 1) >> sh
    mm = jnp.zeros((rp, 1), F32)
    for b in range(nbb):
        mb = m_ref[b, 0][:, 0:1]
        mm = jnp.where(rbatch == b, mb, mm) if nbb > 1 else jnp.broadcast_to(mb, (rp, 1))

    dmat = jnp.where(causal, fcum - fcum_row + ig_row, -jnp.inf)
    dprev = fcum + mm
    m = jnp.maximum(jnp.max(dmat, axis=1, keepdims=True), dprev)
    w = jnp.exp(dmat - m + math.log(ksc))
    wp = jnp.exp(dprev - m)
    qb = q.astype(BF16)
    kb = k.astype(BF16)
    vb = v.astype(BF16)
    sc = _dot_nt(qb, kb) * w
    num = _dot(sc.astype(BF16), vb)
    den = jnp.sum(sc, axis=1, keepdims=True)

    k_t = jnp.transpose(k)
    for b in range(nbb):
        cm = c_ref[b, 0]
        nm = n_ref[b, 0]
        mb = m_ref[b, 0][:, 0:1]
        qc = wp * _dot(qb, cm.astype(BF16))
        nm8 = jnp.broadcast_to(nm, (SUBLANES, ML_DH)).astype(BF16)
        qn = wp * _dot_nt(qb, nm8)[:, 0:1]
        if nbb > 1:
            rsel = rbatch == b
            qc = jnp.where(rsel, qc, 0.0)
            qn = jnp.where(rsel, qn, 0.0)
        num = num + qc
        den = den + qn
        last = b * cl + cl - 1
        m_new = m[last:last + 1, :]
        fl = fcum[last:last + 1, :]
        decay = jnp.exp(fl + mb - m_new)
        wk_row = ksc * jnp.exp(fl - fcum_row + ig_row - m_new)
        if nbb > 1:
            wk_row = jnp.where(cbatch == b, wk_row, 0.0)
        c_ref[b, 0] = decay * cm + _dot((k_t * wk_row).astype(BF16), vb)
        wk8 = jnp.broadcast_to(wk_row, (SUBLANES, rp)).astype(BF16)
        n_ref[b, 0] = decay * nm + _dot(wk8, kb)[0:1, :]
        m_ref[b, 0] = jnp.broadcast_to(m_new, (1, LANES))

    hh = num / jnp.maximum(jnp.abs(den), jnp.exp(-m))
    hh = hh * _sigmoid(og)
    hh = hh * lax.rsqrt(jnp.mean(hh * hh, axis=-1, keepdims=True) + EPS) * nw_ref[...]
    y_ref[...] = hh[:rows].astype(y_ref.dtype)


def _mlstm_call(projm, projt, gate_b, nw, init, *, row0, nseq_blocks, ntiles, nbb, cl, rp, batch):
    rows = nbb * cl
    has_init = init is not None
    rb0 = row0 // rows
    rowblk = lambda s, h, t: rb0 + s * ntiles + t
    hd = ML_HEADS
    in_specs = [
        pl.BlockSpec((rows, ML_DH), lambda s, h, t: (rowblk(s, h, t), h)),
        pl.BlockSpec((rows, ML_DH), lambda s, h, t: (rowblk(s, h, t), hd + h)),
        pl.BlockSpec((rows, ML_DH), lambda s, h, t: (rowblk(s, h, t), 2 * hd + h)),
        pl.BlockSpec((rows, ML_DH), lambda s, h, t: (rowblk(s, h, t), 3 * hd + h)),
        pl.BlockSpec((rows, LANES), lambda s, h, t: (rowblk(s, h, t), 0)),
        pl.BlockSpec((1, LANES), lambda s, h, t: (0, 0)),
        pl.BlockSpec((1, ML_DH), lambda s, h, t: (0, 0)),
    ]
    args = [projm, projm, projm, projm, projt, gate_b, nw]
    c_spec = pl.BlockSpec((nbb, 1, ML_DH, ML_DH), lambda s, h, t: (s, h, 0, 0))
    n_spec = pl.BlockSpec((nbb, 1, 1, ML_DH), lambda s, h, t: (s, h, 0, 0))
    m_spec = pl.BlockSpec((nbb, 1, 1, LANES), lambda s, h, t: (s, h, 0, 0))
    if has_init:
        in_specs += [c_spec, n_spec, m_spec]
        args += list(init)
    body = functools.partial(_mlstm_body, nbb=nbb, cl=cl, rp=rp, has_init=has_init)
    return pl.pallas_call(
        body, grid=(nseq_blocks, ML_HEADS, ntiles), in_specs=in_specs,
        out_specs=[pl.BlockSpec((rows, ML_DH), lambda s, h, t: (s * ntiles + t, h)),
                   c_spec, n_spec, m_spec],
        out_shape=[jax.ShapeDtypeStruct((nseq_blocks * ntiles * rows, ML_HEADS * ML_DH), BF16),
                   jax.ShapeDtypeStruct((batch, ML_HEADS, ML_DH, ML_DH), F32),
                   jax.ShapeDtypeStruct((batch, ML_HEADS, 1, ML_DH), F32),
                   jax.ShapeDtypeStruct((batch, ML_HEADS, 1, LANES), F32)],
        compiler_params=_cparams(("arbitrary", "arbitrary", "arbitrary")),
        name="mlstm_mixer",
    )(*args)


def kernel(x_prompt, x_sample, state_s5_re, state_s5_im, state_gla, state_mlstm_c, state_mlstm_n,
           state_mlstm_m, norm_mix, norm_mlp, norm_final, w_in_even, s5_a_re, s5_a_im, s5_log_step,
           s5_b_re, s5_b_im, s5_c_re, s5_c_im, s5_d, s5_glu_w, s5_glu_b, gla_gk_up, gla_gk_b,
           gla_norm, w_out_even, w_in_odd, mlstm_b_i, mlstm_b_f, mlstm_norm, w_out_odd,
           w_mlp_up, w_mlp_down):
    bp, lp, d = x_prompt.shape
    bs, ls, _ = x_sample.shape
    mp = bp * lp
    msamp = bs * ls
    xp2 = x_prompt.reshape(mp, d)
    xs2 = x_sample.reshape(msamp, d)

    n_even = S5_WIDTH + 2 * GLA_HEADS * GLA_HK + 2 * GLA_HEADS * GLA_HV
    tail_e = jnp.pad(w_in_even[0][:, n_even:], ((0, 0), (0, LANES - (w_in_even.shape[2] - n_even))))
    projm, projt = _norm_mm((xp2, xs2), norm_mix[0], w_in_even, 0, n_even, tail=tail_e, tm=1024)

    abar_re, abar_im, bbr, bbi = _s5_discretise(s5_a_re[0], s5_a_im[0], s5_log_step[0],
                                                s5_b_re[0], s5_b_im[0])
    bre = _block_diag(bbr, S5_NBLK).astype(BF16)
    bim = _block_diag(bbi, S5_NBLK).astype(BF16)
    cre = _block_diag(jnp.transpose(s5_c_re[0], (0, 2, 1)), S5_NBLK).astype(BF16)
    cim = _block_diag(jnp.transpose(s5_c_im[0], (0, 2, 1)), S5_NBLK).astype(BF16)
    s5_consts = (bre, bim, cre, cim, abar_re, abar_im, s5_d[0].reshape(1, S5_WIDTH),
                 s5_glu_w[0].astype(BF16), s5_glu_b[0].reshape(1, S5_WIDTH))

    s5_tl = 64
    ys5_p, hr_p, hi_p = _s5_call(projm, *s5_consts, None, row0=0, nsrc=bp, rs=s5_tl, seq_len=lp,
                                 nb=bp, tl=s5_tl, permute=True)
    ys5_p = ys5_p.reshape(mp, S5_WIDTH)
    hr_p = hr_p[SUBLANES - bp:]
    hi_p = hi_p[SUBLANES - bp:]
    u_s = jnp.transpose(projm[mp:mp + msamp, :S5_WIDTH].reshape(bs, ls, S5_WIDTH), (1, 0, 2))
    h0 = (state_s5_re[0].reshape(bs, S5_CH), state_s5_im[0].reshape(bs, S5_CH))
    ys5_s, hr_s, hi_s = _s5_call(u_s.reshape(msamp, S5_WIDTH), *s5_consts, h0, row0=0, nsrc=1, rs=msamp,
                                 seq_len=msamp, nb=bs, tl=ls, permute=False)
    ys5_s = jnp.transpose(ys5_s.reshape(ls, bs, S5_WIDTH), (1, 0, 2)).reshape(msamp, S5_WIDTH)

    up_pad = jnp.pad(gla_gk_up[0], ((0, LANES - gla_gk_up.shape[1]), (0, 0)))
    gkb = gla_gk_b[0].reshape(1, -1)
    gnw = gla_norm[0].reshape(1, GLA_HV)
    ptile = 256
    ygla_p, sg_p = _gla_call(projm, projt, up_pad, gkb, gnw, None, row0=0, nseq_blocks=bp,
                             ntiles=lp // ptile, nbb=1, nch=ptile // GLA_CHUNK, cs=GLA_CHUNK, batch=bp,
                             hps=GLA_HEADS)
    gnbb = 32
    ygla_s, sg_s = _gla_call(projm, projt, up_pad, gkb, gnw, state_gla[0], row0=mp,
                             nseq_blocks=bs // gnbb, ntiles=1, nbb=gnbb, nch=1, cs=ls, batch=bs, hps=1)

    h = _mm_out([ys5_p, ygla_p], [ys5_s, ygla_s], w_out_even, 0, (xp2, 0), (xs2, 0))
    hid = _norm_mm(h, norm_mlp[0], w_mlp_up, 0, D_FF, act="relu2", out_dtype=BF16)
    h = _mm_res(hid, w_mlp_down, 0, h)

    n_odd = 4 * ML_HEADS * ML_DH
    tail_o = jnp.pad(w_in_odd[0][:, n_odd:], ((0, 0), (0, LANES - (w_in_odd.shape[2] - n_odd))))
    projm, projt = _norm_mm(h, norm_mix[1], w_in_odd, 0, n_odd, tail=tail_o)
    gate_b = jnp.pad(jnp.concatenate([mlstm_b_i[0], mlstm_b_f[0]]), (0, LANES - 2 * ML_HEADS)).reshape(1, LANES)
    mnw = mlstm_norm[0].reshape(1, ML_DH)
    yml_p, c_p, n_p, m_p = _mlstm_call(projm, projt, gate_b, mnw, None, row0=0, nseq_blocks=bp,
                                       ntiles=lp // ML_CHUNK, nbb=1, cl=ML_CHUNK, rp=ML_CHUNK, batch=bp)
    mnbb = 8
    init = (state_mlstm_c[0], state_mlstm_n[0].reshape(bs, ML_HEADS, 1, ML_DH),
            jnp.broadcast_to(state_mlstm_m[0][:, :, None, None], (bs, ML_HEADS, 1, LANES)))
    yml_s, c_s, n_s, m_s = _mlstm_call(projm, projt, gate_b, mnw, init, row0=mp, nseq_blocks=bs // mnbb,
                                       ntiles=1, nbb=mnbb, cl=ls, rp=LANES, batch=bs)
    h = _mm_out([yml_p], [yml_s], w_out_odd, 0, (h, 0), (h, mp // 512))
    hid = _norm_mm(h, norm_mlp[1], w_mlp_up, 1, D_FF, act="relu2", out_dtype=BF16)
    h = _mm_res(hid, w_mlp_down, 1, h)

    y_p, y_s = _rmsnorm(h, norm_final, mp)
    g, p = S5_GROUPS, S5_STATE
    return (y_p.reshape(bp, lp, d), y_s.reshape(bs, ls, d),
            hr_p.reshape(1, bp, g, p), hi_p.reshape(1, bp, g, p), sg_p[None],
            c_p[None], n_p.reshape(1, bp, ML_HEADS, ML_DH), m_p[:, :, 0, 0][None],
            hr_s.reshape(1, bs, g, p), hi_s.reshape(1, bs, g, p), sg_s[None],
            c_s[None], n_s.reshape(1, bs, ML_HEADS, ML_DH), m_s[:, :, 0, 0][None])
```

```python
import functools
import math

import jax
import jax.numpy as jnp
import numpy as np
from jax import lax
from jax.experimental import pallas as pl
from jax.experimental.pallas import tpu as pltpu

F32 = jnp.float32
BF16 = jnp.bfloat16
EPS = 1e-6

D_MODEL = 2048
D_FF = 4 * D_MODEL
S5_WIDTH = 1024
S5_GROUPS = 64
S5_GROUP = 16
S5_STATE = 64
S5_CH = S5_GROUPS * S5_STATE
S5_NBLK = 4
GLA_HEADS = 4
GLA_HK = 128
GLA_HV = 256
GLA_TAU = 16.0
GLA_CHUNK = 64
ML_HEADS = 4
ML_DH = 512
ML_CHUNK = 256

SUBLANES = 8
LANES = 128
VMEM_LIMIT = 56 * 1024 * 1024


def _cparams(sem):
    return pltpu.CompilerParams(dimension_semantics=sem, vmem_limit_bytes=VMEM_LIMIT)


def _dot(a, b):
    return jnp.dot(a, b, preferred_element_type=F32)


def _dot_nt(a, b):
    return lax.dot_general(a, b, (((1,), (1,)), ((), ())), preferred_element_type=F32)


def _split3(x):
    p1 = x.astype(BF16)
    r1 = x - p1.astype(F32)
    p2 = r1.astype(BF16)
    r2 = r1 - p2.astype(F32)
    p3 = r2.astype(BF16)
    return p1, p2, p3


def _dot_exact_lhs(m, x):
    p1, p2, p3 = _split3(x)
    return _dot(m, p1) + _dot(m, p2) + _dot(m, p3)


def _dot_nt_exact_lhs(m, x):
    p1, p2, p3 = _split3(x)
    return _dot_nt(m, p1) + _dot_nt(m, p2) + _dot_nt(m, p3)


def _log_sigmoid(x):
    return jnp.minimum(x, 0.0) - jnp.log1p(jnp.exp(-jnp.abs(x)))


def _sigmoid(x):
    return 1.0 / (1.0 + jnp.exp(-x))


def _norm_mm_body(*refs, act, n_tail, n_ptiles, tm, rchunk):
    dual = n_ptiles is not None
    has_tail = n_tail > 0
    x_ref = refs[0]
    xs_ref = refs[1] if dual else None
    rest = refs[2:] if dual else refs[1:]
    if has_tail:
        g_ref, w_ref, wt_ref, o_ref, ot_ref, xn_ref = rest
    else:
        g_ref, w_ref, o_ref, xn_ref = rest

    def normalise(src_ref, nrows):
        g = g_ref[...]
        for r in range(0, nrows, rchunk):
            x = src_ref[r:r + rchunk, :]
            ms = jnp.mean(x * x, axis=-1, keepdims=True)
            xn_ref[r:r + rchunk, :] = (x * lax.rsqrt(ms + EPS) * g).astype(BF16)

    @pl.when(pl.program_id(1) == 0)
    def _():
        if dual:
            @pl.when(pl.program_id(0) < n_ptiles)
            def _():
                normalise(x_ref, tm)

            @pl.when(pl.program_id(0) >= n_ptiles)
            def _():
                ns = xs_ref.shape[0]
                normalise(xs_ref, ns)
                xn_ref[ns:, :] = jnp.zeros((tm - ns, xn_ref.shape[1]), BF16)
        else:
            normalise(x_ref, tm)
        if has_tail:
            col = lax.broadcasted_iota(jnp.int32, wt_ref.shape, 0)
            wt = jnp.where(col < n_tail, wt_ref[...], 0.0)
            ot_ref[...] = _dot_nt(xn_ref[...], wt.astype(BF16))

    mm = _dot_nt if has_tail else _dot
    acc = mm(xn_ref[...], w_ref[...].astype(BF16))
    if act == "relu2":
        acc = jnp.square(jnp.maximum(acc, 0.0))
    o_ref[...] = acc.astype(o_ref.dtype)


def _norm_mm(x, g, w3, layer, n_main, *, act=None, out_dtype=F32, tm=1088, tn=512):
    dual = isinstance(x, tuple)
    if dual:
        xp, xs = x
        n_ptiles = xp.shape[0] // tm
        m, kdim = (n_ptiles + 1) * tm, xp.shape[1]
        grid = (n_ptiles + 1, n_main // tn)
        in_specs = [
            pl.BlockSpec((tm, kdim), lambda i, j: (jnp.minimum(i, n_ptiles - 1), 0)),
            pl.BlockSpec(xs.shape, lambda i, j: (0, 0)),
        ]
        args = [xp, xs]
    else:
        n_ptiles = None
        m, kdim = x.shape
        grid = (m // tm, n_main // tn)
        in_specs = [pl.BlockSpec((tm, kdim), lambda i, j: (i, 0))]
        args = [x]
    n_tail = w3.shape[2] - n_main
    in_specs.append(pl.BlockSpec((1, kdim), lambda i, j: (0, 0)))
    args.append(g.reshape(1, kdim))
    out_shape = [jax.ShapeDtypeStruct((m, n_main), out_dtype)]
    out_specs = [pl.BlockSpec((tm, tn), lambda i, j: (i, j))]
    if n_tail > 0:
        w3t = jnp.transpose(w3, (0, 2, 1))
        in_specs += [pl.BlockSpec((None, tn, kdim), lambda i, j: (layer, j, 0)),
                     pl.BlockSpec((None, LANES, kdim), lambda i, j: (layer, n_main // LANES, 0))]
        args += [w3t, w3t]
        out_shape.append(jax.ShapeDtypeStruct((m, LANES), F32))
        out_specs.append(pl.BlockSpec((tm, LANES), lambda i, j: (i, 0)))
    else:
        in_specs.append(pl.BlockSpec((None, kdim, tn), lambda i, j: (layer, 0, j)))
        args.append(w3)
    body = functools.partial(_norm_mm_body, act=act, n_tail=n_tail, n_ptiles=n_ptiles,
                             tm=tm, rchunk=tm // 4)
    res = pl.pallas_call(
        body, grid=grid, in_specs=in_specs, out_specs=out_specs, out_shape=out_shape,
        scratch_shapes=[pltpu.VMEM((tm, kdim), BF16)],
        compiler_params=_cparams(("arbitrary", "arbitrary")),
        name="norm_mm",
    )(*args)
    return res if n_tail > 0 else res[0]


def _mm_res_body(a_ref, w_ref, r_ref, o_ref):
    @pl.when(pl.program_id(2) == 0)
    def _():
        o_ref[...] = r_ref[...] + _dot(a_ref[...].astype(BF16), w_ref[...].astype(BF16))

    @pl.when(pl.program_id(2) > 0)
    def _():
        o_ref[...] = o_ref[...] + _dot(a_ref[...].astype(BF16), w_ref[...].astype(BF16))


def _mm_res(a, w3, layer, res, *, tm=1088, tn=1024, tk=1024):
    m, kdim = a.shape
    n = w3.shape[2]
    grid = (m // tm, n // tn, kdim // tk)
    return pl.pallas_call(
        _mm_res_body, grid=grid,
        in_specs=[
            pl.BlockSpec((tm, tk), lambda i, j, k: (i, k)),
            pl.BlockSpec((None, tk, tn), lambda i, j, k: (layer, k, j)),
            pl.BlockSpec((tm, tn), lambda i, j, k: (i, j)),
        ],
        out_specs=pl.BlockSpec((tm, tn), lambda i, j, k: (i, j)),
        out_shape=jax.ShapeDtypeStruct((m, n), F32),
        compiler_params=_cparams(("arbitrary", "arbitrary", "arbitrary")),
        name="mm_res",
    )(a, w3, res)


def _mm_out_body(*refs, kcs, n_ptiles):
    n = len(kcs)
    ap, asm = refs[:n], refs[n:2 * n]
    w_ref, rp_ref, rs_ref, o_ref, wb_ref = refs[2 * n:]
    i = pl.program_id(1)

    @pl.when(i == 0)
    def _():
        wb_ref[...] = w_ref[...].astype(BF16)

    def compute(srcs, r_ref):
        acc = r_ref[...]
        off = 0
        for a_ref, kc in zip(srcs, kcs):
            acc = acc + _dot(a_ref[...], wb_ref[off:off + kc, :])
            off += kc
        o_ref[...] = acc

    @pl.when(i < n_ptiles)
    def _():
        compute(ap, rp_ref)

    @pl.when(i >= n_ptiles)
    def _():
        compute(asm, rs_ref)


def _mm_out(a_prompt, a_sample, w3, layer, res_prompt, res_sample, *, tm=512, tn=1024):
    kcs = tuple(a.shape[1] for a in a_prompt)
    kdim, n = sum(kcs), w3.shape[2]
    n_ptiles = a_prompt[0].shape[0] // tm
    m = a_prompt[0].shape[0] + a_sample[0].shape[0]
    (rp, rp0), (rs, rs0) = res_prompt, res_sample
    pidx = lambda i: jnp.minimum(i, n_ptiles - 1)
    in_specs = ([pl.BlockSpec((tm, kc), lambda j, i: (pidx(i), 0)) for kc in kcs]
                + [pl.BlockSpec((tm, kc), lambda j, i: (0, 0)) for kc in kcs]
                + [pl.BlockSpec((None, kdim, tn), lambda j, i: (layer, 0, j)),
                   pl.BlockSpec((tm, tn), lambda j, i: (rp0 + pidx(i), j)),
                   pl.BlockSpec((tm, tn), lambda j, i: (rs0, j))])
    body = functools.partial(_mm_out_body, kcs=kcs, n_ptiles=n_ptiles)
    return pl.pallas_call(
        body, grid=(n // tn, n_ptiles + 1), in_specs=in_specs,
        out_specs=pl.BlockSpec((tm, tn), lambda j, i: (i, j)),
        out_shape=jax.ShapeDtypeStruct((m, n), F32),
        scratch_shapes=[pltpu.VMEM((kdim, tn), BF16)],
        compiler_params=_cparams(("arbitrary", "arbitrary")),
        name="mm_out",
    )(*a_prompt, *a_sample, w3, rp, rs)


def _rmsnorm_body(x_ref, g_ref, op_ref, os_ref, *, n_ptiles):
    x = x_ref[...]
    ms = jnp.mean(x * x, axis=-1, keepdims=True)
    y = x * lax.rsqrt(ms + EPS) * g_ref[...]

    @pl.when(pl.program_id(0) < n_ptiles)
    def _():
        op_ref[...] = y

    @pl.when(pl.program_id(0) >= n_ptiles)
    def _():
        os_ref[...] = y


def _rmsnorm(x, g, m_prompt, *, tm=512):
    m, d = x.shape
    n_ptiles = m_prompt // tm
    return pl.pallas_call(
        functools.partial(_rmsnorm_body, n_ptiles=n_ptiles), grid=(n_ptiles + 1,),
        in_specs=[pl.BlockSpec((tm, d), lambda i: (i, 0)), pl.BlockSpec((1, d), lambda i: (0, 0))],
        out_specs=[pl.BlockSpec((tm, d), lambda i: (jnp.minimum(i, n_ptiles - 1), 0)),
                   pl.BlockSpec((tm, d), lambda i: (0, 0))],
        out_shape=[jax.ShapeDtypeStruct((m_prompt, d), F32),
                   jax.ShapeDtypeStruct((m - m_prompt, d), F32)],
        compiler_params=_cparams(("arbitrary",)),
        name="final_norm",
    )(x, g.reshape(1, d))


def _s5_disc_body(ar_ref, ai_ref, ls_ref, br_ref, bi_ref, abr_ref, abi_ref, bbr_ref, bbi_ref):
    ar = ar_ref[...]
    ai = ai_ref[...]
    dt = jnp.exp(ls_ref[...])
    mag = jnp.exp(ar * dt)
    abr = mag * jnp.cos(ai * dt)
    abi = mag * jnp.sin(ai * dt)
    lam2 = ar * ar + ai * ai
    zr = abr - 1.0
    cr = (zr * ar + abi * ai) / lam2
    ci = (abi * ar - zr * ai) / lam2
    br = br_ref[...]
    bi = bi_ref[...]
    abr_ref[...] = abr
    abi_ref[...] = abi
    bbr_ref[...] = cr * br - ci * bi
    bbi_ref[...] = cr * bi + ci * br


def _s5_discretise(a_re, a_im, log_step, b_re, b_im):
    g, p, q = S5_GROUPS, S5_STATE, S5_GROUP
    rep = lambda t: jnp.repeat(t, q, axis=0)
    ls = jnp.broadcast_to(log_step[:, None], (g, p))
    brt = jnp.transpose(b_re, (0, 2, 1)).reshape(g * q, p)
    bit = jnp.transpose(b_im, (0, 2, 1)).reshape(g * q, p)
    shp = jax.ShapeDtypeStruct((g * q, p), F32)
    spec = pl.BlockSpec((g * q, p), lambda: (0, 0))
    abr, abi, bbr, bbi = pl.pallas_call(
        _s5_disc_body, in_specs=[spec] * 5, out_specs=[spec] * 4, out_shape=[shp] * 4,
        name="s5_disc",
    )(rep(a_re), rep(a_im), rep(ls), brt, bit)
    abar_re = abr[::q].reshape(1, S5_CH)
    abar_im = abi[::q].reshape(1, S5_CH)
    return abar_re, abar_im, bbr.reshape(g, q, p), bbi.reshape(g, q, p)


def _block_diag(t, nblk):
    g, a, b = t.shape
    gb = g // nblk
    t = t.reshape(nblk, gb, a, b)
    eye = jnp.eye(gb, dtype=t.dtype)
    out = t[:, :, :, None, :] * eye[None, :, None, :, None]
    return out.reshape(nblk, gb * a, gb * b)


def _s5_perm(nb, tl):
    r = np.arange(nb * tl)
    p = np.zeros((nb * tl, nb * tl), np.float32)
    p[r, (r % nb) * tl + r // nb] = 1.0
    return p


def _s5_body(*refs, nsrc, nb, has_init, permute):
    u_refs, refs = refs[:nsrc], refs[nsrc:]
    if permute:
        pm_ref, pmt_ref = refs[:2]
        refs = refs[2:]
    bre_ref, bim_ref, cre_ref, cim_ref, ar_ref, ai_ref, d_ref, gw_ref, gb_ref = refs[:9]
    refs = refs[9:]
    if has_init:
        h0r_ref, h0i_ref, y_ref, hr_ref, hi_ref, sre, sim, ysc = refs
    else:
        y_ref, hr_ref, hi_ref, sre, sim, ysc = refs
    rows = sre.shape[0]
    tl = rows // nb
    cb = S5_CH // S5_NBLK
    ub = S5_WIDTH // S5_NBLK
    hw = cb // 2

    @pl.when(pl.program_id(0) == 0)
    def _():
        if has_init:
            hr_ref[...] = h0r_ref[...]
            hi_ref[...] = h0i_ref[...]
        else:
            hr_ref[...] = jnp.zeros_like(hr_ref)
            hi_ref[...] = jnp.zeros_like(hi_ref)

    u = u_refs[0][...] if nsrc == 1 else jnp.concatenate([r[...] for r in u_refs], axis=0)
    if permute:
        moved = _dot(pm_ref[...], jnp.concatenate(_split3(u), axis=1))
        t1 = moved[:, :S5_WIDTH]
        u_tm = t1 + moved[:, S5_WIDTH:2 * S5_WIDTH] + moved[:, 2 * S5_WIDTH:]
        u_b = t1.astype(BF16)
    else:
        u_tm = u
        u_b = u.astype(BF16)
    low_rows = lax.broadcasted_iota(jnp.int32, (SUBLANES, hw), 0) < nb

    for k in range(S5_NBLK):
        uk = u_b[:, k * ub:(k + 1) * ub]
        sre[...] = _dot(uk, bre_ref[k])
        sim[...] = _dot(uk, bim_ref[k])
        for half in range(2):
            c0 = k * cb + half * hw
            l0 = half * hw
            ar8 = jnp.broadcast_to(ar_ref[:, c0:c0 + hw], (SUBLANES, hw))
            ai8 = jnp.broadcast_to(ai_ref[:, c0:c0 + hw], (SUBLANES, hw))

            if nb < SUBLANES:
                def j_body(j, c, l0=l0, ar8=ar8, ai8=ai8):
                    sr, si = c
                    row = pl.multiple_of(j * SUBLANES, SUBLANES)
                    dr = sre[pl.ds(row, SUBLANES), l0:l0 + hw]
                    di = sim[pl.ds(row, SUBLANES), l0:l0 + hw]
                    pr = pltpu.roll(sr, nb, 0)
                    pi = pltpu.roll(si, nb, 0)
                    yr = ar8 * pr - ai8 * pi + dr
                    yi = ar8 * pi + ai8 * pr + di
                    qr = pltpu.roll(yr, nb, 0)
                    qi = pltpu.roll(yi, nb, 0)
                    zr = ar8 * qr - ai8 * qi + dr
                    zi = ar8 * qi + ai8 * qr + di
                    sre[pl.ds(row, SUBLANES), l0:l0 + hw] = jnp.where(low_rows, yr, zr)
                    sim[pl.ds(row, SUBLANES), l0:l0 + hw] = jnp.where(low_rows, yi, zi)
                    return zr, zi

                sr, si = lax.fori_loop(0, rows // SUBLANES, j_body,
                                       (hr_ref[:, c0:c0 + hw], hi_ref[:, c0:c0 + hw]), unroll=True)
                hr_ref[:, c0:c0 + hw] = sr
                hi_ref[:, c0:c0 + hw] = si
            else:
                def rg_body(rg, carry, c0=c0, l0=l0, ar8=ar8, ai8=ai8):
                    r0 = pl.multiple_of(rg * SUBLANES, SUBLANES)
                    xr = hr_ref[pl.ds(r0, SUBLANES), c0:c0 + hw]
                    xi = hi_ref[pl.ds(r0, SUBLANES), c0:c0 + hw]
                    for t in range(tl):
                        row = pl.multiple_of(t * nb + r0, SUBLANES)
                        nr = ar8 * xr - ai8 * xi + sre[pl.ds(row, SUBLANES), l0:l0 + hw]
                        ni = ar8 * xi + ai8 * xr + sim[pl.ds(row, SUBLANES), l0:l0 + hw]
                        sre[pl.ds(row, SUBLANES), l0:l0 + hw] = nr
                        sim[pl.ds(row, SUBLANES), l0:l0 + hw] = ni
                        xr, xi = nr, ni
                    hr_ref[pl.ds(r0, SUBLANES), c0:c0 + hw] = xr
                    hi_ref[pl.ds(r0, SUBLANES), c0:c0 + hw] = xi
                    return carry

                lax.fori_loop(0, nb // SUBLANES, rg_body, 0)
        ysc[:, k * ub:(k + 1) * ub] = (_dot(sre[...].astype(BF16), cre_ref[k])
                                       - _dot(sim[...].astype(BF16), cim_ref[k]))

    y = ysc[...] + d_ref[...] * u_tm
    y = 0.5 * y * (1.0 + jnp.tanh(math.sqrt(2.0 / math.pi) * (y + 0.044715 * (y * y * y))))
    z = _dot(y.astype(BF16), gw_ref[...]) + gb_ref[...]
    out = (y * _sigmoid(z)).astype(BF16)
    if permute:
        out = _dot(pmt_ref[...], out).astype(BF16)
    y_ref[...] = out.reshape(y_ref.shape)


def _s5_call(src, bre, bim, cre, cim, abar_re, abar_im, d_skip, glu_w, glu_b, h0, *,
             row0, nsrc, rs, seq_len, nb, tl, permute):
    rows = nsrc * rs
    ntiles = seq_len // rs if nsrc > 1 else 1
    has_init = h0 is not None
    srows = max(nb, SUBLANES)
    const = lambda shape: pl.BlockSpec(shape, lambda t: (0,) * len(shape))
    in_specs = [pl.BlockSpec((rs, S5_WIDTH), lambda t, s=s: ((row0 + s * seq_len) // rs + t, 0))
                for s in range(nsrc)]
    args = [src] * nsrc
    if permute:
        pm = _s5_perm(nb, tl)
        in_specs += [const((rows, rows)), const((rows, rows))]
        args += [jnp.asarray(pm, BF16), jnp.asarray(pm.T, BF16)]
    in_specs += [
        const(bre.shape), const(bim.shape), const(cre.shape), const(cim.shape),
        const((1, S5_CH)), const((1, S5_CH)), const((1, S5_WIDTH)),
        const((S5_WIDTH, S5_WIDTH)), const((1, S5_WIDTH)),
    ]
    args += [bre, bim, cre, cim, abar_re, abar_im, d_skip, glu_w, glu_b]
    if has_init:
        in_specs += [const((srows, S5_CH)), const((srows, S5_CH))]
        args += [h0[0], h0[1]]
    body = functools.partial(_s5_body, nsrc=nsrc, nb=nb, has_init=has_init, permute=permute)
    return pl.pallas_call(
        body, grid=(ntiles,), in_specs=in_specs,
        out_specs=[pl.BlockSpec((nsrc, rs, S5_WIDTH), lambda t: (0, t, 0)),
                   const((srows, S5_CH)), const((srows, S5_CH))],
        out_shape=[jax.ShapeDtypeStruct((nsrc, ntiles * rs, S5_WIDTH), BF16),
                   jax.ShapeDtypeStruct((srows, S5_CH), F32),
                   jax.ShapeDtypeStruct((srows, S5_CH), F32)],
        scratch_shapes=[pltpu.VMEM((rows, S5_CH // S5_NBLK), F32),
                        pltpu.VMEM((rows, S5_CH // S5_NBLK), F32),
                        pltpu.VMEM((rows, S5_WIDTH), F32)],
        compiler_params=_cparams(("arbitrary",)),
        name="s5_mixer",
    )(*args)


def _gla_body(*refs, nbb, nch, cs, has_init, hps):
    if has_init:
        (q_ref, k_ref, v_ref, g_ref, gl_ref, up_ref, gb_ref, nw_ref, s0_ref, y_ref, s_ref) = refs
    else:
        (q_ref, k_ref, v_ref, g_ref, gl_ref, up_ref, gb_ref, nw_ref, y_ref, s_ref) = refs
    rows = nbb * nch * cs
    sh = int(math.log2(cs))

    @pl.when(pl.program_id(2) == 0)
    def _():
        if has_init:
            s_ref[...] = s0_ref[...]
        else:
            s_ref[...] = jnp.zeros_like(s_ref)

    ri = lax.broadcasted_iota(jnp.int32, (rows, rows), 0)
    ci = lax.broadcasted_iota(jnp.int32, (rows, rows), 1)
    same = (ri >> sh) == (ci >> sh)
    causal = jnp.logical_and(same, ci <= ri)
    lmat = jnp.where(causal, 1.0, 0.0).astype(BF16)
    tmat = jnp.where(same, 1.0, 0.0).astype(BF16)
    rchunk = lax.broadcasted_iota(jnp.int32, (rows, 1), 0) >> sh
    cchunk = lax.broadcasted_iota(jnp.int32, (1, rows), 1) >> sh
    single = nbb * nch == 1
    z = _dot(gl_ref[...].astype(BF16), up_ref[...].astype(BF16)) + gb_ref[...]
    la = _log_sigmoid(z) * (1.0 / GLA_TAU)
    p1, p2, p3 = _split3(la)
    bcum_all = _dot(lmat, p1) + _dot(lmat, p2) + _dot(lmat, p3)
    tot_all = _dot(tmat, p1) + _dot(tmat, p2) + _dot(tmat, p3)

    for hh in range(hps):
        ks = slice(hh * GLA_HK, (hh + 1) * GLA_HK)
        vs = slice(hh * GLA_HV, (hh + 1) * GLA_HV)
        q = q_ref[:, ks] * (GLA_HK ** -0.5)
        k = k_ref[:, ks]
        vb = v_ref[:, vs].astype(BF16)
        bcum = bcum_all[:, ks]
        tot = tot_all[:, ks]

        qd = q * jnp.exp(bcum)
        kd = k * jnp.exp(-bcum)
        kdec = k * jnp.exp(tot - bcum)
        qdb = qd.astype(BF16)
        att = jnp.where(causal, _dot_nt(qdb, kd.astype(BF16)), 0.0)
        o = _dot(att.astype(BF16), vb)

        kdec_t = jnp.transpose(kdec)
        dec_t = jnp.transpose(jnp.exp(tot))
        for b in range(nbb):
            s = s_ref[b, hh]
            for c in range(nch):
                idx = b * nch + c
                oi = _dot(qdb, s.astype(BF16))
                o = o + (oi if single else jnp.where(rchunk == idx, oi, 0.0))
                kt = kdec_t if single else jnp.where(cchunk == idx, kdec_t, 0.0)
                s = dec_t[:, idx * cs:idx * cs + 1] * s + _dot(kt.astype(BF16), vb)
            s_ref[b, hh] = s

        o = o * lax.rsqrt(jnp.mean(o * o, axis=-1, keepdims=True) + EPS) * nw_ref[...]
        gg = g_ref[:, vs]
        y_ref[:, vs] = (o * (gg * _sigmoid(gg))).astype(y_ref.dtype)


def _gla_call(projm, projt, up_pad, gkb, nw, s0, *, row0, nseq_blocks, ntiles, nbb, nch, cs, batch, hps):
    rows = nbb * nch * cs
    has_init = s0 is not None
    rb0 = row0 // rows
    rowblk = lambda s, h, t: rb0 + s * ntiles + t
    wk, wv = hps * GLA_HK, hps * GLA_HV
    qoff = S5_WIDTH // wk
    koff = qoff + GLA_HEADS // hps
    voff = (S5_WIDTH + 2 * GLA_HEADS * GLA_HK) // wv
    goff = voff + GLA_HEADS // hps
    in_specs = [
        pl.BlockSpec((rows, wk), lambda s, h, t: (rowblk(s, h, t), qoff + h)),
        pl.BlockSpec((rows, wk), lambda s, h, t: (rowblk(s, h, t), koff + h)),
        pl.BlockSpec((rows, wv), lambda s, h, t: (rowblk(s, h, t), voff + h)),
        pl.BlockSpec((rows, wv), lambda s, h, t: (rowblk(s, h, t), goff + h)),
        pl.BlockSpec((rows, LANES), lambda s, h, t: (rowblk(s, h, t), 0)),
        pl.BlockSpec((LANES, wk), lambda s, h, t: (0, h)),
        pl.BlockSpec((1, wk), lambda s, h, t: (0, h)),
        pl.BlockSpec((1, GLA_HV), lambda s, h, t: (0, 0)),
    ]
    args = [projm, projm, projm, projm, projt, up_pad, gkb, nw]
    if has_init:
        in_specs.append(pl.BlockSpec((nbb, hps, GLA_HK, GLA_HV), lambda s, h, t: (s, h, 0, 0)))
        args.append(s0)
    body = functools.partial(_gla_body, nbb=nbb, nch=nch, cs=cs, has_init=has_init, hps=hps)
    return pl.pallas_call(
        body, grid=(nseq_blocks, GLA_HEADS // hps, ntiles), in_specs=in_specs,
        out_specs=[pl.BlockSpec((rows, wv), lambda s, h, t: (s * ntiles + t, h)),
                   pl.BlockSpec((nbb, hps, GLA_HK, GLA_HV), lambda s, h, t: (s, h, 0, 0))],
        out_shape=[jax.ShapeDtypeStruct((nseq_blocks * ntiles * rows, GLA_HEADS * GLA_HV), BF16),
                   jax.ShapeDtypeStruct((batch, GLA_HEADS, GLA_HK, GLA_HV), F32)],
        compiler_params=_cparams(("arbitrary", "arbitrary", "arbitrary")),
        name="gla_mixer",
    )(*args)


def _mlstm_body(*refs, nbb, cl, rp, has_init):
    if has_init:
        (q_ref, k_ref, v_ref, og_ref, gt_ref, gb_ref, nw_ref, c0_ref, n0_ref, m0_ref,
         y_ref, c_ref, n_ref, m_ref) = refs
    else:
        (q_ref, k_ref, v_ref, og_ref, gt_ref, gb_ref, nw_ref, y_ref, c_ref, n_ref, m_ref) = refs
    rows = nbb * cl
    sh = int(math.log2(cl))
    head = pl.program_id(1)

    @pl.when(pl.program_id(2) == 0)
    def _():
        if has_init:
            c_ref[...] = c0_ref[...]
            n_ref[...] = n0_ref[...]
            m_ref[...] = m0_ref[...]
        else:
            c_ref[...] = jnp.zeros_like(c_ref)
            n_ref[...] = jnp.zeros_like(n_ref)
            m_ref[...] = jnp.full(m_ref.shape, -jnp.inf, F32)

    def pad(x):
        if rp == rows:
            return x
        return jnp.concatenate([x, jnp.zeros((rp - rows, x.shape[1]), x.dtype)], axis=0)

    ksc = ML_DH ** -0.5
    q = pad(q_ref[...])
    k = pad(k_ref[...])
    v = pad(v_ref[...])
    og = pad(og_ref[...])
    gates = pad(gt_ref[...] + gb_ref[...])
    lane = lax.broadcasted_iota(jnp.int32, (rp, LANES), 1)
    ig = jnp.sum(jnp.where(lane == head, gates, 0.0), axis=1, keepdims=True)
    fg = jnp.sum(jnp.where(lane == head + ML_HEADS, gates, 0.0), axis=1, keepdims=True)
    lf = _log_sigmoid(fg)

    ri = lax.broadcasted_iota(jnp.int32, (rp, rp), 0)
    ci = lax.broadcasted_iota(jnp.int32, (rp, rp), 1)
    causal = jnp.logical_and((ri >> sh) == (ci >> sh), ci <= ri)
    lmat = jnp.where(causal, 1.0, 0.0).astype(BF16)
    fcum = _dot_exact_lhs(lmat, jnp.where(lane == 0, lf, 0.0))[:, 0:1]
    cols = jnp.where(lane == 0, fcum, jnp.where(lane == 1, ig, 0.0))
    er = lax.broadcasted_iota(jnp.int32, (SUBLANES, LANES), 0)
    ec = lax.broadcasted_iota(jnp.int32, (SUBLANES, LANES), 1)
    pick = jnp.where(er == ec, 1.0, 0.0).astype(BF16)
    as_rows = _dot_nt_exact_lhs(pick, cols)
    fcum_row = as_rows[0:1, :]
    ig_row = as_rows[1:2, :]

    rbatch = lax.broadcasted_iota(jnp.int32, (rp, 1), 0) >> sh
    cbatch = lax.broadcasted_iota(jnp.int32, (1, rp), 1) >> sh
    mm = jnp.zeros((rp, 1), F32)
    for b in range(nbb):
        mb = m_ref[b, 0][:, 0:1]
        mm = jnp.where(rbatch == b, mb, mm) if nbb > 1 else jnp.broadcast_to(mb, (rp, 1))

    dmat = jnp.where(causal, fcum - fcum_row + ig_row, -jnp.inf)
    dprev = fcum + mm
    m = jnp.maximum(jnp.max(dmat, axis=1, keepdims=True), dprev)
    w = jnp.exp(dmat - m + math.log(ksc))
    wp = jnp.exp(dprev - m)
    qb = q.astype(BF16)
    kb = k.astype(BF16)
    vb = v.astype(BF16)
    sc = _dot_nt(qb, kb) * w
    num = _dot(sc.astype(BF16), vb)
    den = jnp.sum(sc, axis=1, keepdims=True)

    k_t = jnp.transpose(k)
    for b in range(nbb):
        cm = c_ref[b, 0]
        nm = n_ref[b, 0]
        mb = m_ref[b, 0][:, 0:1]
        qc = wp * _dot(qb, cm.astype(BF16))
        nm8 = jnp.broadcast_to(nm, (SUBLANES, ML_DH)).astype(BF16)
        qn = wp * _dot_nt(qb, nm8)[:, 0:1]
        if nbb > 1:
            rsel = rbatch == b
            qc = jnp.where(rsel, qc, 0.0)
            qn = jnp.where(rsel, qn, 0.0)
        num = num + qc
        den = den + qn
        last = b * cl + cl - 1
        m_new = m[last:last + 1, :]
        fl = fcum[last:last + 1, :]
        decay = jnp.exp(fl + mb - m_new)
        wk_row = ksc * jnp.exp(fl - fcum_row + ig_row - m_new)
        if nbb > 1:
            wk_row = jnp.where(cbatch == b, wk_row, 0.0)
        c_ref[b, 0] = decay * cm + _dot((k_t * wk_row).astype(BF16), vb)
        wk8 = jnp.broadcast_to(wk_row, (SUBLANES, rp)).astype(BF16)
        n_ref[b, 0] = decay * nm + _dot(wk8, kb)[0:1, :]
        m_ref[b, 0] = jnp.broadcast_to(m_new, (1, LANES))

    hh = num / jnp.maximum(jnp.abs(den), jnp.exp(-m))
    hh = hh * _sigmoid(og)
    hh = hh * lax.rsqrt(jnp.mean(hh * hh, axis=-1, keepdims=True) + EPS) * nw_ref[...]
    y_ref[...] = hh[:rows].astype(y_ref.dtype)


def _mlstm_call(projm, projt, gate_b, nw, init, *, row0, nseq_blocks, ntiles, nbb, cl, rp, batch):
    rows = nbb * cl
    has_init = init is not None
    rb0 = row0 // rows
    rowblk = lambda s, h, t: rb0 + s * ntiles + t
    hd = ML_HEADS
    in_specs = [
        pl.BlockSpec((rows, ML_DH), lambda s, h, t: (rowblk(s, h, t), h)),
        pl.BlockSpec((rows, ML_DH), lambda s, h, t: (rowblk(s, h, t), hd + h)),
        pl.BlockSpec((rows, ML_DH), lambda s, h, t: (rowblk(s, h, t), 2 * hd + h)),
        pl.BlockSpec((rows, ML_DH), lambda s, h, t: (rowblk(s, h, t), 3 * hd + h)),
        pl.BlockSpec((rows, LANES), lambda s, h, t: (rowblk(s, h, t), 0)),
        pl.BlockSpec((1, LANES), lambda s, h, t: (0, 0)),
        pl.BlockSpec((1, ML_DH), lambda s, h, t: (0, 0)),
    ]
    args = [projm, projm, projm, projm, projt, gate_b, nw]
    c_spec = pl.BlockSpec((nbb, 1, ML_DH, ML_DH), lambda s, h, t: (s, h, 0, 0))
    n_spec = pl.BlockSpec((nbb, 1, 1, ML_DH), lambda s, h, t: (s, h, 0, 0))
    m_spec = pl.BlockSpec((nbb, 1, 1, LANES), lambda s, h, t: (s, h, 0, 0))
    if has_init:
        in_specs += [c_spec, n_spec, m_spec]
        args += list(init)
    body = functools.partial(_mlstm_body, nbb=nbb, cl=cl, rp=rp, has_init=has_init)
    return pl.pallas_call(
        body, grid=(nseq_blocks, ML_HEADS, ntiles), in_specs=in_specs,
        out_specs=[pl.BlockSpec((rows, ML_DH), lambda s, h, t: (s * ntiles + t, h)),
                   c_spec, n_spec, m_spec],
        out_shape=[jax.ShapeDtypeStruct((nseq_blocks * ntiles * rows, ML_HEADS * ML_DH), BF16),
                   jax.ShapeDtypeStruct((batch, ML_HEADS, ML_DH, ML_DH), F32),
                   jax.ShapeDtypeStruct((batch, ML_HEADS, 1, ML_DH), F32),
                   jax.ShapeDtypeStruct((batch, ML_HEADS, 1, LANES), F32)],
        compiler_params=_cparams(("arbitrary", "arbitrary", "arbitrary")),
        name="mlstm_mixer",
    )(*args)


def kernel(x_prompt, x_sample, state_s5_re, state_s5_im, state_gla, state_mlstm_c, state_mlstm_n,
           state_mlstm_m, norm_mix, norm_mlp, norm_final, w_in_even, s5_a_re, s5_a_im, s5_log_step,
           s5_b_re, s5_b_im, s5_c_re, s5_c_im, s5_d, s5_glu_w, s5_glu_b, gla_gk_up, gla_gk_b,
           gla_norm, w_out_even, w_in_odd, mlstm_b_i, mlstm_b_f, mlstm_norm, w_out_odd,
           w_mlp_up, w_mlp_down):
    bp, lp, d = x_prompt.shape
    bs, ls, _ = x_sample.shape
    mp = bp * lp
    msamp = bs * ls
    xp2 = x_prompt.reshape(mp, d)
    xs2 = x_sample.reshape(msamp, d)

    n_even = S5_WIDTH + 2 * GLA_HEADS * GLA_HK + 2 * GLA_HEADS * GLA_HV
    projm, projt = _norm_mm((xp2, xs2), norm_mix[0], w_in_even, 0, n_even, tm=1024)

    abar_re, abar_im, bbr, bbi = _s5_discretise(s5_a_re[0], s5_a_im[0], s5_log_step[0],
                                                s5_b_re[0], s5_b_im[0])
    bre = _block_diag(bbr, S5_NBLK).astype(BF16)
    bim = _block_diag(bbi, S5_NBLK).astype(BF16)
    cre = _block_diag(jnp.transpose(s5_c_re[0], (0, 2, 1)), S5_NBLK).astype(BF16)
    cim = _block_diag(jnp.transpose(s5_c_im[0], (0, 2, 1)), S5_NBLK).astype(BF16)
    s5_consts = (bre, bim, cre, cim, abar_re, abar_im, s5_d[0].reshape(1, S5_WIDTH),
                 s5_glu_w[0].astype(BF16), s5_glu_b[0].reshape(1, S5_WIDTH))

    s5_tl = 64
    ys5_p, hr_p, hi_p = _s5_call(projm, *s5_consts, None, row0=0, nsrc=bp, rs=s5_tl, seq_len=lp,
                                 nb=bp, tl=s5_tl, permute=True)
    ys5_p = ys5_p.reshape(mp, S5_WIDTH)
    hr_p = hr_p[SUBLANES - bp:]
    hi_p = hi_p[SUBLANES - bp:]
    u_s = jnp.transpose(projm[mp:mp + msamp, :S5_WIDTH].reshape(bs, ls, S5_WIDTH), (1, 0, 2))
    h0 = (state_s5_re[0].reshape(bs, S5_CH), state_s5_im[0].reshape(bs, S5_CH))
    ys5_s, hr_s, hi_s = _s5_call(u_s.reshape(msamp, S5_WIDTH), *s5_consts, h0, row0=0, nsrc=1, rs=msamp,
                                 seq_len=msamp, nb=bs, tl=ls, permute=False)
    ys5_s = jnp.transpose(ys5_s.reshape(ls, bs, S5_WIDTH), (1, 0, 2)).reshape(msamp, S5_WIDTH)

    up_pad = jnp.pad(gla_gk_up[0], ((0, LANES - gla_gk_up.shape[1]), (0, 0)))
    gkb = gla_gk_b[0].reshape(1, -1)
    gnw = gla_norm[0].reshape(1, GLA_HV)
    ptile = 256
    ygla_p, sg_p = _gla_call(projm, projt, up_pad, gkb, gnw, None, row0=0, nseq_blocks=bp,
                             ntiles=lp // ptile, nbb=1, nch=ptile // GLA_CHUNK, cs=GLA_CHUNK, batch=bp,
                             hps=GLA_HEADS)
    gnbb = 32
    ygla_s, sg_s = _gla_call(projm, projt, up_pad, gkb, gnw, state_gla[0], row0=mp,
                             nseq_blocks=bs // gnbb, ntiles=1, nbb=gnbb, nch=1, cs=ls, batch=bs, hps=1)

    h = _mm_out([ys5_p, ygla_p], [ys5_s, ygla_s], w_out_even, 0, (xp2, 0), (xs2, 0))
    hid = _norm_mm(h, norm_mlp[0], w_mlp_up, 0, D_FF, act="relu2", out_dtype=BF16)
    h = _mm_res(hid, w_mlp_down, 0, h)

    n_odd = 4 * ML_HEADS * ML_DH
    projm, projt = _norm_mm(h, norm_mix[1], w_in_odd, 0, n_odd)
    gate_b = jnp.pad(jnp.concatenate([mlstm_b_i[0], mlstm_b_f[0]]), (0, LANES - 2 * ML_HEADS)).reshape(1, LANES)
    mnw = mlstm_norm[0].reshape(1, ML_DH)
    yml_p, c_p, n_p, m_p = _mlstm_call(projm, projt, gate_b, mnw, None, row0=0, nseq_blocks=bp,
                                       ntiles=lp // ML_CHUNK, nbb=1, cl=ML_CHUNK, rp=ML_CHUNK, batch=bp)
    mnbb = 8
    init = (state_mlstm_c[0], state_mlstm_n[0].reshape(bs, ML_HEADS, 1, ML_DH),
            jnp.broadcast_to(state_mlstm_m[0][:, :, None, None], (bs, ML_HEADS, 1, LANES)))
    yml_s, c_s, n_s, m_s = _mlstm_call(projm, projt, gate_b, mnw, init, row0=mp, nseq_blocks=bs // mnbb,
                                       ntiles=1, nbb=mnbb, cl=ls, rp=LANES, batch=bs)
    h = _mm_out([yml_p], [yml_s], w_out_odd, 0, (h, 0), (h, mp // 512))
    hid = _norm_mm(h, norm_mlp[1], w_mlp_up, 1, D_FF, act="relu2", out_dtype=BF16)
    h = _mm_res(hid, w_mlp_down, 1, h)

    y_p, y_s = _rmsnorm(h, norm_final, mp)
    g, p = S5_GROUPS, S5_STATE
    return (y_p.reshape(bp, lp, d), y_s.reshape(bs, ls, d),
            hr_p.reshape(1, bp, g, p), hi_p.reshape(1, bp, g, p), sg_p[None],
            c_p[None], n_p.reshape(1, bp, ML_HEADS, ML_DH), m_p[:, :, 0, 0][None],
            hr_s.reshape(1, bs, g, p), hi_s.reshape(1, bs, g, p), sg_s[None],
            c_s[None], n_s.reshape(1, bs, ML_HEADS, ML_DH), m_s[:, :, 0, 0][None])
```

```python
import functools
import math

import jax
import jax.numpy as jnp
import numpy as np
from jax import lax
from jax.experimental import pallas as pl
from jax.experimental.pallas import tpu as pltpu

F32 = jnp.float32
BF16 = jnp.bfloat16
EPS = 1e-6

D_MODEL = 2048
D_FF = 4 * D_MODEL
S5_WIDTH = 1024
S5_GROUPS = 64
S5_GROUP = 16
S5_STATE = 64
S5_CH = S5_GROUPS * S5_STATE
S5_NBLK = 4
GLA_HEADS = 4
GLA_HK = 128
GLA_HV = 256
GLA_TAU = 16.0
GLA_CHUNK = 64
ML_HEADS = 4
ML_DH = 512
ML_CHUNK = 256

SUBLANES = 8
LANES = 128
VMEM_LIMIT = 56 * 1024 * 1024


def _cparams(sem):
    return pltpu.CompilerParams(dimension_semantics=sem, vmem_limit_bytes=VMEM_LIMIT)


def _dot(a, b):
    return jnp.dot(a, b, preferred_element_type=F32)


def _dot_nt(a, b):
    return lax.dot_general(a, b, (((1,), (1,)), ((), ())), preferred_element_type=F32)


def _split3(x):
    p1 = x.astype(BF16)
    r1 = x - p1.astype(F32)
    p2 = r1.astype(BF16)
    r2 = r1 - p2.astype(F32)
    p3 = r2.astype(BF16)
    return p1, p2, p3


def _dot_exact_lhs(m, x):
    p1, p2, p3 = _split3(x)
    return _dot(m, p1) + _dot(m, p2) + _dot(m, p3)


def _dot_nt_exact_lhs(m, x):
    p1, p2, p3 = _split3(x)
    return _dot_nt(m, p1) + _dot_nt(m, p2) + _dot_nt(m, p3)


def _log_sigmoid(x):
    return jnp.minimum(x, 0.0) - jnp.log1p(jnp.exp(-jnp.abs(x)))


def _sigmoid(x):
    return 1.0 / (1.0 + jnp.exp(-x))


def _norm_mm_body(*refs, act, n_tail, n_ptiles, tm, rchunk):
    dual = n_ptiles is not None
    has_tail = n_tail > 0
    x_ref = refs[0]
    xs_ref = refs[1] if dual else None
    rest = refs[2:] if dual else refs[1:]
    if has_tail:
        g_ref, w_ref, wt_ref, o_ref, ot_ref, xn_ref = rest
    else:
        g_ref, w_ref, o_ref, xn_ref = rest

    def normalise(src_ref, nrows):
        g = g_ref[...]
        for r in range(0, nrows, rchunk):
            x = src_ref[r:r + rchunk, :]
            ms = jnp.mean(x * x, axis=-1, keepdims=True)
            xn_ref[r:r + rchunk, :] = (x * lax.rsqrt(ms + EPS) * g).astype(BF16)

    @pl.when(pl.program_id(1) == 0)
    def _():
        if dual:
            @pl.when(pl.program_id(0) < n_ptiles)
            def _():
                normalise(x_ref, tm)

            @pl.when(pl.program_id(0) >= n_ptiles)
            def _():
                ns = xs_ref.shape[0]
                normalise(xs_ref, ns)
                xn_ref[ns:, :] = jnp.zeros((tm - ns, xn_ref.shape[1]), BF16)
        else:
            normalise(x_ref, tm)
        if has_tail:
            col = lax.broadcasted_iota(jnp.int32, wt_ref.shape, 0)
            wt = jnp.where(col < n_tail, wt_ref[...], 0.0)
            ot_ref[...] = _dot_nt(xn_ref[...], wt.astype(BF16))

    mm = _dot_nt if has_tail else _dot
    acc = mm(xn_ref[...], w_ref[...].astype(BF16))
    if act == "relu2":
        acc = jnp.square(jnp.maximum(acc, 0.0))
    o_ref[...] = acc.astype(o_ref.dtype)


def _norm_mm(x, g, w3, layer, n_main, *, act=None, out_dtype=F32, tm=1088, tn=512):
    dual = isinstance(x, tuple)
    if dual:
        xp, xs = x
        n_ptiles = xp.shape[0] // tm
        m, kdim = (n_ptiles + 1) * tm, xp.shape[1]
        grid = (n_ptiles + 1, n_main // tn)
        in_specs = [
            pl.BlockSpec((tm, kdim), lambda i, j: (jnp.minimum(i, n_ptiles - 1), 0)),
            pl.BlockSpec(xs.shape, lambda i, j: (0, 0)),
        ]
        args = [xp, xs]
    else:
        n_ptiles = None
        m, kdim = x.shape
        grid = (m // tm, n_main // tn)
        in_specs = [pl.BlockSpec((tm, kdim), lambda i, j: (i, 0))]
        args = [x]
    n_tail = w3.shape[2] - n_main
    in_specs.append(pl.BlockSpec((1, kdim), lambda i, j: (0, 0)))
    args.append(g.reshape(1, kdim))
    out_shape = [jax.ShapeDtypeStruct((m, n_main), out_dtype)]
    out_specs = [pl.BlockSpec((tm, tn), lambda i, j: (i, j))]
    if n_tail > 0:
        w3t = jnp.transpose(w3, (0, 2, 1))
        in_specs += [pl.BlockSpec((None, tn, kdim), lambda i, j: (layer, j, 0)),
                     pl.BlockSpec((None, LANES, kdim), lambda i, j: (layer, n_main // LANES, 0))]
        args += [w3t, w3t]
        out_shape.append(jax.ShapeDtypeStruct((m, LANES), F32))
        out_specs.append(pl.BlockSpec((tm, LANES), lambda i, j: (i, 0)))
    else:
        in_specs.append(pl.BlockSpec((None, kdim, tn), lambda i, j: (layer, 0, j)))
        args.append(w3)
    body = functools.partial(_norm_mm_body, act=act, n_tail=n_tail, n_ptiles=n_ptiles,
                             tm=tm, rchunk=tm // 4)
    res = pl.pallas_call(
        body, grid=grid, in_specs=in_specs, out_specs=out_specs, out_shape=out_shape,
        scratch_shapes=[pltpu.VMEM((tm, kdim), BF16)],
        compiler_params=_cparams(("arbitrary", "arbitrary")),
        name="norm_mm",
    )(*args)
    return res if n_tail > 0 else res[0]


def _mm_res_body(a_ref, w_ref, r_ref, o_ref):
    @pl.when(pl.program_id(2) == 0)
    def _():
        o_ref[...] = r_ref[...] + _dot(a_ref[...].astype(BF16), w_ref[...].astype(BF16))

    @pl.when(pl.program_id(2) > 0)
    def _():
        o_ref[...] = o_ref[...] + _dot(a_ref[...].astype(BF16), w_ref[...].astype(BF16))


def _mm_res(a, w3, layer, res, *, tm=1088, tn=1024, tk=1024):
    m, kdim = a.shape
    n = w3.shape[2]
    grid = (m // tm, n // tn, kdim // tk)
    return pl.pallas_call(
        _mm_res_body, grid=grid,
        in_specs=[
            pl.BlockSpec((tm, tk), lambda i, j, k: (i, k)),
            pl.BlockSpec((None, tk, tn), lambda i, j, k: (layer, k, j)),
            pl.BlockSpec((tm, tn), lambda i, j, k: (i, j)),
        ],
        out_specs=pl.BlockSpec((tm, tn), lambda i, j, k: (i, j)),
        out_shape=jax.ShapeDtypeStruct((m, n), F32),
        compiler_params=_cparams(("arbitrary", "arbitrary", "arbitrary")),
        name="mm_res",
    )(a, w3, res)


def _mm_out_body(*refs, kcs, n_ptiles):
    n = len(kcs)
    ap, asm = refs[:n], refs[n:2 * n]
    w_ref, rp_ref, rs_ref, o_ref, wb_ref = refs[2 * n:]
    i = pl.program_id(1)

    @pl.when(i == 0)
    def _():
        wb_ref[...] = w_ref[...].astype(BF16)

    def compute(srcs, r_ref):
        acc = r_ref[...]
        off = 0
        for a_ref, kc in zip(srcs, kcs):
            acc = acc + _dot(a_ref[...], wb_ref[off:off + kc, :])
            off += kc
        o_ref[...] = acc

    @pl.when(i < n_ptiles)
    def _():
        compute(ap, rp_ref)

    @pl.when(i >= n_ptiles)
    def _():
        compute(asm, rs_ref)


def _mm_out(a_prompt, a_sample, w3, layer, res_prompt, res_sample, *, tm=512, tn=1024):
    kcs = tuple(a.shape[1] for a in a_prompt)
    kdim, n = sum(kcs), w3.shape[2]
    n_ptiles = a_prompt[0].shape[0] // tm
    m = a_prompt[0].shape[0] + a_sample[0].shape[0]
    (rp, rp0), (rs, rs0) = res_prompt, res_sample
    pidx = lambda i: jnp.minimum(i, n_ptiles - 1)
    in_specs = ([pl.BlockSpec((tm, kc), lambda j, i: (pidx(i), 0)) for kc in kcs]
                + [pl.BlockSpec((tm, kc), lambda j, i: (0, 0)) for kc in kcs]
                + [pl.BlockSpec((None, kdim, tn), lambda j, i: (layer, 0, j)),
                   pl.BlockSpec((tm, tn), lambda j, i: (rp0 + pidx(i), j)),
                   pl.BlockSpec((tm, tn), lambda j, i: (rs0, j))])
    body = functools.partial(_mm_out_body, kcs=kcs, n_ptiles=n_ptiles)
    return pl.pallas_call(
        body, grid=(n // tn, n_ptiles + 1), in_specs=in_specs,
        out_specs=pl.BlockSpec((tm, tn), lambda j, i: (i, j)),
        out_shape=jax.ShapeDtypeStruct((m, n), F32),
        scratch_shapes=[pltpu.VMEM((kdim, tn), BF16)],
        compiler_params=_cparams(("arbitrary", "arbitrary")),
        name="mm_out",
    )(*a_prompt, *a_sample, w3, rp, rs)


def _rmsnorm_body(x_ref, g_ref, op_ref, os_ref, *, n_ptiles):
    x = x_ref[...]
    ms = jnp.mean(x * x, axis=-1, keepdims=True)
    y = x * lax.rsqrt(ms + EPS) * g_ref[...]

    @pl.when(pl.program_id(0) < n_ptiles)
    def _():
        op_ref[...] = y

    @pl.when(pl.program_id(0) >= n_ptiles)
    def _():
        os_ref[...] = y


def _rmsnorm(x, g, m_prompt, *, tm=512):
    m, d = x.shape
    n_ptiles = m_prompt // tm
    return pl.pallas_call(
        functools.partial(_rmsnorm_body, n_ptiles=n_ptiles), grid=(n_ptiles + 1,),
        in_specs=[pl.BlockSpec((tm, d), lambda i: (i, 0)), pl.BlockSpec((1, d), lambda i: (0, 0))],
        out_specs=[pl.BlockSpec((tm, d), lambda i: (jnp.minimum(i, n_ptiles - 1), 0)),
                   pl.BlockSpec((tm, d), lambda i: (0, 0))],
        out_shape=[jax.ShapeDtypeStruct((m_prompt, d), F32),
                   jax.ShapeDtypeStruct((m - m_prompt, d), F32)],
        compiler_params=_cparams(("arbitrary",)),
        name="final_norm",
    )(x, g.reshape(1, d))


def _s5_disc_body(ar_ref, ai_ref, ls_ref, br_ref, bi_ref, abr_ref, abi_ref, bbr_ref, bbi_ref):
    ar = ar_ref[...]
    ai = ai_ref[...]
    dt = jnp.exp(ls_ref[...])
    mag = jnp.exp(ar * dt)
    abr = mag * jnp.cos(ai * dt)
    abi = mag * jnp.sin(ai * dt)
    lam2 = ar * ar + ai * ai
    zr = abr - 1.0
    cr = (zr * ar + abi * ai) / lam2
    ci = (abi * ar - zr * ai) / lam2
    br = br_ref[...]
    bi = bi_ref[...]
    abr_ref[...] = abr
    abi_ref[...] = abi
    bbr_ref[...] = cr * br - ci * bi
    bbi_ref[...] = cr * bi + ci * br


def _s5_discretise(a_re, a_im, log_step, b_re, b_im):
    g, p, q = S5_GROUPS, S5_STATE, S5_GROUP
    rep = lambda t: jnp.repeat(t, q, axis=0)
    ls = jnp.broadcast_to(log_step[:, None], (g, p))
    brt = jnp.transpose(b_re, (0, 2, 1)).reshape(g * q, p)
    bit = jnp.transpose(b_im, (0, 2, 1)).reshape(g * q, p)
    shp = jax.ShapeDtypeStruct((g * q, p), F32)
    spec = pl.BlockSpec((g * q, p), lambda: (0, 0))
    abr, abi, bbr, bbi = pl.pallas_call(
        _s5_disc_body, in_specs=[spec] * 5, out_specs=[spec] * 4, out_shape=[shp] * 4,
        name="s5_disc",
    )(rep(a_re), rep(a_im), rep(ls), brt, bit)
    abar_re = abr[::q].reshape(1, S5_CH)
    abar_im = abi[::q].reshape(1, S5_CH)
    return abar_re, abar_im, bbr.reshape(g, q, p), bbi.reshape(g, q, p)


def _block_diag(t, nblk):
    g, a, b = t.shape
    gb = g // nblk
    t = t.reshape(nblk, gb, a, b)
    eye = jnp.eye(gb, dtype=t.dtype)
    out = t[:, :, :, None, :] * eye[None, :, None, :, None]
    return out.reshape(nblk, gb * a, gb * b)


def _s5_perm(nb, tl):
    r = np.arange(nb * tl)
    p = np.zeros((nb * tl, nb * tl), np.float32)
    p[r, (r % nb) * tl + r // nb] = 1.0
    return p


def _s5_body(*refs, nsrc, nb, has_init, permute):
    u_refs, refs = refs[:nsrc], refs[nsrc:]
    if permute:
        pm_ref, pmt_ref = refs[:2]
        refs = refs[2:]
    bre_ref, bim_ref, cre_ref, cim_ref, ar_ref, ai_ref, d_ref, gw_ref, gb_ref = refs[:9]
    refs = refs[9:]
    if has_init:
        h0r_ref, h0i_ref, y_ref, hr_ref, hi_ref, sre, sim, ysc = refs
    else:
        y_ref, hr_ref, hi_ref, sre, sim, ysc = refs
    rows = sre.shape[0]
    tl = rows // nb
    cb = S5_CH // S5_NBLK
    ub = S5_WIDTH // S5_NBLK
    hw = cb // 2

    @pl.when(pl.program_id(0) == 0)
    def _():
        if has_init:
            hr_ref[...] = h0r_ref[...]
            hi_ref[...] = h0i_ref[...]
        else:
            hr_ref[...] = jnp.zeros_like(hr_ref)
            hi_ref[...] = jnp.zeros_like(hi_ref)

    u = u_refs[0][...] if nsrc == 1 else jnp.concatenate([r[...] for r in u_refs], axis=0)
    if permute:
        moved = _dot(pm_ref[...], jnp.concatenate(_split3(u), axis=1))
        t1 = moved[:, :S5_WIDTH]
        u_tm = t1 + moved[:, S5_WIDTH:2 * S5_WIDTH] + moved[:, 2 * S5_WIDTH:]
        u_b = t1.astype(BF16)
    else:
        u_tm = u
        u_b = u.astype(BF16)
    low_rows = lax.broadcasted_iota(jnp.int32, (SUBLANES, hw), 0) < nb

    for k in range(S5_NBLK):
        uk = u_b[:, k * ub:(k + 1) * ub]
        sre[...] = _dot(uk, bre_ref[k])
        sim[...] = _dot(uk, bim_ref[k])
        for half in range(2):
            c0 = k * cb + half * hw
            l0 = half * hw
            ar8 = jnp.broadcast_to(ar_ref[:, c0:c0 + hw], (SUBLANES, hw))
            ai8 = jnp.broadcast_to(ai_ref[:, c0:c0 + hw], (SUBLANES, hw))

            if nb < SUBLANES:
                def j_body(j, c, l0=l0, ar8=ar8, ai8=ai8):
                    sr, si = c
                    row = pl.multiple_of(j * SUBLANES, SUBLANES)
                    dr = sre[pl.ds(row, SUBLANES), l0:l0 + hw]
                    di = sim[pl.ds(row, SUBLANES), l0:l0 + hw]
                    pr = pltpu.roll(sr, nb, 0)
                    pi = pltpu.roll(si, nb, 0)
                    yr = ar8 * pr - ai8 * pi + dr
                    yi = ar8 * pi + ai8 * pr + di
                    qr = pltpu.roll(yr, nb, 0)
                    qi = pltpu.roll(yi, nb, 0)
                    zr = ar8 * qr - ai8 * qi + dr
                    zi = ar8 * qi + ai8 * qr + di
                    sre[pl.ds(row, SUBLANES), l0:l0 + hw] = jnp.where(low_rows, yr, zr)
                    sim[pl.ds(row, SUBLANES), l0:l0 + hw] = jnp.where(low_rows, yi, zi)
                    return zr, zi

                sr, si = lax.fori_loop(0, rows // SUBLANES, j_body,
                                       (hr_ref[:, c0:c0 + hw], hi_ref[:, c0:c0 + hw]), unroll=True)
                hr_ref[:, c0:c0 + hw] = sr
                hi_ref[:, c0:c0 + hw] = si
            else:
                def rg_body(rg, carry, c0=c0, l0=l0, ar8=ar8, ai8=ai8):
                    r0 = pl.multiple_of(rg * SUBLANES, SUBLANES)
                    xr = hr_ref[pl.ds(r0, SUBLANES), c0:c0 + hw]
                    xi = hi_ref[pl.ds(r0, SUBLANES), c0:c0 + hw]
                    for t in range(tl):
                        row = pl.multiple_of(t * nb + r0, SUBLANES)
                        nr = ar8 * xr - ai8 * xi + sre[pl.ds(row, SUBLANES), l0:l0 + hw]
                        ni = ar8 * xi + ai8 * xr + sim[pl.ds(row, SUBLANES), l0:l0 + hw]
                        sre[pl.ds(row, SUBLANES), l0:l0 + hw] = nr
                        sim[pl.ds(row, SUBLANES), l0:l0 + hw] = ni
                        xr, xi = nr, ni
                    hr_ref[pl.ds(r0, SUBLANES), c0:c0 + hw] = xr
                    hi_ref[pl.ds(r0, SUBLANES), c0:c0 + hw] = xi
                    return carry

                lax.fori_loop(0, nb // SUBLANES, rg_body, 0)
        ysc[:, k * ub:(k + 1) * ub] = (_dot(sre[...].astype(BF16), cre_ref[k])
                                       - _dot(sim[...].astype(BF16), cim_ref[k]))

    y = ysc[...] + d_ref[...] * u_tm
    y = 0.5 * y * (1.0 + jnp.tanh(math.sqrt(2.0 / math.pi) * (y + 0.044715 * (y * y * y))))
    z = _dot(y.astype(BF16), gw_ref[...]) + gb_ref[...]
    out = (y * _sigmoid(z)).astype(BF16)
    if permute:
        out = _dot(pmt_ref[...], out).astype(BF16)
    y_ref[...] = out.reshape(y_ref.shape)


def _s5_call(src, bre, bim, cre, cim, abar_re, abar_im, d_skip, glu_w, glu_b, h0, *,
             row0, nsrc, rs, seq_len, nb, tl, permute):
    rows = nsrc * rs
    ntiles = seq_len // rs if nsrc > 1 else 1
    has_init = h0 is not None
    srows = max(nb, SUBLANES)
    const = lambda shape: pl.BlockSpec(shape, lambda t: (0,) * len(shape))
    in_specs = [pl.BlockSpec((rs, S5_WIDTH), lambda t, s=s: ((row0 + s * seq_len) // rs + t, 0))
                for s in range(nsrc)]
    args = [src] * nsrc
    if permute:
        pm = _s5_perm(nb, tl)
        in_specs += [const((rows, rows)), const((rows, rows))]
        args += [jnp.asarray(pm, BF16), jnp.asarray(pm.T, BF16)]
    in_specs += [
        const(bre.shape), const(bim.shape), const(cre.shape), const(cim.shape),
        const((1, S5_CH)), const((1, S5_CH)), const((1, S5_WIDTH)),
        const((S5_WIDTH, S5_WIDTH)), const((1, S5_WIDTH)),
    ]
    args += [bre, bim, cre, cim, abar_re, abar_im, d_skip, glu_w, glu_b]
    if has_init:
        in_specs += [const((srows, S5_CH)), const((srows, S5_CH))]
        args += [h0[0], h0[1]]
    body = functools.partial(_s5_body, nsrc=nsrc, nb=nb, has_init=has_init, permute=permute)
    return pl.pallas_call(
        body, grid=(ntiles,), in_specs=in_specs,
        out_specs=[pl.BlockSpec((nsrc, rs, S5_WIDTH), lambda t: (0, t, 0)),
                   const((srows, S5_CH)), const((srows, S5_CH))],
        out_shape=[jax.ShapeDtypeStruct((nsrc, ntiles * rs, S5_WIDTH), BF16),
                   jax.ShapeDtypeStruct((srows, S5_CH), F32),
                   jax.ShapeDtypeStruct((srows, S5_CH), F32)],
        scratch_shapes=[pltpu.VMEM((rows, S5_CH // S5_NBLK), F32),
                        pltpu.VMEM((rows, S5_CH // S5_NBLK), F32),
                        pltpu.VMEM((rows, S5_WIDTH), F32)],
        compiler_params=_cparams(("arbitrary",)),
        name="s5_mixer",
    )(*args)


def _gla_body(*refs, nbb, nch, cs, has_init, hps):
    if has_init:
        (q_ref, k_ref, v_ref, g_ref, gl_ref, up_ref, gb_ref, nw_ref, s0_ref, y_ref, s_ref) = refs
    else:
        (q_ref, k_ref, v_ref, g_ref, gl_ref, up_ref, gb_ref, nw_ref, y_ref, s_ref) = refs
    rows = nbb * nch * cs
    sh = int(math.log2(cs))

    @pl.when(pl.program_id(2) == 0)
    def _():
        if has_init:
            s_ref[...] = s0_ref[...]
        else:
            s_ref[...] = jnp.zeros_like(s_ref)

    ri = lax.broadcasted_iota(jnp.int32, (rows, rows), 0)
    ci = lax.broadcasted_iota(jnp.int32, (rows, rows), 1)
    same = (ri >> sh) == (ci >> sh)
    causal = jnp.logical_and(same, ci <= ri)
    lmat = jnp.where(causal, 1.0, 0.0).astype(BF16)
    tmat = jnp.where(same, 1.0, 0.0).astype(BF16)
    rchunk = lax.broadcasted_iota(jnp.int32, (rows, 1), 0) >> sh
    cchunk = lax.broadcasted_iota(jnp.int32, (1, rows), 1) >> sh
    single = nbb * nch == 1
    z = _dot(gl_ref[...].astype(BF16), up_ref[...].astype(BF16)) + gb_ref[...]
    la = _log_sigmoid(z) * (1.0 / GLA_TAU)
    p1, p2, p3 = _split3(la)
    bcum_all = _dot(lmat, p1) + _dot(lmat, p2) + _dot(lmat, p3)
    tot_all = _dot(tmat, p1) + _dot(tmat, p2) + _dot(tmat, p3)

    for hh in range(hps):
        ks = slice(hh * GLA_HK, (hh + 1) * GLA_HK)
        vs = slice(hh * GLA_HV, (hh + 1) * GLA_HV)
        q = q_ref[:, ks] * (GLA_HK ** -0.5)
        k = k_ref[:, ks]
        vb = v_ref[:, vs].astype(BF16)
        bcum = bcum_all[:, ks]
        tot = tot_all[:, ks]

        qd = q * jnp.exp(bcum)
        kd = k * jnp.exp(-bcum)
        kdec = k * jnp.exp(tot - bcum)
        qdb = qd.astype(BF16)
        att = jnp.where(causal, _dot_nt(qdb, kd.astype(BF16)), 0.0)
        o = _dot(att.astype(BF16), vb)

        kdec_t = jnp.transpose(kdec)
        dec_t = jnp.transpose(jnp.exp(tot))
        for b in range(nbb):
            s = s_ref[b, hh]
            for c in range(nch):
                idx = b * nch + c
                oi = _dot(qdb, s.astype(BF16))
                o = o + (oi if single else jnp.where(rchunk == idx, oi, 0.0))
                kt = kdec_t if single else jnp.where(cchunk == idx, kdec_t, 0.0)
                s = dec_t[:, idx * cs:idx * cs + 1] * s + _dot(kt.astype(BF16), vb)
            s_ref[b, hh] = s

        o = o * lax.rsqrt(jnp.mean(o * o, axis=-1, keepdims=True) + EPS) * nw_ref[...]
        gg = g_ref[:, vs]
        y_ref[:, vs] = (o * (gg * _sigmoid(gg))).astype(y_ref.dtype)


def _gla_call(projm, projt, up_pad, gkb, nw, s0, *, row0, nseq_blocks, ntiles, nbb, nch, cs, batch, hps):
    rows = nbb * nch * cs
    has_init = s0 is not None
    rb0 = row0 // rows
    rowblk = lambda s, h, t: rb0 + s * ntiles + t
    wk, wv = hps * GLA_HK, hps * GLA_HV
    qoff = S5_WIDTH // wk
    koff = qoff + GLA_HEADS // hps
    voff = (S5_WIDTH + 2 * GLA_HEADS * GLA_HK) // wv
    goff = voff + GLA_HEADS // hps
    in_specs = [
        pl.BlockSpec((rows, wk), lambda s, h, t: (rowblk(s, h, t), qoff + h)),
        pl.BlockSpec((rows, wk), lambda s, h, t: (rowblk(s, h, t), koff + h)),
        pl.BlockSpec((rows, wv), lambda s, h, t: (rowblk(s, h, t), voff + h)),
        pl.BlockSpec((rows, wv), lambda s, h, t: (rowblk(s, h, t), goff + h)),
        pl.BlockSpec((rows, LANES), lambda s, h, t: (rowblk(s, h, t), 0)),
        pl.BlockSpec((LANES, wk), lambda s, h, t: (0, h)),
        pl.BlockSpec((1, wk), lambda s, h, t: (0, h)),
        pl.BlockSpec((1, GLA_HV), lambda s, h, t: (0, 0)),
    ]
    args = [projm, projm, projm, projm, projt, up_pad, gkb, nw]
    if has_init:
        in_specs.append(pl.BlockSpec((nbb, hps, GLA_HK, GLA_HV), lambda s, h, t: (s, h, 0, 0)))
        args.append(s0)
    body = functools.partial(_gla_body, nbb=nbb, nch=nch, cs=cs, has_init=has_init, hps=hps)
    return pl.pallas_call(
        body, grid=(nseq_blocks, GLA_HEADS // hps, ntiles), in_specs=in_specs,
        out_specs=[pl.BlockSpec((rows, wv), lambda s, h, t: (s * ntiles + t, h)),
                   pl.BlockSpec((nbb, hps, GLA_HK, GLA_HV), lambda s, h, t: (s, h, 0, 0))],
        out_shape=[jax.ShapeDtypeStruct((nseq_blocks * ntiles * rows, GLA_HEADS * GLA_HV), BF16),
                   jax.ShapeDtypeStruct((batch, GLA_HEADS, GLA_HK, GLA_HV), F32)],
        compiler_params=_cparams(("arbitrary", "arbitrary", "arbitrary")),
        name="gla_mixer",
    )(*args)


def _mlstm_body(*refs, nbb, cl, rp, has_init, hps):
    if has_init:
        (q_ref, k_ref, v_ref, og_ref, gt_ref, gb_ref, nw_ref, c0_ref, n0_ref, m0_ref,
         y_ref, c_ref, n_ref, m_ref) = refs
    else:
        (q_ref, k_ref, v_ref, og_ref, gt_ref, gb_ref, nw_ref, y_ref, c_ref, n_ref, m_ref) = refs
    rows = nbb * cl
    sh = int(math.log2(cl))
    head0 = pl.program_id(1) * hps

    @pl.when(pl.program_id(2) == 0)
    def _():
        if has_init:
            c_ref[...] = c0_ref[...]
            n_ref[...] = n0_ref[...]
            m_ref[...] = m0_ref[...]
        else:
            c_ref[...] = jnp.zeros_like(c_ref)
            n_ref[...] = jnp.zeros_like(n_ref)
            m_ref[...] = jnp.full(m_ref.shape, -jnp.inf, F32)

    def pad(x):
        if rp == rows:
            return x
        return jnp.concatenate([x, jnp.zeros((rp - rows, x.shape[1]), x.dtype)], axis=0)

    ksc = ML_DH ** -0.5
    gates = pad(gt_ref[...] + gb_ref[...])
    lane = lax.broadcasted_iota(jnp.int32, (rp, LANES), 1)
    ri = lax.broadcasted_iota(jnp.int32, (rp, rp), 0)
    ci = lax.broadcasted_iota(jnp.int32, (rp, rp), 1)
    causal = jnp.logical_and((ri >> sh) == (ci >> sh), ci <= ri)
    lmat = jnp.where(causal, 1.0, 0.0).astype(BF16)
    neg = jnp.where(causal, 0.0, -jnp.inf)
    er = lax.broadcasted_iota(jnp.int32, (SUBLANES, LANES), 0)
    ec = lax.broadcasted_iota(jnp.int32, (SUBLANES, LANES), 1)
    pick = jnp.where(er == ec, 1.0, 0.0).astype(BF16)
    rbatch = lax.broadcasted_iota(jnp.int32, (rp, 1), 0) >> sh
    cbatch = lax.broadcasted_iota(jnp.int32, (1, rp), 1) >> sh

    for hh in range(hps):
        head = head0 + hh
        hs = slice(hh * ML_DH, (hh + 1) * ML_DH)
        q = pad(q_ref[:, hs])
        k = pad(k_ref[:, hs])
        v = pad(v_ref[:, hs])
        og = pad(og_ref[:, hs])
        ig = jnp.sum(jnp.where(lane == head, gates, 0.0), axis=1, keepdims=True)
        fg = jnp.sum(jnp.where(lane == head + ML_HEADS, gates, 0.0), axis=1, keepdims=True)
        lf = _log_sigmoid(fg)
        fcum = _dot_exact_lhs(lmat, jnp.where(lane == 0, lf, 0.0))[:, 0:1]
        cols = jnp.where(lane == 0, fcum, jnp.where(lane == 1, ig, 0.0))
        as_rows = _dot_nt_exact_lhs(pick, cols)
        fcum_row = as_rows[0:1, :]
        ig_row = as_rows[1:2, :]

        mm = jnp.zeros((rp, 1), F32)
        for b in range(nbb):
            mb = m_ref[b, hh][:, 0:1]
            mm = jnp.where(rbatch == b, mb, mm) if nbb > 1 else jnp.broadcast_to(mb, (rp, 1))

        dmat = fcum - fcum_row + ig_row + neg
        dprev = fcum + mm
        m = jnp.maximum(jnp.max(dmat, axis=1, keepdims=True), dprev)
        w = jnp.exp(dmat - m + math.log(ksc))
        wp = jnp.exp(dprev - m)
        qb = q.astype(BF16)
        kb = k.astype(BF16)
        vb = v.astype(BF16)
        sc = _dot_nt(qb, kb) * w
        num = _dot(sc.astype(BF16), vb)
        den = jnp.sum(sc, axis=1, keepdims=True)

        k_t = jnp.transpose(k)
        for b in range(nbb):
            cm = c_ref[b, hh]
            nm = n_ref[b, hh]
            mb = m_ref[b, hh][:, 0:1]
            qc = wp * _dot(qb, cm.astype(BF16))
            nm8 = jnp.broadcast_to(nm, (SUBLANES, ML_DH)).astype(BF16)
            qn = wp * _dot_nt(qb, nm8)[:, 0:1]
            if nbb > 1:
                rsel = rbatch == b
                qc = jnp.where(rsel, qc, 0.0)
                qn = jnp.where(rsel, qn, 0.0)
            num = num + qc
            den = den + qn
            last = b * cl + cl - 1
            m_new = m[last:last + 1, :]
            fl = fcum[last:last + 1, :]
            decay = jnp.exp(fl + mb - m_new)
            wk_row = ksc * jnp.exp(fl - fcum_row + ig_row - m_new)
            if nbb > 1:
                wk_row = jnp.where(cbatch == b, wk_row, 0.0)
            c_ref[b, hh] = decay * cm + _dot((k_t * wk_row).astype(BF16), vb)
            wk8 = jnp.broadcast_to(wk_row, (SUBLANES, rp)).astype(BF16)
            n_ref[b, hh] = decay * nm + _dot(wk8, kb)[0:1, :]
            m_ref[b, hh] = jnp.broadcast_to(m_new, (1, LANES))

        hv = num / jnp.maximum(jnp.abs(den), jnp.exp(-m))
        hv = hv * _sigmoid(og)
        hv = hv * lax.rsqrt(jnp.mean(hv * hv, axis=-1, keepdims=True) + EPS) * nw_ref[...]
        y_ref[:, hs] = hv[:rows].astype(y_ref.dtype)


def _mlstm_call(projm, projt, gate_b, nw, init, *, row0, nseq_blocks, ntiles, nbb, cl, rp, batch, hps):
    rows = nbb * cl
    has_init = init is not None
    rb0 = row0 // rows
    rowblk = lambda s, h, t: rb0 + s * ntiles + t
    hd = ML_HEADS // hps
    wd = hps * ML_DH
    in_specs = [
        pl.BlockSpec((rows, wd), lambda s, h, t: (rowblk(s, h, t), h)),
        pl.BlockSpec((rows, wd), lambda s, h, t: (rowblk(s, h, t), hd + h)),
        pl.BlockSpec((rows, wd), lambda s, h, t: (rowblk(s, h, t), 2 * hd + h)),
        pl.BlockSpec((rows, wd), lambda s, h, t: (rowblk(s, h, t), 3 * hd + h)),
        pl.BlockSpec((rows, LANES), lambda s, h, t: (rowblk(s, h, t), 0)),
        pl.BlockSpec((1, LANES), lambda s, h, t: (0, 0)),
        pl.BlockSpec((1, ML_DH), lambda s, h, t: (0, 0)),
    ]
    args = [projm, projm, projm, projm, projt, gate_b, nw]
    c_spec = pl.BlockSpec((nbb, hps, ML_DH, ML_DH), lambda s, h, t: (s, h, 0, 0))
    n_spec = pl.BlockSpec((nbb, hps, 1, ML_DH), lambda s, h, t: (s, h, 0, 0))
    m_spec = pl.BlockSpec((nbb, hps, 1, LANES), lambda s, h, t: (s, h, 0, 0))
    if has_init:
        in_specs += [c_spec, n_spec, m_spec]
        args += list(init)
    body = functools.partial(_mlstm_body, nbb=nbb, cl=cl, rp=rp, has_init=has_init, hps=hps)
    return pl.pallas_call(
        body, grid=(nseq_blocks, hd, ntiles), in_specs=in_specs,
        out_specs=[pl.BlockSpec((rows, wd), lambda s, h, t: (s * ntiles + t, h)),
                   c_spec, n_spec, m_spec],
        out_shape=[jax.ShapeDtypeStruct((nseq_blocks * ntiles * rows, ML_HEADS * ML_DH), BF16),
                   jax.ShapeDtypeStruct((batch, ML_HEADS, ML_DH, ML_DH), F32),
                   jax.ShapeDtypeStruct((batch, ML_HEADS, 1, ML_DH), F32),
                   jax.ShapeDtypeStruct((batch, ML_HEADS, 1, LANES), F32)],
        compiler_params=_cparams(("arbitrary", "arbitrary", "arbitrary")),
        name="mlstm_mixer",
    )(*args)


def kernel(x_prompt, x_sample, state_s5_re, state_s5_im, state_gla, state_mlstm_c, state_mlstm_n,
           state_mlstm_m, norm_mix, norm_mlp, norm_final, w_in_even, s5_a_re, s5_a_im, s5_log_step,
           s5_b_re, s5_b_im, s5_c_re, s5_c_im, s5_d, s5_glu_w, s5_glu_b, gla_gk_up, gla_gk_b,
           gla_norm, w_out_even, w_in_odd, mlstm_b_i, mlstm_b_f, mlstm_norm, w_out_odd,
           w_mlp_up, w_mlp_down):
    bp, lp, d = x_prompt.shape
    bs, ls, _ = x_sample.shape
    mp = bp * lp
    msamp = bs * ls
    xp2 = x_prompt.reshape(mp, d)
    xs2 = x_sample.reshape(msamp, d)

    n_even = S5_WIDTH + 2 * GLA_HEADS * GLA_HK + 2 * GLA_HEADS * GLA_HV
    projm, projt = _norm_mm((xp2, xs2), norm_mix[0], w_in_even, 0, n_even, tm=1024)

    abar_re, abar_im, bbr, bbi = _s5_discretise(s5_a_re[0], s5_a_im[0], s5_log_step[0],
                                                s5_b_re[0], s5_b_im[0])
    bre = _block_diag(bbr, S5_NBLK).astype(BF16)
    bim = _block_diag(bbi, S5_NBLK).astype(BF16)
    cre = _block_diag(jnp.transpose(s5_c_re[0], (0, 2, 1)), S5_NBLK).astype(BF16)
    cim = _block_diag(jnp.transpose(s5_c_im[0], (0, 2, 1)), S5_NBLK).astype(BF16)
    s5_consts = (bre, bim, cre, cim, abar_re, abar_im, s5_d[0].reshape(1, S5_WIDTH),
                 s5_glu_w[0].astype(BF16), s5_glu_b[0].reshape(1, S5_WIDTH))

    s5_tl = 64
    ys5_p, hr_p, hi_p = _s5_call(projm, *s5_consts, None, row0=0, nsrc=bp, rs=s5_tl, seq_len=lp,
                                 nb=bp, tl=s5_tl, permute=True)
    ys5_p = ys5_p.reshape(mp, S5_WIDTH)
    hr_p = hr_p[SUBLANES - bp:]
    hi_p = hi_p[SUBLANES - bp:]
    u_s = jnp.transpose(projm[mp:mp + msamp, :S5_WIDTH].reshape(bs, ls, S5_WIDTH), (1, 0, 2))
    h0 = (state_s5_re[0].reshape(bs, S5_CH), state_s5_im[0].reshape(bs, S5_CH))
    ys5_s, hr_s, hi_s = _s5_call(u_s.reshape(msamp, S5_WIDTH), *s5_consts, h0, row0=0, nsrc=1, rs=msamp,
                                 seq_len=msamp, nb=bs, tl=ls, permute=False)
    ys5_s = jnp.transpose(ys5_s.reshape(ls, bs, S5_WIDTH), (1, 0, 2)).reshape(msamp, S5_WIDTH)

    up_pad = jnp.pad(gla_gk_up[0], ((0, LANES - gla_gk_up.shape[1]), (0, 0)))
    gkb = gla_gk_b[0].reshape(1, -1)
    gnw = gla_norm[0].reshape(1, GLA_HV)
    ptile = 256
    ygla_p, sg_p = _gla_call(projm, projt, up_pad, gkb, gnw, None, row0=0, nseq_blocks=bp,
                             ntiles=lp // ptile, nbb=1, nch=ptile // GLA_CHUNK, cs=GLA_CHUNK, batch=bp,
                             hps=GLA_HEADS)
    gnbb = 32
    ygla_s, sg_s = _gla_call(projm, projt, up_pad, gkb, gnw, state_gla[0], row0=mp,
                             nseq_blocks=bs // gnbb, ntiles=1, nbb=gnbb, nch=1, cs=ls, batch=bs, hps=1)

    h = _mm_out([ys5_p, ygla_p], [ys5_s, ygla_s], w_out_even, 0, (xp2, 0), (xs2, 0))
    hid = _norm_mm(h, norm_mlp[0], w_mlp_up, 0, D_FF, act="relu2", out_dtype=BF16, tn=1024)
    h = _mm_res(hid, w_mlp_down, 0, h, tn=D_MODEL, tk=512)

    n_odd = 4 * ML_HEADS * ML_DH
    projm, projt = _norm_mm(h, norm_mix[1], w_in_odd, 0, n_odd)
    gate_b = jnp.pad(jnp.concatenate([mlstm_b_i[0], mlstm_b_f[0]]), (0, LANES - 2 * ML_HEADS)).reshape(1, LANES)
    mnw = mlstm_norm[0].reshape(1, ML_DH)
    yml_p, c_p, n_p, m_p = _mlstm_call(projm, projt, gate_b, mnw, None, row0=0, nseq_blocks=bp,
                                       ntiles=lp // ML_CHUNK, nbb=1, cl=ML_CHUNK, rp=ML_CHUNK, batch=bp,
                                       hps=ML_HEADS)
    mnbb = 8
    init = (state_mlstm_c[0], state_mlstm_n[0].reshape(bs, ML_HEADS, 1, ML_DH),
            jnp.broadcast_to(state_mlstm_m[0][:, :, None, None], (bs, ML_HEADS, 1, LANES)))
    yml_s, c_s, n_s, m_s = _mlstm_call(projm, projt, gate_b, mnw, init, row0=mp, nseq_blocks=bs // mnbb,
                                       ntiles=1, nbb=mnbb, cl=ls, rp=LANES, batch=bs, hps=1)
    h = _mm_out([yml_p], [yml_s], w_out_odd, 0, (h, 0), (h, mp // 512))
    hid = _norm_mm(h, norm_mlp[1], w_mlp_up, 1, D_FF, act="relu2", out_dtype=BF16, tn=1024)
    h = _mm_res(hid, w_mlp_down, 1, h, tn=D_MODEL, tk=512)

    y_p, y_s = _rmsnorm(h, norm_final, mp)
    g, p = S5_GROUPS, S5_STATE
    return (y_p.reshape(bp, lp, d), y_s.reshape(bs, ls, d),
            hr_p.reshape(1, bp, g, p), hi_p.reshape(1, bp, g, p), sg_p[None],
            c_p[None], n_p.reshape(1, bp, ML_HEADS, ML_DH), m_p[:, :, 0, 0][None],
            hr_s.reshape(1, bs, g, p), hi_s.reshape(1, bs, g, p), sg_s[None],
            c_s[None], n_s.reshape(1, bs, ML_HEADS, ML_DH), m_s[:, :, 0, 0][None])
```

```python
import functools
import math

import jax
import jax.numpy as jnp
import numpy as np
from jax import lax
from jax.experimental import pallas as pl
from jax.experimental.pallas import tpu as pltpu

F32 = jnp.float32
BF16 = jnp.bfloat16
EPS = 1e-6

D_MODEL = 2048
D_FF = 4 * D_MODEL
S5_WIDTH = 1024
S5_GROUPS = 64
S5_GROUP = 16
S5_STATE = 64
S5_CH = S5_GROUPS * S5_STATE
S5_NBLK = 4
GLA_HEADS = 4
GLA_HK = 128
GLA_HV = 256
GLA_TAU = 16.0
GLA_CHUNK = 64
ML_HEADS = 4
ML_DH = 512
ML_CHUNK = 256

SUBLANES = 8
LANES = 128
VMEM_LIMIT = 56 * 1024 * 1024


def _cparams(sem):
    return pltpu.CompilerParams(dimension_semantics=sem, vmem_limit_bytes=VMEM_LIMIT)


def _dot(a, b):
    return jnp.dot(a, b, preferred_element_type=F32)


def _dot_nt(a, b):
    return lax.dot_general(a, b, (((1,), (1,)), ((), ())), preferred_element_type=F32)


def _split3(x):
    p1 = x.astype(BF16)
    r1 = x - p1.astype(F32)
    p2 = r1.astype(BF16)
    r2 = r1 - p2.astype(F32)
    p3 = r2.astype(BF16)
    return p1, p2, p3


def _dot_exact_lhs(m, x):
    p1, p2, p3 = _split3(x)
    return _dot(m, p1) + _dot(m, p2) + _dot(m, p3)


def _dot_nt_exact_lhs(m, x):
    p1, p2, p3 = _split3(x)
    return _dot_nt(m, p1) + _dot_nt(m, p2) + _dot_nt(m, p3)


def _log_sigmoid(x):
    return jnp.minimum(x, 0.0) - jnp.log1p(jnp.exp(-jnp.abs(x)))


def _sigmoid(x):
    return 1.0 / (1.0 + jnp.exp(-x))


def _norm_mm_body(*refs, act, n_tail, n_ptiles, tm, rchunk):
    dual = n_ptiles is not None
    has_tail = n_tail > 0
    x_ref = refs[0]
    xs_ref = refs[1] if dual else None
    rest = refs[2:] if dual else refs[1:]
    if has_tail:
        g_ref, w_ref, wt_ref, o_ref, ot_ref, xn_ref = rest
    else:
        g_ref, w_ref, o_ref, xn_ref = rest

    def normalise(src_ref, nrows):
        g = g_ref[...]
        for r in range(0, nrows, rchunk):
            x = src_ref[r:r + rchunk, :]
            ms = jnp.mean(x * x, axis=-1, keepdims=True)
            xn_ref[r:r + rchunk, :] = (x * lax.rsqrt(ms + EPS) * g).astype(BF16)

    @pl.when(pl.program_id(1) == 0)
    def _():
        if dual:
            @pl.when(pl.program_id(0) < n_ptiles)
            def _():
                normalise(x_ref, tm)

            @pl.when(pl.program_id(0) >= n_ptiles)
            def _():
                ns = xs_ref.shape[0]
                normalise(xs_ref, ns)
                xn_ref[ns:, :] = jnp.zeros((tm - ns, xn_ref.shape[1]), BF16)
        else:
            normalise(x_ref, tm)
        if has_tail:
            col = lax.broadcasted_iota(jnp.int32, wt_ref.shape, 0)
            wt = jnp.where(col < n_tail, wt_ref[...], 0.0)
            ot_ref[...] = _dot_nt(xn_ref[...], wt.astype(BF16))

    mm = _dot_nt if has_tail else _dot
    acc = mm(xn_ref[...], w_ref[...].astype(BF16))
    if act == "relu2":
        acc = jnp.square(jnp.maximum(acc, 0.0))
    o_ref[...] = acc.astype(o_ref.dtype)


def _norm_mm(x, g, w3, layer, n_main, *, act=None, out_dtype=F32, tm=1088, tn=512):
    dual = isinstance(x, tuple)
    if dual:
        xp, xs = x
        n_ptiles = xp.shape[0] // tm
        m, kdim = (n_ptiles + 1) * tm, xp.shape[1]
        grid = (n_ptiles + 1, n_main // tn)
        in_specs = [
            pl.BlockSpec((tm, kdim), lambda i, j: (jnp.minimum(i, n_ptiles - 1), 0)),
            pl.BlockSpec(xs.shape, lambda i, j: (0, 0)),
        ]
        args = [xp, xs]
    else:
        n_ptiles = None
        m, kdim = x.shape
        grid = (m // tm, n_main // tn)
        in_specs = [pl.BlockSpec((tm, kdim), lambda i, j: (i, 0))]
        args = [x]
    n_tail = w3.shape[2] - n_main
    in_specs.append(pl.BlockSpec((1, kdim), lambda i, j: (0, 0)))
    args.append(g.reshape(1, kdim))
    out_shape = [jax.ShapeDtypeStruct((m, n_main), out_dtype)]
    out_specs = [pl.BlockSpec((tm, tn), lambda i, j: (i, j))]
    if n_tail > 0:
        w3t = jnp.transpose(w3, (0, 2, 1))
        in_specs += [pl.BlockSpec((None, tn, kdim), lambda i, j: (layer, j, 0)),
                     pl.BlockSpec((None, LANES, kdim), lambda i, j: (layer, n_main // LANES, 0))]
        args += [w3t, w3t]
        out_shape.append(jax.ShapeDtypeStruct((m, LANES), F32))
        out_specs.append(pl.BlockSpec((tm, LANES), lambda i, j: (i, 0)))
    else:
        in_specs.append(pl.BlockSpec((None, kdim, tn), lambda i, j: (layer, 0, j)))
        args.append(w3)
    body = functools.partial(_norm_mm_body, act=act, n_tail=n_tail, n_ptiles=n_ptiles,
                             tm=tm, rchunk=tm // 4)
    res = pl.pallas_call(
        body, grid=grid, in_specs=in_specs, out_specs=out_specs, out_shape=out_shape,
        scratch_shapes=[pltpu.VMEM((tm, kdim), BF16)],
        compiler_params=_cparams(("arbitrary", "arbitrary")),
        name="norm_mm",
    )(*args)
    return res if n_tail > 0 else res[0]


def _mm_res_body(a_ref, w_ref, r_ref, o_ref):
    @pl.when(pl.program_id(2) == 0)
    def _():
        o_ref[...] = r_ref[...] + _dot(a_ref[...].astype(BF16), w_ref[...].astype(BF16))

    @pl.when(pl.program_id(2) > 0)
    def _():
        o_ref[...] = o_ref[...] + _dot(a_ref[...].astype(BF16), w_ref[...].astype(BF16))


def _mm_res(a, w3, layer, res, *, tm=1088, tn=1024, tk=1024):
    m, kdim = a.shape
    n = w3.shape[2]
    grid = (m // tm, n // tn, kdim // tk)
    return pl.pallas_call(
        _mm_res_body, grid=grid,
        in_specs=[
            pl.BlockSpec((tm, tk), lambda i, j, k: (i, k)),
            pl.BlockSpec((None, tk, tn), lambda i, j, k: (layer, k, j)),
            pl.BlockSpec((tm, tn), lambda i, j, k: (i, j)),
        ],
        out_specs=pl.BlockSpec((tm, tn), lambda i, j, k: (i, j)),
        out_shape=jax.ShapeDtypeStruct((m, n), F32),
        compiler_params=_cparams(("arbitrary", "arbitrary", "arbitrary")),
        name="mm_res",
    )(a, w3, res)


def _mlp_body(x_ref, g_ref, wu_ref, wd_ref, o_ref, xn_ref, hs_ref, *, tm, rchunk):
    @pl.when(pl.program_id(1) == 0)
    def _():
        g = g_ref[...]
        for r in range(0, tm, rchunk):
            x = x_ref[r:r + rchunk, :]
            ms = jnp.mean(x * x, axis=-1, keepdims=True)
            xn_ref[r:r + rchunk, :] = (x * lax.rsqrt(ms + EPS) * g).astype(BF16)
            o_ref[r:r + rchunk, :] = x
        hs_ref[...] = jnp.zeros_like(hs_ref)

    down = _dot(hs_ref[...], wd_ref[...].astype(BF16))
    hid = _dot(xn_ref[...], wu_ref[...].astype(BF16))
    hs_ref[...] = jnp.square(jnp.maximum(hid, 0.0)).astype(BF16)
    o_ref[...] += down


def _mlp(x, g, w_up3, w_down3, layer, *, tm=1088, tf=256):
    m, d = x.shape
    nf = w_up3.shape[2] // tf
    return pl.pallas_call(
        functools.partial(_mlp_body, tm=tm, rchunk=tm // 4), grid=(m // tm, nf + 1),
        in_specs=[
            pl.BlockSpec((tm, d), lambda i, f: (i, 0)),
            pl.BlockSpec((1, d), lambda i, f: (0, 0)),
            pl.BlockSpec((None, d, tf), lambda i, f: (layer, 0, jnp.minimum(f, nf - 1))),
            pl.BlockSpec((None, tf, d), lambda i, f: (layer, jnp.maximum(f - 1, 0), 0)),
        ],
        out_specs=pl.BlockSpec((tm, d), lambda i, f: (i, 0)),
        out_shape=jax.ShapeDtypeStruct((m, d), F32),
        scratch_shapes=[pltpu.VMEM((tm, d), BF16), pltpu.VMEM((tm, tf), BF16)],
        compiler_params=_cparams(("arbitrary", "arbitrary")),
        name="mlp",
    )(x, g.reshape(1, d), w_up3, w_down3)


def _mm_out_body(*refs, kcs, n_ptiles):
    n = len(kcs)
    ap, asm = refs[:n], refs[n:2 * n]
    w_ref, rp_ref, rs_ref, o_ref, wb_ref = refs[2 * n:]
    i = pl.program_id(1)

    @pl.when(i == 0)
    def _():
        wb_ref[...] = w_ref[...].astype(BF16)

    def compute(srcs, r_ref):
        acc = r_ref[...]
        off = 0
        for a_ref, kc in zip(srcs, kcs):
            acc = acc + _dot(a_ref[...], wb_ref[off:off + kc, :])
            off += kc
        o_ref[...] = acc

    @pl.when(i < n_ptiles)
    def _():
        compute(ap, rp_ref)

    @pl.when(i >= n_ptiles)
    def _():
        compute(asm, rs_ref)


def _mm_out(a_prompt, a_sample, w3, layer, res_prompt, res_sample, *, tm=512, tn=1024):
    kcs = tuple(a.shape[1] for a in a_prompt)
    kdim, n = sum(kcs), w3.shape[2]
    n_ptiles = a_prompt[0].shape[0] // tm
    m = a_prompt[0].shape[0] + a_sample[0].shape[0]
    (rp, rp0), (rs, rs0) = res_prompt, res_sample
    pidx = lambda i: jnp.minimum(i, n_ptiles - 1)
    in_specs = ([pl.BlockSpec((tm, kc), lambda j, i: (pidx(i), 0)) for kc in kcs]
                + [pl.BlockSpec((tm, kc), lambda j, i: (0, 0)) for kc in kcs]
                + [pl.BlockSpec((None, kdim, tn), lambda j, i: (layer, 0, j)),
                   pl.BlockSpec((tm, tn), lambda j, i: (rp0 + pidx(i), j)),
                   pl.BlockSpec((tm, tn), lambda j, i: (rs0, j))])
    body = functools.partial(_mm_out_body, kcs=kcs, n_ptiles=n_ptiles)
    return pl.pallas_call(
        body, grid=(n // tn, n_ptiles + 1), in_specs=in_specs,
        out_specs=pl.BlockSpec((tm, tn), lambda j, i: (i, j)),
        out_shape=jax.ShapeDtypeStruct((m, n), F32),
        scratch_shapes=[pltpu.VMEM((kdim, tn), BF16)],
        compiler_params=_cparams(("arbitrary", "arbitrary")),
        name="mm_out",
    )(*a_prompt, *a_sample, w3, rp, rs)


def _rmsnorm_body(x_ref, g_ref, op_ref, os_ref, *, n_ptiles):
    x = x_ref[...]
    ms = jnp.mean(x * x, axis=-1, keepdims=True)
    y = x * lax.rsqrt(ms + EPS) * g_ref[...]

    @pl.when(pl.program_id(0) < n_ptiles)
    def _():
        op_ref[...] = y

    @pl.when(pl.program_id(0) >= n_ptiles)
    def _():
        os_ref[...] = y


def _rmsnorm(x, g, m_prompt, *, tm=512):
    m, d = x.shape
    n_ptiles = m_prompt // tm
    return pl.pallas_call(
        functools.partial(_rmsnorm_body, n_ptiles=n_ptiles), grid=(n_ptiles + 1,),
        in_specs=[pl.BlockSpec((tm, d), lambda i: (i, 0)), pl.BlockSpec((1, d), lambda i: (0, 0))],
        out_specs=[pl.BlockSpec((tm, d), lambda i: (jnp.minimum(i, n_ptiles - 1), 0)),
                   pl.BlockSpec((tm, d), lambda i: (0, 0))],
        out_shape=[jax.ShapeDtypeStruct((m_prompt, d), F32),
                   jax.ShapeDtypeStruct((m - m_prompt, d), F32)],
        compiler_params=_cparams(("arbitrary",)),
        name="final_norm",
    )(x, g.reshape(1, d))


def _s5_disc_body(ar_ref, ai_ref, ls_ref, br_ref, bi_ref, abr_ref, abi_ref, bbr_ref, bbi_ref):
    ar = ar_ref[...]
    ai = ai_ref[...]
    dt = jnp.exp(ls_ref[...])
    mag = jnp.exp(ar * dt)
    abr = mag * jnp.cos(ai * dt)
    abi = mag * jnp.sin(ai * dt)
    lam2 = ar * ar + ai * ai
    zr = abr - 1.0
    cr = (zr * ar + abi * ai) / lam2
    ci = (abi * ar - zr * ai) / lam2
    br = br_ref[...]
    bi = bi_ref[...]
    abr_ref[...] = abr
    abi_ref[...] = abi
    bbr_ref[...] = cr * br - ci * bi
    bbi_ref[...] = cr * bi + ci * br


def _s5_discretise(a_re, a_im, log_step, b_re, b_im):
    g, p, q = S5_GROUPS, S5_STATE, S5_GROUP
    rep = lambda t: jnp.repeat(t, q, axis=0)
    ls = jnp.broadcast_to(log_step[:, None], (g, p))
    brt = jnp.transpose(b_re, (0, 2, 1)).reshape(g * q, p)
    bit = jnp.transpose(b_im, (0, 2, 1)).reshape(g * q, p)
    shp = jax.ShapeDtypeStruct((g * q, p), F32)
    spec = pl.BlockSpec((g * q, p), lambda: (0, 0))
    abr, abi, bbr, bbi = pl.pallas_call(
        _s5_disc_body, in_specs=[spec] * 5, out_specs=[spec] * 4, out_shape=[shp] * 4,
        name="s5_disc",
    )(rep(a_re), rep(a_im), rep(ls), brt, bit)
    abar_re = abr[::q].reshape(1, S5_CH)
    abar_im = abi[::q].reshape(1, S5_CH)
    return abar_re, abar_im, bbr.reshape(g, q, p), bbi.reshape(g, q, p)


def _block_diag(t, nblk):
    g, a, b = t.shape
    gb = g // nblk
    t = t.reshape(nblk, gb, a, b)
    eye = jnp.eye(gb, dtype=t.dtype)
    out = t[:, :, :, None, :] * eye[None, :, None, :, None]
    return out.reshape(nblk, gb * a, gb * b)


def _s5_perm(nb, tl):
    r = np.arange(nb * tl)
    p = np.zeros((nb * tl, nb * tl), np.float32)
    p[r, (r % nb) * tl + r // nb] = 1.0
    return p


def _s5_body(*refs, nsrc, nb, has_init, permute):
    u_refs, refs = refs[:nsrc], refs[nsrc:]
    if permute:
        pm_ref, pmt_ref = refs[:2]
        refs = refs[2:]
    bre_ref, bim_ref, cre_ref, cim_ref, ar_ref, ai_ref, d_ref, gw_ref, gb_ref = refs[:9]
    refs = refs[9:]
    if has_init:
        h0r_ref, h0i_ref, y_ref, hr_ref, hi_ref, sre, sim, ysc = refs
    else:
        y_ref, hr_ref, hi_ref, sre, sim, ysc = refs
    rows = sre.shape[0]
    tl = rows // nb
    cb = S5_CH // S5_NBLK
    ub = S5_WIDTH // S5_NBLK
    hw = cb // 2

    @pl.when(pl.program_id(0) == 0)
    def _():
        if has_init:
            hr_ref[...] = h0r_ref[...]
            hi_ref[...] = h0i_ref[...]
        else:
            hr_ref[...] = jnp.zeros_like(hr_ref)
            hi_ref[...] = jnp.zeros_like(hi_ref)

    u = u_refs[0][...] if nsrc == 1 else jnp.concatenate([r[...] for r in u_refs], axis=0)
    if permute:
        moved = _dot(pm_ref[...], jnp.concatenate(_split3(u), axis=1))
        t1 = moved[:, :S5_WIDTH]
        u_tm = t1 + moved[:, S5_WIDTH:2 * S5_WIDTH] + moved[:, 2 * S5_WIDTH:]
        u_b = t1.astype(BF16)
    else:
        u_tm = u
        u_b = u.astype(BF16)
    low_rows = lax.broadcasted_iota(jnp.int32, (SUBLANES, hw), 0) < nb

    for k in range(S5_NBLK):
        uk = u_b[:, k * ub:(k + 1) * ub]
        sre[...] = _dot(uk, bre_ref[k])
        sim[...] = _dot(uk, bim_ref[k])
        for half in range(2):
            c0 = k * cb + half * hw
            l0 = half * hw
            ar8 = jnp.broadcast_to(ar_ref[:, c0:c0 + hw], (SUBLANES, hw))
            ai8 = jnp.broadcast_to(ai_ref[:, c0:c0 + hw], (SUBLANES, hw))

            if nb < SUBLANES:
                def j_body(j, c, l0=l0, ar8=ar8, ai8=ai8):
                    sr, si = c
                    row = pl.multiple_of(j * SUBLANES, SUBLANES)
                    dr = sre[pl.ds(row, SUBLANES), l0:l0 + hw]
                    di = sim[pl.ds(row, SUBLANES), l0:l0 + hw]
                    pr = pltpu.roll(sr, nb, 0)
                    pi = pltpu.roll(si, nb, 0)
                    yr = ar8 * pr - ai8 * pi + dr
                    yi = ar8 * pi + ai8 * pr + di
                    qr = pltpu.roll(yr, nb, 0)
                    qi = pltpu.roll(yi, nb, 0)
                    zr = ar8 * qr - ai8 * qi + dr
                    zi = ar8 * qi + ai8 * qr + di
                    sre[pl.ds(row, SUBLANES), l0:l0 + hw] = jnp.where(low_rows, yr, zr)
                    sim[pl.ds(row, SUBLANES), l0:l0 + hw] = jnp.where(low_rows, yi, zi)
                    return zr, zi

                sr, si = lax.fori_loop(0, rows // SUBLANES, j_body,
                                       (hr_ref[:, c0:c0 + hw], hi_ref[:, c0:c0 + hw]), unroll=True)
                hr_ref[:, c0:c0 + hw] = sr
                hi_ref[:, c0:c0 + hw] = si
            else:
                def rg_body(rg, carry, c0=c0, l0=l0, ar8=ar8, ai8=ai8):
                    r0 = pl.multiple_of(rg * SUBLANES, SUBLANES)
                    xr = hr_ref[pl.ds(r0, SUBLANES), c0:c0 + hw]
                    xi = hi_ref[pl.ds(r0, SUBLANES), c0:c0 + hw]
                    for t in range(tl):
                        row = pl.multiple_of(t * nb + r0, SUBLANES)
                        nr = ar8 * xr - ai8 * xi + sre[pl.ds(row, SUBLANES), l0:l0 + hw]
                        ni = ar8 * xi + ai8 * xr + sim[pl.ds(row, SUBLANES), l0:l0 + hw]
                        sre[pl.ds(row, SUBLANES), l0:l0 + hw] = nr
                        sim[pl.ds(row, SUBLANES), l0:l0 + hw] = ni
                        xr, xi = nr, ni
                    hr_ref[pl.ds(r0, SUBLANES), c0:c0 + hw] = xr
                    hi_ref[pl.ds(r0, SUBLANES), c0:c0 + hw] = xi
                    return carry

                lax.fori_loop(0, nb // SUBLANES, rg_body, 0)
        ysc[:, k * ub:(k + 1) * ub] = (_dot(sre[...].astype(BF16), cre_ref[k])
                                       - _dot(sim[...].astype(BF16), cim_ref[k]))

    y = ysc[...] + d_ref[...] * u_tm
    y = 0.5 * y * (1.0 + jnp.tanh(math.sqrt(2.0 / math.pi) * (y + 0.044715 * (y * y * y))))
    z = _dot(y.astype(BF16), gw_ref[...]) + gb_ref[...]
    out = (y * _sigmoid(z)).astype(BF16)
    if permute:
        out = _dot(pmt_ref[...], out).astype(BF16)
    y_ref[...] = out.reshape(y_ref.shape)


def _s5_call(src, bre, bim, cre, cim, abar_re, abar_im, d_skip, glu_w, glu_b, h0, *,
             row0, nsrc, rs, seq_len, nb, tl, permute):
    rows = nsrc * rs
    ntiles = seq_len // rs if nsrc > 1 else 1
    has_init = h0 is not None
    srows = max(nb, SUBLANES)
    const = lambda shape: pl.BlockSpec(shape, lambda t: (0,) * len(shape))
    in_specs = [pl.BlockSpec((rs, S5_WIDTH), lambda t, s=s: ((row0 + s * seq_len) // rs + t, 0))
                for s in range(nsrc)]
    args = [src] * nsrc
    if permute:
        pm = _s5_perm(nb, tl)
        in_specs += [const((rows, rows)), const((rows, rows))]
        args += [jnp.asarray(pm, BF16), jnp.asarray(pm.T, BF16)]
    in_specs += [
        const(bre.shape), const(bim.shape), const(cre.shape), const(cim.shape),
        const((1, S5_CH)), const((1, S5_CH)), const((1, S5_WIDTH)),
        const((S5_WIDTH, S5_WIDTH)), const((1, S5_WIDTH)),
    ]
    args += [bre, bim, cre, cim, abar_re, abar_im, d_skip, glu_w, glu_b]
    if has_init:
        in_specs += [const((srows, S5_CH)), const((srows, S5_CH))]
        args += [h0[0], h0[1]]
    body = functools.partial(_s5_body, nsrc=nsrc, nb=nb, has_init=has_init, permute=permute)
    return pl.pallas_call(
        body, grid=(ntiles,), in_specs=in_specs,
        out_specs=[pl.BlockSpec((nsrc, rs, S5_WIDTH), lambda t: (0, t, 0)),
                   const((srows, S5_CH)), const((srows, S5_CH))],
        out_shape=[jax.ShapeDtypeStruct((nsrc, ntiles * rs, S5_WIDTH), BF16),
                   jax.ShapeDtypeStruct((srows, S5_CH), F32),
                   jax.ShapeDtypeStruct((srows, S5_CH), F32)],
        scratch_shapes=[pltpu.VMEM((rows, S5_CH // S5_NBLK), F32),
                        pltpu.VMEM((rows, S5_CH // S5_NBLK), F32),
                        pltpu.VMEM((rows, S5_WIDTH), F32)],
        compiler_params=_cparams(("arbitrary",)),
        name="s5_mixer",
    )(*args)


def _gla_body(*refs, nbb, nch, cs, has_init, hps):
    if has_init:
        (q_ref, k_ref, v_ref, g_ref, gl_ref, up_ref, gb_ref, nw_ref, s0_ref, y_ref, s_ref) = refs
    else:
        (q_ref, k_ref, v_ref, g_ref, gl_ref, up_ref, gb_ref, nw_ref, y_ref, s_ref) = refs
    rows = nbb * nch * cs
    sh = int(math.log2(cs))

    @pl.when(pl.program_id(2) == 0)
    def _():
        if has_init:
            s_ref[...] = s0_ref[...]
        else:
            s_ref[...] = jnp.zeros_like(s_ref)

    ri = lax.broadcasted_iota(jnp.int32, (rows, rows), 0)
    ci = lax.broadcasted_iota(jnp.int32, (rows, rows), 1)
    same = (ri >> sh) == (ci >> sh)
    causal = jnp.logical_and(same, ci <= ri)
    lmat = jnp.where(causal, 1.0, 0.0).astype(BF16)
    tmat = jnp.where(same, 1.0, 0.0).astype(BF16)
    rchunk = lax.broadcasted_iota(jnp.int32, (rows, 1), 0) >> sh
    cchunk = lax.broadcasted_iota(jnp.int32, (1, rows), 1) >> sh
    single = nbb * nch == 1
    z = _dot(gl_ref[...].astype(BF16), up_ref[...].astype(BF16)) + gb_ref[...]
    la = _log_sigmoid(z) * (1.0 / GLA_TAU)
    p1, p2, p3 = _split3(la)
    bcum_all = _dot(lmat, p1) + _dot(lmat, p2) + _dot(lmat, p3)
    tot_all = _dot(tmat, p1) + _dot(tmat, p2) + _dot(tmat, p3)

    for hh in range(hps):
        ks = slice(hh * GLA_HK, (hh + 1) * GLA_HK)
        vs = slice(hh * GLA_HV, (hh + 1) * GLA_HV)
        q = q_ref[:, ks] * (GLA_HK ** -0.5)
        k = k_ref[:, ks]
        vb = v_ref[:, vs].astype(BF16)
        bcum = bcum_all[:, ks]
        tot = tot_all[:, ks]

        qd = q * jnp.exp(bcum)
        kd = k * jnp.exp(-bcum)
        kdec = k * jnp.exp(tot - bcum)
        qdb = qd.astype(BF16)
        att = jnp.where(causal, _dot_nt(qdb, kd.astype(BF16)), 0.0)
        o = _dot(att.astype(BF16), vb)

        kdec_t = jnp.transpose(kdec)
        dec_t = jnp.transpose(jnp.exp(tot))
        for b in range(nbb):
            s = s_ref[b, hh]
            for c in range(nch):
                idx = b * nch + c
                oi = _dot(qdb, s.astype(BF16))
                o = o + (oi if single else jnp.where(rchunk == idx, oi, 0.0))
                kt = kdec_t if single else jnp.where(cchunk == idx, kdec_t, 0.0)
                s = dec_t[:, idx * cs:idx * cs + 1] * s + _dot(kt.astype(BF16), vb)
            s_ref[b, hh] = s

        o = o * lax.rsqrt(jnp.mean(o * o, axis=-1, keepdims=True) + EPS) * nw_ref[...]
        gg = g_ref[:, vs]
        y_ref[:, vs] = (o * (gg * _sigmoid(gg))).astype(y_ref.dtype)


def _gla_call(projm, projt, up_pad, gkb, nw, s0, *, row0, nseq_blocks, ntiles, nbb, nch, cs, batch, hps):
    rows = nbb * nch * cs
    has_init = s0 is not None
    rb0 = row0 // rows
    rowblk = lambda s, h, t: rb0 + s * ntiles + t
    wk, wv = hps * GLA_HK, hps * GLA_HV
    qoff = S5_WIDTH // wk
    koff = qoff + GLA_HEADS // hps
    voff = (S5_WIDTH + 2 * GLA_HEADS * GLA_HK) // wv
    goff = voff + GLA_HEADS // hps
    in_specs = [
        pl.BlockSpec((rows, wk), lambda s, h, t: (rowblk(s, h, t), qoff + h)),
        pl.BlockSpec((rows, wk), lambda s, h, t: (rowblk(s, h, t), koff + h)),
        pl.BlockSpec((rows, wv), lambda s, h, t: (rowblk(s, h, t), voff + h)),
        pl.BlockSpec((rows, wv), lambda s, h, t: (rowblk(s, h, t), goff + h)),
        pl.BlockSpec((rows, LANES), lambda s, h, t: (rowblk(s, h, t), 0)),
        pl.BlockSpec((LANES, wk), lambda s, h, t: (0, h)),
        pl.BlockSpec((1, wk), lambda s, h, t: (0, h)),
        pl.BlockSpec((1, GLA_HV), lambda s, h, t: (0, 0)),
    ]
    args = [projm, projm, projm, projm, projt, up_pad, gkb, nw]
    if has_init:
        in_specs.append(pl.BlockSpec((nbb, hps, GLA_HK, GLA_HV), lambda s, h, t: (s, h, 0, 0)))
        args.append(s0)
    body = functools.partial(_gla_body, nbb=nbb, nch=nch, cs=cs, has_init=has_init, hps=hps)
    return pl.pallas_call(
        body, grid=(nseq_blocks, GLA_HEADS // hps, ntiles), in_specs=in_specs,
        out_specs=[pl.BlockSpec((rows, wv), lambda s, h, t: (s * ntiles + t, h)),
                   pl.BlockSpec((nbb, hps, GLA_HK, GLA_HV), lambda s, h, t: (s, h, 0, 0))],
        out_shape=[jax.ShapeDtypeStruct((nseq_blocks * ntiles * rows, GLA_HEADS * GLA_HV), BF16),
                   jax.ShapeDtypeStruct((batch, GLA_HEADS, GLA_HK, GLA_HV), F32)],
        compiler_params=_cparams(("arbitrary", "arbitrary", "arbitrary")),
        name="gla_mixer",
    )(*args)


def _mlstm_body(*refs, nbb, cl, rp, has_init, hps):
    if has_init:
        (q_ref, k_ref, v_ref, og_ref, gt_ref, gb_ref, nw_ref, c0_ref, n0_ref, m0_ref,
         y_ref, c_ref, n_ref, m_ref) = refs
    else:
        (q_ref, k_ref, v_ref, og_ref, gt_ref, gb_ref, nw_ref, y_ref, c_ref, n_ref, m_ref) = refs
    rows = nbb * cl
    sh = int(math.log2(cl))
    head0 = pl.program_id(1) * hps

    @pl.when(pl.program_id(2) == 0)
    def _():
        if has_init:
            c_ref[...] = c0_ref[...]
            n_ref[...] = n0_ref[...]
            m_ref[...] = m0_ref[...]
        else:
            c_ref[...] = jnp.zeros_like(c_ref)
            n_ref[...] = jnp.zeros_like(n_ref)
            m_ref[...] = jnp.full(m_ref.shape, -jnp.inf, F32)

    def pad(x):
        if rp == rows:
            return x
        return jnp.concatenate([x, jnp.zeros((rp - rows, x.shape[1]), x.dtype)], axis=0)

    ksc = ML_DH ** -0.5
    gates = pad(gt_ref[...] + gb_ref[...])
    lane = lax.broadcasted_iota(jnp.int32, (rp, LANES), 1)
    ri = lax.broadcasted_iota(jnp.int32, (rp, rp), 0)
    ci = lax.broadcasted_iota(jnp.int32, (rp, rp), 1)
    causal = jnp.logical_and((ri >> sh) == (ci >> sh), ci <= ri)
    lmat = jnp.where(causal, 1.0, 0.0).astype(BF16)
    neg = jnp.where(causal, 0.0, -jnp.inf)
    er = lax.broadcasted_iota(jnp.int32, (SUBLANES, LANES), 0)
    ec = lax.broadcasted_iota(jnp.int32, (SUBLANES, LANES), 1)
    pick = jnp.where(er == ec, 1.0, 0.0).astype(BF16)
    rbatch = lax.broadcasted_iota(jnp.int32, (rp, 1), 0) >> sh
    cbatch = lax.broadcasted_iota(jnp.int32, (1, rp), 1) >> sh

    for hh in range(hps):
        head = head0 + hh
        hs = slice(hh * ML_DH, (hh + 1) * ML_DH)
        q = pad(q_ref[:, hs])
        k = pad(k_ref[:, hs])
        v = pad(v_ref[:, hs])
        og = pad(og_ref[:, hs])
        ig = jnp.sum(jnp.where(lane == head, gates, 0.0), axis=1, keepdims=True)
        fg = jnp.sum(jnp.where(lane == head + ML_HEADS, gates, 0.0), axis=1, keepdims=True)
        lf = _log_sigmoid(fg)
        fcum = _dot_exact_lhs(lmat, jnp.where(lane == 0, lf, 0.0))[:, 0:1]
        cols = jnp.where(lane == 0, fcum, jnp.where(lane == 1, ig, 0.0))
        as_rows = _dot_nt_exact_lhs(pick, cols)
        fcum_row = as_rows[0:1, :]
        ig_row = as_rows[1:2, :]

        mm = jnp.zeros((rp, 1), F32)
        for b in range(nbb):
            mb = m_ref[b, hh][:, 0:1]
            mm = jnp.where(rbatch == b, mb, mm) if nbb > 1 else jnp.broadcast_to(mb, (rp, 1))

        dmat = fcum - fcum_row + ig_row + neg
        dprev = fcum + mm
        m = jnp.maximum(jnp.max(dmat, axis=1, keepdims=True), dprev)
        w = jnp.exp(dmat - m + math.log(ksc))
        wp = jnp.exp(dprev - m)
        qb = q.astype(BF16)
        kb = k.astype(BF16)
        vb = v.astype(BF16)
        sc = _dot_nt(qb, kb) * w
        num = _dot(sc.astype(BF16), vb)
        den = jnp.sum(sc, axis=1, keepdims=True)

        k_t = jnp.transpose(k)
        for b in range(nbb):
            cm = c_ref[b, hh]
            nm = n_ref[b, hh]
            mb = m_ref[b, hh][:, 0:1]
            qc = wp * _dot(qb, cm.astype(BF16))
            nm8 = jnp.broadcast_to(nm, (SUBLANES, ML_DH)).astype(BF16)
            qn = wp * _dot_nt(qb, nm8)[:, 0:1]
            if nbb > 1:
                rsel = rbatch == b
                qc = jnp.where(rsel, qc, 0.0)
                qn = jnp.where(rsel, qn, 0.0)
            num = num + qc
            den = den + qn
            last = b * cl + cl - 1
            m_new = m[last:last + 1, :]
            fl = fcum[last:last + 1, :]
            decay = jnp.exp(fl + mb - m_new)
            wk_row = ksc * jnp.exp(fl - fcum_row + ig_row - m_new)
            if nbb > 1:
                wk_row = jnp.where(cbatch == b, wk_row, 0.0)
            c_ref[b, hh] = decay * cm + _dot((k_t * wk_row).astype(BF16), vb)
            wk8 = jnp.broadcast_to(wk_row, (SUBLANES, rp)).astype(BF16)
            n_ref[b, hh] = decay * nm + _dot(wk8, kb)[0:1, :]
            m_ref[b, hh] = jnp.broadcast_to(m_new, (1, LANES))

        hv = num / jnp.maximum(jnp.abs(den), jnp.exp(-m))
        hv = hv * _sigmoid(og)
        hv = hv * lax.rsqrt(jnp.mean(hv * hv, axis=-1, keepdims=True) + EPS) * nw_ref[...]
        y_ref[:, hs] = hv[:rows].astype(y_ref.dtype)


def _mlstm_call(projm, projt, gate_b, nw, init, *, row0, nseq_blocks, ntiles, nbb, cl, rp, batch, hps):
    rows = nbb * cl
    has_init = init is not None
    rb0 = row0 // rows
    rowblk = lambda s, h, t: rb0 + s * ntiles + t
    hd = ML_HEADS // hps
    wd = hps * ML_DH
    in_specs = [
        pl.BlockSpec((rows, wd), lambda s, h, t: (rowblk(s, h, t), h)),
        pl.BlockSpec((rows, wd), lambda s, h, t: (rowblk(s, h, t), hd + h)),
        pl.BlockSpec((rows, wd), lambda s, h, t: (rowblk(s, h, t), 2 * hd + h)),
        pl.BlockSpec((rows, wd), lambda s, h, t: (rowblk(s, h, t), 3 * hd + h)),
        pl.BlockSpec((rows, LANES), lambda s, h, t: (rowblk(s, h, t), 0)),
        pl.BlockSpec((1, LANES), lambda s, h, t: (0, 0)),
        pl.BlockSpec((1, ML_DH), lambda s, h, t: (0, 0)),
    ]
    args = [projm, projm, projm, projm, projt, gate_b, nw]
    c_spec = pl.BlockSpec((nbb, hps, ML_DH, ML_DH), lambda s, h, t: (s, h, 0, 0))
    n_spec = pl.BlockSpec((nbb, hps, 1, ML_DH), lambda s, h, t: (s, h, 0, 0))
    m_spec = pl.BlockSpec((nbb, hps, 1, LANES), lambda s, h, t: (s, h, 0, 0))
    if has_init:
        in_specs += [c_spec, n_spec, m_spec]
        args += list(init)
    body = functools.partial(_mlstm_body, nbb=nbb, cl=cl, rp=rp, has_init=has_init, hps=hps)
    return pl.pallas_call(
        body, grid=(nseq_blocks, hd, ntiles), in_specs=in_specs,
        out_specs=[pl.BlockSpec((rows, wd), lambda s, h, t: (s * ntiles + t, h)),
                   c_spec, n_spec, m_spec],
        out_shape=[jax.ShapeDtypeStruct((nseq_blocks * ntiles * rows, ML_HEADS * ML_DH), BF16),
                   jax.ShapeDtypeStruct((batch, ML_HEADS, ML_DH, ML_DH), F32),
                   jax.ShapeDtypeStruct((batch, ML_HEADS, 1, ML_DH), F32),
                   jax.ShapeDtypeStruct((batch, ML_HEADS, 1, LANES), F32)],
        compiler_params=_cparams(("arbitrary", "arbitrary", "arbitrary")),
        name="mlstm_mixer",
    )(*args)


def kernel(x_prompt, x_sample, state_s5_re, state_s5_im, state_gla, state_mlstm_c, state_mlstm_n,
           state_mlstm_m, norm_mix, norm_mlp, norm_final, w_in_even, s5_a_re, s5_a_im, s5_log_step,
           s5_b_re, s5_b_im, s5_c_re, s5_c_im, s5_d, s5_glu_w, s5_glu_b, gla_gk_up, gla_gk_b,
           gla_norm, w_out_even, w_in_odd, mlstm_b_i, mlstm_b_f, mlstm_norm, w_out_odd,
           w_mlp_up, w_mlp_down):
    bp, lp, d = x_prompt.shape
    bs, ls, _ = x_sample.shape
    mp = bp * lp
    msamp = bs * ls
    xp2 = x_prompt.reshape(mp, d)
    xs2 = x_sample.reshape(msamp, d)

    n_even = S5_WIDTH + 2 * GLA_HEADS * GLA_HK + 2 * GLA_HEADS * GLA_HV
    projm, projt = _norm_mm((xp2, xs2), norm_mix[0], w_in_even, 0, n_even, tm=1024)

    abar_re, abar_im, bbr, bbi = _s5_discretise(s5_a_re[0], s5_a_im[0], s5_log_step[0],
                                                s5_b_re[0], s5_b_im[0])
    bre = _block_diag(bbr, S5_NBLK).astype(BF16)
    bim = _block_diag(bbi, S5_NBLK).astype(BF16)
    cre = _block_diag(jnp.transpose(s5_c_re[0], (0, 2, 1)), S5_NBLK).astype(BF16)
    cim = _block_diag(jnp.transpose(s5_c_im[0], (0, 2, 1)), S5_NBLK).astype(BF16)
    s5_consts = (bre, bim, cre, cim, abar_re, abar_im, s5_d[0].reshape(1, S5_WIDTH),
                 s5_glu_w[0].astype(BF16), s5_glu_b[0].reshape(1, S5_WIDTH))

    s5_tl = 64
    ys5_p, hr_p, hi_p = _s5_call(projm, *s5_consts, None, row0=0, nsrc=bp, rs=s5_tl, seq_len=lp,
                                 nb=bp, tl=s5_tl, permute=True)
    ys5_p = ys5_p.reshape(mp, S5_WIDTH)
    hr_p = hr_p[SUBLANES - bp:]
    hi_p = hi_p[SUBLANES - bp:]
    u_s = jnp.transpose(projm[mp:mp + msamp, :S5_WIDTH].reshape(bs, ls, S5_WIDTH), (1, 0, 2))
    h0 = (state_s5_re[0].reshape(bs, S5_CH), state_s5_im[0].reshape(bs, S5_CH))
    ys5_s, hr_s, hi_s = _s5_call(u_s.reshape(msamp, S5_WIDTH), *s5_consts, h0, row0=0, nsrc=1, rs=msamp,
                                 seq_len=msamp, nb=bs, tl=ls, permute=False)
    ys5_s = jnp.transpose(ys5_s.reshape(ls, bs, S5_WIDTH), (1, 0, 2)).reshape(msamp, S5_WIDTH)

    up_pad = jnp.pad(gla_gk_up[0], ((0, LANES - gla_gk_up.shape[1]), (0, 0)))
    gkb = gla_gk_b[0].reshape(1, -1)
    gnw = gla_norm[0].reshape(1, GLA_HV)
    ptile = 256
    ygla_p, sg_p = _gla_call(projm, projt, up_pad, gkb, gnw, None, row0=0, nseq_blocks=bp,
                             ntiles=lp // ptile, nbb=1, nch=ptile // GLA_CHUNK, cs=GLA_CHUNK, batch=bp,
                             hps=GLA_HEADS)
    gnbb = 32
    ygla_s, sg_s = _gla_call(projm, projt, up_pad, gkb, gnw, state_gla[0], row0=mp,
                             nseq_blocks=bs // gnbb, ntiles=1, nbb=gnbb, nch=1, cs=ls, batch=bs, hps=1)

    h = _mm_out([ys5_p, ygla_p], [ys5_s, ygla_s], w_out_even, 0, (xp2, 0), (xs2, 0))
    h = _mlp(h, norm_mlp[0], w_mlp_up, w_mlp_down, 0)

    n_odd = 4 * ML_HEADS * ML_DH
    projm, projt = _norm_mm(h, norm_mix[1], w_in_odd, 0, n_odd)
    gate_b = jnp.pad(jnp.concatenate([mlstm_b_i[0], mlstm_b_f[0]]), (0, LANES - 2 * ML_HEADS)).reshape(1, LANES)
    mnw = mlstm_norm[0].reshape(1, ML_DH)
    yml_p, c_p, n_p, m_p = _mlstm_call(projm, projt, gate_b, mnw, None, row0=0, nseq_blocks=bp,
                                       ntiles=lp // ML_CHUNK, nbb=1, cl=ML_CHUNK, rp=ML_CHUNK, batch=bp,
                                       hps=ML_HEADS)
    mnbb = 8
    init = (state_mlstm_c[0], state_mlstm_n[0].reshape(bs, ML_HEADS, 1, ML_DH),
            jnp.broadcast_to(state_mlstm_m[0][:, :, None, None], (bs, ML_HEADS, 1, LANES)))
    yml_s, c_s, n_s, m_s = _mlstm_call(projm, projt, gate_b, mnw, init, row0=mp, nseq_blocks=bs // mnbb,
                                       ntiles=1, nbb=mnbb, cl=ls, rp=LANES, batch=bs, hps=1)
    h = _mm_out([yml_p], [yml_s], w_out_odd, 0, (h, 0), (h, mp // 512))
    h = _mlp(h, norm_mlp[1], w_mlp_up, w_mlp_down, 1)

    y_p, y_s = _rmsnorm(h, norm_final, mp)
    g, p = S5_GROUPS, S5_STATE
    return (y_p.reshape(bp, lp, d), y_s.reshape(bs, ls, d),
            hr_p.reshape(1, bp, g, p), hi_p.reshape(1, bp, g, p), sg_p[None],
            c_p[None], n_p.reshape(1, bp, ML_HEADS, ML_DH), m_p[:, :, 0, 0][None],
            hr_s.reshape(1, bs, g, p), hi_s.reshape(1, bs, g, p), sg_s[None],
            c_s[None], n_s.reshape(1, bs, ML_HEADS, ML_DH), m_s[:, :, 0, 0][None])
```

```python
import functools
import math

import jax
import jax.numpy as jnp
import numpy as np
from jax import lax
from jax.experimental import pallas as pl
from jax.experimental.pallas import tpu as pltpu

F32 = jnp.float32
BF16 = jnp.bfloat16
EPS = 1e-6

D_MODEL = 2048
D_FF = 4 * D_MODEL
S5_WIDTH = 1024
S5_GROUPS = 64
S5_GROUP = 16
S5_STATE = 64
S5_CH = S5_GROUPS * S5_STATE
S5_NBLK = 4
GLA_HEADS = 4
GLA_HK = 128
GLA_HV = 256
GLA_TAU = 16.0
GLA_CHUNK = 64
ML_HEADS = 4
ML_DH = 512
ML_CHUNK = 256

SUBLANES = 8
LANES = 128
VMEM_LIMIT = 56 * 1024 * 1024


def _cparams(sem):
    return pltpu.CompilerParams(dimension_semantics=sem, vmem_limit_bytes=VMEM_LIMIT)


def _dot(a, b):
    return jnp.dot(a, b, preferred_element_type=F32)


def _dot_nt(a, b):
    return lax.dot_general(a, b, (((1,), (1,)), ((), ())), preferred_element_type=F32)


def _split3(x):
    p1 = x.astype(BF16)
    r1 = x - p1.astype(F32)
    p2 = r1.astype(BF16)
    r2 = r1 - p2.astype(F32)
    p3 = r2.astype(BF16)
    return p1, p2, p3


def _dot_exact_lhs(m, x):
    p1, p2, p3 = _split3(x)
    return _dot(m, p1) + _dot(m, p2) + _dot(m, p3)


def _dot_nt_exact_lhs(m, x):
    p1, p2, p3 = _split3(x)
    return _dot_nt(m, p1) + _dot_nt(m, p2) + _dot_nt(m, p3)


def _log_sigmoid(x):
    return jnp.minimum(x, 0.0) - jnp.log1p(jnp.exp(-jnp.abs(x)))


def _sigmoid(x):
    return 1.0 / (1.0 + jnp.exp(-x))


def _norm_mm_body(*refs, act, n_tail, n_ptiles, tm, rchunk):
    dual = n_ptiles is not None
    has_tail = n_tail > 0
    x_ref = refs[0]
    xs_ref = refs[1] if dual else None
    rest = refs[2:] if dual else refs[1:]
    if has_tail:
        g_ref, w_ref, wt_ref, o_ref, ot_ref, xn_ref = rest
    else:
        g_ref, w_ref, o_ref, xn_ref = rest

    def normalise(src_ref, nrows):
        g = g_ref[...]
        for r in range(0, nrows, rchunk):
            x = src_ref[r:r + rchunk, :]
            ms = jnp.mean(x * x, axis=-1, keepdims=True)
            xn_ref[r:r + rchunk, :] = (x * lax.rsqrt(ms + EPS) * g).astype(BF16)

    @pl.when(pl.program_id(1) == 0)
    def _():
        if dual:
            @pl.when(pl.program_id(0) < n_ptiles)
            def _():
                normalise(x_ref, tm)

            @pl.when(pl.program_id(0) >= n_ptiles)
            def _():
                ns = xs_ref.shape[0]
                normalise(xs_ref, ns)
                xn_ref[ns:, :] = jnp.zeros((tm - ns, xn_ref.shape[1]), BF16)
        else:
            normalise(x_ref, tm)
        if has_tail:
            col = lax.broadcasted_iota(jnp.int32, wt_ref.shape, 0)
            wt = jnp.where(col < n_tail, wt_ref[...], 0.0)
            ot_ref[...] = _dot_nt(xn_ref[...], wt.astype(BF16))

    mm = _dot_nt if has_tail else _dot
    acc = mm(xn_ref[...], w_ref[...].astype(BF16))
    if act == "relu2":
        acc = jnp.square(jnp.maximum(acc, 0.0))
    o_ref[...] = acc.astype(o_ref.dtype)


def _norm_mm(x, g, w3, layer, n_main, *, act=None, out_dtype=F32, tm=1088, tn=512):
    dual = isinstance(x, tuple)
    if dual:
        xp, xs = x
        n_ptiles = xp.shape[0] // tm
        m, kdim = (n_ptiles + 1) * tm, xp.shape[1]
        grid = (n_ptiles + 1, n_main // tn)
        in_specs = [
            pl.BlockSpec((tm, kdim), lambda i, j: (jnp.minimum(i, n_ptiles - 1), 0)),
            pl.BlockSpec(xs.shape, lambda i, j: (0, 0)),
        ]
        args = [xp, xs]
    else:
        n_ptiles = None
        m, kdim = x.shape
        grid = (m // tm, n_main // tn)
        in_specs = [pl.BlockSpec((tm, kdim), lambda i, j: (i, 0))]
        args = [x]
    n_tail = w3.shape[2] - n_main
    in_specs.append(pl.BlockSpec((1, kdim), lambda i, j: (0, 0)))
    args.append(g.reshape(1, kdim))
    out_shape = [jax.ShapeDtypeStruct((m, n_main), out_dtype)]
    out_specs = [pl.BlockSpec((tm, tn), lambda i, j: (i, j))]
    if n_tail > 0:
        w3t = jnp.transpose(w3, (0, 2, 1))
        in_specs += [pl.BlockSpec((None, tn, kdim), lambda i, j: (layer, j, 0)),
                     pl.BlockSpec((None, LANES, kdim), lambda i, j: (layer, n_main // LANES, 0))]
        args += [w3t, w3t]
        out_shape.append(jax.ShapeDtypeStruct((m, LANES), F32))
        out_specs.append(pl.BlockSpec((tm, LANES), lambda i, j: (i, 0)))
    else:
        in_specs.append(pl.BlockSpec((None, kdim, tn), lambda i, j: (layer, 0, j)))
        args.append(w3)
    body = functools.partial(_norm_mm_body, act=act, n_tail=n_tail, n_ptiles=n_ptiles,
                             tm=tm, rchunk=tm // 4)
    res = pl.pallas_call(
        body, grid=grid, in_specs=in_specs, out_specs=out_specs, out_shape=out_shape,
        scratch_shapes=[pltpu.VMEM((tm, kdim), BF16)],
        compiler_params=_cparams(("arbitrary", "arbitrary")),
        name="norm_mm",
    )(*args)
    return res if n_tail > 0 else res[0]


def _mm_res_body(a_ref, w_ref, r_ref, o_ref):
    @pl.when(pl.program_id(2) == 0)
    def _():
        o_ref[...] = r_ref[...] + _dot(a_ref[...].astype(BF16), w_ref[...].astype(BF16))

    @pl.when(pl.program_id(2) > 0)
    def _():
        o_ref[...] = o_ref[...] + _dot(a_ref[...].astype(BF16), w_ref[...].astype(BF16))


def _mm_res(a, w3, layer, res, *, tm=1088, tn=1024, tk=1024):
    m, kdim = a.shape
    n = w3.shape[2]
    grid = (m // tm, n // tn, kdim // tk)
    return pl.pallas_call(
        _mm_res_body, grid=grid,
        in_specs=[
            pl.BlockSpec((tm, tk), lambda i, j, k: (i, k)),
            pl.BlockSpec((None, tk, tn), lambda i, j, k: (layer, k, j)),
            pl.BlockSpec((tm, tn), lambda i, j, k: (i, j)),
        ],
        out_specs=pl.BlockSpec((tm, tn), lambda i, j, k: (i, j)),
        out_shape=jax.ShapeDtypeStruct((m, n), F32),
        compiler_params=_cparams(("arbitrary", "arbitrary", "arbitrary")),
        name="mm_res",
    )(a, w3, res)


def _mm_out_body(*refs, kcs, n_ptiles):
    n = len(kcs)
    ap, asm = refs[:n], refs[n:2 * n]
    w_ref, rp_ref, rs_ref, o_ref, wb_ref = refs[2 * n:]
    i = pl.program_id(1)

    @pl.when(i == 0)
    def _():
        wb_ref[...] = w_ref[...].astype(BF16)

    def compute(srcs, r_ref):
        acc = r_ref[...]
        off = 0
        for a_ref, kc in zip(srcs, kcs):
            acc = acc + _dot(a_ref[...], wb_ref[off:off + kc, :])
            off += kc
        o_ref[...] = acc

    @pl.when(i < n_ptiles)
    def _():
        compute(ap, rp_ref)

    @pl.when(i >= n_ptiles)
    def _():
        compute(asm, rs_ref)


def _mm_out(a_prompt, a_sample, w3, layer, res_prompt, res_sample, *, tm=512, tn=1024):
    kcs = tuple(a.shape[1] for a in a_prompt)
    kdim, n = sum(kcs), w3.shape[2]
    n_ptiles = a_prompt[0].shape[0] // tm
    m = a_prompt[0].shape[0] + a_sample[0].shape[0]
    (rp, rp0), (rs, rs0) = res_prompt, res_sample
    pidx = lambda i: jnp.minimum(i, n_ptiles - 1)
    in_specs = ([pl.BlockSpec((tm, kc), lambda j, i: (pidx(i), 0)) for kc in kcs]
                + [pl.BlockSpec((tm, kc), lambda j, i: (0, 0)) for kc in kcs]
                + [pl.BlockSpec((None, kdim, tn), lambda j, i: (layer, 0, j)),
                   pl.BlockSpec((tm, tn), lambda j, i: (rp0 + pidx(i), j)),
                   pl.BlockSpec((tm, tn), lambda j, i: (rs0, j))])
    body = functools.partial(_mm_out_body, kcs=kcs, n_ptiles=n_ptiles)
    return pl.pallas_call(
        body, grid=(n // tn, n_ptiles + 1), in_specs=in_specs,
        out_specs=pl.BlockSpec((tm, tn), lambda j, i: (i, j)),
        out_shape=jax.ShapeDtypeStruct((m, n), F32),
        scratch_shapes=[pltpu.VMEM((kdim, tn), BF16)],
        compiler_params=_cparams(("arbitrary", "arbitrary")),
        name="mm_out",
    )(*a_prompt, *a_sample, w3, rp, rs)


def _rmsnorm_body(x_ref, g_ref, op_ref, os_ref, *, n_ptiles):
    x = x_ref[...]
    ms = jnp.mean(x * x, axis=-1, keepdims=True)
    y = x * lax.rsqrt(ms + EPS) * g_ref[...]

    @pl.when(pl.program_id(0) < n_ptiles)
    def _():
        op_ref[...] = y

    @pl.when(pl.program_id(0) >= n_ptiles)
    def _():
        os_ref[...] = y


def _rmsnorm(x, g, m_prompt, *, tm=512):
    m, d = x.shape
    n_ptiles = m_prompt // tm
    return pl.pallas_call(
        functools.partial(_rmsnorm_body, n_ptiles=n_ptiles), grid=(n_ptiles + 1,),
        in_specs=[pl.BlockSpec((tm, d), lambda i: (i, 0)), pl.BlockSpec((1, d), lambda i: (0, 0))],
        out_specs=[pl.BlockSpec((tm, d), lambda i: (jnp.minimum(i, n_ptiles - 1), 0)),
                   pl.BlockSpec((tm, d), lambda i: (0, 0))],
        out_shape=[jax.ShapeDtypeStruct((m_prompt, d), F32),
                   jax.ShapeDtypeStruct((m - m_prompt, d), F32)],
        compiler_params=_cparams(("arbitrary",)),
        name="final_norm",
    )(x, g.reshape(1, d))


def _s5_disc_body(ar_ref, ai_ref, ls_ref, br_ref, bi_ref, abr_ref, abi_ref, bbr_ref, bbi_ref):
    ar = ar_ref[...]
    ai = ai_ref[...]
    dt = jnp.exp(ls_ref[...])
    mag = jnp.exp(ar * dt)
    abr = mag * jnp.cos(ai * dt)
    abi = mag * jnp.sin(ai * dt)
    lam2 = ar * ar + ai * ai
    zr = abr - 1.0
    cr = (zr * ar + abi * ai) / lam2
    ci = (abi * ar - zr * ai) / lam2
    br = br_ref[...]
    bi = bi_ref[...]
    abr_ref[...] = abr
    abi_ref[...] = abi
    bbr_ref[...] = cr * br - ci * bi
    bbi_ref[...] = cr * bi + ci * br


def _s5_discretise(a_re, a_im, log_step, b_re, b_im):
    g, p, q = S5_GROUPS, S5_STATE, S5_GROUP
    rep = lambda t: jnp.repeat(t, q, axis=0)
    ls = jnp.broadcast_to(log_step[:, None], (g, p))
    brt = jnp.transpose(b_re, (0, 2, 1)).reshape(g * q, p)
    bit = jnp.transpose(b_im, (0, 2, 1)).reshape(g * q, p)
    shp = jax.ShapeDtypeStruct((g * q, p), F32)
    spec = pl.BlockSpec((g * q, p), lambda: (0, 0))
    abr, abi, bbr, bbi = pl.pallas_call(
        _s5_disc_body, in_specs=[spec] * 5, out_specs=[spec] * 4, out_shape=[shp] * 4,
        name="s5_disc",
    )(rep(a_re), rep(a_im), rep(ls), brt, bit)
    abar_re = abr[::q].reshape(1, S5_CH)
    abar_im = abi[::q].reshape(1, S5_CH)
    return abar_re, abar_im, bbr.reshape(g, q, p), bbi.reshape(g, q, p)


def _block_diag(t, nblk):
    g, a, b = t.shape
    gb = g // nblk
    t = t.reshape(nblk, gb, a, b)
    eye = jnp.eye(gb, dtype=t.dtype)
    out = t[:, :, :, None, :] * eye[None, :, None, :, None]
    return out.reshape(nblk, gb * a, gb * b)


def _s5_perm(nb, tl):
    r = np.arange(nb * tl)
    p = np.zeros((nb * tl, nb * tl), np.float32)
    p[r, (r % nb) * tl + r // nb] = 1.0
    return p


def _s5_body(*refs, nsrc, nb, has_init, permute):
    u_refs, refs = refs[:nsrc], refs[nsrc:]
    if permute:
        pm_ref, pmt_ref = refs[:2]
        refs = refs[2:]
    bre_ref, bim_ref, cre_ref, cim_ref, ar_ref, ai_ref, d_ref, gw_ref, gb_ref = refs[:9]
    refs = refs[9:]
    if has_init:
        h0r_ref, h0i_ref, y_ref, hr_ref, hi_ref, sre, sim, ysc = refs
    else:
        y_ref, hr_ref, hi_ref, sre, sim, ysc = refs
    rows = sre.shape[0]
    tl = rows // nb
    cb = S5_CH // S5_NBLK
    ub = S5_WIDTH // S5_NBLK
    hw = cb // 2

    @pl.when(pl.program_id(0) == 0)
    def _():
        if has_init:
            hr_ref[...] = h0r_ref[...]
            hi_ref[...] = h0i_ref[...]
        else:
            hr_ref[...] = jnp.zeros_like(hr_ref)
            hi_ref[...] = jnp.zeros_like(hi_ref)

    u_b = u_refs[0][...] if nsrc == 1 else jnp.concatenate([r[...] for r in u_refs], axis=0)
    if permute:
        u_tm = _dot(pm_ref[...], u_b)
        u_b = u_tm.astype(BF16)
    else:
        u_tm = u_b.astype(F32)
    low_rows = lax.broadcasted_iota(jnp.int32, (SUBLANES, hw), 0) < nb

    for k in range(S5_NBLK):
        uk = u_b[:, k * ub:(k + 1) * ub]
        sre[...] = _dot(uk, bre_ref[k])
        sim[...] = _dot(uk, bim_ref[k])
        for half in range(2):
            c0 = k * cb + half * hw
            l0 = half * hw
            ar8 = jnp.broadcast_to(ar_ref[:, c0:c0 + hw], (SUBLANES, hw))
            ai8 = jnp.broadcast_to(ai_ref[:, c0:c0 + hw], (SUBLANES, hw))

            if nb < SUBLANES:
                def j_body(j, c, l0=l0, ar8=ar8, ai8=ai8):
                    sr, si = c
                    row = pl.multiple_of(j * SUBLANES, SUBLANES)
                    dr = sre[pl.ds(row, SUBLANES), l0:l0 + hw]
                    di = sim[pl.ds(row, SUBLANES), l0:l0 + hw]
                    pr = pltpu.roll(sr, nb, 0)
                    pi = pltpu.roll(si, nb, 0)
                    yr = ar8 * pr - ai8 * pi + dr
                    yi = ar8 * pi + ai8 * pr + di
                    qr = pltpu.roll(yr, nb, 0)
                    qi = pltpu.roll(yi, nb, 0)
                    zr = ar8 * qr - ai8 * qi + dr
                    zi = ar8 * qi + ai8 * qr + di
                    sre[pl.ds(row, SUBLANES), l0:l0 + hw] = jnp.where(low_rows, yr, zr)
                    sim[pl.ds(row, SUBLANES), l0:l0 + hw] = jnp.where(low_rows, yi, zi)
                    return zr, zi

                sr, si = lax.fori_loop(0, rows // SUBLANES, j_body,
                                       (hr_ref[:, c0:c0 + hw], hi_ref[:, c0:c0 + hw]), unroll=True)
                hr_ref[:, c0:c0 + hw] = sr
                hi_ref[:, c0:c0 + hw] = si
            else:
                def rg_body(rg, carry, c0=c0, l0=l0, ar8=ar8, ai8=ai8):
                    r0 = pl.multiple_of(rg * SUBLANES, SUBLANES)
                    xr = hr_ref[pl.ds(r0, SUBLANES), c0:c0 + hw]
                    xi = hi_ref[pl.ds(r0, SUBLANES), c0:c0 + hw]
                    for t in range(tl):
                        row = pl.multiple_of(t * nb + r0, SUBLANES)
                        nr = ar8 * xr - ai8 * xi + sre[pl.ds(row, SUBLANES), l0:l0 + hw]
                        ni = ar8 * xi + ai8 * xr + sim[pl.ds(row, SUBLANES), l0:l0 + hw]
                        sre[pl.ds(row, SUBLANES), l0:l0 + hw] = nr
                        sim[pl.ds(row, SUBLANES), l0:l0 + hw] = ni
                        xr, xi = nr, ni
                    hr_ref[pl.ds(r0, SUBLANES), c0:c0 + hw] = xr
                    hi_ref[pl.ds(r0, SUBLANES), c0:c0 + hw] = xi
                    return carry

                lax.fori_loop(0, nb // SUBLANES, rg_body, 0)
        ysc[:, k * ub:(k + 1) * ub] = (_dot(sre[...].astype(BF16), cre_ref[k])
                                       - _dot(sim[...].astype(BF16), cim_ref[k]))

    y = ysc[...] + d_ref[...] * u_tm
    y = 0.5 * y * (1.0 + jnp.tanh(math.sqrt(2.0 / math.pi) * (y + 0.044715 * (y * y * y))))
    z = _dot(y.astype(BF16), gw_ref[...]) + gb_ref[...]
    out = (y * _sigmoid(z)).astype(BF16)
    if permute:
        out = _dot(pmt_ref[...], out).astype(BF16)
    y_ref[...] = out.reshape(y_ref.shape)


def _s5_call(src, bre, bim, cre, cim, abar_re, abar_im, d_skip, glu_w, glu_b, h0, *,
             row0, nsrc, rs, seq_len, nb, tl, permute):
    rows = nsrc * rs
    ntiles = seq_len // rs if nsrc > 1 else 1
    has_init = h0 is not None
    srows = max(nb, SUBLANES)
    const = lambda shape: pl.BlockSpec(shape, lambda t: (0,) * len(shape))
    in_specs = [pl.BlockSpec((rs, S5_WIDTH), lambda t, s=s: ((row0 + s * seq_len) // rs + t, 0))
                for s in range(nsrc)]
    args = [src] * nsrc
    if permute:
        pm = _s5_perm(nb, tl)
        in_specs += [const((rows, rows)), const((rows, rows))]
        args += [jnp.asarray(pm, BF16), jnp.asarray(pm.T, BF16)]
    in_specs += [
        const(bre.shape), const(bim.shape), const(cre.shape), const(cim.shape),
        const((1, S5_CH)), const((1, S5_CH)), const((1, S5_WIDTH)),
        const((S5_WIDTH, S5_WIDTH)), const((1, S5_WIDTH)),
    ]
    args += [bre, bim, cre, cim, abar_re, abar_im, d_skip, glu_w, glu_b]
    if has_init:
        in_specs += [const((srows, S5_CH)), const((srows, S5_CH))]
        args += [h0[0], h0[1]]
    body = functools.partial(_s5_body, nsrc=nsrc, nb=nb, has_init=has_init, permute=permute)
    return pl.pallas_call(
        body, grid=(ntiles,), in_specs=in_specs,
        out_specs=[pl.BlockSpec((nsrc, rs, S5_WIDTH), lambda t: (0, t, 0)),
                   const((srows, S5_CH)), const((srows, S5_CH))],
        out_shape=[jax.ShapeDtypeStruct((nsrc, ntiles * rs, S5_WIDTH), BF16),
                   jax.ShapeDtypeStruct((srows, S5_CH), F32),
                   jax.ShapeDtypeStruct((srows, S5_CH), F32)],
        scratch_shapes=[pltpu.VMEM((rows, S5_CH // S5_NBLK), F32),
                        pltpu.VMEM((rows, S5_CH // S5_NBLK), F32),
                        pltpu.VMEM((rows, S5_WIDTH), F32)],
        compiler_params=_cparams(("arbitrary",)),
        name="s5_mixer",
    )(*args)


def _gla_body(*refs, nbb, nch, cs, has_init, hps):
    if has_init:
        (q_ref, k_ref, v_ref, g_ref, gl_ref, up_ref, gb_ref, nw_ref, s0_ref, y_ref, s_ref) = refs
    else:
        (q_ref, k_ref, v_ref, g_ref, gl_ref, up_ref, gb_ref, nw_ref, y_ref, s_ref) = refs
    rows = nbb * nch * cs
    sh = int(math.log2(cs))

    @pl.when(pl.program_id(2) == 0)
    def _():
        if has_init:
            s_ref[...] = s0_ref[...]
        else:
            s_ref[...] = jnp.zeros_like(s_ref)

    ri = lax.broadcasted_iota(jnp.int32, (rows, rows), 0)
    ci = lax.broadcasted_iota(jnp.int32, (rows, rows), 1)
    same = (ri >> sh) == (ci >> sh)
    causal = jnp.logical_and(same, ci <= ri)
    lmat = jnp.where(causal, 1.0, 0.0).astype(BF16)
    tmat = jnp.where(same, 1.0, 0.0).astype(BF16)
    rchunk = lax.broadcasted_iota(jnp.int32, (rows, 1), 0) >> sh
    cchunk = lax.broadcasted_iota(jnp.int32, (1, rows), 1) >> sh
    single = nbb * nch == 1
    z = _dot(gl_ref[...].astype(BF16), up_ref[...].astype(BF16)) + gb_ref[...]
    la = _log_sigmoid(z) * (1.0 / GLA_TAU)
    p1, p2, p3 = _split3(la)
    bcum_all = _dot(lmat, p1) + _dot(lmat, p2) + _dot(lmat, p3)
    tot_all = _dot(tmat, p1) + _dot(tmat, p2) + _dot(tmat, p3)

    for hh in range(hps):
        ks = slice(hh * GLA_HK, (hh + 1) * GLA_HK)
        vs = slice(hh * GLA_HV, (hh + 1) * GLA_HV)
        q = q_ref[:, ks].astype(F32) * (GLA_HK ** -0.5)
        k = k_ref[:, ks].astype(F32)
        vb = v_ref[:, vs]
        bcum = bcum_all[:, ks]
        tot = tot_all[:, ks]

        qd = q * jnp.exp(bcum)
        kd = k * jnp.exp(-bcum)
        kdec = k * jnp.exp(tot - bcum)
        qdb = qd.astype(BF16)
        att = jnp.where(causal, _dot_nt(qdb, kd.astype(BF16)), 0.0)
        o = _dot(att.astype(BF16), vb)

        kdec_t = jnp.transpose(kdec)
        dec_t = jnp.transpose(jnp.exp(tot))
        for b in range(nbb):
            s = s_ref[b, hh]
            for c in range(nch):
                idx = b * nch + c
                oi = _dot(qdb, s.astype(BF16))
                o = o + (oi if single else jnp.where(rchunk == idx, oi, 0.0))
                kt = kdec_t if single else jnp.where(cchunk == idx, kdec_t, 0.0)
                s = dec_t[:, idx * cs:idx * cs + 1] * s + _dot(kt.astype(BF16), vb)
            s_ref[b, hh] = s

        o = o * lax.rsqrt(jnp.mean(o * o, axis=-1, keepdims=True) + EPS) * nw_ref[...]
        gg = g_ref[:, vs].astype(F32)
        y_ref[:, vs] = (o * (gg * _sigmoid(gg))).astype(y_ref.dtype)


def _gla_call(projm, projt, up_pad, gkb, nw, s0, *, row0, nseq_blocks, ntiles, nbb, nch, cs, batch, hps):
    rows = nbb * nch * cs
    has_init = s0 is not None
    rb0 = row0 // rows
    rowblk = lambda s, h, t: rb0 + s * ntiles + t
    wk, wv = hps * GLA_HK, hps * GLA_HV
    qoff = S5_WIDTH // wk
    koff = qoff + GLA_HEADS // hps
    voff = (S5_WIDTH + 2 * GLA_HEADS * GLA_HK) // wv
    goff = voff + GLA_HEADS // hps
    in_specs = [
        pl.BlockSpec((rows, wk), lambda s, h, t: (rowblk(s, h, t), qoff + h)),
        pl.BlockSpec((rows, wk), lambda s, h, t: (rowblk(s, h, t), koff + h)),
        pl.BlockSpec((rows, wv), lambda s, h, t: (rowblk(s, h, t), voff + h)),
        pl.BlockSpec((rows, wv), lambda s, h, t: (rowblk(s, h, t), goff + h)),
        pl.BlockSpec((rows, LANES), lambda s, h, t: (rowblk(s, h, t), 0)),
        pl.BlockSpec((LANES, wk), lambda s, h, t: (0, h)),
        pl.BlockSpec((1, wk), lambda s, h, t: (0, h)),
        pl.BlockSpec((1, GLA_HV), lambda s, h, t: (0, 0)),
    ]
    args = [projm, projm, projm, projm, projt, up_pad, gkb, nw]
    if has_init:
        in_specs.append(pl.BlockSpec((nbb, hps, GLA_HK, GLA_HV), lambda s, h, t: (s, h, 0, 0)))
        args.append(s0)
    body = functools.partial(_gla_body, nbb=nbb, nch=nch, cs=cs, has_init=has_init, hps=hps)
    return pl.pallas_call(
        body, grid=(nseq_blocks, GLA_HEADS // hps, ntiles), in_specs=in_specs,
        out_specs=[pl.BlockSpec((rows, wv), lambda s, h, t: (s * ntiles + t, h)),
                   pl.BlockSpec((nbb, hps, GLA_HK, GLA_HV), lambda s, h, t: (s, h, 0, 0))],
        out_shape=[jax.ShapeDtypeStruct((nseq_blocks * ntiles * rows, GLA_HEADS * GLA_HV), BF16),
                   jax.ShapeDtypeStruct((batch, GLA_HEADS, GLA_HK, GLA_HV), F32)],
        compiler_params=_cparams(("arbitrary", "arbitrary", "arbitrary")),
        name="gla_mixer",
    )(*args)


def _mlstm_body(*refs, nbb, cl, rp, has_init, hps):
    if has_init:
        (q_ref, k_ref, v_ref, og_ref, gt_ref, gb_ref, nw_ref, c0_ref, n0_ref, m0_ref,
         y_ref, c_ref, n_ref, m_ref) = refs
    else:
        (q_ref, k_ref, v_ref, og_ref, gt_ref, gb_ref, nw_ref, y_ref, c_ref, n_ref, m_ref) = refs
    rows = nbb * cl
    sh = int(math.log2(cl))
    head0 = pl.program_id(1) * hps

    @pl.when(pl.program_id(2) == 0)
    def _():
        if has_init:
            c_ref[...] = c0_ref[...]
            n_ref[...] = n0_ref[...]
            m_ref[...] = m0_ref[...]
        else:
            c_ref[...] = jnp.zeros_like(c_ref)
            n_ref[...] = jnp.zeros_like(n_ref)
            m_ref[...] = jnp.full(m_ref.shape, -jnp.inf, F32)

    def pad(x):
        if rp == rows:
            return x
        return jnp.concatenate([x, jnp.zeros((rp - rows, x.shape[1]), x.dtype)], axis=0)

    ksc = ML_DH ** -0.5
    gates = pad(gt_ref[...] + gb_ref[...])
    lane = lax.broadcasted_iota(jnp.int32, (rp, LANES), 1)
    ri = lax.broadcasted_iota(jnp.int32, (rp, rp), 0)
    ci = lax.broadcasted_iota(jnp.int32, (rp, rp), 1)
    causal = jnp.logical_and((ri >> sh) == (ci >> sh), ci <= ri)
    lmat = jnp.where(causal, 1.0, 0.0).astype(BF16)
    neg = jnp.where(causal, 0.0, -jnp.inf)
    er = lax.broadcasted_iota(jnp.int32, (SUBLANES, LANES), 0)
    ec = lax.broadcasted_iota(jnp.int32, (SUBLANES, LANES), 1)
    pick = jnp.where(er == ec, 1.0, 0.0).astype(BF16)
    rbatch = lax.broadcasted_iota(jnp.int32, (rp, 1), 0) >> sh
    cbatch = lax.broadcasted_iota(jnp.int32, (1, rp), 1) >> sh

    for hh in range(hps):
        head = head0 + hh
        hs = slice(hh * ML_DH, (hh + 1) * ML_DH)
        qb = pad(q_ref[:, hs])
        kb = pad(k_ref[:, hs])
        vb = pad(v_ref[:, hs])
        og = pad(og_ref[:, hs]).astype(F32)
        ig = jnp.sum(jnp.where(lane == head, gates, 0.0), axis=1, keepdims=True)
        fg = jnp.sum(jnp.where(lane == head + ML_HEADS, gates, 0.0), axis=1, keepdims=True)
        lf = _log_sigmoid(fg)
        fcum = _dot_exact_lhs(lmat, jnp.where(lane == 0, lf, 0.0))[:, 0:1]
        cols = jnp.where(lane == 0, fcum, jnp.where(lane == 1, ig, 0.0))
        as_rows = _dot_nt_exact_lhs(pick, cols)
        fcum_row = as_rows[0:1, :]
        ig_row = as_rows[1:2, :]

        mm = jnp.zeros((rp, 1), F32)
        for b in range(nbb):
            mb = m_ref[b, hh][:, 0:1]
            mm = jnp.where(rbatch == b, mb, mm) if nbb > 1 else jnp.broadcast_to(mb, (rp, 1))

        dmat = fcum - fcum_row + ig_row + neg
        dprev = fcum + mm
        m = jnp.maximum(jnp.max(dmat, axis=1, keepdims=True), dprev)
        w = jnp.exp(dmat - m + math.log(ksc))
        wp = jnp.exp(dprev - m)
        sc = _dot_nt(qb, kb) * w
        num = _dot(sc.astype(BF16), vb)
        den = jnp.sum(sc, axis=1, keepdims=True)

        k_t = jnp.transpose(kb.astype(F32))
        for b in range(nbb):
            cm = c_ref[b, hh]
            nm = n_ref[b, hh]
            mb = m_ref[b, hh][:, 0:1]
            qc = wp * _dot(qb, cm.astype(BF16))
            nm8 = jnp.broadcast_to(nm, (SUBLANES, ML_DH)).astype(BF16)
            qn = wp * _dot_nt(qb, nm8)[:, 0:1]
            if nbb > 1:
                rsel = rbatch == b
                qc = jnp.where(rsel, qc, 0.0)
                qn = jnp.where(rsel, qn, 0.0)
            num = num + qc
            den = den + qn
            last = b * cl + cl - 1
            m_new = m[last:last + 1, :]
            fl = fcum[last:last + 1, :]
            decay = jnp.exp(fl + mb - m_new)
            wk_row = ksc * jnp.exp(fl - fcum_row + ig_row - m_new)
            if nbb > 1:
                wk_row = jnp.where(cbatch == b, wk_row, 0.0)
            c_ref[b, hh] = decay * cm + _dot((k_t * wk_row).astype(BF16), vb)
            wk8 = jnp.broadcast_to(wk_row, (SUBLANES, rp)).astype(BF16)
            n_ref[b, hh] = decay * nm + _dot(wk8, kb)[0:1, :]
            m_ref[b, hh] = jnp.broadcast_to(m_new, (1, LANES))

        hv = num / jnp.maximum(jnp.abs(den), jnp.exp(-m))
        hv = hv * _sigmoid(og)
        hv = hv * lax.rsqrt(jnp.mean(hv * hv, axis=-1, keepdims=True) + EPS) * nw_ref[...]
        y_ref[:, hs] = hv[:rows].astype(y_ref.dtype)


def _mlstm_call(projm, projt, gate_b, nw, init, *, row0, nseq_blocks, ntiles, nbb, cl, rp, batch, hps):
    rows = nbb * cl
    has_init = init is not None
    rb0 = row0 // rows
    rowblk = lambda s, h, t: rb0 + s * ntiles + t
    hd = ML_HEADS // hps
    wd = hps * ML_DH
    in_specs = [
        pl.BlockSpec((rows, wd), lambda s, h, t: (rowblk(s, h, t), h)),
        pl.BlockSpec((rows, wd), lambda s, h, t: (rowblk(s, h, t), hd + h)),
        pl.BlockSpec((rows, wd), lambda s, h, t: (rowblk(s, h, t), 2 * hd + h)),
        pl.BlockSpec((rows, wd), lambda s, h, t: (rowblk(s, h, t), 3 * hd + h)),
        pl.BlockSpec((rows, LANES), lambda s, h, t: (rowblk(s, h, t), 0)),
        pl.BlockSpec((1, LANES), lambda s, h, t: (0, 0)),
        pl.BlockSpec((1, ML_DH), lambda s, h, t: (0, 0)),
    ]
    args = [projm, projm, projm, projm, projt, gate_b, nw]
    c_spec = pl.BlockSpec((nbb, hps, ML_DH, ML_DH), lambda s, h, t: (s, h, 0, 0))
    n_spec = pl.BlockSpec((nbb, hps, 1, ML_DH), lambda s, h, t: (s, h, 0, 0))
    m_spec = pl.BlockSpec((nbb, hps, 1, LANES), lambda s, h, t: (s, h, 0, 0))
    if has_init:
        in_specs += [c_spec, n_spec, m_spec]
        args += list(init)
    body = functools.partial(_mlstm_body, nbb=nbb, cl=cl, rp=rp, has_init=has_init, hps=hps)
    return pl.pallas_call(
        body, grid=(nseq_blocks, hd, ntiles), in_specs=in_specs,
        out_specs=[pl.BlockSpec((rows, wd), lambda s, h, t: (s * ntiles + t, h)),
                   c_spec, n_spec, m_spec],
        out_shape=[jax.ShapeDtypeStruct((nseq_blocks * ntiles * rows, ML_HEADS * ML_DH), BF16),
                   jax.ShapeDtypeStruct((batch, ML_HEADS, ML_DH, ML_DH), F32),
                   jax.ShapeDtypeStruct((batch, ML_HEADS, 1, ML_DH), F32),
                   jax.ShapeDtypeStruct((batch, ML_HEADS, 1, LANES), F32)],
        compiler_params=_cparams(("arbitrary", "arbitrary", "arbitrary")),
        name="mlstm_mixer",
    )(*args)


def kernel(x_prompt, x_sample, state_s5_re, state_s5_im, state_gla, state_mlstm_c, state_mlstm_n,
           state_mlstm_m, norm_mix, norm_mlp, norm_final, w_in_even, s5_a_re, s5_a_im, s5_log_step,
           s5_b_re, s5_b_im, s5_c_re, s5_c_im, s5_d, s5_glu_w, s5_glu_b, gla_gk_up, gla_gk_b,
           gla_norm, w_out_even, w_in_odd, mlstm_b_i, mlstm_b_f, mlstm_norm, w_out_odd,
           w_mlp_up, w_mlp_down):
    bp, lp, d = x_prompt.shape
    bs, ls, _ = x_sample.shape
    mp = bp * lp
    msamp = bs * ls
    xp2 = x_prompt.reshape(mp, d)
    xs2 = x_sample.reshape(msamp, d)

    n_even = S5_WIDTH + 2 * GLA_HEADS * GLA_HK + 2 * GLA_HEADS * GLA_HV
    projm, projt = _norm_mm((xp2, xs2), norm_mix[0], w_in_even, 0, n_even, tm=1024, out_dtype=BF16)

    abar_re, abar_im, bbr, bbi = _s5_discretise(s5_a_re[0], s5_a_im[0], s5_log_step[0],
                                                s5_b_re[0], s5_b_im[0])
    bre = _block_diag(bbr, S5_NBLK).astype(BF16)
    bim = _block_diag(bbi, S5_NBLK).astype(BF16)
    cre = _block_diag(jnp.transpose(s5_c_re[0], (0, 2, 1)), S5_NBLK).astype(BF16)
    cim = _block_diag(jnp.transpose(s5_c_im[0], (0, 2, 1)), S5_NBLK).astype(BF16)
    s5_consts = (bre, bim, cre, cim, abar_re, abar_im, s5_d[0].reshape(1, S5_WIDTH),
                 s5_glu_w[0].astype(BF16), s5_glu_b[0].reshape(1, S5_WIDTH))

    s5_tl = 64
    ys5_p, hr_p, hi_p = _s5_call(projm, *s5_consts, None, row0=0, nsrc=bp, rs=s5_tl, seq_len=lp,
                                 nb=bp, tl=s5_tl, permute=True)
    ys5_p = ys5_p.reshape(mp, S5_WIDTH)
    hr_p = hr_p[SUBLANES - bp:]
    hi_p = hi_p[SUBLANES - bp:]
    u_s = jnp.transpose(projm[mp:mp + msamp, :S5_WIDTH].reshape(bs, ls, S5_WIDTH), (1, 0, 2))
    h0 = (state_s5_re[0].reshape(bs, S5_CH), state_s5_im[0].reshape(bs, S5_CH))
    ys5_s, hr_s, hi_s = _s5_call(u_s.reshape(msamp, S5_WIDTH), *s5_consts, h0, row0=0, nsrc=1, rs=msamp,
                                 seq_len=msamp, nb=bs, tl=ls, permute=False)
    ys5_s = jnp.transpose(ys5_s.reshape(ls, bs, S5_WIDTH), (1, 0, 2)).reshape(msamp, S5_WIDTH)

    up_pad = jnp.pad(gla_gk_up[0], ((0, LANES - gla_gk_up.shape[1]), (0, 0)))
    gkb = gla_gk_b[0].reshape(1, -1)
    gnw = gla_norm[0].reshape(1, GLA_HV)
    ptile = 256
    ygla_p, sg_p = _gla_call(projm, projt, up_pad, gkb, gnw, None, row0=0, nseq_blocks=bp,
                             ntiles=lp // ptile, nbb=1, nch=ptile // GLA_CHUNK, cs=GLA_CHUNK, batch=bp,
                             hps=GLA_HEADS)
    gnbb = 32
    ygla_s, sg_s = _gla_call(projm, projt, up_pad, gkb, gnw, state_gla[0], row0=mp,
                             nseq_blocks=bs // gnbb, ntiles=1, nbb=gnbb, nch=1, cs=ls, batch=bs, hps=1)

    h = _mm_out([ys5_p, ygla_p], [ys5_s, ygla_s], w_out_even, 0, (xp2, 0), (xs2, 0))
    hid = _norm_mm(h, norm_mlp[0], w_mlp_up, 0, D_FF, act="relu2", out_dtype=BF16, tn=1024)
    h = _mm_res(hid, w_mlp_down, 0, h, tn=D_MODEL, tk=512)

    n_odd = 4 * ML_HEADS * ML_DH
    projm, projt = _norm_mm(h, norm_mix[1], w_in_odd, 0, n_odd, out_dtype=BF16, tn=1024)
    gate_b = jnp.pad(jnp.concatenate([mlstm_b_i[0], mlstm_b_f[0]]), (0, LANES - 2 * ML_HEADS)).reshape(1, LANES)
    mnw = mlstm_norm[0].reshape(1, ML_DH)
    yml_p, c_p, n_p, m_p = _mlstm_call(projm, projt, gate_b, mnw, None, row0=0, nseq_blocks=bp,
                                       ntiles=lp // ML_CHUNK, nbb=1, cl=ML_CHUNK, rp=ML_CHUNK, batch=bp,
                                       hps=ML_HEADS)
    mnbb = 8
    init = (state_mlstm_c[0], state_mlstm_n[0].reshape(bs, ML_HEADS, 1, ML_DH),
            jnp.broadcast_to(state_mlstm_m[0][:, :, None, None], (bs, ML_HEADS, 1, LANES)))
    yml_s, c_s, n_s, m_s = _mlstm_call(projm, projt, gate_b, mnw, init, row0=mp, nseq_blocks=bs // mnbb,
                                       ntiles=1, nbb=mnbb, cl=ls, rp=LANES, batch=bs, hps=1)
    h = _mm_out([yml_p], [yml_s], w_out_odd, 0, (h, 0), (h, mp // 512))
    hid = _norm_mm(h, norm_mlp[1], w_mlp_up, 1, D_FF, act="relu2", out_dtype=BF16, tn=1024)
    h = _mm_res(hid, w_mlp_down, 1, h, tn=D_MODEL, tk=512)

    y_p, y_s = _rmsnorm(h, norm_final, mp)
    g, p = S5_GROUPS, S5_STATE
    return (y_p.reshape(bp, lp, d), y_s.reshape(bs, ls, d),
            hr_p.reshape(1, bp, g, p), hi_p.reshape(1, bp, g, p), sg_p[None],
            c_p[None], n_p.reshape(1, bp, ML_HEADS, ML_DH), m_p[:, :, 0, 0][None],
            hr_s.reshape(1, bs, g, p), hi_s.reshape(1, bs, g, p), sg_s[None],
            c_s[None], n_s.reshape(1, bs, ML_HEADS, ML_DH), m_s[:, :, 0, 0][None])
```

```python
import functools
import math

import jax
import jax.numpy as jnp
import numpy as np
from jax import lax
from jax.experimental import pallas as pl
from jax.experimental.pallas import tpu as pltpu

F32 = jnp.float32
BF16 = jnp.bfloat16
EPS = 1e-6

D_MODEL = 2048
D_FF = 4 * D_MODEL
S5_WIDTH = 1024
S5_GROUPS = 64
S5_GROUP = 16
S5_STATE = 64
S5_CH = S5_GROUPS * S5_STATE
S5_NBLK = 4
GLA_HEADS = 4
GLA_HK = 128
GLA_HV = 256
GLA_TAU = 16.0
GLA_CHUNK = 64
ML_HEADS = 4
ML_DH = 512
ML_CHUNK = 256

SUBLANES = 8
LANES = 128
VMEM_LIMIT = 56 * 1024 * 1024


def _cparams(sem):
    return pltpu.CompilerParams(dimension_semantics=sem, vmem_limit_bytes=VMEM_LIMIT)


def _dot(a, b):
    return jnp.dot(a, b, preferred_element_type=F32)


def _dot_nt(a, b):
    return lax.dot_general(a, b, (((1,), (1,)), ((), ())), preferred_element_type=F32)


def _split3(x):
    p1 = x.astype(BF16)
    r1 = x - p1.astype(F32)
    p2 = r1.astype(BF16)
    r2 = r1 - p2.astype(F32)
    p3 = r2.astype(BF16)
    return p1, p2, p3


def _dot_exact_lhs(m, x):
    p1, p2, p3 = _split3(x)
    return _dot(m, p1) + _dot(m, p2) + _dot(m, p3)


def _dot_nt_exact_lhs(m, x):
    p1, p2, p3 = _split3(x)
    return _dot_nt(m, p1) + _dot_nt(m, p2) + _dot_nt(m, p3)


def _log_sigmoid(x):
    return jnp.minimum(x, 0.0) - jnp.log1p(jnp.exp(-jnp.abs(x)))


def _sigmoid(x):
    return 1.0 / (1.0 + jnp.exp(-x))


def _norm_mm_body(*refs, act, n_tail, n_ptiles, tm, rchunk):
    dual = n_ptiles is not None
    has_tail = n_tail > 0
    x_ref = refs[0]
    xs_ref = refs[1] if dual else None
    rest = refs[2:] if dual else refs[1:]
    if has_tail:
        g_ref, w_ref, wt_ref, o_ref, ot_ref, xn_ref = rest
    else:
        g_ref, w_ref, o_ref, xn_ref = rest

    def normalise(src_ref, nrows):
        g = g_ref[...]
        for r in range(0, nrows, rchunk):
            x = src_ref[r:r + rchunk, :]
            ms = jnp.mean(x * x, axis=-1, keepdims=True)
            xn_ref[r:r + rchunk, :] = (x * lax.rsqrt(ms + EPS) * g).astype(BF16)

    @pl.when(pl.program_id(1) == 0)
    def _():
        if dual:
            @pl.when(pl.program_id(0) < n_ptiles)
            def _():
                normalise(x_ref, tm)

            @pl.when(pl.program_id(0) >= n_ptiles)
            def _():
                ns = xs_ref.shape[0]
                normalise(xs_ref, ns)
                xn_ref[ns:, :] = jnp.zeros((tm - ns, xn_ref.shape[1]), BF16)
        else:
            normalise(x_ref, tm)
        if has_tail:
            col = lax.broadcasted_iota(jnp.int32, wt_ref.shape, 0)
            wt = jnp.where(col < n_tail, wt_ref[...], 0.0)
            ot_ref[...] = _dot_nt(xn_ref[...], wt.astype(BF16))

    mm = _dot_nt if has_tail else _dot
    acc = mm(xn_ref[...], w_ref[...].astype(BF16))
    if act == "relu2":
        acc = jnp.square(jnp.maximum(acc, 0.0))
    o_ref[...] = acc.astype(o_ref.dtype)


def _norm_mm(x, g, w3, layer, n_main, *, act=None, out_dtype=F32, tm=1088, tn=512):
    dual = isinstance(x, tuple)
    if dual:
        xp, xs = x
        n_ptiles = xp.shape[0] // tm
        m, kdim = (n_ptiles + 1) * tm, xp.shape[1]
        grid = (n_ptiles + 1, n_main // tn)
        in_specs = [
            pl.BlockSpec((tm, kdim), lambda i, j: (jnp.minimum(i, n_ptiles - 1), 0)),
            pl.BlockSpec(xs.shape, lambda i, j: (0, 0)),
        ]
        args = [xp, xs]
    else:
        n_ptiles = None
        m, kdim = x.shape
        grid = (m // tm, n_main // tn)
        in_specs = [pl.BlockSpec((tm, kdim), lambda i, j: (i, 0))]
        args = [x]
    n_tail = w3.shape[2] - n_main
    in_specs.append(pl.BlockSpec((1, kdim), lambda i, j: (0, 0)))
    args.append(g.reshape(1, kdim))
    out_shape = [jax.ShapeDtypeStruct((m, n_main), out_dtype)]
    out_specs = [pl.BlockSpec((tm, tn), lambda i, j: (i, j))]
    if n_tail > 0:
        w3t = jnp.transpose(w3, (0, 2, 1))
        in_specs += [pl.BlockSpec((None, tn, kdim), lambda i, j: (layer, j, 0)),
                     pl.BlockSpec((None, LANES, kdim), lambda i, j: (layer, n_main // LANES, 0))]
        args += [w3t, w3t]
        out_shape.append(jax.ShapeDtypeStruct((m, LANES), F32))
        out_specs.append(pl.BlockSpec((tm, LANES), lambda i, j: (i, 0)))
    else:
        in_specs.append(pl.BlockSpec((None, kdim, tn), lambda i, j: (layer, 0, j)))
        args.append(w3)
    body = functools.partial(_norm_mm_body, act=act, n_tail=n_tail, n_ptiles=n_ptiles,
                             tm=tm, rchunk=tm // 4)
    res = pl.pallas_call(
        body, grid=grid, in_specs=in_specs, out_specs=out_specs, out_shape=out_shape,
        scratch_shapes=[pltpu.VMEM((tm, kdim), BF16)],
        compiler_params=_cparams(("arbitrary", "arbitrary")),
        name="norm_mm",
    )(*args)
    return res if n_tail > 0 else res[0]


W_RING = 3


def _mm_res_body(a_ref, w_hbm, r_ref, o_ref, wbuf, sem, *, layer, grid, tk, tn):
    ni, nj, nk = grid
    nsteps = ni * nj * nk
    step = (pl.program_id(0) * nj + pl.program_id(1)) * nk + pl.program_id(2)

    def w_copy(s):
        kk = s % nk
        jj = (s // nk) % nj
        slot = s % W_RING
        src = w_hbm.at[layer, pl.ds(pl.multiple_of(kk * tk, tk), tk), pl.ds(pl.multiple_of(jj * tn, tn), tn)]
        return pltpu.make_async_copy(src, wbuf.at[slot], sem.at[slot])

    @pl.when(step == 0)
    def _():
        for s in range(W_RING - 1):
            w_copy(s).start()

    @pl.when(step + (W_RING - 1) < nsteps)
    def _():
        w_copy(step + (W_RING - 1)).start()

    w_copy(step).wait()
    slot = step % W_RING

    @pl.when(pl.program_id(2) == 0)
    def _():
        o_ref[...] = r_ref[...] + _dot(a_ref[...].astype(BF16), wbuf[slot].astype(BF16))

    @pl.when(pl.program_id(2) > 0)
    def _():
        o_ref[...] = o_ref[...] + _dot(a_ref[...].astype(BF16), wbuf[slot].astype(BF16))


def _mm_res(a, w3, layer, res, *, tm=1088, tn=1024, tk=1024):
    m, kdim = a.shape
    n = w3.shape[2]
    grid = (m // tm, n // tn, kdim // tk)
    assert grid[0] * grid[1] * grid[2] >= W_RING - 1
    body = functools.partial(_mm_res_body, layer=layer, grid=grid, tk=tk, tn=tn)
    return pl.pallas_call(
        body, grid=grid,
        in_specs=[
            pl.BlockSpec((tm, tk), lambda i, j, k: (i, k)),
            pl.BlockSpec(memory_space=pl.ANY),
            pl.BlockSpec((tm, tn), lambda i, j, k: (i, j)),
        ],
        out_specs=pl.BlockSpec((tm, tn), lambda i, j, k: (i, j)),
        out_shape=jax.ShapeDtypeStruct((m, n), F32),
        scratch_shapes=[pltpu.VMEM((W_RING, tk, tn), F32), pltpu.SemaphoreType.DMA((W_RING,))],
        compiler_params=_cparams(("arbitrary", "arbitrary", "arbitrary")),
        name="mm_res",
    )(a, w3, res)


def _mm_out_body(*refs, kcs, n_ptiles):
    n = len(kcs)
    ap, asm = refs[:n], refs[n:2 * n]
    w_ref, rp_ref, rs_ref, o_ref, wb_ref = refs[2 * n:]
    i = pl.program_id(1)

    @pl.when(i == 0)
    def _():
        wb_ref[...] = w_ref[...].astype(BF16)

    def compute(srcs, r_ref):
        acc = r_ref[...]
        off = 0
        for a_ref, kc in zip(srcs, kcs):
            acc = acc + _dot(a_ref[...], wb_ref[off:off + kc, :])
            off += kc
        o_ref[...] = acc

    @pl.when(i < n_ptiles)
    def _():
        compute(ap, rp_ref)

    @pl.when(i >= n_ptiles)
    def _():
        compute(asm, rs_ref)


def _mm_out(a_prompt, a_sample, w3, layer, res_prompt, res_sample, *, tm=512, tn=1024):
    kcs = tuple(a.shape[1] for a in a_prompt)
    kdim, n = sum(kcs), w3.shape[2]
    n_ptiles = a_prompt[0].shape[0] // tm
    m = a_prompt[0].shape[0] + a_sample[0].shape[0]
    (rp, rp0), (rs, rs0) = res_prompt, res_sample
    pidx = lambda i: jnp.minimum(i, n_ptiles - 1)
    in_specs = ([pl.BlockSpec((tm, kc), lambda j, i: (pidx(i), 0)) for kc in kcs]
                + [pl.BlockSpec((tm, kc), lambda j, i: (0, 0)) for kc in kcs]
                + [pl.BlockSpec((None, kdim, tn), lambda j, i: (layer, 0, j)),
                   pl.BlockSpec((tm, tn), lambda j, i: (rp0 + pidx(i), j)),
                   pl.BlockSpec((tm, tn), lambda j, i: (rs0, j))])
    body = functools.partial(_mm_out_body, kcs=kcs, n_ptiles=n_ptiles)
    return pl.pallas_call(
        body, grid=(n // tn, n_ptiles + 1), in_specs=in_specs,
        out_specs=pl.BlockSpec((tm, tn), lambda j, i: (i, j)),
        out_shape=jax.ShapeDtypeStruct((m, n), F32),
        scratch_shapes=[pltpu.VMEM((kdim, tn), BF16)],
        compiler_params=_cparams(("arbitrary", "arbitrary")),
        name="mm_out",
    )(*a_prompt, *a_sample, w3, rp, rs)


def _rmsnorm_body(x_ref, g_ref, op_ref, os_ref, *, n_ptiles):
    x = x_ref[...]
    ms = jnp.mean(x * x, axis=-1, keepdims=True)
    y = x * lax.rsqrt(ms + EPS) * g_ref[...]

    @pl.when(pl.program_id(0) < n_ptiles)
    def _():
        op_ref[...] = y

    @pl.when(pl.program_id(0) >= n_ptiles)
    def _():
        os_ref[...] = y


def _rmsnorm(x, g, m_prompt, *, tm=512):
    m, d = x.shape
    n_ptiles = m_prompt // tm
    return pl.pallas_call(
        functools.partial(_rmsnorm_body, n_ptiles=n_ptiles), grid=(n_ptiles + 1,),
        in_specs=[pl.BlockSpec((tm, d), lambda i: (i, 0)), pl.BlockSpec((1, d), lambda i: (0, 0))],
        out_specs=[pl.BlockSpec((tm, d), lambda i: (jnp.minimum(i, n_ptiles - 1), 0)),
                   pl.BlockSpec((tm, d), lambda i: (0, 0))],
        out_shape=[jax.ShapeDtypeStruct((m_prompt, d), F32),
                   jax.ShapeDtypeStruct((m - m_prompt, d), F32)],
        compiler_params=_cparams(("arbitrary",)),
        name="final_norm",
    )(x, g.reshape(1, d))


def _s5_disc_body(ar_ref, ai_ref, ls_ref, br_ref, bi_ref, abr_ref, abi_ref, bbr_ref, bbi_ref):
    ar = ar_ref[...]
    ai = ai_ref[...]
    dt = jnp.exp(ls_ref[...])
    mag = jnp.exp(ar * dt)
    abr = mag * jnp.cos(ai * dt)
    abi = mag * jnp.sin(ai * dt)
    lam2 = ar * ar + ai * ai
    zr = abr - 1.0
    cr = (zr * ar + abi * ai) / lam2
    ci = (abi * ar - zr * ai) / lam2
    br = br_ref[...]
    bi = bi_ref[...]
    abr_ref[...] = abr
    abi_ref[...] = abi
    bbr_ref[...] = cr * br - ci * bi
    bbi_ref[...] = cr * bi + ci * br


def _s5_discretise(a_re, a_im, log_step, b_re, b_im):
    g, p, q = S5_GROUPS, S5_STATE, S5_GROUP
    rep = lambda t: jnp.repeat(t, q, axis=0)
    ls = jnp.broadcast_to(log_step[:, None], (g, p))
    brt = jnp.transpose(b_re, (0, 2, 1)).reshape(g * q, p)
    bit = jnp.transpose(b_im, (0, 2, 1)).reshape(g * q, p)
    shp = jax.ShapeDtypeStruct((g * q, p), F32)
    spec = pl.BlockSpec((g * q, p), lambda: (0, 0))
    abr, abi, bbr, bbi = pl.pallas_call(
        _s5_disc_body, in_specs=[spec] * 5, out_specs=[spec] * 4, out_shape=[shp] * 4,
        name="s5_disc",
    )(rep(a_re), rep(a_im), rep(ls), brt, bit)
    abar_re = abr[::q].reshape(1, S5_CH)
    abar_im = abi[::q].reshape(1, S5_CH)
    return abar_re, abar_im, bbr.reshape(g, q, p), bbi.reshape(g, q, p)


def _block_diag(t, nblk):
    g, a, b = t.shape
    gb = g // nblk
    t = t.reshape(nblk, gb, a, b)
    eye = jnp.eye(gb, dtype=t.dtype)
    out = t[:, :, :, None, :] * eye[None, :, None, :, None]
    return out.reshape(nblk, gb * a, gb * b)


def _s5_perm(nb, tl):
    r = np.arange(nb * tl)
    p = np.zeros((nb * tl, nb * tl), np.float32)
    p[r, (r % nb) * tl + r // nb] = 1.0
    return p


def _s5_body(*refs, nsrc, nb, has_init, permute):
    u_refs, refs = refs[:nsrc], refs[nsrc:]
    if permute:
        pm_ref, pmt_ref = refs[:2]
        refs = refs[2:]
    bre_ref, bim_ref, cre_ref, cim_ref, ar_ref, ai_ref, d_ref, gw_ref, gb_ref = refs[:9]
    refs = refs[9:]
    if has_init:
        h0r_ref, h0i_ref, y_ref, hr_ref, hi_ref, sre, sim, ysc = refs
    else:
        y_ref, hr_ref, hi_ref, sre, sim, ysc = refs
    rows = sre.shape[0]
    tl = rows // nb
    cb = S5_CH // S5_NBLK
    ub = S5_WIDTH // S5_NBLK
    hw = cb // 2

    @pl.when(pl.program_id(0) == 0)
    def _():
        if has_init:
            hr_ref[...] = h0r_ref[...]
            hi_ref[...] = h0i_ref[...]
        else:
            hr_ref[...] = jnp.zeros_like(hr_ref)
            hi_ref[...] = jnp.zeros_like(hi_ref)

    u_b = u_refs[0][...] if nsrc == 1 else jnp.concatenate([r[...] for r in u_refs], axis=0)
    if permute:
        u_tm = _dot(pm_ref[...], u_b)
        u_b = u_tm.astype(BF16)
    else:
        u_tm = u_b.astype(F32)
    low_rows = lax.broadcasted_iota(jnp.int32, (SUBLANES, hw), 0) < nb

    for k in range(S5_NBLK):
        uk = u_b[:, k * ub:(k + 1) * ub]
        sre[...] = _dot(uk, bre_ref[k])
        sim[...] = _dot(uk, bim_ref[k])
        for half in range(2):
            c0 = k * cb + half * hw
            l0 = half * hw
            ar8 = jnp.broadcast_to(ar_ref[:, c0:c0 + hw], (SUBLANES, hw))
            ai8 = jnp.broadcast_to(ai_ref[:, c0:c0 + hw], (SUBLANES, hw))

            if nb < SUBLANES:
                def j_body(j, c, l0=l0, ar8=ar8, ai8=ai8):
                    sr, si = c
                    row = pl.multiple_of(j * SUBLANES, SUBLANES)
                    dr = sre[pl.ds(row, SUBLANES), l0:l0 + hw]
                    di = sim[pl.ds(row, SUBLANES), l0:l0 + hw]
                    pr = pltpu.roll(sr, nb, 0)
                    pi = pltpu.roll(si, nb, 0)
                    yr = ar8 * pr - ai8 * pi + dr
                    yi = ar8 * pi + ai8 * pr + di
                    qr = pltpu.roll(yr, nb, 0)
                    qi = pltpu.roll(yi, nb, 0)
                    zr = ar8 * qr - ai8 * qi + dr
                    zi = ar8 * qi + ai8 * qr + di
                    sre[pl.ds(row, SUBLANES), l0:l0 + hw] = jnp.where(low_rows, yr, zr)
                    sim[pl.ds(row, SUBLANES), l0:l0 + hw] = jnp.where(low_rows, yi, zi)
                    return zr, zi

                sr, si = lax.fori_loop(0, rows // SUBLANES, j_body,
                                       (hr_ref[:, c0:c0 + hw], hi_ref[:, c0:c0 + hw]), unroll=True)
                hr_ref[:, c0:c0 + hw] = sr
                hi_ref[:, c0:c0 + hw] = si
            else:
                def rg_body(rg, carry, c0=c0, l0=l0, ar8=ar8, ai8=ai8):
                    r0 = pl.multiple_of(rg * SUBLANES, SUBLANES)
                    xr = hr_ref[pl.ds(r0, SUBLANES), c0:c0 + hw]
                    xi = hi_ref[pl.ds(r0, SUBLANES), c0:c0 + hw]
                    for t in range(tl):
                        row = pl.multiple_of(t * nb + r0, SUBLANES)
                        nr = ar8 * xr - ai8 * xi + sre[pl.ds(row, SUBLANES), l0:l0 + hw]
                        ni = ar8 * xi + ai8 * xr + sim[pl.ds(row, SUBLANES), l0:l0 + hw]
                        sre[pl.ds(row, SUBLANES), l0:l0 + hw] = nr
                        sim[pl.ds(row, SUBLANES), l0:l0 + hw] = ni
                        xr, xi = nr, ni
                    hr_ref[pl.ds(r0, SUBLANES), c0:c0 + hw] = xr
                    hi_ref[pl.ds(r0, SUBLANES), c0:c0 + hw] = xi
                    return carry

                lax.fori_loop(0, nb // SUBLANES, rg_body, 0)
        ysc[:, k * ub:(k + 1) * ub] = (_dot(sre[...].astype(BF16), cre_ref[k])
                                       - _dot(sim[...].astype(BF16), cim_ref[k]))

    y = ysc[...] + d_ref[...] * u_tm
    y = 0.5 * y * (1.0 + jnp.tanh(math.sqrt(2.0 / math.pi) * (y + 0.044715 * (y * y * y))))
    z = _dot(y.astype(BF16), gw_ref[...]) + gb_ref[...]
    out = (y * _sigmoid(z)).astype(BF16)
    if permute:
        out = _dot(pmt_ref[...], out).astype(BF16)
    y_ref[...] = out.reshape(y_ref.shape)


def _s5_call(src, bre, bim, cre, cim, abar_re, abar_im, d_skip, glu_w, glu_b, h0, *,
             row0, nsrc, rs, seq_len, nb, tl, permute):
    rows = nsrc * rs
    ntiles = seq_len // rs if nsrc > 1 else 1
    has_init = h0 is not None
    srows = max(nb, SUBLANES)
    const = lambda shape: pl.BlockSpec(shape, lambda t: (0,) * len(shape))
    in_specs = [pl.BlockSpec((rs, S5_WIDTH), lambda t, s=s: ((row0 + s * seq_len) // rs + t, 0))
                for s in range(nsrc)]
    args = [src] * nsrc
    if permute:
        pm = _s5_perm(nb, tl)
        in_specs += [const((rows, rows)), const((rows, rows))]
        args += [jnp.asarray(pm, BF16), jnp.asarray(pm.T, BF16)]
    in_specs += [
        const(bre.shape), const(bim.shape), const(cre.shape), const(cim.shape),
        const((1, S5_CH)), const((1, S5_CH)), const((1, S5_WIDTH)),
        const((S5_WIDTH, S5_WIDTH)), const((1, S5_WIDTH)),
    ]
    args += [bre, bim, cre, cim, abar_re, abar_im, d_skip, glu_w, glu_b]
    if has_init:
        in_specs += [const((srows, S5_CH)), const((srows, S5_CH))]
        args += [h0[0], h0[1]]
    body = functools.partial(_s5_body, nsrc=nsrc, nb=nb, has_init=has_init, permute=permute)
    return pl.pallas_call(
        body, grid=(ntiles,), in_specs=in_specs,
        out_specs=[pl.BlockSpec((nsrc, rs, S5_WIDTH), lambda t: (0, t, 0)),
                   const((srows, S5_CH)), const((srows, S5_CH))],
        out_shape=[jax.ShapeDtypeStruct((nsrc, ntiles * rs, S5_WIDTH), BF16),
                   jax.ShapeDtypeStruct((srows, S5_CH), F32),
                   jax.ShapeDtypeStruct((srows, S5_CH), F32)],
        scratch_shapes=[pltpu.VMEM((rows, S5_CH // S5_NBLK), F32),
                        pltpu.VMEM((rows, S5_CH // S5_NBLK), F32),
                        pltpu.VMEM((rows, S5_WIDTH), F32)],
        compiler_params=_cparams(("arbitrary",)),
        name="s5_mixer",
    )(*args)


def _gla_body(*refs, nbb, nch, cs, has_init, hps):
    if has_init:
        (q_ref, k_ref, v_ref, g_ref, gl_ref, up_ref, gb_ref, nw_ref, s0_ref, y_ref, s_ref) = refs
    else:
        (q_ref, k_ref, v_ref, g_ref, gl_ref, up_ref, gb_ref, nw_ref, y_ref, s_ref) = refs
    rows = nbb * nch * cs
    sh = int(math.log2(cs))

    @pl.when(pl.program_id(2) == 0)
    def _():
        if has_init:
            s_ref[...] = s0_ref[...]
        else:
            s_ref[...] = jnp.zeros_like(s_ref)

    ri = lax.broadcasted_iota(jnp.int32, (rows, rows), 0)
    ci = lax.broadcasted_iota(jnp.int32, (rows, rows), 1)
    same = (ri >> sh) == (ci >> sh)
    causal = jnp.logical_and(same, ci <= ri)
    lmat = jnp.where(causal, 1.0, 0.0).astype(BF16)
    tmat = jnp.where(same, 1.0, 0.0).astype(BF16)
    rchunk = lax.broadcasted_iota(jnp.int32, (rows, 1), 0) >> sh
    cchunk = lax.broadcasted_iota(jnp.int32, (1, rows), 1) >> sh
    single = nbb * nch == 1
    z = _dot(gl_ref[...].astype(BF16), up_ref[...].astype(BF16)) + gb_ref[...]
    la = _log_sigmoid(z) * (1.0 / GLA_TAU)
    p1, p2, p3 = _split3(la)
    bcum_all = _dot(lmat, p1) + _dot(lmat, p2) + _dot(lmat, p3)
    tot_all = _dot(tmat, p1) + _dot(tmat, p2) + _dot(tmat, p3)

    for hh in range(hps):
        ks = slice(hh * GLA_HK, (hh + 1) * GLA_HK)
        vs = slice(hh * GLA_HV, (hh + 1) * GLA_HV)
        q = q_ref[:, ks].astype(F32) * (GLA_HK ** -0.5)
        k = k_ref[:, ks].astype(F32)
        vb = v_ref[:, vs]
        bcum = bcum_all[:, ks]
        tot = tot_all[:, ks]

        qd = q * jnp.exp(bcum)
        kd = k * jnp.exp(-bcum)
        kdec = k * jnp.exp(tot - bcum)
        qdb = qd.astype(BF16)
        att = jnp.where(causal, _dot_nt(qdb, kd.astype(BF16)), 0.0)
        o = _dot(att.astype(BF16), vb)

        kdec_t = jnp.transpose(kdec)
        dec_t = jnp.transpose(jnp.exp(tot))
        for b in range(nbb):
            s = s_ref[b, hh]
            for c in range(nch):
                idx = b * nch + c
                oi = _dot(qdb, s.astype(BF16))
                o = o + (oi if single else jnp.where(rchunk == idx, oi, 0.0))
                kt = kdec_t if single else jnp.where(cchunk == idx, kdec_t, 0.0)
                s = dec_t[:, idx * cs:idx * cs + 1] * s + _dot(kt.astype(BF16), vb)
            s_ref[b, hh] = s

        o = o * lax.rsqrt(jnp.mean(o * o, axis=-1, keepdims=True) + EPS) * nw_ref[...]
        gg = g_ref[:, vs].astype(F32)
        y_ref[:, vs] = (o * (gg * _sigmoid(gg))).astype(y_ref.dtype)


def _gla_call(projm, projt, up_pad, gkb, nw, s0, *, row0, nseq_blocks, ntiles, nbb, nch, cs, batch, hps):
    rows = nbb * nch * cs
    has_init = s0 is not None
    rb0 = row0 // rows
    rowblk = lambda s, h, t: rb0 + s * ntiles + t
    wk, wv = hps * GLA_HK, hps * GLA_HV
    qoff = S5_WIDTH // wk
    koff = qoff + GLA_HEADS // hps
    voff = (S5_WIDTH + 2 * GLA_HEADS * GLA_HK) // wv
    goff = voff + GLA_HEADS // hps
    in_specs = [
        pl.BlockSpec((rows, wk), lambda s, h, t: (rowblk(s, h, t), qoff + h)),
        pl.BlockSpec((rows, wk), lambda s, h, t: (rowblk(s, h, t), koff + h)),
        pl.BlockSpec((rows, wv), lambda s, h, t: (rowblk(s, h, t), voff + h)),
        pl.BlockSpec((rows, wv), lambda s, h, t: (rowblk(s, h, t), goff + h)),
        pl.BlockSpec((rows, LANES), lambda s, h, t: (rowblk(s, h, t), 0)),
        pl.BlockSpec((LANES, wk), lambda s, h, t: (0, h)),
        pl.BlockSpec((1, wk), lambda s, h, t: (0, h)),
        pl.BlockSpec((1, GLA_HV), lambda s, h, t: (0, 0)),
    ]
    args = [projm, projm, projm, projm, projt, up_pad, gkb, nw]
    if has_init:
        in_specs.append(pl.BlockSpec((nbb, hps, GLA_HK, GLA_HV), lambda s, h, t: (s, h, 0, 0)))
        args.append(s0)
    body = functools.partial(_gla_body, nbb=nbb, nch=nch, cs=cs, has_init=has_init, hps=hps)
    return pl.pallas_call(
        body, grid=(nseq_blocks, GLA_HEADS // hps, ntiles), in_specs=in_specs,
        out_specs=[pl.BlockSpec((rows, wv), lambda s, h, t: (s * ntiles + t, h)),
                   pl.BlockSpec((nbb, hps, GLA_HK, GLA_HV), lambda s, h, t: (s, h, 0, 0))],
        out_shape=[jax.ShapeDtypeStruct((nseq_blocks * ntiles * rows, GLA_HEADS * GLA_HV), BF16),
                   jax.ShapeDtypeStruct((batch, GLA_HEADS, GLA_HK, GLA_HV), F32)],
        compiler_params=_cparams(("arbitrary", "arbitrary", "arbitrary")),
        name="gla_mixer",
    )(*args)


def _mlstm_body(*refs, nbb, cl, rp, has_init, hps):
    if has_init:
        (q_ref, k_ref, v_ref, og_ref, gt_ref, gb_ref, nw_ref, c0_ref, n0_ref, m0_ref,
         y_ref, c_ref, n_ref, m_ref) = refs
    else:
        (q_ref, k_ref, v_ref, og_ref, gt_ref, gb_ref, nw_ref, y_ref, c_ref, n_ref, m_ref) = refs
    rows = nbb * cl
    sh = int(math.log2(cl))
    head0 = pl.program_id(1) * hps

    @pl.when(pl.program_id(2) == 0)
    def _():
        if has_init:
            c_ref[...] = c0_ref[...]
            n_ref[...] = n0_ref[...]
            m_ref[...] = m0_ref[...]
        else:
            c_ref[...] = jnp.zeros_like(c_ref)
            n_ref[...] = jnp.zeros_like(n_ref)
            m_ref[...] = jnp.full(m_ref.shape, -jnp.inf, F32)

    def pad(x):
        if rp == rows:
            return x
        return jnp.concatenate([x, jnp.zeros((rp - rows, x.shape[1]), x.dtype)], axis=0)

    ksc = ML_DH ** -0.5
    gates = pad(gt_ref[...] + gb_ref[...])
    lane = lax.broadcasted_iota(jnp.int32, (rp, LANES), 1)
    ri = lax.broadcasted_iota(jnp.int32, (rp, rp), 0)
    ci = lax.broadcasted_iota(jnp.int32, (rp, rp), 1)
    causal = jnp.logical_and((ri >> sh) == (ci >> sh), ci <= ri)
    lmat = jnp.where(causal, 1.0, 0.0).astype(BF16)
    neg = jnp.where(causal, 0.0, -jnp.inf)
    er = lax.broadcasted_iota(jnp.int32, (SUBLANES, LANES), 0)
    ec = lax.broadcasted_iota(jnp.int32, (SUBLANES, LANES), 1)
    pick = jnp.where(er == ec, 1.0, 0.0).astype(BF16)
    rbatch = lax.broadcasted_iota(jnp.int32, (rp, 1), 0) >> sh
    cbatch = lax.broadcasted_iota(jnp.int32, (1, rp), 1) >> sh

    for hh in range(hps):
        head = head0 + hh
        hs = slice(hh * ML_DH, (hh + 1) * ML_DH)
        qb = pad(q_ref[:, hs])
        kb = pad(k_ref[:, hs])
        vb = pad(v_ref[:, hs])
        og = pad(og_ref[:, hs]).astype(F32)
        ig = jnp.sum(jnp.where(lane == head, gates, 0.0), axis=1, keepdims=True)
        fg = jnp.sum(jnp.where(lane == head + ML_HEADS, gates, 0.0), axis=1, keepdims=True)
        lf = _log_sigmoid(fg)
        fcum = _dot_exact_lhs(lmat, jnp.where(lane == 0, lf, 0.0))[:, 0:1]
        cols = jnp.where(lane == 0, fcum, jnp.where(lane == 1, ig, 0.0))
        as_rows = _dot_nt_exact_lhs(pick, cols)
        fcum_row = as_rows[0:1, :]
        ig_row = as_rows[1:2, :]

        mm = jnp.zeros((rp, 1), F32)
        for b in range(nbb):
            mb = m_ref[b, hh][:, 0:1]
            mm = jnp.where(rbatch == b, mb, mm) if nbb > 1 else jnp.broadcast_to(mb, (rp, 1))

        dmat = fcum - fcum_row + ig_row + neg
        dprev = fcum + mm
        m = jnp.maximum(jnp.max(dmat, axis=1, keepdims=True), dprev)
        w = jnp.exp(dmat - m + math.log(ksc))
        wp = jnp.exp(dprev - m)
        sc = _dot_nt(qb, kb) * w
        num = _dot(sc.astype(BF16), vb)
        den = jnp.sum(sc, axis=1, keepdims=True)

        k_t = jnp.transpose(kb.astype(F32))
        for b in range(nbb):
            cm = c_ref[b, hh]
            nm = n_ref[b, hh]
            mb = m_ref[b, hh][:, 0:1]
            qc = wp * _dot(qb, cm.astype(BF16))
            nm8 = jnp.broadcast_to(nm, (SUBLANES, ML_DH)).astype(BF16)
            qn = wp * _dot_nt(qb, nm8)[:, 0:1]
            if nbb > 1:
                rsel = rbatch == b
                qc = jnp.where(rsel, qc, 0.0)
                qn = jnp.where(rsel, qn, 0.0)
            num = num + qc
            den = den + qn
            last = b * cl + cl - 1
            m_new = m[last:last + 1, :]
            fl = fcum[last:last + 1, :]
            decay = jnp.exp(fl + mb - m_new)
            wk_row = ksc * jnp.exp(fl - fcum_row + ig_row - m_new)
            if nbb > 1:
                wk_row = jnp.where(cbatch == b, wk_row, 0.0)
            c_ref[b, hh] = decay * cm + _dot((k_t * wk_row).astype(BF16), vb)
            wk8 = jnp.broadcast_to(wk_row, (SUBLANES, rp)).astype(BF16)
            n_ref[b, hh] = decay * nm + _dot(wk8, kb)[0:1, :]
            m_ref[b, hh] = jnp.broadcast_to(m_new, (1, LANES))

        hv = num / jnp.maximum(jnp.abs(den), jnp.exp(-m))
        hv = hv * _sigmoid(og)
        hv = hv * lax.rsqrt(jnp.mean(hv * hv, axis=-1, keepdims=True) + EPS) * nw_ref[...]
        y_ref[:, hs] = hv[:rows].astype(y_ref.dtype)


def _mlstm_call(projm, projt, gate_b, nw, init, *, row0, nseq_blocks, ntiles, nbb, cl, rp, batch, hps):
    rows = nbb * cl
    has_init = init is not None
    rb0 = row0 // rows
    rowblk = lambda s, h, t: rb0 + s * ntiles + t
    hd = ML_HEADS // hps
    wd = hps * ML_DH
    in_specs = [
        pl.BlockSpec((rows, wd), lambda s, h, t: (rowblk(s, h, t), h)),
        pl.BlockSpec((rows, wd), lambda s, h, t: (rowblk(s, h, t), hd + h)),
        pl.BlockSpec((rows, wd), lambda s, h, t: (rowblk(s, h, t), 2 * hd + h)),
        pl.BlockSpec((rows, wd), lambda s, h, t: (rowblk(s, h, t), 3 * hd + h)),
        pl.BlockSpec((rows, LANES), lambda s, h, t: (rowblk(s, h, t), 0)),
        pl.BlockSpec((1, LANES), lambda s, h, t: (0, 0)),
        pl.BlockSpec((1, ML_DH), lambda s, h, t: (0, 0)),
    ]
    args = [projm, projm, projm, projm, projt, gate_b, nw]
    c_spec = pl.BlockSpec((nbb, hps, ML_DH, ML_DH), lambda s, h, t: (s, h, 0, 0))
    n_spec = pl.BlockSpec((nbb, hps, 1, ML_DH), lambda s, h, t: (s, h, 0, 0))
    m_spec = pl.BlockSpec((nbb, hps, 1, LANES), lambda s, h, t: (s, h, 0, 0))
    if has_init:
        in_specs += [c_spec, n_spec, m_spec]
        args += list(init)
    body = functools.partial(_mlstm_body, nbb=nbb, cl=cl, rp=rp, has_init=has_init, hps=hps)
    return pl.pallas_call(
        body, grid=(nseq_blocks, hd, ntiles), in_specs=in_specs,
        out_specs=[pl.BlockSpec((rows, wd), lambda s, h, t: (s * ntiles + t, h)),
                   c_spec, n_spec, m_spec],
        out_shape=[jax.ShapeDtypeStruct((nseq_blocks * ntiles * rows, ML_HEADS * ML_DH), BF16),
                   jax.ShapeDtypeStruct((batch, ML_HEADS, ML_DH, ML_DH), F32),
                   jax.ShapeDtypeStruct((batch, ML_HEADS, 1, ML_DH), F32),
                   jax.ShapeDtypeStruct((batch, ML_HEADS, 1, LANES), F32)],
        compiler_params=_cparams(("arbitrary", "arbitrary", "arbitrary")),
        name="mlstm_mixer",
    )(*args)


def kernel(x_prompt, x_sample, state_s5_re, state_s5_im, state_gla, state_mlstm_c, state_mlstm_n,
           state_mlstm_m, norm_mix, norm_mlp, norm_final, w_in_even, s5_a_re, s5_a_im, s5_log_step,
           s5_b_re, s5_b_im, s5_c_re, s5_c_im, s5_d, s5_glu_w, s5_glu_b, gla_gk_up, gla_gk_b,
           gla_norm, w_out_even, w_in_odd, mlstm_b_i, mlstm_b_f, mlstm_norm, w_out_odd,
           w_mlp_up, w_mlp_down):
    bp, lp, d = x_prompt.shape
    bs, ls, _ = x_sample.shape
    mp = bp * lp
    msamp = bs * ls
    xp2 = x_prompt.reshape(mp, d)
    xs2 = x_sample.reshape(msamp, d)

    n_even = S5_WIDTH + 2 * GLA_HEADS * GLA_HK + 2 * GLA_HEADS * GLA_HV
    projm, projt = _norm_mm((xp2, xs2), norm_mix[0], w_in_even, 0, n_even, tm=1024, out_dtype=BF16)

    abar_re, abar_im, bbr, bbi = _s5_discretise(s5_a_re[0], s5_a_im[0], s5_log_step[0],
                                                s5_b_re[0], s5_b_im[0])
    bre = _block_diag(bbr, S5_NBLK).astype(BF16)
    bim = _block_diag(bbi, S5_NBLK).astype(BF16)
    cre = _block_diag(jnp.transpose(s5_c_re[0], (0, 2, 1)), S5_NBLK).astype(BF16)
    cim = _block_diag(jnp.transpose(s5_c_im[0], (0, 2, 1)), S5_NBLK).astype(BF16)
    s5_consts = (bre, bim, cre, cim, abar_re, abar_im, s5_d[0].reshape(1, S5_WIDTH),
                 s5_glu_w[0].astype(BF16), s5_glu_b[0].reshape(1, S5_WIDTH))

    s5_tl = 64
    ys5_p, hr_p, hi_p = _s5_call(projm, *s5_consts, None, row0=0, nsrc=bp, rs=s5_tl, seq_len=lp,
                                 nb=bp, tl=s5_tl, permute=True)
    ys5_p = ys5_p.reshape(mp, S5_WIDTH)
    hr_p = hr_p[SUBLANES - bp:]
    hi_p = hi_p[SUBLANES - bp:]
    u_s = jnp.transpose(projm[mp:mp + msamp, :S5_WIDTH].reshape(bs, ls, S5_WIDTH), (1, 0, 2))
    h0 = (state_s5_re[0].reshape(bs, S5_CH), state_s5_im[0].reshape(bs, S5_CH))
    ys5_s, hr_s, hi_s = _s5_call(u_s.reshape(msamp, S5_WIDTH), *s5_consts, h0, row0=0, nsrc=1, rs=msamp,
                                 seq_len=msamp, nb=bs, tl=ls, permute=False)
    ys5_s = jnp.transpose(ys5_s.reshape(ls, bs, S5_WIDTH), (1, 0, 2)).reshape(msamp, S5_WIDTH)

    up_pad = jnp.pad(gla_gk_up[0], ((0, LANES - gla_gk_up.shape[1]), (0, 0)))
    gkb = gla_gk_b[0].reshape(1, -1)
    gnw = gla_norm[0].reshape(1, GLA_HV)
    ptile = 256
    ygla_p, sg_p = _gla_call(projm, projt, up_pad, gkb, gnw, None, row0=0, nseq_blocks=bp,
                             ntiles=lp // ptile, nbb=1, nch=ptile // GLA_CHUNK, cs=GLA_CHUNK, batch=bp,
                             hps=GLA_HEADS)
    gnbb = 32
    ygla_s, sg_s = _gla_call(projm, projt, up_pad, gkb, gnw, state_gla[0], row0=mp,
                             nseq_blocks=bs // gnbb, ntiles=1, nbb=gnbb, nch=1, cs=ls, batch=bs, hps=1)

    h = _mm_out([ys5_p, ygla_p], [ys5_s, ygla_s], w_out_even, 0, (xp2, 0), (xs2, 0))
    hid = _norm_mm(h, norm_mlp[0], w_mlp_up, 0, D_FF, act="relu2", out_dtype=BF16, tn=1024)
    h = _mm_res(hid, w_mlp_down, 0, h, tn=D_MODEL, tk=512)

    n_odd = 4 * ML_HEADS * ML_DH
    projm, projt = _norm_mm(h, norm_mix[1], w_in_odd, 0, n_odd, out_dtype=BF16, tn=1024)
    gate_b = jnp.pad(jnp.concatenate([mlstm_b_i[0], mlstm_b_f[0]]), (0, LANES - 2 * ML_HEADS)).reshape(1, LANES)
    mnw = mlstm_norm[0].reshape(1, ML_DH)
    yml_p, c_p, n_p, m_p = _mlstm_call(projm, projt, gate_b, mnw, None, row0=0, nseq_blocks=bp,
                                       ntiles=lp // ML_CHUNK, nbb=1, cl=ML_CHUNK, rp=ML_CHUNK, batch=bp,
                                       hps=ML_HEADS)
    mnbb = 8
    init = (state_mlstm_c[0], state_mlstm_n[0].reshape(bs, ML_HEADS, 1, ML_DH),
            jnp.broadcast_to(state_mlstm_m[0][:, :, None, None], (bs, ML_HEADS, 1, LANES)))
    yml_s, c_s, n_s, m_s = _mlstm_call(projm, projt, gate_b, mnw, init, row0=mp, nseq_blocks=bs // mnbb,
                                       ntiles=1, nbb=mnbb, cl=ls, rp=LANES, batch=bs, hps=1)
    h = _mm_out([yml_p], [yml_s], w_out_odd, 0, (h, 0), (h, mp // 512))
    hid = _norm_mm(h, norm_mlp[1], w_mlp_up, 1, D_FF, act="relu2", out_dtype=BF16, tn=1024)
    h = _mm_res(hid, w_mlp_down, 1, h, tn=D_MODEL, tk=512)

    y_p, y_s = _rmsnorm(h, norm_final, mp)
    g, p = S5_GROUPS, S5_STATE
    return (y_p.reshape(bp, lp, d), y_s.reshape(bs, ls, d),
            hr_p.reshape(1, bp, g, p), hi_p.reshape(1, bp, g, p), sg_p[None],
            c_p[None], n_p.reshape(1, bp, ML_HEADS, ML_DH), m_p[:, :, 0, 0][None],
            hr_s.reshape(1, bs, g, p), hi_s.reshape(1, bs, g, p), sg_s[None],
            c_s[None], n_s.reshape(1, bs, ML_HEADS, ML_DH), m_s[:, :, 0, 0][None])
```

```python
import functools
import math

import jax
import jax.numpy as jnp
import numpy as np
from jax import lax
from jax.experimental import pallas as pl
from jax.experimental.pallas import tpu as pltpu

F32 = jnp.float32
BF16 = jnp.bfloat16
EPS = 1e-6

D_MODEL = 2048
D_FF = 4 * D_MODEL
S5_WIDTH = 1024
S5_GROUPS = 64
S5_GROUP = 16
S5_STATE = 64
S5_CH = S5_GROUPS * S5_STATE
S5_NBLK = 4
GLA_HEADS = 4
GLA_HK = 128
GLA_HV = 256
GLA_TAU = 16.0
GLA_CHUNK = 64
ML_HEADS = 4
ML_DH = 512
ML_CHUNK = 256

SUBLANES = 8
LANES = 128
VMEM_LIMIT = 56 * 1024 * 1024


def _cparams(sem):
    return pltpu.CompilerParams(dimension_semantics=sem, vmem_limit_bytes=VMEM_LIMIT)


def _dot(a, b):
    return jnp.dot(a, b, preferred_element_type=F32)


def _dot_nt(a, b):
    return lax.dot_general(a, b, (((1,), (1,)), ((), ())), preferred_element_type=F32)


def _split3(x):
    p1 = x.astype(BF16)
    r1 = x - p1.astype(F32)
    p2 = r1.astype(BF16)
    r2 = r1 - p2.astype(F32)
    p3 = r2.astype(BF16)
    return p1, p2, p3


def _dot_exact_lhs(m, x):
    p1, p2, p3 = _split3(x)
    return _dot(m, p1) + _dot(m, p2) + _dot(m, p3)


def _dot_nt_exact_lhs(m, x):
    p1, p2, p3 = _split3(x)
    return _dot_nt(m, p1) + _dot_nt(m, p2) + _dot_nt(m, p3)


def _log_sigmoid(x):
    return jnp.minimum(x, 0.0) - jnp.log1p(jnp.exp(-jnp.abs(x)))


def _sigmoid(x):
    return 1.0 / (1.0 + jnp.exp(-x))


def _norm_mm_body(*refs, act, n_tail, n_ptiles, tm, rchunk):
    dual = n_ptiles is not None
    has_tail = n_tail > 0
    x_ref = refs[0]
    xs_ref = refs[1] if dual else None
    rest = refs[2:] if dual else refs[1:]
    if has_tail:
        g_ref, w_ref, wt_ref, o_ref, ot_ref, xn_ref = rest
    else:
        g_ref, w_ref, o_ref, xn_ref = rest

    def normalise(src_ref, nrows):
        g = g_ref[...]
        for r in range(0, nrows, rchunk):
            x = src_ref[r:r + rchunk, :]
            ms = jnp.mean(x * x, axis=-1, keepdims=True)
            xn_ref[r:r + rchunk, :] = (x * lax.rsqrt(ms + EPS) * g).astype(BF16)

    @pl.when(pl.program_id(1) == 0)
    def _():
        if dual:
            @pl.when(pl.program_id(0) < n_ptiles)
            def _():
                normalise(x_ref, tm)

            @pl.when(pl.program_id(0) >= n_ptiles)
            def _():
                ns = xs_ref.shape[0]
                normalise(xs_ref, ns)
                xn_ref[ns:, :] = jnp.zeros((tm - ns, xn_ref.shape[1]), BF16)
        else:
            normalise(x_ref, tm)
        if has_tail:
            col = lax.broadcasted_iota(jnp.int32, wt_ref.shape, 0)
            wt = jnp.where(col < n_tail, wt_ref[...], 0.0)
            ot_ref[...] = _dot_nt(xn_ref[...], wt.astype(BF16))

    mm = _dot_nt if has_tail else _dot
    acc = mm(xn_ref[...], w_ref[...].astype(BF16))
    if act == "relu2":
        acc = jnp.square(jnp.maximum(acc, 0.0))
    o_ref[...] = acc.astype(o_ref.dtype)


def _norm_mm(x, g, w3, layer, n_main, *, act=None, out_dtype=F32, tm=1088, tn=512):
    dual = isinstance(x, tuple)
    if dual:
        xp, xs = x
        n_ptiles = xp.shape[0] // tm
        m, kdim = (n_ptiles + 1) * tm, xp.shape[1]
        grid = (n_ptiles + 1, n_main // tn)
        in_specs = [
            pl.BlockSpec((tm, kdim), lambda i, j: (jnp.minimum(i, n_ptiles - 1), 0)),
            pl.BlockSpec(xs.shape, lambda i, j: (0, 0)),
        ]
        args = [xp, xs]
    else:
        n_ptiles = None
        m, kdim = x.shape
        grid = (m // tm, n_main // tn)
        in_specs = [pl.BlockSpec((tm, kdim), lambda i, j: (i, 0))]
        args = [x]
    n_tail = w3.shape[2] - n_main
    in_specs.append(pl.BlockSpec((1, kdim), lambda i, j: (0, 0)))
    args.append(g.reshape(1, kdim))
    out_shape = [jax.ShapeDtypeStruct((m, n_main), out_dtype)]
    out_specs = [pl.BlockSpec((tm, tn), lambda i, j: (i, j))]
    if n_tail > 0:
        w3t = jnp.transpose(w3, (0, 2, 1))
        in_specs += [pl.BlockSpec((None, tn, kdim), lambda i, j: (layer, j, 0)),
                     pl.BlockSpec((None, LANES, kdim), lambda i, j: (layer, n_main // LANES, 0))]
        args += [w3t, w3t]
        out_shape.append(jax.ShapeDtypeStruct((m, LANES), F32))
        out_specs.append(pl.BlockSpec((tm, LANES), lambda i, j: (i, 0)))
    else:
        in_specs.append(pl.BlockSpec((None, kdim, tn), lambda i, j: (layer, 0, j)))
        args.append(w3)
    body = functools.partial(_norm_mm_body, act=act, n_tail=n_tail, n_ptiles=n_ptiles,
                             tm=tm, rchunk=tm // 4)
    res = pl.pallas_call(
        body, grid=grid, in_specs=in_specs, out_specs=out_specs, out_shape=out_shape,
        scratch_shapes=[pltpu.VMEM((tm, kdim), BF16)],
        compiler_params=_cparams(("arbitrary", "arbitrary")),
        name="norm_mm",
    )(*args)
    return res if n_tail > 0 else res[0]


def _mm_res_body(a_ref, w_ref, r_ref, o_ref):
    @pl.when(pl.program_id(2) == 0)
    def _():
        o_ref[...] = r_ref[...] + _dot(a_ref[...].astype(BF16), w_ref[...].astype(BF16))

    @pl.when(pl.program_id(2) > 0)
    def _():
        o_ref[...] = o_ref[...] + _dot(a_ref[...].astype(BF16), w_ref[...].astype(BF16))


def _mm_res(a, w3, layer, res, *, tm=1088, tn=1024, tk=1024):
    m, kdim = a.shape
    n = w3.shape[2]
    grid = (m // tm, n // tn, kdim // tk)
    return pl.pallas_call(
        _mm_res_body, grid=grid,
        in_specs=[
            pl.BlockSpec((tm, tk), lambda i, j, k: (i, k)),
            pl.BlockSpec((None, tk, tn), lambda i, j, k: (layer, k, j)),
            pl.BlockSpec((tm, tn), lambda i, j, k: (i, j)),
        ],
        out_specs=pl.BlockSpec((tm, tn), lambda i, j, k: (i, j)),
        out_shape=jax.ShapeDtypeStruct((m, n), F32),
        compiler_params=_cparams(("arbitrary", "arbitrary", "arbitrary")),
        name="mm_res",
    )(a, w3, res)


def _mm_out_body(*refs, kcs, n_ptiles):
    n = len(kcs)
    ap, asm = refs[:n], refs[n:2 * n]
    w_ref, rp_ref, rs_ref, o_ref, wb_ref = refs[2 * n:]
    i = pl.program_id(1)

    @pl.when(i == 0)
    def _():
        wb_ref[...] = w_ref[...].astype(BF16)

    def compute(srcs, r_ref):
        acc = r_ref[...]
        off = 0
        for a_ref, kc in zip(srcs, kcs):
            acc = acc + _dot(a_ref[...], wb_ref[off:off + kc, :])
            off += kc
        o_ref[...] = acc

    @pl.when(i < n_ptiles)
    def _():
        compute(ap, rp_ref)

    @pl.when(i >= n_ptiles)
    def _():
        compute(asm, rs_ref)


def _mm_out(a_prompt, a_sample, w3, layer, res_prompt, res_sample, *, tm=512, tn=1024):
    kcs = tuple(a.shape[1] for a in a_prompt)
    kdim, n = sum(kcs), w3.shape[2]
    n_ptiles = a_prompt[0].shape[0] // tm
    m = a_prompt[0].shape[0] + a_sample[0].shape[0]
    (rp, rp0), (rs, rs0) = res_prompt, res_sample
    pidx = lambda i: jnp.minimum(i, n_ptiles - 1)
    in_specs = ([pl.BlockSpec((tm, kc), lambda j, i: (pidx(i), 0)) for kc in kcs]
                + [pl.BlockSpec((tm, kc), lambda j, i: (0, 0)) for kc in kcs]
                + [pl.BlockSpec((None, kdim, tn), lambda j, i: (layer, 0, j)),
                   pl.BlockSpec((tm, tn), lambda j, i: (rp0 + pidx(i), j)),
                   pl.BlockSpec((tm, tn), lambda j, i: (rs0, j))])
    body = functools.partial(_mm_out_body, kcs=kcs, n_ptiles=n_ptiles)
    return pl.pallas_call(
        body, grid=(n // tn, n_ptiles + 1), in_specs=in_specs,
        out_specs=pl.BlockSpec((tm, tn), lambda j, i: (i, j)),
        out_shape=jax.ShapeDtypeStruct((m, n), F32),
        scratch_shapes=[pltpu.VMEM((kdim, tn), BF16)],
        compiler_params=_cparams(("arbitrary", "arbitrary")),
        name="mm_out",
    )(*a_prompt, *a_sample, w3, rp, rs)


def _rmsnorm_body(x_ref, g_ref, op_ref, os_ref, *, n_ptiles):
    x = x_ref[...]
    ms = jnp.mean(x * x, axis=-1, keepdims=True)
    y = x * lax.rsqrt(ms + EPS) * g_ref[...]

    @pl.when(pl.program_id(0) < n_ptiles)
    def _():
        op_ref[...] = y

    @pl.when(pl.program_id(0) >= n_ptiles)
    def _():
        os_ref[...] = y


def _rmsnorm(x, g, m_prompt, *, tm=512):
    m, d = x.shape
    n_ptiles = m_prompt // tm
    return pl.pallas_call(
        functools.partial(_rmsnorm_body, n_ptiles=n_ptiles), grid=(n_ptiles + 1,),
        in_specs=[pl.BlockSpec((tm, d), lambda i: (i, 0)), pl.BlockSpec((1, d), lambda i: (0, 0))],
        out_specs=[pl.BlockSpec((tm, d), lambda i: (jnp.minimum(i, n_ptiles - 1), 0)),
                   pl.BlockSpec((tm, d), lambda i: (0, 0))],
        out_shape=[jax.ShapeDtypeStruct((m_prompt, d), F32),
                   jax.ShapeDtypeStruct((m - m_prompt, d), F32)],
        compiler_params=_cparams(("arbitrary",)),
        name="final_norm",
    )(x, g.reshape(1, d))


def _s5_consts_body(ar_ref, ai_ref, ls_ref, br_ref, bi_ref, cr_ref, ci_ref,
                    abr_ref, abi_ref, bre_ref, bim_ref, cre_ref, cim_ref):
    g, p, q = S5_GROUPS, S5_STATE, S5_GROUP
    gb = g // S5_NBLK
    gq, wid = g * q, gb * p
    ar = ar_ref[...]
    ai = ai_ref[...]
    dt = jnp.exp(ls_ref[...])
    mag = jnp.exp(ar * dt)
    abr = mag * jnp.cos(ai * dt)
    abi = mag * jnp.sin(ai * dt)
    abr_ref[...] = abr
    abi_ref[...] = abi
    lam2 = ar * ar + ai * ai
    zr = abr - 1.0
    cr = (zr * ar + abi * ai) / lam2
    ci = (abi * ar - zr * ai) / lam2

    rep = jnp.where((lax.broadcasted_iota(jnp.int32, (gq, g), 0) // q)
                    == lax.broadcasted_iota(jnp.int32, (gq, g), 1), 1.0, 0.0).astype(BF16)
    cr_r = _dot_exact_lhs(rep, cr)
    ci_r = _dot_exact_lhs(rep, ci)
    br = br_ref[...]
    bi = bi_ref[...]
    bbr = cr_r * br - ci_r * bi
    bbi = cr_r * bi + ci_r * br

    tile = jnp.where(lax.broadcasted_iota(jnp.int32, (p, wid), 0)
                     == lax.broadcasted_iota(jnp.int32, (p, wid), 1) % p, 1.0, 0.0).astype(BF16)
    keep = ((lax.broadcasted_iota(jnp.int32, (gq, wid), 0) // q) % gb
            == lax.broadcasted_iota(jnp.int32, (gq, wid), 1) // p)

    def spread(t):
        return jnp.where(keep, _dot(t.astype(BF16), tile), 0.0)

    bre_ref[...] = spread(bbr).astype(BF16).reshape(bre_ref.shape)
    bim_ref[...] = spread(bbi).astype(BF16).reshape(bim_ref.shape)
    dre = spread(cr_ref[...])
    dim = spread(ci_ref[...])
    rows = gb * q
    for k in range(S5_NBLK):
        cre_ref[k] = jnp.transpose(dre[k * rows:(k + 1) * rows, :]).astype(BF16)
        cim_ref[k] = jnp.transpose(dim[k * rows:(k + 1) * rows, :]).astype(BF16)


def _s5_consts(a_re, a_im, log_step, b_re, b_im, c_re, c_im):
    g, p, q = S5_GROUPS, S5_STATE, S5_GROUP
    gb = g // S5_NBLK
    brt = jnp.transpose(b_re, (0, 2, 1)).reshape(g * q, p)
    bit = jnp.transpose(b_im, (0, 2, 1)).reshape(g * q, p)
    full = lambda shape: pl.BlockSpec(shape, lambda: (0,) * len(shape))
    ins = [(g, p), (g, p), (g, 1)] + [(g * q, p)] * 4
    outs = [((g, p), F32)] * 2 + [((S5_NBLK, gb * q, gb * p), BF16)] * 2 + [((S5_NBLK, gb * p, gb * q), BF16)] * 2
    abr, abi, bre, bim, cre, cim = pl.pallas_call(
        _s5_consts_body, in_specs=[full(s) for s in ins], out_specs=[full(s) for s, _ in outs],
        out_shape=[jax.ShapeDtypeStruct(s, d) for s, d in outs],
        compiler_params=pltpu.CompilerParams(vmem_limit_bytes=VMEM_LIMIT),
        name="s5_consts",
    )(a_re, a_im, log_step.reshape(g, 1), brt, bit, c_re.reshape(g * q, p), c_im.reshape(g * q, p))
    return abr.reshape(1, S5_CH), abi.reshape(1, S5_CH), bre, bim, cre, cim


def _s5_perm(nb, tl):
    r = np.arange(nb * tl)
    p = np.zeros((nb * tl, nb * tl), np.float32)
    p[r, (r % nb) * tl + r // nb] = 1.0
    return p


def _s5_body(*refs, nsrc, nb, has_init, permute):
    u_refs, refs = refs[:nsrc], refs[nsrc:]
    if permute:
        pm_ref, pmt_ref = refs[:2]
        refs = refs[2:]
    bre_ref, bim_ref, cre_ref, cim_ref, ar_ref, ai_ref, d_ref, gw_ref, gb_ref = refs[:9]
    refs = refs[9:]
    if has_init:
        h0r_ref, h0i_ref, y_ref, hr_ref, hi_ref, sre, sim, ysc = refs
    else:
        y_ref, hr_ref, hi_ref, sre, sim, ysc = refs
    rows = sre.shape[0]
    tl = rows // nb
    cb = S5_CH // S5_NBLK
    ub = S5_WIDTH // S5_NBLK
    hw = cb // 2

    @pl.when(pl.program_id(0) == 0)
    def _():
        if has_init:
            hr_ref[...] = h0r_ref[...]
            hi_ref[...] = h0i_ref[...]
        else:
            hr_ref[...] = jnp.zeros_like(hr_ref)
            hi_ref[...] = jnp.zeros_like(hi_ref)

    u_b = u_refs[0][...] if nsrc == 1 else jnp.concatenate([r[...] for r in u_refs], axis=0)
    if permute:
        u_tm = _dot(pm_ref[...], u_b)
        u_b = u_tm.astype(BF16)
    else:
        u_tm = u_b.astype(F32)
    low_rows = lax.broadcasted_iota(jnp.int32, (SUBLANES, hw), 0) < nb

    for k in range(S5_NBLK):
        uk = u_b[:, k * ub:(k + 1) * ub]
        sre[...] = _dot(uk, bre_ref[k])
        sim[...] = _dot(uk, bim_ref[k])
        for half in range(2):
            c0 = k * cb + half * hw
            l0 = half * hw
            ar8 = jnp.broadcast_to(ar_ref[:, c0:c0 + hw], (SUBLANES, hw))
            ai8 = jnp.broadcast_to(ai_ref[:, c0:c0 + hw], (SUBLANES, hw))

            if nb < SUBLANES:
                def j_body(j, c, l0=l0, ar8=ar8, ai8=ai8):
                    sr, si = c
                    row = pl.multiple_of(j * SUBLANES, SUBLANES)
                    dr = sre[pl.ds(row, SUBLANES), l0:l0 + hw]
                    di = sim[pl.ds(row, SUBLANES), l0:l0 + hw]
                    pr = pltpu.roll(sr, nb, 0)
                    pi = pltpu.roll(si, nb, 0)
                    yr = ar8 * pr - ai8 * pi + dr
                    yi = ar8 * pi + ai8 * pr + di
                    qr = pltpu.roll(yr, nb, 0)
                    qi = pltpu.roll(yi, nb, 0)
                    zr = ar8 * qr - ai8 * qi + dr
                    zi = ar8 * qi + ai8 * qr + di
                    sre[pl.ds(row, SUBLANES), l0:l0 + hw] = jnp.where(low_rows, yr, zr)
                    sim[pl.ds(row, SUBLANES), l0:l0 + hw] = jnp.where(low_rows, yi, zi)
                    return zr, zi

                sr, si = lax.fori_loop(0, rows // SUBLANES, j_body,
                                       (hr_ref[:, c0:c0 + hw], hi_ref[:, c0:c0 + hw]), unroll=True)
                hr_ref[:, c0:c0 + hw] = sr
                hi_ref[:, c0:c0 + hw] = si
            else:
                def rg_body(rg, carry, c0=c0, l0=l0, ar8=ar8, ai8=ai8):
                    r0 = pl.multiple_of(rg * SUBLANES, SUBLANES)
                    xr = hr_ref[pl.ds(r0, SUBLANES), c0:c0 + hw]
                    xi = hi_ref[pl.ds(r0, SUBLANES), c0:c0 + hw]
                    for t in range(tl):
                        row = pl.multiple_of(t * nb + r0, SUBLANES)
                        nr = ar8 * xr - ai8 * xi + sre[pl.ds(row, SUBLANES), l0:l0 + hw]
                        ni = ar8 * xi + ai8 * xr + sim[pl.ds(row, SUBLANES), l0:l0 + hw]
                        sre[pl.ds(row, SUBLANES), l0:l0 + hw] = nr
                        sim[pl.ds(row, SUBLANES), l0:l0 + hw] = ni
                        xr, xi = nr, ni
                    hr_ref[pl.ds(r0, SUBLANES), c0:c0 + hw] = xr
                    hi_ref[pl.ds(r0, SUBLANES), c0:c0 + hw] = xi
                    return carry

                lax.fori_loop(0, nb // SUBLANES, rg_body, 0)
        ysc[:, k * ub:(k + 1) * ub] = (_dot(sre[...].astype(BF16), cre_ref[k])
                                       - _dot(sim[...].astype(BF16), cim_ref[k]))

    y = ysc[...] + d_ref[...] * u_tm
    y = 0.5 * y * (1.0 + jnp.tanh(math.sqrt(2.0 / math.pi) * (y + 0.044715 * (y * y * y))))
    z = _dot(y.astype(BF16), gw_ref[...]) + gb_ref[...]
    out = (y * _sigmoid(z)).astype(BF16)
    if permute:
        out = _dot(pmt_ref[...], out).astype(BF16)
    y_ref[...] = out.reshape(y_ref.shape)


def _s5_call(src, bre, bim, cre, cim, abar_re, abar_im, d_skip, glu_w, glu_b, h0, *,
             row0, nsrc, rs, seq_len, nb, tl, permute):
    rows = nsrc * rs
    ntiles = seq_len // rs if nsrc > 1 else 1
    has_init = h0 is not None
    srows = max(nb, SUBLANES)
    const = lambda shape: pl.BlockSpec(shape, lambda t: (0,) * len(shape))
    in_specs = [pl.BlockSpec((rs, S5_WIDTH), lambda t, s=s: ((row0 + s * seq_len) // rs + t, 0))
                for s in range(nsrc)]
    args = [src] * nsrc
    if permute:
        pm = _s5_perm(nb, tl)
        in_specs += [const((rows, rows)), const((rows, rows))]
        args += [jnp.asarray(pm, BF16), jnp.asarray(pm.T, BF16)]
    in_specs += [
        const(bre.shape), const(bim.shape), const(cre.shape), const(cim.shape),
        const((1, S5_CH)), const((1, S5_CH)), const((1, S5_WIDTH)),
        const((S5_WIDTH, S5_WIDTH)), const((1, S5_WIDTH)),
    ]
    args += [bre, bim, cre, cim, abar_re, abar_im, d_skip, glu_w, glu_b]
    if has_init:
        in_specs += [const((srows, S5_CH)), const((srows, S5_CH))]
        args += [h0[0], h0[1]]
    body = functools.partial(_s5_body, nsrc=nsrc, nb=nb, has_init=has_init, permute=permute)
    return pl.pallas_call(
        body, grid=(ntiles,), in_specs=in_specs,
        out_specs=[pl.BlockSpec((nsrc, rs, S5_WIDTH), lambda t: (0, t, 0)),
                   const((srows, S5_CH)), const((srows, S5_CH))],
        out_shape=[jax.ShapeDtypeStruct((nsrc, ntiles * rs, S5_WIDTH), BF16),
                   jax.ShapeDtypeStruct((srows, S5_CH), F32),
                   jax.ShapeDtypeStruct((srows, S5_CH), F32)],
        scratch_shapes=[pltpu.VMEM((rows, S5_CH // S5_NBLK), F32),
                        pltpu.VMEM((rows, S5_CH // S5_NBLK), F32),
                        pltpu.VMEM((rows, S5_WIDTH), F32)],
        compiler_params=_cparams(("arbitrary",)),
        name="s5_mixer",
    )(*args)


def _gla_body(*refs, nbb, nch, cs, has_init, hps):
    if has_init:
        (q_ref, k_ref, v_ref, g_ref, gl_ref, up_ref, gb_ref, nw_ref, s0_ref, y_ref, s_ref) = refs
    else:
        (q_ref, k_ref, v_ref, g_ref, gl_ref, up_ref, gb_ref, nw_ref, y_ref, s_ref) = refs
    rows = nbb * nch * cs
    sh = int(math.log2(cs))

    @pl.when(pl.program_id(2) == 0)
    def _():
        if has_init:
            s_ref[...] = s0_ref[...]
        else:
            s_ref[...] = jnp.zeros_like(s_ref)

    ri = lax.broadcasted_iota(jnp.int32, (rows, rows), 0)
    ci = lax.broadcasted_iota(jnp.int32, (rows, rows), 1)
    same = (ri >> sh) == (ci >> sh)
    causal = jnp.logical_and(same, ci <= ri)
    lmat = jnp.where(causal, 1.0, 0.0).astype(BF16)
    tmat = jnp.where(same, 1.0, 0.0).astype(BF16)
    rchunk = lax.broadcasted_iota(jnp.int32, (rows, 1), 0) >> sh
    cchunk = lax.broadcasted_iota(jnp.int32, (1, rows), 1) >> sh
    single = nbb * nch == 1
    z = _dot(gl_ref[...].astype(BF16), up_ref[...].astype(BF16)) + gb_ref[...]
    la = _log_sigmoid(z) * (1.0 / GLA_TAU)
    p1, p2, p3 = _split3(la)
    bcum_all = _dot(lmat, p1) + _dot(lmat, p2) + _dot(lmat, p3)
    tot_all = _dot(tmat, p1) + _dot(tmat, p2) + _dot(tmat, p3)

    for hh in range(hps):
        ks = slice(hh * GLA_HK, (hh + 1) * GLA_HK)
        vs = slice(hh * GLA_HV, (hh + 1) * GLA_HV)
        q = q_ref[:, ks].astype(F32) * (GLA_HK ** -0.5)
        k = k_ref[:, ks].astype(F32)
        vb = v_ref[:, vs]
        bcum = bcum_all[:, ks]
        tot = tot_all[:, ks]

        qd = q * jnp.exp(bcum)
        kd = k * jnp.exp(-bcum)
        kdec = k * jnp.exp(tot - bcum)
        qdb = qd.astype(BF16)
        att = jnp.where(causal, _dot_nt(qdb, kd.astype(BF16)), 0.0)
        o = _dot(att.astype(BF16), vb)

        kdec_t = jnp.transpose(kdec)
        dec_t = jnp.transpose(jnp.exp(tot))
        for b in range(nbb):
            s = s_ref[b, hh]
            for c in range(nch):
                idx = b * nch + c
                oi = _dot(qdb, s.astype(BF16))
                o = o + (oi if single else jnp.where(rchunk == idx, oi, 0.0))
                kt = kdec_t if single else jnp.where(cchunk == idx, kdec_t, 0.0)
                s = dec_t[:, idx * cs:idx * cs + 1] * s + _dot(kt.astype(BF16), vb)
            s_ref[b, hh] = s

        o = o * lax.rsqrt(jnp.mean(o * o, axis=-1, keepdims=True) + EPS) * nw_ref[...]
        gg = g_ref[:, vs].astype(F32)
        y_ref[:, vs] = (o * (gg * _sigmoid(gg))).astype(y_ref.dtype)


def _gla_call(projm, projt, up_pad, gkb, nw, s0, *, row0, nseq_blocks, ntiles, nbb, nch, cs, batch, hps):
    rows = nbb * nch * cs
    has_init = s0 is not None
    rb0 = row0 // rows
    rowblk = lambda s, h, t: rb0 + s * ntiles + t
    wk, wv = hps * GLA_HK, hps * GLA_HV
    qoff = S5_WIDTH // wk
    koff = qoff + GLA_HEADS // hps
    voff = (S5_WIDTH + 2 * GLA_HEADS * GLA_HK) // wv
    goff = voff + GLA_HEADS // hps
    in_specs = [
        pl.BlockSpec((rows, wk), lambda s, h, t: (rowblk(s, h, t), qoff + h)),
        pl.BlockSpec((rows, wk), lambda s, h, t: (rowblk(s, h, t), koff + h)),
        pl.BlockSpec((rows, wv), lambda s, h, t: (rowblk(s, h, t), voff + h)),
        pl.BlockSpec((rows, wv), lambda s, h, t: (rowblk(s, h, t), goff + h)),
        pl.BlockSpec((rows, LANES), lambda s, h, t: (rowblk(s, h, t), 0)),
        pl.BlockSpec((LANES, wk), lambda s, h, t: (0, h)),
        pl.BlockSpec((1, wk), lambda s, h, t: (0, h)),
        pl.BlockSpec((1, GLA_HV), lambda s, h, t: (0, 0)),
    ]
    args = [projm, projm, projm, projm, projt, up_pad, gkb, nw]
    if has_init:
        in_specs.append(pl.BlockSpec((nbb, hps, GLA_HK, GLA_HV), lambda s, h, t: (s, h, 0, 0)))
        args.append(s0)
    body = functools.partial(_gla_body, nbb=nbb, nch=nch, cs=cs, has_init=has_init, hps=hps)
    return pl.pallas_call(
        body, grid=(nseq_blocks, GLA_HEADS // hps, ntiles), in_specs=in_specs,
        out_specs=[pl.BlockSpec((rows, wv), lambda s, h, t: (s * ntiles + t, h)),
                   pl.BlockSpec((nbb, hps, GLA_HK, GLA_HV), lambda s, h, t: (s, h, 0, 0))],
        out_shape=[jax.ShapeDtypeStruct((nseq_blocks * ntiles * rows, GLA_HEADS * GLA_HV), BF16),
                   jax.ShapeDtypeStruct((batch, GLA_HEADS, GLA_HK, GLA_HV), F32)],
        compiler_params=_cparams(("arbitrary", "arbitrary", "arbitrary")),
        name="gla_mixer",
    )(*args)


def _mlstm_body(*refs, nbb, cl, rp, has_init, hps):
    if has_init:
        (q_ref, k_ref, v_ref, og_ref, gt_ref, gb_ref, nw_ref, c0_ref, n0_ref, m0_ref,
         y_ref, c_ref, n_ref, m_ref) = refs
    else:
        (q_ref, k_ref, v_ref, og_ref, gt_ref, gb_ref, nw_ref, y_ref, c_ref, n_ref, m_ref) = refs
    rows = nbb * cl
    sh = int(math.log2(cl))
    head0 = pl.program_id(1) * hps

    @pl.when(pl.program_id(2) == 0)
    def _():
        if has_init:
            c_ref[...] = c0_ref[...]
            n_ref[...] = n0_ref[...]
            m_ref[...] = m0_ref[...]
        else:
            c_ref[...] = jnp.zeros_like(c_ref)
            n_ref[...] = jnp.zeros_like(n_ref)
            m_ref[...] = jnp.full(m_ref.shape, -jnp.inf, F32)

    def pad(x):
        if rp == rows:
            return x
        return jnp.concatenate([x, jnp.zeros((rp - rows, x.shape[1]), x.dtype)], axis=0)

    ksc = ML_DH ** -0.5
    gates = pad(gt_ref[...] + gb_ref[...])
    lane = lax.broadcasted_iota(jnp.int32, (rp, LANES), 1)
    ri = lax.broadcasted_iota(jnp.int32, (rp, rp), 0)
    ci = lax.broadcasted_iota(jnp.int32, (rp, rp), 1)
    causal = jnp.logical_and((ri >> sh) == (ci >> sh), ci <= ri)
    lmat = jnp.where(causal, 1.0, 0.0).astype(BF16)
    neg = jnp.where(causal, 0.0, -jnp.inf)
    er = lax.broadcasted_iota(jnp.int32, (SUBLANES, LANES), 0)
    ec = lax.broadcasted_iota(jnp.int32, (SUBLANES, LANES), 1)
    pick = jnp.where(er == ec, 1.0, 0.0).astype(BF16)
    rbatch = lax.broadcasted_iota(jnp.int32, (rp, 1), 0) >> sh
    cbatch = lax.broadcasted_iota(jnp.int32, (1, rp), 1) >> sh

    for hh in range(hps):
        head = head0 + hh
        hs = slice(hh * ML_DH, (hh + 1) * ML_DH)
        qb = pad(q_ref[:, hs])
        kb = pad(k_ref[:, hs])
        vb = pad(v_ref[:, hs])
        og = pad(og_ref[:, hs]).astype(F32)
        ig = jnp.sum(jnp.where(lane == head, gates, 0.0), axis=1, keepdims=True)
        fg = jnp.sum(jnp.where(lane == head + ML_HEADS, gates, 0.0), axis=1, keepdims=True)
        lf = _log_sigmoid(fg)
        fcum = _dot_exact_lhs(lmat, jnp.where(lane == 0, lf, 0.0))[:, 0:1]
        cols = jnp.where(lane == 0, fcum, jnp.where(lane == 1, ig, 0.0))
        as_rows = _dot_nt_exact_lhs(pick, cols)
        fcum_row = as_rows[0:1, :]
        ig_row = as_rows[1:2, :]

        mm = jnp.zeros((rp, 1), F32)
        for b in range(nbb):
            mb = m_ref[b, hh][:, 0:1]
            mm = jnp.where(rbatch == b, mb, mm) if nbb > 1 else jnp.broadcast_to(mb, (rp, 1))

        dmat = fcum - fcum_row + ig_row + neg
        dprev = fcum + mm
        m = jnp.maximum(jnp.max(dmat, axis=1, keepdims=True), dprev)
        w = jnp.exp(dmat - m + math.log(ksc))
        wp = jnp.exp(dprev - m)
        sc = _dot_nt(qb, kb) * w
        num = _dot(sc.astype(BF16), vb)
        den = jnp.sum(sc, axis=1, keepdims=True)

        k_t = jnp.transpose(kb.astype(F32))
        for b in range(nbb):
            cm = c_ref[b, hh]
            nm = n_ref[b, hh]
            mb = m_ref[b, hh][:, 0:1]
            qc = wp * _dot(qb, cm.astype(BF16))
            nm8 = jnp.broadcast_to(nm, (SUBLANES, ML_DH)).astype(BF16)
            qn = wp * _dot_nt(qb, nm8)[:, 0:1]
            if nbb > 1:
                rsel = rbatch == b
                qc = jnp.where(rsel, qc, 0.0)
                qn = jnp.where(rsel, qn, 0.0)
            num = num + qc
            den = den + qn
            last = b * cl + cl - 1
            m_new = m[last:last + 1, :]
            fl = fcum[last:last + 1, :]
            decay = jnp.exp(fl + mb - m_new)
            wk_row = ksc * jnp.exp(fl - fcum_row + ig_row - m_new)
            if nbb > 1:
                wk_row = jnp.where(cbatch == b, wk_row, 0.0)
            c_ref[b, hh] = decay * cm + _dot((k_t * wk_row).astype(BF16), vb)
            wk8 = jnp.broadcast_to(wk_row, (SUBLANES, rp)).astype(BF16)
            n_ref[b, hh] = decay * nm + _dot(wk8, kb)[0:1, :]
            m_ref[b, hh] = jnp.broadcast_to(m_new, (1, LANES))

        hv = num / jnp.maximum(jnp.abs(den), jnp.exp(-m))
        hv = hv * _sigmoid(og)
        hv = hv * lax.rsqrt(jnp.mean(hv * hv, axis=-1, keepdims=True) + EPS) * nw_ref[...]
        y_ref[:, hs] = hv[:rows].astype(y_ref.dtype)


def _mlstm_call(projm, projt, gate_b, nw, init, *, row0, nseq_blocks, ntiles, nbb, cl, rp, batch, hps):
    rows = nbb * cl
    has_init = init is not None
    rb0 = row0 // rows
    rowblk = lambda s, h, t: rb0 + s * ntiles + t
    hd = ML_HEADS // hps
    wd = hps * ML_DH
    in_specs = [
        pl.BlockSpec((rows, wd), lambda s, h, t: (rowblk(s, h, t), h)),
        pl.BlockSpec((rows, wd), lambda s, h, t: (rowblk(s, h, t), hd + h)),
        pl.BlockSpec((rows, wd), lambda s, h, t: (rowblk(s, h, t), 2 * hd + h)),
        pl.BlockSpec((rows, wd), lambda s, h, t: (rowblk(s, h, t), 3 * hd + h)),
        pl.BlockSpec((rows, LANES), lambda s, h, t: (rowblk(s, h, t), 0)),
        pl.BlockSpec((1, LANES), lambda s, h, t: (0, 0)),
        pl.BlockSpec((1, ML_DH), lambda s, h, t: (0, 0)),
    ]
    args = [projm, projm, projm, projm, projt, gate_b, nw]
    c_spec = pl.BlockSpec((nbb, hps, ML_DH, ML_DH), lambda s, h, t: (s, h, 0, 0))
    n_spec = pl.BlockSpec((nbb, hps, 1, ML_DH), lambda s, h, t: (s, h, 0, 0))
    m_spec = pl.BlockSpec((nbb, hps, 1, LANES), lambda s, h, t: (s, h, 0, 0))
    if has_init:
        in_specs += [c_spec, n_spec, m_spec]
        args += list(init)
    body = functools.partial(_mlstm_body, nbb=nbb, cl=cl, rp=rp, has_init=has_init, hps=hps)
    return pl.pallas_call(
        body, grid=(nseq_blocks, hd, ntiles), in_specs=in_specs,
        out_specs=[pl.BlockSpec((rows, wd), lambda s, h, t: (s * ntiles + t, h)),
                   c_spec, n_spec, m_spec],
        out_shape=[jax.ShapeDtypeStruct((nseq_blocks * ntiles * rows, ML_HEADS * ML_DH), BF16),
                   jax.ShapeDtypeStruct((batch, ML_HEADS, ML_DH, ML_DH), F32),
                   jax.ShapeDtypeStruct((batch, ML_HEADS, 1, ML_DH), F32),
                   jax.ShapeDtypeStruct((batch, ML_HEADS, 1, LANES), F32)],
        compiler_params=_cparams(("arbitrary", "arbitrary", "arbitrary")),
        name="mlstm_mixer",
    )(*args)


def kernel(x_prompt, x_sample, state_s5_re, state_s5_im, state_gla, state_mlstm_c, state_mlstm_n,
           state_mlstm_m, norm_mix, norm_mlp, norm_final, w_in_even, s5_a_re, s5_a_im, s5_log_step,
           s5_b_re, s5_b_im, s5_c_re, s5_c_im, s5_d, s5_glu_w, s5_glu_b, gla_gk_up, gla_gk_b,
           gla_norm, w_out_even, w_in_odd, mlstm_b_i, mlstm_b_f, mlstm_norm, w_out_odd,
           w_mlp_up, w_mlp_down):
    bp, lp, d = x_prompt.shape
    bs, ls, _ = x_sample.shape
    mp = bp * lp
    msamp = bs * ls
    xp2 = x_prompt.reshape(mp, d)
    xs2 = x_sample.reshape(msamp, d)

    n_even = S5_WIDTH + 2 * GLA_HEADS * GLA_HK + 2 * GLA_HEADS * GLA_HV
    projm, projt = _norm_mm((xp2, xs2), norm_mix[0], w_in_even, 0, n_even, tm=1024, out_dtype=BF16)

    abar_re, abar_im, bre, bim, cre, cim = _s5_consts(s5_a_re[0], s5_a_im[0], s5_log_step[0], s5_b_re[0],
                                                      s5_b_im[0], s5_c_re[0], s5_c_im[0])
    s5_consts = (bre, bim, cre, cim, abar_re, abar_im, s5_d[0].reshape(1, S5_WIDTH),
                 s5_glu_w[0].astype(BF16), s5_glu_b[0].reshape(1, S5_WIDTH))

    s5_tl = 64
    ys5_p, hr_p, hi_p = _s5_call(projm, *s5_consts, None, row0=0, nsrc=bp, rs=s5_tl, seq_len=lp,
                                 nb=bp, tl=s5_tl, permute=True)
    ys5_p = ys5_p.reshape(mp, S5_WIDTH)
    hr_p = hr_p[SUBLANES - bp:]
    hi_p = hi_p[SUBLANES - bp:]
    u_s = jnp.transpose(projm[mp:mp + msamp, :S5_WIDTH].reshape(bs, ls, S5_WIDTH), (1, 0, 2))
    h0 = (state_s5_re[0].reshape(bs, S5_CH), state_s5_im[0].reshape(bs, S5_CH))
    ys5_s, hr_s, hi_s = _s5_call(u_s.reshape(msamp, S5_WIDTH), *s5_consts, h0, row0=0, nsrc=1, rs=msamp,
                                 seq_len=msamp, nb=bs, tl=ls, permute=False)
    ys5_s = jnp.transpose(ys5_s.reshape(ls, bs, S5_WIDTH), (1, 0, 2)).reshape(msamp, S5_WIDTH)

    up_pad = jnp.pad(gla_gk_up[0], ((0, LANES - gla_gk_up.shape[1]), (0, 0)))
    gkb = gla_gk_b[0].reshape(1, -1)
    gnw = gla_norm[0].reshape(1, GLA_HV)
    ptile = 256
    ygla_p, sg_p = _gla_call(projm, projt, up_pad, gkb, gnw, None, row0=0, nseq_blocks=bp,
                             ntiles=lp // ptile, nbb=1, nch=ptile // GLA_CHUNK, cs=GLA_CHUNK, batch=bp,
                             hps=GLA_HEADS)
    gnbb = 32
    ygla_s, sg_s = _gla_call(projm, projt, up_pad, gkb, gnw, state_gla[0], row0=mp,
                             nseq_blocks=bs // gnbb, ntiles=1, nbb=gnbb, nch=1, cs=ls, batch=bs, hps=1)

    h = _mm_out([ys5_p, ygla_p], [ys5_s, ygla_s], w_out_even, 0, (xp2, 0), (xs2, 0))
    hid = _norm_mm(h, norm_mlp[0], w_mlp_up, 0, D_FF, act="relu2", out_dtype=BF16, tn=1024)
    h = _mm_res(hid, w_mlp_down, 0, h, tn=D_MODEL, tk=512)

    n_odd = 4 * ML_HEADS * ML_DH
    projm, projt = _norm_mm(h, norm_mix[1], w_in_odd, 0, n_odd, out_dtype=BF16, tn=1024)
    gate_b = jnp.pad(jnp.concatenate([mlstm_b_i[0], mlstm_b_f[0]]), (0, LANES - 2 * ML_HEADS)).reshape(1, LANES)
    mnw = mlstm_norm[0].reshape(1, ML_DH)
    yml_p, c_p, n_p, m_p = _mlstm_call(projm, projt, gate_b, mnw, None, row0=0, nseq_blocks=bp,
                                       ntiles=lp // ML_CHUNK, nbb=1, cl=ML_CHUNK, rp=ML_CHUNK, batch=bp,
                                       hps=ML_HEADS)
    mnbb = 8
    init = (state_mlstm_c[0], state_mlstm_n[0].reshape(bs, ML_HEADS, 1, ML_DH),
            jnp.broadcast_to(state_mlstm_m[0][:, :, None, None], (bs, ML_HEADS, 1, LANES)))
    yml_s, c_s, n_s, m_s = _mlstm_call(projm, projt, gate_b, mnw, init, row0=mp, nseq_blocks=bs // mnbb,
                                       ntiles=1, nbb=mnbb, cl=ls, rp=LANES, batch=bs, hps=1)
    h = _mm_out([yml_p], [yml_s], w_out_odd, 0, (h, 0), (h, mp // 512))
    hid = _norm_mm(h, norm_mlp[1], w_mlp_up, 1, D_FF, act="relu2", out_dtype=BF16, tn=1024)
    h = _mm_res(hid, w_mlp_down, 1, h, tn=D_MODEL, tk=512)

    y_p, y_s = _rmsnorm(h, norm_final, mp)
    g, p = S5_GROUPS, S5_STATE
    return (y_p.reshape(bp, lp, d), y_s.reshape(bs, ls, d),
            hr_p.reshape(1, bp, g, p), hi_p.reshape(1, bp, g, p), sg_p[None],
            c_p[None], n_p.reshape(1, bp, ML_HEADS, ML_DH), m_p[:, :, 0, 0][None],
            hr_s.reshape(1, bs, g, p), hi_s.reshape(1, bs, g, p), sg_s[None],
            c_s[None], n_s.reshape(1, bs, ML_HEADS, ML_DH), m_s[:, :, 0, 0][None])
```

```python
import functools
import math

import jax
import jax.numpy as jnp
import numpy as np
from jax import lax
from jax.experimental import pallas as pl
from jax.experimental.pallas import tpu as pltpu

F32 = jnp.float32
BF16 = jnp.bfloat16
EPS = 1e-6

D_MODEL = 2048
D_FF = 4 * D_MODEL
S5_WIDTH = 1024
S5_GROUPS = 64
S5_GROUP = 16
S5_STATE = 64
S5_CH = S5_GROUPS * S5_STATE
S5_NBLK = 4
GLA_HEADS = 4
GLA_HK = 128
GLA_HV = 256
GLA_TAU = 16.0
GLA_CHUNK = 64
ML_HEADS = 4
ML_DH = 512
ML_CHUNK = 256

SUBLANES = 8
LANES = 128
VMEM_LIMIT = 56 * 1024 * 1024


def _cparams(sem):
    return pltpu.CompilerParams(dimension_semantics=sem, vmem_limit_bytes=VMEM_LIMIT)


def _dot(a, b):
    return jnp.dot(a, b, preferred_element_type=F32)


def _dot_nt(a, b):
    return lax.dot_general(a, b, (((1,), (1,)), ((), ())), preferred_element_type=F32)


def _split3(x):
    p1 = x.astype(BF16)
    r1 = x - p1.astype(F32)
    p2 = r1.astype(BF16)
    r2 = r1 - p2.astype(F32)
    p3 = r2.astype(BF16)
    return p1, p2, p3


def _dot_exact_lhs(m, x):
    p1, p2, p3 = _split3(x)
    return _dot(m, p1) + _dot(m, p2) + _dot(m, p3)


def _dot_nt_exact_lhs(m, x):
    p1, p2, p3 = _split3(x)
    return _dot_nt(m, p1) + _dot_nt(m, p2) + _dot_nt(m, p3)


def _log_sigmoid(x):
    return jnp.minimum(x, 0.0) - jnp.log1p(jnp.exp(-jnp.abs(x)))


def _sigmoid(x):
    return 1.0 / (1.0 + jnp.exp(-x))


def _norm_mm_body(*refs, act, n_tail, tm, rchunk):
    has_tail = n_tail > 0
    if has_tail:
        x_ref, g_ref, w_ref, wt_ref, o_ref, ot_ref, xn_ref = refs
    else:
        x_ref, g_ref, w_ref, o_ref, xn_ref = refs

    @pl.when(pl.program_id(1) == 0)
    def _():
        g = g_ref[...]
        for r in range(0, tm, rchunk):
            x = x_ref[r:r + rchunk, :]
            ms = jnp.mean(x * x, axis=-1, keepdims=True)
            xn_ref[r:r + rchunk, :] = (x * lax.rsqrt(ms + EPS) * g).astype(BF16)
        if has_tail:
            col = lax.broadcasted_iota(jnp.int32, wt_ref.shape, 0)
            wt = jnp.where(col < n_tail, wt_ref[...], 0.0)
            ot_ref[...] = _dot_nt(xn_ref[...], wt.astype(BF16))

    mm = _dot_nt if has_tail else _dot
    acc = mm(xn_ref[...], w_ref[...].astype(BF16))
    if act == "relu2":
        acc = jnp.square(jnp.maximum(acc, 0.0))
    o_ref[...] = acc.astype(o_ref.dtype)


def _norm_mm(x, g, w3, layer, n_main, *, act=None, out_dtype=F32, tm=1088, tn=512):
    m, kdim = x.shape
    grid = (m // tm, n_main // tn)
    n_tail = w3.shape[2] - n_main
    in_specs = [pl.BlockSpec((tm, kdim), lambda i, j: (i, 0)),
                pl.BlockSpec((1, kdim), lambda i, j: (0, 0))]
    args = [x, g.reshape(1, kdim)]
    out_shape = [jax.ShapeDtypeStruct((m, n_main), out_dtype)]
    out_specs = [pl.BlockSpec((tm, tn), lambda i, j: (i, j))]
    if n_tail > 0:
        w3t = jnp.transpose(w3, (0, 2, 1))
        in_specs += [pl.BlockSpec((None, tn, kdim), lambda i, j: (layer, j, 0)),
                     pl.BlockSpec((None, LANES, kdim), lambda i, j: (layer, n_main // LANES, 0))]
        args += [w3t, w3t]
        out_shape.append(jax.ShapeDtypeStruct((m, LANES), F32))
        out_specs.append(pl.BlockSpec((tm, LANES), lambda i, j: (i, 0)))
    else:
        in_specs.append(pl.BlockSpec((None, kdim, tn), lambda i, j: (layer, 0, j)))
        args.append(w3)
    body = functools.partial(_norm_mm_body, act=act, n_tail=n_tail, tm=tm, rchunk=tm // 4)
    res = pl.pallas_call(
        body, grid=grid, in_specs=in_specs, out_specs=out_specs, out_shape=out_shape,
        scratch_shapes=[pltpu.VMEM((tm, kdim), BF16)],
        compiler_params=_cparams(("arbitrary", "arbitrary")),
        name="norm_mm",
    )(*args)
    return res if n_tail > 0 else res[0]


def _mm_res_body(a_ref, w_ref, r_ref, o_ref):
    @pl.when(pl.program_id(2) == 0)
    def _():
        o_ref[...] = r_ref[...] + _dot(a_ref[...].astype(BF16), w_ref[...].astype(BF16))

    @pl.when(pl.program_id(2) > 0)
    def _():
        o_ref[...] = o_ref[...] + _dot(a_ref[...].astype(BF16), w_ref[...].astype(BF16))


def _mm_res(a, w3, layer, res, *, tm=1088, tn=1024, tk=1024):
    m, kdim = a.shape
    n = w3.shape[2]
    grid = (m // tm, n // tn, kdim // tk)
    return pl.pallas_call(
        _mm_res_body, grid=grid,
        in_specs=[
            pl.BlockSpec((tm, tk), lambda i, j, k: (i, k)),
            pl.BlockSpec((None, tk, tn), lambda i, j, k: (layer, k, j)),
            pl.BlockSpec((tm, tn), lambda i, j, k: (i, j)),
        ],
        out_specs=pl.BlockSpec((tm, tn), lambda i, j, k: (i, j)),
        out_shape=jax.ShapeDtypeStruct((m, n), F32),
        compiler_params=_cparams(("arbitrary", "arbitrary", "arbitrary")),
        name="mm_res",
    )(a, w3, res)


def _mm_out_body(*refs, kcs, n_ptiles):
    n = len(kcs)
    ap, asm = refs[:n], refs[n:2 * n]
    w_ref, rp_ref, rs_ref, o_ref, wb_ref = refs[2 * n:]
    i = pl.program_id(1)

    @pl.when(i == 0)
    def _():
        wb_ref[...] = w_ref[...].astype(BF16)

    def compute(srcs, r_ref):
        acc = r_ref[...]
        off = 0
        for a_ref, kc in zip(srcs, kcs):
            acc = acc + _dot(a_ref[...], wb_ref[off:off + kc, :])
            off += kc
        o_ref[...] = acc

    @pl.when(i < n_ptiles)
    def _():
        compute(ap, rp_ref)

    @pl.when(i >= n_ptiles)
    def _():
        compute(asm, rs_ref)


def _mm_out(a_prompt, a_sample, w3, layer, res_prompt, res_sample, *, tm=512, tn=1024):
    kcs = tuple(a.shape[1] for a in a_prompt)
    kdim, n = sum(kcs), w3.shape[2]
    n_ptiles = a_prompt[0].shape[0] // tm
    m = a_prompt[0].shape[0] + a_sample[0].shape[0]
    (rp, rp0), (rs, rs0) = res_prompt, res_sample
    pidx = lambda i: jnp.minimum(i, n_ptiles - 1)
    in_specs = ([pl.BlockSpec((tm, kc), lambda j, i: (pidx(i), 0)) for kc in kcs]
                + [pl.BlockSpec((tm, kc), lambda j, i: (0, 0)) for kc in kcs]
                + [pl.BlockSpec((None, kdim, tn), lambda j, i: (layer, 0, j)),
                   pl.BlockSpec((tm, tn), lambda j, i: (rp0 + pidx(i), j)),
                   pl.BlockSpec((tm, tn), lambda j, i: (rs0, j))])
    body = functools.partial(_mm_out_body, kcs=kcs, n_ptiles=n_ptiles)
    return pl.pallas_call(
        body, grid=(n // tn, n_ptiles + 1), in_specs=in_specs,
        out_specs=pl.BlockSpec((tm, tn), lambda j, i: (i, j)),
        out_shape=jax.ShapeDtypeStruct((m, n), F32),
        scratch_shapes=[pltpu.VMEM((kdim, tn), BF16)],
        compiler_params=_cparams(("arbitrary", "arbitrary")),
        name="mm_out",
    )(*a_prompt, *a_sample, w3, rp, rs)


def _rmsnorm_body(x_ref, g_ref, op_ref, os_ref, *, n_ptiles):
    x = x_ref[...]
    ms = jnp.mean(x * x, axis=-1, keepdims=True)
    y = x * lax.rsqrt(ms + EPS) * g_ref[...]

    @pl.when(pl.program_id(0) < n_ptiles)
    def _():
        op_ref[...] = y

    @pl.when(pl.program_id(0) >= n_ptiles)
    def _():
        os_ref[...] = y


def _rmsnorm(x, g, m_prompt, *, tm=512):
    m, d = x.shape
    n_ptiles = m_prompt // tm
    return pl.pallas_call(
        functools.partial(_rmsnorm_body, n_ptiles=n_ptiles), grid=(n_ptiles + 1,),
        in_specs=[pl.BlockSpec((tm, d), lambda i: (i, 0)), pl.BlockSpec((1, d), lambda i: (0, 0))],
        out_specs=[pl.BlockSpec((tm, d), lambda i: (jnp.minimum(i, n_ptiles - 1), 0)),
                   pl.BlockSpec((tm, d), lambda i: (0, 0))],
        out_shape=[jax.ShapeDtypeStruct((m_prompt, d), F32),
                   jax.ShapeDtypeStruct((m - m_prompt, d), F32)],
        compiler_params=_cparams(("arbitrary",)),
        name="final_norm",
    )(x, g.reshape(1, d))


def _s5_consts_body(ar_ref, ai_ref, ls_ref, br_ref, bi_ref, cr_ref, ci_ref,
                    abr_ref, abi_ref, bre_ref, bim_ref, cre_ref, cim_ref):
    g, p, q = S5_GROUPS, S5_STATE, S5_GROUP
    gb = g // S5_NBLK
    gq, wid = g * q, gb * p
    ar = ar_ref[...]
    ai = ai_ref[...]
    dt = jnp.exp(ls_ref[...])
    mag = jnp.exp(ar * dt)
    abr = mag * jnp.cos(ai * dt)
    abi = mag * jnp.sin(ai * dt)
    abr_ref[...] = abr
    abi_ref[...] = abi
    lam2 = ar * ar + ai * ai
    zr = abr - 1.0
    cr = (zr * ar + abi * ai) / lam2
    ci = (abi * ar - zr * ai) / lam2

    rep = jnp.where((lax.broadcasted_iota(jnp.int32, (gq, g), 0) // q)
                    == lax.broadcasted_iota(jnp.int32, (gq, g), 1), 1.0, 0.0).astype(BF16)
    cr_r = _dot_exact_lhs(rep, cr)
    ci_r = _dot_exact_lhs(rep, ci)
    br = br_ref[...]
    bi = bi_ref[...]
    bbr = cr_r * br - ci_r * bi
    bbi = cr_r * bi + ci_r * br

    tile = jnp.where(lax.broadcasted_iota(jnp.int32, (p, wid), 0)
                     == lax.broadcasted_iota(jnp.int32, (p, wid), 1) % p, 1.0, 0.0).astype(BF16)
    keep = ((lax.broadcasted_iota(jnp.int32, (gq, wid), 0) // q) % gb
            == lax.broadcasted_iota(jnp.int32, (gq, wid), 1) // p)

    def spread(t):
        return jnp.where(keep, _dot(t.astype(BF16), tile), 0.0)

    bre_ref[...] = spread(bbr).astype(BF16).reshape(bre_ref.shape)
    bim_ref[...] = spread(bbi).astype(BF16).reshape(bim_ref.shape)
    dre = spread(cr_ref[...])
    dim = spread(ci_ref[...])
    rows = gb * q
    for k in range(S5_NBLK):
        cre_ref[k] = jnp.transpose(dre[k * rows:(k + 1) * rows, :]).astype(BF16)
        cim_ref[k] = jnp.transpose(dim[k * rows:(k + 1) * rows, :]).astype(BF16)


def _s5_consts(a_re, a_im, log_step, b_re, b_im, c_re, c_im):
    g, p, q = S5_GROUPS, S5_STATE, S5_GROUP
    gb = g // S5_NBLK
    brt = jnp.transpose(b_re, (0, 2, 1)).reshape(g * q, p)
    bit = jnp.transpose(b_im, (0, 2, 1)).reshape(g * q, p)
    full = lambda shape: pl.BlockSpec(shape, lambda: (0,) * len(shape))
    ins = [(g, p), (g, p), (g, 1)] + [(g * q, p)] * 4
    outs = [((g, p), F32)] * 2 + [((S5_NBLK, gb * q, gb * p), BF16)] * 2 + [((S5_NBLK, gb * p, gb * q), BF16)] * 2
    abr, abi, bre, bim, cre, cim = pl.pallas_call(
        _s5_consts_body, in_specs=[full(s) for s in ins], out_specs=[full(s) for s, _ in outs],
        out_shape=[jax.ShapeDtypeStruct(s, d) for s, d in outs],
        compiler_params=pltpu.CompilerParams(vmem_limit_bytes=VMEM_LIMIT),
        name="s5_consts",
    )(a_re, a_im, log_step.reshape(g, 1), brt, bit, c_re.reshape(g * q, p), c_im.reshape(g * q, p))
    return abr.reshape(1, S5_CH), abi.reshape(1, S5_CH), bre, bim, cre, cim


def _s5_perm(nb, tl):
    r = np.arange(nb * tl)
    p = np.zeros((nb * tl, nb * tl), np.float32)
    p[r, (r % nb) * tl + r // nb] = 1.0
    return p


def _s5_body(*refs, nsrc, nb, has_init, permute):
    u_refs, refs = refs[:nsrc], refs[nsrc:]
    if permute:
        pm_ref, pmt_ref = refs[:2]
        refs = refs[2:]
    bre_ref, bim_ref, cre_ref, cim_ref, ar_ref, ai_ref, d_ref, gw_ref, gb_ref = refs[:9]
    refs = refs[9:]
    if has_init:
        h0r_ref, h0i_ref, y_ref, hr_ref, hi_ref, sre, sim, ysc = refs
    else:
        y_ref, hr_ref, hi_ref, sre, sim, ysc = refs
    rows = sre.shape[0]
    tl = rows // nb
    cb = S5_CH // S5_NBLK
    ub = S5_WIDTH // S5_NBLK
    hw = cb // 2

    @pl.when(pl.program_id(0) == 0)
    def _():
        if has_init:
            hr_ref[...] = h0r_ref[...]
            hi_ref[...] = h0i_ref[...]
        else:
            hr_ref[...] = jnp.zeros_like(hr_ref)
            hi_ref[...] = jnp.zeros_like(hi_ref)

    u_b = u_refs[0][...] if nsrc == 1 else jnp.concatenate([r[...] for r in u_refs], axis=0)
    if permute:
        u_tm = _dot(pm_ref[...], u_b)
        u_b = u_tm.astype(BF16)
    else:
        u_tm = u_b.astype(F32)
    low_rows = lax.broadcasted_iota(jnp.int32, (SUBLANES, hw), 0) < nb

    for k in range(S5_NBLK):
        uk = u_b[:, k * ub:(k + 1) * ub]
        sre[...] = _dot(uk, bre_ref[k])
        sim[...] = _dot(uk, bim_ref[k])
        for half in range(2):
            c0 = k * cb + half * hw
            l0 = half * hw
            ar8 = jnp.broadcast_to(ar_ref[:, c0:c0 + hw], (SUBLANES, hw))
            ai8 = jnp.broadcast_to(ai_ref[:, c0:c0 + hw], (SUBLANES, hw))

            if nb < SUBLANES:
                def j_body(j, c, l0=l0, ar8=ar8, ai8=ai8):
                    sr, si = c
                    row = pl.multiple_of(j * SUBLANES, SUBLANES)
                    dr = sre[pl.ds(row, SUBLANES), l0:l0 + hw]
                    di = sim[pl.ds(row, SUBLANES), l0:l0 + hw]
                    pr = pltpu.roll(sr, nb, 0)
                    pi = pltpu.roll(si, nb, 0)
                    yr = ar8 * pr - ai8 * pi + dr
                    yi = ar8 * pi + ai8 * pr + di
                    qr = pltpu.roll(yr, nb, 0)
                    qi = pltpu.roll(yi, nb, 0)
                    zr = ar8 * qr - ai8 * qi + dr
                    zi = ar8 * qi + ai8 * qr + di
                    sre[pl.ds(row, SUBLANES), l0:l0 + hw] = jnp.where(low_rows, yr, zr)
                    sim[pl.ds(row, SUBLANES), l0:l0 + hw] = jnp.where(low_rows, yi, zi)
                    return zr, zi

                sr, si = lax.fori_loop(0, rows // SUBLANES, j_body,
                                       (hr_ref[:, c0:c0 + hw], hi_ref[:, c0:c0 + hw]), unroll=True)
                hr_ref[:, c0:c0 + hw] = sr
                hi_ref[:, c0:c0 + hw] = si
            else:
                def rg_body(rg, carry, c0=c0, l0=l0, ar8=ar8, ai8=ai8):
                    r0 = pl.multiple_of(rg * SUBLANES, SUBLANES)
                    xr = hr_ref[pl.ds(r0, SUBLANES), c0:c0 + hw]
                    xi = hi_ref[pl.ds(r0, SUBLANES), c0:c0 + hw]
                    for t in range(tl):
                        row = pl.multiple_of(t * nb + r0, SUBLANES)
                        nr = ar8 * xr - ai8 * xi + sre[pl.ds(row, SUBLANES), l0:l0 + hw]
                        ni = ar8 * xi + ai8 * xr + sim[pl.ds(row, SUBLANES), l0:l0 + hw]
                        sre[pl.ds(row, SUBLANES), l0:l0 + hw] = nr
                        sim[pl.ds(row, SUBLANES), l0:l0 + hw] = ni
                        xr, xi = nr, ni
                    hr_ref[pl.ds(r0, SUBLANES), c0:c0 + hw] = xr
                    hi_ref[pl.ds(r0, SUBLANES), c0:c0 + hw] = xi
                    return carry

                lax.fori_loop(0, nb // SUBLANES, rg_body, 0)
        ysc[:, k * ub:(k + 1) * ub] = (_dot(sre[...].astype(BF16), cre_ref[k])
                                       - _dot(sim[...].astype(BF16), cim_ref[k]))

    y = ysc[...] + d_ref[...] * u_tm
    y = 0.5 * y * (1.0 + jnp.tanh(math.sqrt(2.0 / math.pi) * (y + 0.044715 * (y * y * y))))
    z = _dot(y.astype(BF16), gw_ref[...]) + gb_ref[...]
    out = (y * _sigmoid(z)).astype(BF16)
    if permute:
        out = _dot(pmt_ref[...], out).astype(BF16)
    y_ref[...] = out.reshape(y_ref.shape)


def _s5_call(src, bre, bim, cre, cim, abar_re, abar_im, d_skip, glu_w, glu_b, h0, *,
             row0, nsrc, rs, seq_len, nb, tl, permute):
    rows = nsrc * rs
    ntiles = seq_len // rs if nsrc > 1 else 1
    has_init = h0 is not None
    srows = max(nb, SUBLANES)
    const = lambda shape: pl.BlockSpec(shape, lambda t: (0,) * len(shape))
    in_specs = [pl.BlockSpec((rs, S5_WIDTH), lambda t, s=s: ((row0 + s * seq_len) // rs + t, 0))
                for s in range(nsrc)]
    args = [src] * nsrc
    if permute:
        pm = _s5_perm(nb, tl)
        in_specs += [const((rows, rows)), const((rows, rows))]
        args += [jnp.asarray(pm, BF16), jnp.asarray(pm.T, BF16)]
    in_specs += [
        const(bre.shape), const(bim.shape), const(cre.shape), const(cim.shape),
        const((1, S5_CH)), const((1, S5_CH)), const((1, S5_WIDTH)),
        const((S5_WIDTH, S5_WIDTH)), const((1, S5_WIDTH)),
    ]
    args += [bre, bim, cre, cim, abar_re, abar_im, d_skip, glu_w, glu_b]
    if has_init:
        in_specs += [const((srows, S5_CH)), const((srows, S5_CH))]
        args += [h0[0], h0[1]]
    body = functools.partial(_s5_body, nsrc=nsrc, nb=nb, has_init=has_init, permute=permute)
    return pl.pallas_call(
        body, grid=(ntiles,), in_specs=in_specs,
        out_specs=[pl.BlockSpec((nsrc, rs, S5_WIDTH), lambda t: (0, t, 0)),
                   const((srows, S5_CH)), const((srows, S5_CH))],
        out_shape=[jax.ShapeDtypeStruct((nsrc, ntiles * rs, S5_WIDTH), BF16),
                   jax.ShapeDtypeStruct((srows, S5_CH), F32),
                   jax.ShapeDtypeStruct((srows, S5_CH), F32)],
        scratch_shapes=[pltpu.VMEM((rows, S5_CH // S5_NBLK), F32),
                        pltpu.VMEM((rows, S5_CH // S5_NBLK), F32),
                        pltpu.VMEM((rows, S5_WIDTH), F32)],
        compiler_params=_cparams(("arbitrary",)),
        name="s5_mixer",
    )(*args)


def _gla_body(*refs, nbb, nch, cs, has_init, hps):
    if has_init:
        (q_ref, k_ref, v_ref, g_ref, gl_ref, up_ref, gb_ref, nw_ref, s0_ref, y_ref, s_ref) = refs
    else:
        (q_ref, k_ref, v_ref, g_ref, gl_ref, up_ref, gb_ref, nw_ref, y_ref, s_ref) = refs
    rows = nbb * nch * cs
    sh = int(math.log2(cs))

    @pl.when(pl.program_id(2) == 0)
    def _():
        if has_init:
            s_ref[...] = s0_ref[...]
        else:
            s_ref[...] = jnp.zeros_like(s_ref)

    ri = lax.broadcasted_iota(jnp.int32, (rows, rows), 0)
    ci = lax.broadcasted_iota(jnp.int32, (rows, rows), 1)
    same = (ri >> sh) == (ci >> sh)
    causal = jnp.logical_and(same, ci <= ri)
    lmat = jnp.where(causal, 1.0, 0.0).astype(BF16)
    tmat = jnp.where(same, 1.0, 0.0).astype(BF16)
    rchunk = lax.broadcasted_iota(jnp.int32, (rows, 1), 0) >> sh
    cchunk = lax.broadcasted_iota(jnp.int32, (1, rows), 1) >> sh
    single = nbb * nch == 1
    z = _dot(gl_ref[...].astype(BF16), up_ref[...].astype(BF16)) + gb_ref[...]
    la = _log_sigmoid(z) * (1.0 / GLA_TAU)
    p1, p2, p3 = _split3(la)
    bcum_all = _dot(lmat, p1) + _dot(lmat, p2) + _dot(lmat, p3)
    tot_all = _dot(tmat, p1) + _dot(tmat, p2) + _dot(tmat, p3)

    for hh in range(hps):
        ks = slice(hh * GLA_HK, (hh + 1) * GLA_HK)
        vs = slice(hh * GLA_HV, (hh + 1) * GLA_HV)
        q = q_ref[:, ks].astype(F32) * (GLA_HK ** -0.5)
        k = k_ref[:, ks].astype(F32)
        vb = v_ref[:, vs]
        bcum = bcum_all[:, ks]
        tot = tot_all[:, ks]

        qd = q * jnp.exp(bcum)
        kd = k * jnp.exp(-bcum)
        kdec = k * jnp.exp(tot - bcum)
        qdb = qd.astype(BF16)
        att = jnp.where(causal, _dot_nt(qdb, kd.astype(BF16)), 0.0)
        o = _dot(att.astype(BF16), vb)

        kdec_t = jnp.transpose(kdec)
        dec_t = jnp.transpose(jnp.exp(tot))
        for b in range(nbb):
            s = s_ref[b, hh]
            for c in range(nch):
                idx = b * nch + c
                oi = _dot(qdb, s.astype(BF16))
                o = o + (oi if single else jnp.where(rchunk == idx, oi, 0.0))
                kt = kdec_t if single else jnp.where(cchunk == idx, kdec_t, 0.0)
                s = dec_t[:, idx * cs:idx * cs + 1] * s + _dot(kt.astype(BF16), vb)
            s_ref[b, hh] = s

        o = o * lax.rsqrt(jnp.mean(o * o, axis=-1, keepdims=True) + EPS) * nw_ref[...]
        gg = g_ref[:, vs].astype(F32)
        y_ref[:, vs] = (o * (gg * _sigmoid(gg))).astype(y_ref.dtype)


def _gla_call(projm, projt, up_pad, gkb, nw, s0, *, row0, nseq_blocks, ntiles, nbb, nch, cs, batch, hps):
    rows = nbb * nch * cs
    has_init = s0 is not None
    rb0 = row0 // rows
    rowblk = lambda s, h, t: rb0 + s * ntiles + t
    wk, wv = hps * GLA_HK, hps * GLA_HV
    qoff = S5_WIDTH // wk
    koff = qoff + GLA_HEADS // hps
    voff = (S5_WIDTH + 2 * GLA_HEADS * GLA_HK) // wv
    goff = voff + GLA_HEADS // hps
    in_specs = [
        pl.BlockSpec((rows, wk), lambda s, h, t: (rowblk(s, h, t), qoff + h)),
        pl.BlockSpec((rows, wk), lambda s, h, t: (rowblk(s, h, t), koff + h)),
        pl.BlockSpec((rows, wv), lambda s, h, t: (rowblk(s, h, t), voff + h)),
        pl.BlockSpec((rows, wv), lambda s, h, t: (rowblk(s, h, t), goff + h)),
        pl.BlockSpec((rows, LANES), lambda s, h, t: (rowblk(s, h, t), 0)),
        pl.BlockSpec((LANES, wk), lambda s, h, t: (0, h)),
        pl.BlockSpec((1, wk), lambda s, h, t: (0, h)),
        pl.BlockSpec((1, GLA_HV), lambda s, h, t: (0, 0)),
    ]
    args = [projm, projm, projm, projm, projt, up_pad, gkb, nw]
    if has_init:
        in_specs.append(pl.BlockSpec((nbb, hps, GLA_HK, GLA_HV), lambda s, h, t: (s, h, 0, 0)))
        args.append(s0)
    body = functools.partial(_gla_body, nbb=nbb, nch=nch, cs=cs, has_init=has_init, hps=hps)
    return pl.pallas_call(
        body, grid=(nseq_blocks, GLA_HEADS // hps, ntiles), in_specs=in_specs,
        out_specs=[pl.BlockSpec((rows, wv), lambda s, h, t: (s * ntiles + t, h)),
                   pl.BlockSpec((nbb, hps, GLA_HK, GLA_HV), lambda s, h, t: (s, h, 0, 0))],
        out_shape=[jax.ShapeDtypeStruct((nseq_blocks * ntiles * rows, GLA_HEADS * GLA_HV), BF16),
                   jax.ShapeDtypeStruct((batch, GLA_HEADS, GLA_HK, GLA_HV), F32)],
        compiler_params=_cparams(("arbitrary", "arbitrary", "arbitrary")),
        name="gla_mixer",
    )(*args)


def _mlstm_body(*refs, nbb, cl, rp, has_init, hps):
    if has_init:
        (q_ref, k_ref, v_ref, og_ref, gt_ref, gb_ref, nw_ref, c0_ref, n0_ref, m0_ref,
         y_ref, c_ref, n_ref, m_ref) = refs
    else:
        (q_ref, k_ref, v_ref, og_ref, gt_ref, gb_ref, nw_ref, y_ref, c_ref, n_ref, m_ref) = refs
    rows = nbb * cl
    sh = int(math.log2(cl))
    head0 = pl.program_id(1) * hps

    @pl.when(pl.program_id(2) == 0)
    def _():
        if has_init:
            c_ref[...] = c0_ref[...]
            n_ref[...] = n0_ref[...]
            m_ref[...] = m0_ref[...]
        else:
            c_ref[...] = jnp.zeros_like(c_ref)
            n_ref[...] = jnp.zeros_like(n_ref)
            m_ref[...] = jnp.full(m_ref.shape, -jnp.inf, F32)

    def pad(x):
        if rp == rows:
            return x
        return jnp.concatenate([x, jnp.zeros((rp - rows, x.shape[1]), x.dtype)], axis=0)

    ksc = ML_DH ** -0.5
    gates = pad(gt_ref[...] + gb_ref[...])
    lane = lax.broadcasted_iota(jnp.int32, (rp, LANES), 1)
    ri = lax.broadcasted_iota(jnp.int32, (rp, rp), 0)
    ci = lax.broadcasted_iota(jnp.int32, (rp, rp), 1)
    causal = jnp.logical_and((ri >> sh) == (ci >> sh), ci <= ri)
    lmat = jnp.where(causal, 1.0, 0.0).astype(BF16)
    neg = jnp.where(causal, 0.0, -jnp.inf)
    er = lax.broadcasted_iota(jnp.int32, (SUBLANES, LANES), 0)
    ec = lax.broadcasted_iota(jnp.int32, (SUBLANES, LANES), 1)
    pick = jnp.where(er == ec, 1.0, 0.0).astype(BF16)
    rbatch = lax.broadcasted_iota(jnp.int32, (rp, 1), 0) >> sh
    cbatch = lax.broadcasted_iota(jnp.int32, (1, rp), 1) >> sh

    for hh in range(hps):
        head = head0 + hh
        hs = slice(hh * ML_DH, (hh + 1) * ML_DH)
        qb = pad(q_ref[:, hs])
        kb = pad(k_ref[:, hs])
        vb = pad(v_ref[:, hs])
        og = pad(og_ref[:, hs]).astype(F32)
        ig = jnp.sum(jnp.where(lane == head, gates, 0.0), axis=1, keepdims=True)
        fg = jnp.sum(jnp.where(lane == head + ML_HEADS, gates, 0.0), axis=1, keepdims=True)
        lf = _log_sigmoid(fg)
        fcum = _dot_exact_lhs(lmat, jnp.where(lane == 0, lf, 0.0))[:, 0:1]
        cols = jnp.where(lane == 0, fcum, jnp.where(lane == 1, ig, 0.0))
        as_rows = _dot_nt_exact_lhs(pick, cols)
        fcum_row = as_rows[0:1, :]
        ig_row = as_rows[1:2, :]

        mm = jnp.zeros((rp, 1), F32)
        for b in range(nbb):
            mb = m_ref[b, hh][:, 0:1]
            mm = jnp.where(rbatch == b, mb, mm) if nbb > 1 else jnp.broadcast_to(mb, (rp, 1))

        dmat = fcum - fcum_row + ig_row + neg
        dprev = fcum + mm
        m = jnp.maximum(jnp.max(dmat, axis=1, keepdims=True), dprev)
        w = jnp.exp(dmat - m + math.log(ksc))
        wp = jnp.exp(dprev - m)
        sc = _dot_nt(qb, kb) * w
        num = _dot(sc.astype(BF16), vb)
        den = jnp.sum(sc, axis=1, keepdims=True)

        k_t = jnp.transpose(kb.astype(F32))
        for b in range(nbb):
            cm = c_ref[b, hh]
            nm = n_ref[b, hh]
            mb = m_ref[b, hh][:, 0:1]
            qc = wp * _dot(qb, cm.astype(BF16))
            nm8 = jnp.broadcast_to(nm, (SUBLANES, ML_DH)).astype(BF16)
            qn = wp * _dot_nt(qb, nm8)[:, 0:1]
            if nbb > 1:
                rsel = rbatch == b
                qc = jnp.where(rsel, qc, 0.0)
                qn = jnp.where(rsel, qn, 0.0)
            num = num + qc
            den = den + qn
            last = b * cl + cl - 1
            m_new = m[last:last + 1, :]
            fl = fcum[last:last + 1, :]
            decay = jnp.exp(fl + mb - m_new)
            wk_row = ksc * jnp.exp(fl - fcum_row + ig_row - m_new)
            if nbb > 1:
                wk_row = jnp.where(cbatch == b, wk_row, 0.0)
            c_ref[b, hh] = decay * cm + _dot((k_t * wk_row).astype(BF16), vb)
            wk8 = jnp.broadcast_to(wk_row, (SUBLANES, rp)).astype(BF16)
            n_ref[b, hh] = decay * nm + _dot(wk8, kb)[0:1, :]
            m_ref[b, hh] = jnp.broadcast_to(m_new, (1, LANES))

        hv = num / jnp.maximum(jnp.abs(den), jnp.exp(-m))
        hv = hv * _sigmoid(og)
        hv = hv * lax.rsqrt(jnp.mean(hv * hv, axis=-1, keepdims=True) + EPS) * nw_ref[...]
        y_ref[:, hs] = hv[:rows].astype(y_ref.dtype)


def _mlstm_call(projm, projt, gate_b, nw, init, *, row0, nseq_blocks, ntiles, nbb, cl, rp, batch, hps):
    rows = nbb * cl
    has_init = init is not None
    rb0 = row0 // rows
    rowblk = lambda s, h, t: rb0 + s * ntiles + t
    hd = ML_HEADS // hps
    wd = hps * ML_DH
    in_specs = [
        pl.BlockSpec((rows, wd), lambda s, h, t: (rowblk(s, h, t), h)),
        pl.BlockSpec((rows, wd), lambda s, h, t: (rowblk(s, h, t), hd + h)),
        pl.BlockSpec((rows, wd), lambda s, h, t: (rowblk(s, h, t), 2 * hd + h)),
        pl.BlockSpec((rows, wd), lambda s, h, t: (rowblk(s, h, t), 3 * hd + h)),
        pl.BlockSpec((rows, LANES), lambda s, h, t: (rowblk(s, h, t), 0)),
        pl.BlockSpec((1, LANES), lambda s, h, t: (0, 0)),
        pl.BlockSpec((1, ML_DH), lambda s, h, t: (0, 0)),
    ]
    args = [projm, projm, projm, projm, projt, gate_b, nw]
    c_spec = pl.BlockSpec((nbb, hps, ML_DH, ML_DH), lambda s, h, t: (s, h, 0, 0))
    n_spec = pl.BlockSpec((nbb, hps, 1, ML_DH), lambda s, h, t: (s, h, 0, 0))
    m_spec = pl.BlockSpec((nbb, hps, 1, LANES), lambda s, h, t: (s, h, 0, 0))
    if has_init:
        in_specs += [c_spec, n_spec, m_spec]
        args += list(init)
    body = functools.partial(_mlstm_body, nbb=nbb, cl=cl, rp=rp, has_init=has_init, hps=hps)
    return pl.pallas_call(
        body, grid=(nseq_blocks, hd, ntiles), in_specs=in_specs,
        out_specs=[pl.BlockSpec((rows, wd), lambda s, h, t: (s * ntiles + t, h)),
                   c_spec, n_spec, m_spec],
        out_shape=[jax.ShapeDtypeStruct((nseq_blocks * ntiles * rows, ML_HEADS * ML_DH), BF16),
                   jax.ShapeDtypeStruct((batch, ML_HEADS, ML_DH, ML_DH), F32),
                   jax.ShapeDtypeStruct((batch, ML_HEADS, 1, ML_DH), F32),
                   jax.ShapeDtypeStruct((batch, ML_HEADS, 1, LANES), F32)],
        compiler_params=_cparams(("arbitrary", "arbitrary", "arbitrary")),
        name="mlstm_mixer",
    )(*args)


def kernel(x_prompt, x_sample, state_s5_re, state_s5_im, state_gla, state_mlstm_c, state_mlstm_n,
           state_mlstm_m, norm_mix, norm_mlp, norm_final, w_in_even, s5_a_re, s5_a_im, s5_log_step,
           s5_b_re, s5_b_im, s5_c_re, s5_c_im, s5_d, s5_glu_w, s5_glu_b, gla_gk_up, gla_gk_b,
           gla_norm, w_out_even, w_in_odd, mlstm_b_i, mlstm_b_f, mlstm_norm, w_out_odd,
           w_mlp_up, w_mlp_down):
    bp, lp, d = x_prompt.shape
    bs, ls, _ = x_sample.shape
    mp = bp * lp
    msamp = bs * ls
    xp2 = x_prompt.reshape(mp, d)
    xs2 = x_sample.reshape(msamp, d)

    n_even = S5_WIDTH + 2 * GLA_HEADS * GLA_HK + 2 * GLA_HEADS * GLA_HV
    projm, projt = _norm_mm(xp2, norm_mix[0], w_in_even, 0, n_even, tm=1024, tn=1024, out_dtype=BF16)
    projm_s, projt_s = _norm_mm(xs2, norm_mix[0], w_in_even, 0, n_even, tm=msamp, tn=1024, out_dtype=BF16)

    abar_re, abar_im, bre, bim, cre, cim = _s5_consts(s5_a_re[0], s5_a_im[0], s5_log_step[0], s5_b_re[0],
                                                      s5_b_im[0], s5_c_re[0], s5_c_im[0])
    s5_consts = (bre, bim, cre, cim, abar_re, abar_im, s5_d[0].reshape(1, S5_WIDTH),
                 s5_glu_w[0].astype(BF16), s5_glu_b[0].reshape(1, S5_WIDTH))

    s5_tl = 64
    ys5_p, hr_p, hi_p = _s5_call(projm, *s5_consts, None, row0=0, nsrc=bp, rs=s5_tl, seq_len=lp,
                                 nb=bp, tl=s5_tl, permute=True)
    ys5_p = ys5_p.reshape(mp, S5_WIDTH)
    hr_p = hr_p[SUBLANES - bp:]
    hi_p = hi_p[SUBLANES - bp:]
    u_s = jnp.transpose(projm_s[:, :S5_WIDTH].reshape(bs, ls, S5_WIDTH), (1, 0, 2))
    h0 = (state_s5_re[0].reshape(bs, S5_CH), state_s5_im[0].reshape(bs, S5_CH))
    ys5_s, hr_s, hi_s = _s5_call(u_s.reshape(msamp, S5_WIDTH), *s5_consts, h0, row0=0, nsrc=1, rs=msamp,
                                 seq_len=msamp, nb=bs, tl=ls, permute=False)
    ys5_s = jnp.transpose(ys5_s.reshape(ls, bs, S5_WIDTH), (1, 0, 2)).reshape(msamp, S5_WIDTH)

    up_pad = jnp.pad(gla_gk_up[0], ((0, LANES - gla_gk_up.shape[1]), (0, 0)))
    gkb = gla_gk_b[0].reshape(1, -1)
    gnw = gla_norm[0].reshape(1, GLA_HV)
    ptile = 256
    ygla_p, sg_p = _gla_call(projm, projt, up_pad, gkb, gnw, None, row0=0, nseq_blocks=bp,
                             ntiles=lp // ptile, nbb=1, nch=ptile // GLA_CHUNK, cs=GLA_CHUNK, batch=bp,
                             hps=GLA_HEADS)
    gnbb = 32
    ygla_s, sg_s = _gla_call(projm_s, projt_s, up_pad, gkb, gnw, state_gla[0], row0=0,
                             nseq_blocks=bs // gnbb, ntiles=1, nbb=gnbb, nch=1, cs=ls, batch=bs, hps=1)

    h = _mm_out([ys5_p, ygla_p], [ys5_s, ygla_s], w_out_even, 0, (xp2, 0), (xs2, 0))
    hid = _norm_mm(h, norm_mlp[0], w_mlp_up, 0, D_FF, act="relu2", out_dtype=BF16, tn=1024)
    h = _mm_res(hid, w_mlp_down, 0, h, tn=D_MODEL, tk=512)

    n_odd = 4 * ML_HEADS * ML_DH
    projm, projt = _norm_mm(h, norm_mix[1], w_in_odd, 0, n_odd, out_dtype=BF16, tn=1024)
    gate_b = jnp.pad(jnp.concatenate([mlstm_b_i[0], mlstm_b_f[0]]), (0, LANES - 2 * ML_HEADS)).reshape(1, LANES)
    mnw = mlstm_norm[0].reshape(1, ML_DH)
    yml_p, c_p, n_p, m_p = _mlstm_call(projm, projt, gate_b, mnw, None, row0=0, nseq_blocks=bp,
                                       ntiles=lp // ML_CHUNK, nbb=1, cl=ML_CHUNK, rp=ML_CHUNK, batch=bp,
                                       hps=ML_HEADS)
    mnbb = 8
    init = (state_mlstm_c[0], state_mlstm_n[0].reshape(bs, ML_HEADS, 1, ML_DH),
            jnp.broadcast_to(state_mlstm_m[0][:, :, None, None], (bs, ML_HEADS, 1, LANES)))
    yml_s, c_s, n_s, m_s = _mlstm_call(projm, projt, gate_b, mnw, init, row0=mp, nseq_blocks=bs // mnbb,
                                       ntiles=1, nbb=mnbb, cl=ls, rp=LANES, batch=bs, hps=1)
    h = _mm_out([yml_p], [yml_s], w_out_odd, 0, (h, 0), (h, mp // 512))
    hid = _norm_mm(h, norm_mlp[1], w_mlp_up, 1, D_FF, act="relu2", out_dtype=BF16, tn=1024)
    h = _mm_res(hid, w_mlp_down, 1, h, tn=D_MODEL, tk=512)

    y_p, y_s = _rmsnorm(h, norm_final, mp)
    g, p = S5_GROUPS, S5_STATE
    return (y_p.reshape(bp, lp, d), y_s.reshape(bs, ls, d),
            hr_p.reshape(1, bp, g, p), hi_p.reshape(1, bp, g, p), sg_p[None],
            c_p[None], n_p.reshape(1, bp, ML_HEADS, ML_DH), m_p[:, :, 0, 0][None],
            hr_s.reshape(1, bs, g, p), hi_s.reshape(1, bs, g, p), sg_s[None],
            c_s[None], n_s.reshape(1, bs, ML_HEADS, ML_DH), m_s[:, :, 0, 0][None])
```

```python
import functools
import math

import jax
import jax.numpy as jnp
import numpy as np
from jax import lax
from jax.experimental import pallas as pl
from jax.experimental.pallas import tpu as pltpu

F32 = jnp.float32
BF16 = jnp.bfloat16
EPS = 1e-6

D_MODEL = 2048
D_FF = 4 * D_MODEL
S5_WIDTH = 1024
S5_GROUPS = 64
S5_GROUP = 16
S5_STATE = 64
S5_CH = S5_GROUPS * S5_STATE
S5_NBLK = 4
GLA_HEADS = 4
GLA_HK = 128
GLA_HV = 256
GLA_TAU = 16.0
GLA_CHUNK = 64
ML_HEADS = 4
ML_DH = 512
ML_CHUNK = 256

SUBLANES = 8
LANES = 128
VMEM_LIMIT = 56 * 1024 * 1024


def _cparams(sem):
    return pltpu.CompilerParams(dimension_semantics=sem, vmem_limit_bytes=VMEM_LIMIT)


def _dot(a, b):
    return jnp.dot(a, b, preferred_element_type=F32)


def _dot_nt(a, b):
    return lax.dot_general(a, b, (((1,), (1,)), ((), ())), preferred_element_type=F32)


def _split3(x):
    p1 = x.astype(BF16)
    r1 = x - p1.astype(F32)
    p2 = r1.astype(BF16)
    r2 = r1 - p2.astype(F32)
    p3 = r2.astype(BF16)
    return p1, p2, p3


def _dot_exact_lhs(m, x):
    p1, p2, p3 = _split3(x)
    return _dot(m, p1) + _dot(m, p2) + _dot(m, p3)


def _dot_nt_exact_lhs(m, x):
    p1, p2, p3 = _split3(x)
    return _dot_nt(m, p1) + _dot_nt(m, p2) + _dot_nt(m, p3)


def _log_sigmoid(x):
    return jnp.minimum(x, 0.0) - jnp.log1p(jnp.exp(-jnp.abs(x)))


def _sigmoid(x):
    return 1.0 / (1.0 + jnp.exp(-x))


def _norm_mm_body(*refs, act, n_tail, tm, rchunk):
    has_tail = n_tail > 0
    if has_tail:
        x_ref, g_ref, w_ref, wt_ref, o_ref, ot_ref, xn_ref = refs
    else:
        x_ref, g_ref, w_ref, o_ref, xn_ref = refs

    @pl.when(pl.program_id(1) == 0)
    def _():
        g = g_ref[...]
        for r in range(0, tm, rchunk):
            x = x_ref[r:r + rchunk, :]
            ms = jnp.mean(x * x, axis=-1, keepdims=True)
            xn_ref[r:r + rchunk, :] = (x * lax.rsqrt(ms + EPS) * g).astype(BF16)
        if has_tail:
            col = lax.broadcasted_iota(jnp.int32, wt_ref.shape, 0)
            wt = jnp.where(col < n_tail, wt_ref[...], 0.0)
            ot_ref[...] = _dot_nt(xn_ref[...], wt.astype(BF16))

    mm = _dot_nt if has_tail else _dot
    acc = mm(xn_ref[...], w_ref[...].astype(BF16))
    if act == "relu2":
        acc = jnp.square(jnp.maximum(acc, 0.0))
    o_ref[...] = acc.astype(o_ref.dtype)


def _norm_mm(x, g, w3, layer, n_main, *, act=None, out_dtype=F32, tm=1088, tn=512):
    m, kdim = x.shape
    grid = (m // tm, n_main // tn)
    n_tail = w3.shape[2] - n_main
    in_specs = [pl.BlockSpec((tm, kdim), lambda i, j: (i, 0)),
                pl.BlockSpec((1, kdim), lambda i, j: (0, 0))]
    args = [x, g.reshape(1, kdim)]
    out_shape = [jax.ShapeDtypeStruct((m, n_main), out_dtype)]
    out_specs = [pl.BlockSpec((tm, tn), lambda i, j: (i, j))]
    if n_tail > 0:
        w3t = jnp.transpose(w3, (0, 2, 1))
        in_specs += [pl.BlockSpec((None, tn, kdim), lambda i, j: (layer, j, 0)),
                     pl.BlockSpec((None, LANES, kdim), lambda i, j: (layer, n_main // LANES, 0))]
        args += [w3t, w3t]
        out_shape.append(jax.ShapeDtypeStruct((m, LANES), F32))
        out_specs.append(pl.BlockSpec((tm, LANES), lambda i, j: (i, 0)))
    else:
        in_specs.append(pl.BlockSpec((None, kdim, tn), lambda i, j: (layer, 0, j)))
        args.append(w3)
    body = functools.partial(_norm_mm_body, act=act, n_tail=n_tail, tm=tm, rchunk=tm // 4)
    res = pl.pallas_call(
        body, grid=grid, in_specs=in_specs, out_specs=out_specs, out_shape=out_shape,
        scratch_shapes=[pltpu.VMEM((tm, kdim), BF16)],
        compiler_params=_cparams(("arbitrary", "arbitrary")),
        name="norm_mm",
    )(*args)
    return res if n_tail > 0 else res[0]


def _mm_res_body(a_ref, w_ref, r_ref, o_ref):
    @pl.when(pl.program_id(2) == 0)
    def _():
        o_ref[...] = r_ref[...] + _dot(a_ref[...].astype(BF16), w_ref[...].astype(BF16))

    @pl.when(pl.program_id(2) > 0)
    def _():
        o_ref[...] = o_ref[...] + _dot(a_ref[...].astype(BF16), w_ref[...].astype(BF16))


def _mm_res(a, w3, layer, res, *, tm=1088, tn=1024, tk=1024):
    m, kdim = a.shape
    n = w3.shape[2]
    grid = (m // tm, n // tn, kdim // tk)
    return pl.pallas_call(
        _mm_res_body, grid=grid,
        in_specs=[
            pl.BlockSpec((tm, tk), lambda i, j, k: (i, k)),
            pl.BlockSpec((None, tk, tn), lambda i, j, k: (layer, k, j)),
            pl.BlockSpec((tm, tn), lambda i, j, k: (i, j)),
        ],
        out_specs=pl.BlockSpec((tm, tn), lambda i, j, k: (i, j)),
        out_shape=jax.ShapeDtypeStruct((m, n), F32),
        compiler_params=_cparams(("arbitrary", "arbitrary", "arbitrary")),
        name="mm_res",
    )(a, w3, res)


def _mm_out_body(*refs, kcs, n_ptiles):
    n = len(kcs)
    ap, asm = refs[:n], refs[n:2 * n]
    w_ref, rp_ref, rs_ref, o_ref, wb_ref = refs[2 * n:]
    i = pl.program_id(1)

    @pl.when(i == 0)
    def _():
        wb_ref[...] = w_ref[...].astype(BF16)

    def compute(srcs, r_ref):
        acc = r_ref[...]
        off = 0
        for a_ref, kc in zip(srcs, kcs):
            acc = acc + _dot(a_ref[...], wb_ref[off:off + kc, :])
            off += kc
        o_ref[...] = acc

    @pl.when(i < n_ptiles)
    def _():
        compute(ap, rp_ref)

    @pl.when(i >= n_ptiles)
    def _():
        compute(asm, rs_ref)


def _mm_out(a_prompt, a_sample, w3, layer, res_prompt, res_sample, *, tm=512, tn=1024):
    kcs = tuple(a.shape[1] for a in a_prompt)
    kdim, n = sum(kcs), w3.shape[2]
    n_ptiles = a_prompt[0].shape[0] // tm
    m = a_prompt[0].shape[0] + a_sample[0].shape[0]
    (rp, rp0), (rs, rs0) = res_prompt, res_sample
    pidx = lambda i: jnp.minimum(i, n_ptiles - 1)
    in_specs = ([pl.BlockSpec((tm, kc), lambda j, i: (pidx(i), 0)) for kc in kcs]
                + [pl.BlockSpec((tm, kc), lambda j, i: (0, 0)) for kc in kcs]
                + [pl.BlockSpec((None, kdim, tn), lambda j, i: (layer, 0, j)),
                   pl.BlockSpec((tm, tn), lambda j, i: (rp0 + pidx(i), j)),
                   pl.BlockSpec((tm, tn), lambda j, i: (rs0, j))])
    body = functools.partial(_mm_out_body, kcs=kcs, n_ptiles=n_ptiles)
    return pl.pallas_call(
        body, grid=(n // tn, n_ptiles + 1), in_specs=in_specs,
        out_specs=pl.BlockSpec((tm, tn), lambda j, i: (i, j)),
        out_shape=jax.ShapeDtypeStruct((m, n), F32),
        scratch_shapes=[pltpu.VMEM((kdim, tn), BF16)],
        compiler_params=_cparams(("arbitrary", "arbitrary")),
        name="mm_out",
    )(*a_prompt, *a_sample, w3, rp, rs)


def _rmsnorm_body(x_ref, g_ref, op_ref, os_ref, *, n_ptiles):
    x = x_ref[...]
    ms = jnp.mean(x * x, axis=-1, keepdims=True)
    y = x * lax.rsqrt(ms + EPS) * g_ref[...]

    @pl.when(pl.program_id(0) < n_ptiles)
    def _():
        op_ref[...] = y

    @pl.when(pl.program_id(0) >= n_ptiles)
    def _():
        os_ref[...] = y


def _rmsnorm(x, g, m_prompt, *, tm=512):
    m, d = x.shape
    n_ptiles = m_prompt // tm
    return pl.pallas_call(
        functools.partial(_rmsnorm_body, n_ptiles=n_ptiles), grid=(n_ptiles + 1,),
        in_specs=[pl.BlockSpec((tm, d), lambda i: (i, 0)), pl.BlockSpec((1, d), lambda i: (0, 0))],
        out_specs=[pl.BlockSpec((tm, d), lambda i: (jnp.minimum(i, n_ptiles - 1), 0)),
                   pl.BlockSpec((tm, d), lambda i: (0, 0))],
        out_shape=[jax.ShapeDtypeStruct((m_prompt, d), F32),
                   jax.ShapeDtypeStruct((m - m_prompt, d), F32)],
        compiler_params=_cparams(("arbitrary",)),
        name="final_norm",
    )(x, g.reshape(1, d))


def _s5_consts_body(ar_ref, ai_ref, ls_ref, br_ref, bi_ref, cr_ref, ci_ref,
                    abr_ref, abi_ref, bre_ref, bim_ref, cre_ref, cim_ref):
    g, p, q = S5_GROUPS, S5_STATE, S5_GROUP
    gb = g // S5_NBLK
    gq, wid = g * q, gb * p
    ar = ar_ref[...]
    ai = ai_ref[...]
    dt = jnp.exp(ls_ref[...])
    mag = jnp.exp(ar * dt)
    abr = mag * jnp.cos(ai * dt)
    abi = mag * jnp.sin(ai * dt)
    abr_ref[...] = abr
    abi_ref[...] = abi
    lam2 = ar * ar + ai * ai
    zr = abr - 1.0
    cr = (zr * ar + abi * ai) / lam2
    ci = (abi * ar - zr * ai) / lam2

    rep = jnp.where((lax.broadcasted_iota(jnp.int32, (gq, g), 0) // q)
                    == lax.broadcasted_iota(jnp.int32, (gq, g), 1), 1.0, 0.0).astype(BF16)
    cr_r = _dot_exact_lhs(rep, cr)
    ci_r = _dot_exact_lhs(rep, ci)
    br = br_ref[...]
    bi = bi_ref[...]
    bbr = cr_r * br - ci_r * bi
    bbi = cr_r * bi + ci_r * br

    tile = jnp.where(lax.broadcasted_iota(jnp.int32, (p, wid), 0)
                     == lax.broadcasted_iota(jnp.int32, (p, wid), 1) % p, 1.0, 0.0).astype(BF16)
    keep = ((lax.broadcasted_iota(jnp.int32, (gq, wid), 0) // q) % gb
            == lax.broadcasted_iota(jnp.int32, (gq, wid), 1) // p)

    def spread(t):
        return jnp.where(keep, _dot(t.astype(BF16), tile), 0.0)

    bre_ref[...] = spread(bbr).astype(BF16).reshape(bre_ref.shape)
    bim_ref[...] = spread(bbi).astype(BF16).reshape(bim_ref.shape)
    dre = spread(cr_ref[...])
    dim = spread(ci_ref[...])
    rows = gb * q
    for k in range(S5_NBLK):
        cre_ref[k] = jnp.transpose(dre[k * rows:(k + 1) * rows, :]).astype(BF16)
        cim_ref[k] = jnp.transpose(dim[k * rows:(k + 1) * rows, :]).astype(BF16)


def _s5_consts(a_re, a_im, log_step, b_re, b_im, c_re, c_im):
    g, p, q = S5_GROUPS, S5_STATE, S5_GROUP
    gb = g // S5_NBLK
    brt = jnp.transpose(b_re, (0, 2, 1)).reshape(g * q, p)
    bit = jnp.transpose(b_im, (0, 2, 1)).reshape(g * q, p)
    full = lambda shape: pl.BlockSpec(shape, lambda: (0,) * len(shape))
    ins = [(g, p), (g, p), (g, 1)] + [(g * q, p)] * 4
    outs = [((g, p), F32)] * 2 + [((S5_NBLK, gb * q, gb * p), BF16)] * 2 + [((S5_NBLK, gb * p, gb * q), BF16)] * 2
    abr, abi, bre, bim, cre, cim = pl.pallas_call(
        _s5_consts_body, in_specs=[full(s) for s in ins], out_specs=[full(s) for s, _ in outs],
        out_shape=[jax.ShapeDtypeStruct(s, d) for s, d in outs],
        compiler_params=pltpu.CompilerParams(vmem_limit_bytes=VMEM_LIMIT),
        name="s5_consts",
    )(a_re, a_im, log_step.reshape(g, 1), brt, bit, c_re.reshape(g * q, p), c_im.reshape(g * q, p))
    return abr.reshape(1, S5_CH), abi.reshape(1, S5_CH), bre, bim, cre, cim


def _s5_perm(nb, tl):
    r = np.arange(nb * tl)
    p = np.zeros((nb * tl, nb * tl), np.float32)
    p[r, (r % nb) * tl + r // nb] = 1.0
    return p


def _s5_body(*refs, nsrc, nb, has_init, permute):
    u_refs, refs = refs[:nsrc], refs[nsrc:]
    if permute:
        pm_ref, pmt_ref = refs[:2]
        refs = refs[2:]
    bre_ref, bim_ref, cre_ref, cim_ref, ar_ref, ai_ref, d_ref, gw_ref, gb_ref = refs[:9]
    refs = refs[9:]
    if has_init:
        h0r_ref, h0i_ref, y_ref, hr_ref, hi_ref, sre, sim, ysc = refs
    else:
        y_ref, hr_ref, hi_ref, sre, sim, ysc = refs
    rows = sre.shape[0]
    tl = rows // nb
    cb = S5_CH // S5_NBLK
    ub = S5_WIDTH // S5_NBLK
    hw = cb // 2

    @pl.when(pl.program_id(0) == 0)
    def _():
        if has_init:
            hr_ref[...] = h0r_ref[...]
            hi_ref[...] = h0i_ref[...]
        else:
            hr_ref[...] = jnp.zeros_like(hr_ref)
            hi_ref[...] = jnp.zeros_like(hi_ref)

    u_b = u_refs[0][...] if nsrc == 1 else jnp.concatenate([r[...] for r in u_refs], axis=0)
    if permute:
        u_tm = _dot(pm_ref[...], u_b)
        u_b = u_tm.astype(BF16)
    else:
        u_tm = u_b.astype(F32)
    low_rows = lax.broadcasted_iota(jnp.int32, (SUBLANES, hw), 0) < nb

    for k in range(S5_NBLK):
        uk = u_b[:, k * ub:(k + 1) * ub]
        sre[...] = _dot(uk, bre_ref[k])
        sim[...] = _dot(uk, bim_ref[k])
        for half in range(2):
            c0 = k * cb + half * hw
            l0 = half * hw
            ar8 = jnp.broadcast_to(ar_ref[:, c0:c0 + hw], (SUBLANES, hw))
            ai8 = jnp.broadcast_to(ai_ref[:, c0:c0 + hw], (SUBLANES, hw))

            if nb < SUBLANES:
                def j_body(j, c, l0=l0, ar8=ar8, ai8=ai8):
                    sr, si = c
                    row = pl.multiple_of(j * SUBLANES, SUBLANES)
                    dr = sre[pl.ds(row, SUBLANES), l0:l0 + hw]
                    di = sim[pl.ds(row, SUBLANES), l0:l0 + hw]
                    pr = pltpu.roll(sr, nb, 0)
                    pi = pltpu.roll(si, nb, 0)
                    yr = ar8 * pr - ai8 * pi + dr
                    yi = ar8 * pi + ai8 * pr + di
                    qr = pltpu.roll(yr, nb, 0)
                    qi = pltpu.roll(yi, nb, 0)
                    zr = ar8 * qr - ai8 * qi + dr
                    zi = ar8 * qi + ai8 * qr + di
                    sre[pl.ds(row, SUBLANES), l0:l0 + hw] = jnp.where(low_rows, yr, zr)
                    sim[pl.ds(row, SUBLANES), l0:l0 + hw] = jnp.where(low_rows, yi, zi)
                    return zr, zi

                sr, si = lax.fori_loop(0, rows // SUBLANES, j_body,
                                       (hr_ref[:, c0:c0 + hw], hi_ref[:, c0:c0 + hw]), unroll=True)
                hr_ref[:, c0:c0 + hw] = sr
                hi_ref[:, c0:c0 + hw] = si
            else:
                def rg_body(rg, carry, c0=c0, l0=l0, ar8=ar8, ai8=ai8):
                    r0 = pl.multiple_of(rg * SUBLANES, SUBLANES)
                    xr = hr_ref[pl.ds(r0, SUBLANES), c0:c0 + hw]
                    xi = hi_ref[pl.ds(r0, SUBLANES), c0:c0 + hw]
                    for t in range(tl):
                        row = pl.multiple_of(t * nb + r0, SUBLANES)
                        nr = ar8 * xr - ai8 * xi + sre[pl.ds(row, SUBLANES), l0:l0 + hw]
                        ni = ar8 * xi + ai8 * xr + sim[pl.ds(row, SUBLANES), l0:l0 + hw]
                        sre[pl.ds(row, SUBLANES), l0:l0 + hw] = nr
                        sim[pl.ds(row, SUBLANES), l0:l0 + hw] = ni
                        xr, xi = nr, ni
                    hr_ref[pl.ds(r0, SUBLANES), c0:c0 + hw] = xr
                    hi_ref[pl.ds(r0, SUBLANES), c0:c0 + hw] = xi
                    return carry

                lax.fori_loop(0, nb // SUBLANES, rg_body, 0)
        ysc[:, k * ub:(k + 1) * ub] = (_dot(sre[...].astype(BF16), cre_ref[k])
                                       - _dot(sim[...].astype(BF16), cim_ref[k]))

    y = ysc[...] + d_ref[...] * u_tm
    y = 0.5 * y * (1.0 + jnp.tanh(math.sqrt(2.0 / math.pi) * (y + 0.044715 * (y * y * y))))
    z = _dot(y.astype(BF16), gw_ref[...]) + gb_ref[...]
    out = (y * _sigmoid(z)).astype(BF16)
    if permute:
        out = _dot(pmt_ref[...], out).astype(BF16)
    y_ref[...] = out.reshape(y_ref.shape)


def _s5_call(src, bre, bim, cre, cim, abar_re, abar_im, d_skip, glu_w, glu_b, h0, *,
             row0, nsrc, rs, seq_len, nb, tl, permute):
    rows = nsrc * rs
    ntiles = seq_len // rs if nsrc > 1 else 1
    has_init = h0 is not None
    srows = max(nb, SUBLANES)
    const = lambda shape: pl.BlockSpec(shape, lambda t: (0,) * len(shape))
    in_specs = [pl.BlockSpec((rs, S5_WIDTH), lambda t, s=s: ((row0 + s * seq_len) // rs + t, 0))
                for s in range(nsrc)]
    args = [src] * nsrc
    if permute:
        pm = _s5_perm(nb, tl)
        in_specs += [const((rows, rows)), const((rows, rows))]
        args += [jnp.asarray(pm, BF16), jnp.asarray(pm.T, BF16)]
    in_specs += [
        const(bre.shape), const(bim.shape), const(cre.shape), const(cim.shape),
        const((1, S5_CH)), const((1, S5_CH)), const((1, S5_WIDTH)),
        const((S5_WIDTH, S5_WIDTH)), const((1, S5_WIDTH)),
    ]
    args += [bre, bim, cre, cim, abar_re, abar_im, d_skip, glu_w, glu_b]
    if has_init:
        in_specs += [const((srows, S5_CH)), const((srows, S5_CH))]
        args += [h0[0], h0[1]]
    body = functools.partial(_s5_body, nsrc=nsrc, nb=nb, has_init=has_init, permute=permute)
    return pl.pallas_call(
        body, grid=(ntiles,), in_specs=in_specs,
        out_specs=[pl.BlockSpec((nsrc, rs, S5_WIDTH), lambda t: (0, t, 0)),
                   const((srows, S5_CH)), const((srows, S5_CH))],
        out_shape=[jax.ShapeDtypeStruct((nsrc, ntiles * rs, S5_WIDTH), BF16),
                   jax.ShapeDtypeStruct((srows, S5_CH), F32),
                   jax.ShapeDtypeStruct((srows, S5_CH), F32)],
        scratch_shapes=[pltpu.VMEM((rows, S5_CH // S5_NBLK), F32),
                        pltpu.VMEM((rows, S5_CH // S5_NBLK), F32),
                        pltpu.VMEM((rows, S5_WIDTH), F32)],
        compiler_params=_cparams(("arbitrary",)),
        name="s5_mixer",
    )(*args)


def _gla_body(*refs, nbb, nch, cs, has_init, hps):
    if has_init:
        (q_ref, k_ref, v_ref, g_ref, gl_ref, up_ref, gb_ref, nw_ref, s0_ref, y_ref, s_ref) = refs
    else:
        (q_ref, k_ref, v_ref, g_ref, gl_ref, up_ref, gb_ref, nw_ref, y_ref, s_ref) = refs
    rows = nbb * nch * cs
    sh = int(math.log2(cs))

    @pl.when(pl.program_id(2) == 0)
    def _():
        if has_init:
            s_ref[...] = s0_ref[...]
        else:
            s_ref[...] = jnp.zeros_like(s_ref)

    ri = lax.broadcasted_iota(jnp.int32, (rows, rows), 0)
    ci = lax.broadcasted_iota(jnp.int32, (rows, rows), 1)
    same = (ri >> sh) == (ci >> sh)
    causal = jnp.logical_and(same, ci <= ri)
    lmat = jnp.where(causal, 1.0, 0.0).astype(BF16)
    tmat = jnp.where(same, 1.0, 0.0).astype(BF16)
    rchunk = lax.broadcasted_iota(jnp.int32, (rows, 1), 0) >> sh
    cchunk = lax.broadcasted_iota(jnp.int32, (1, rows), 1) >> sh
    single = nbb * nch == 1
    z = _dot(gl_ref[...].astype(BF16), up_ref[...].astype(BF16)) + gb_ref[...]
    la = _log_sigmoid(z) * (1.0 / GLA_TAU)
    p1, p2, p3 = _split3(la)
    bcum_all = _dot(lmat, p1) + _dot(lmat, p2) + _dot(lmat, p3)
    tot_all = _dot(tmat, p1) + _dot(tmat, p2) + _dot(tmat, p3)

    for hh in range(hps):
        ks = slice(hh * GLA_HK, (hh + 1) * GLA_HK)
        vs = slice(hh * GLA_HV, (hh + 1) * GLA_HV)
        q = q_ref[:, ks].astype(F32) * (GLA_HK ** -0.5)
        k = k_ref[:, ks].astype(F32)
        vb = v_ref[:, vs]
        bcum = bcum_all[:, ks]
        tot = tot_all[:, ks]

        qd = q * jnp.exp(bcum)
        kd = k * jnp.exp(-bcum)
        kdec = k * jnp.exp(tot - bcum)
        qdb = qd.astype(BF16)
        att = jnp.where(causal, _dot_nt(qdb, kd.astype(BF16)), 0.0)
        o = _dot(att.astype(BF16), vb)

        kdec_t = jnp.transpose(kdec)
        dec_t = jnp.transpose(jnp.exp(tot))
        for b in range(nbb):
            s = s_ref[b, hh]
            for c in range(nch):
                idx = b * nch + c
                oi = _dot(qdb, s.astype(BF16))
                o = o + (oi if single else jnp.where(rchunk == idx, oi, 0.0))
                kt = kdec_t if single else jnp.where(cchunk == idx, kdec_t, 0.0)
                s = dec_t[:, idx * cs:idx * cs + 1] * s + _dot(kt.astype(BF16), vb)
            s_ref[b, hh] = s

        o = o * lax.rsqrt(jnp.mean(o * o, axis=-1, keepdims=True) + EPS) * nw_ref[...]
        gg = g_ref[:, vs].astype(F32)
        y_ref[:, vs] = (o * (gg * _sigmoid(gg))).astype(y_ref.dtype)


def _gla_call(projm, projt, up_pad, gkb, nw, s0, *, row0, nseq_blocks, ntiles, nbb, nch, cs, batch, hps):
    rows = nbb * nch * cs
    has_init = s0 is not None
    rb0 = row0 // rows
    rowblk = lambda s, h, t: rb0 + s * ntiles + t
    wk, wv = hps * GLA_HK, hps * GLA_HV
    qoff = S5_WIDTH // wk
    koff = qoff + GLA_HEADS // hps
    voff = (S5_WIDTH + 2 * GLA_HEADS * GLA_HK) // wv
    goff = voff + GLA_HEADS // hps
    in_specs = [
        pl.BlockSpec((rows, wk), lambda s, h, t: (rowblk(s, h, t), qoff + h)),
        pl.BlockSpec((rows, wk), lambda s, h, t: (rowblk(s, h, t), koff + h)),
        pl.BlockSpec((rows, wv), lambda s, h, t: (rowblk(s, h, t), voff + h)),
        pl.BlockSpec((rows, wv), lambda s, h, t: (rowblk(s, h, t), goff + h)),
        pl.BlockSpec((rows, LANES), lambda s, h, t: (rowblk(s, h, t), 0)),
        pl.BlockSpec((LANES, wk), lambda s, h, t: (0, h)),
        pl.BlockSpec((1, wk), lambda s, h, t: (0, h)),
        pl.BlockSpec((1, GLA_HV), lambda s, h, t: (0, 0)),
    ]
    args = [projm, projm, projm, projm, projt, up_pad, gkb, nw]
    if has_init:
        in_specs.append(pl.BlockSpec((nbb, hps, GLA_HK, GLA_HV), lambda s, h, t: (s, h, 0, 0)))
        args.append(s0)
    body = functools.partial(_gla_body, nbb=nbb, nch=nch, cs=cs, has_init=has_init, hps=hps)
    return pl.pallas_call(
        body, grid=(nseq_blocks, GLA_HEADS // hps, ntiles), in_specs=in_specs,
        out_specs=[pl.BlockSpec((rows, wv), lambda s, h, t: (s * ntiles + t, h)),
                   pl.BlockSpec((nbb, hps, GLA_HK, GLA_HV), lambda s, h, t: (s, h, 0, 0))],
        out_shape=[jax.ShapeDtypeStruct((nseq_blocks * ntiles * rows, GLA_HEADS * GLA_HV), BF16),
                   jax.ShapeDtypeStruct((batch, GLA_HEADS, GLA_HK, GLA_HV), F32)],
        compiler_params=_cparams(("arbitrary", "arbitrary", "arbitrary")),
        name="gla_mixer",
    )(*args)


def _mlstm_body(*refs, nbb, cl, rp, has_init, hps):
    if has_init:
        (q_ref, k_ref, v_ref, og_ref, gt_ref, gb_ref, nw_ref, c0_ref, n0_ref, m0_ref,
         y_ref, c_ref, n_ref, m_ref) = refs
    else:
        (q_ref, k_ref, v_ref, og_ref, gt_ref, gb_ref, nw_ref, y_ref, c_ref, n_ref, m_ref) = refs
    rows = nbb * cl
    sh = int(math.log2(cl))
    head0 = pl.program_id(1) * hps

    @pl.when(pl.program_id(2) == 0)
    def _():
        if has_init:
            c_ref[...] = c0_ref[...]
            n_ref[...] = n0_ref[...]
            m_ref[...] = m0_ref[...]
        else:
            c_ref[...] = jnp.zeros_like(c_ref)
            n_ref[...] = jnp.zeros_like(n_ref)
            m_ref[...] = jnp.full(m_ref.shape, -jnp.inf, F32)

    def pad(x):
        if rp == rows:
            return x
        return jnp.concatenate([x, jnp.zeros((rp - rows, x.shape[1]), x.dtype)], axis=0)

    ksc = ML_DH ** -0.5
    gates = pad(gt_ref[...] + gb_ref[...])
    lane = lax.broadcasted_iota(jnp.int32, (rp, LANES), 1)
    ri = lax.broadcasted_iota(jnp.int32, (rp, rp), 0)
    ci = lax.broadcasted_iota(jnp.int32, (rp, rp), 1)
    causal = jnp.logical_and((ri >> sh) == (ci >> sh), ci <= ri)
    lmat = jnp.where(causal, 1.0, 0.0).astype(BF16)
    neg = jnp.where(causal, 0.0, -jnp.inf)
    er = lax.broadcasted_iota(jnp.int32, (SUBLANES, LANES), 0)
    ec = lax.broadcasted_iota(jnp.int32, (SUBLANES, LANES), 1)
    pick = jnp.where(er == ec, 1.0, 0.0).astype(BF16)
    rbatch = lax.broadcasted_iota(jnp.int32, (rp, 1), 0) >> sh
    cbatch = lax.broadcasted_iota(jnp.int32, (1, rp), 1) >> sh
    fc_all = _dot_exact_lhs(lmat, _log_sigmoid(gates))
    rows_all = _dot_nt_exact_lhs(pick, jnp.where(lane < ML_HEADS, gates, fc_all))
    sub = lax.broadcasted_iota(jnp.int32, (SUBLANES, rp), 0)

    for hh in range(hps):
        head = head0 + hh
        hs = slice(hh * ML_DH, (hh + 1) * ML_DH)
        qb = pad(q_ref[:, hs])
        kb = pad(k_ref[:, hs])
        vb = pad(v_ref[:, hs])
        og = pad(og_ref[:, hs]).astype(F32)
        ig = jnp.sum(jnp.where(lane == head, gates, 0.0), axis=1, keepdims=True)
        fcum = jnp.sum(jnp.where(lane == head + ML_HEADS, fc_all, 0.0), axis=1, keepdims=True)
        ig_row = jnp.sum(jnp.where(sub == head, rows_all, 0.0), axis=0, keepdims=True)
        fcum_row = jnp.sum(jnp.where(sub == head + ML_HEADS, rows_all, 0.0), axis=0, keepdims=True)

        mm = jnp.zeros((rp, 1), F32)
        for b in range(nbb):
            mb = m_ref[b, hh][:, 0:1]
            mm = jnp.where(rbatch == b, mb, mm) if nbb > 1 else jnp.broadcast_to(mb, (rp, 1))

        dmat = fcum - fcum_row + ig_row + neg
        dprev = fcum + mm
        m = jnp.maximum(jnp.max(dmat, axis=1, keepdims=True), dprev)
        w = jnp.exp(dmat - m + math.log(ksc))
        wp = jnp.exp(dprev - m)
        sc = _dot_nt(qb, kb) * w
        num = _dot(sc.astype(BF16), vb)
        den = jnp.sum(sc, axis=1, keepdims=True)

        k_t = jnp.transpose(kb.astype(F32))
        for b in range(nbb):
            cm = c_ref[b, hh]
            nm = n_ref[b, hh]
            mb = m_ref[b, hh][:, 0:1]
            qc = wp * _dot(qb, cm.astype(BF16))
            nm8 = jnp.broadcast_to(nm, (SUBLANES, ML_DH)).astype(BF16)
            qn = wp * _dot_nt(qb, nm8)[:, 0:1]
            if nbb > 1:
                rsel = rbatch == b
                qc = jnp.where(rsel, qc, 0.0)
                qn = jnp.where(rsel, qn, 0.0)
            num = num + qc
            den = den + qn
            last = b * cl + cl - 1
            m_new = m[last:last + 1, :]
            fl = fcum[last:last + 1, :]
            decay = jnp.exp(fl + mb - m_new)
            wk_row = ksc * jnp.exp(fl - fcum_row + ig_row - m_new)
            if nbb > 1:
                wk_row = jnp.where(cbatch == b, wk_row, 0.0)
            c_ref[b, hh] = decay * cm + _dot((k_t * wk_row).astype(BF16), vb)
            wk8 = jnp.broadcast_to(wk_row, (SUBLANES, rp)).astype(BF16)
            n_ref[b, hh] = decay * nm + _dot(wk8, kb)[0:1, :]
            m_ref[b, hh] = jnp.broadcast_to(m_new, (1, LANES))

        hv = num / jnp.maximum(jnp.abs(den), jnp.exp(-m))
        hv = hv * _sigmoid(og)
        hv = hv * lax.rsqrt(jnp.mean(hv * hv, axis=-1, keepdims=True) + EPS) * nw_ref[...]
        y_ref[:, hs] = hv[:rows].astype(y_ref.dtype)


def _mlstm_call(projm, projt, gate_b, nw, init, *, row0, nseq_blocks, ntiles, nbb, cl, rp, batch, hps):
    rows = nbb * cl
    has_init = init is not None
    rb0 = row0 // rows
    rowblk = lambda s, h, t: rb0 + s * ntiles + t
    hd = ML_HEADS // hps
    wd = hps * ML_DH
    in_specs = [
        pl.BlockSpec((rows, wd), lambda s, h, t: (rowblk(s, h, t), h)),
        pl.BlockSpec((rows, wd), lambda s, h, t: (rowblk(s, h, t), hd + h)),
        pl.BlockSpec((rows, wd), lambda s, h, t: (rowblk(s, h, t), 2 * hd + h)),
        pl.BlockSpec((rows, wd), lambda s, h, t: (rowblk(s, h, t), 3 * hd + h)),
        pl.BlockSpec((rows, LANES), lambda s, h, t: (rowblk(s, h, t), 0)),
        pl.BlockSpec((1, LANES), lambda s, h, t: (0, 0)),
        pl.BlockSpec((1, ML_DH), lambda s, h, t: (0, 0)),
    ]
    args = [projm, projm, projm, projm, projt, gate_b, nw]
    c_spec = pl.BlockSpec((nbb, hps, ML_DH, ML_DH), lambda s, h, t: (s, h, 0, 0))
    n_spec = pl.BlockSpec((nbb, hps, 1, ML_DH), lambda s, h, t: (s, h, 0, 0))
    m_spec = pl.BlockSpec((nbb, hps, 1, LANES), lambda s, h, t: (s, h, 0, 0))
    if has_init:
        in_specs += [c_spec, n_spec, m_spec]
        args += list(init)
    body = functools.partial(_mlstm_body, nbb=nbb, cl=cl, rp=rp, has_init=has_init, hps=hps)
    return pl.pallas_call(
        body, grid=(nseq_blocks, hd, ntiles), in_specs=in_specs,
        out_specs=[pl.BlockSpec((rows, wd), lambda s, h, t: (s * ntiles + t, h)),
                   c_spec, n_spec, m_spec],
        out_shape=[jax.ShapeDtypeStruct((nseq_blocks * ntiles * rows, ML_HEADS * ML_DH), BF16),
                   jax.ShapeDtypeStruct((batch, ML_HEADS, ML_DH, ML_DH), F32),
                   jax.ShapeDtypeStruct((batch, ML_HEADS, 1, ML_DH), F32),
                   jax.ShapeDtypeStruct((batch, ML_HEADS, 1, LANES), F32)],
        compiler_params=_cparams(("arbitrary", "arbitrary", "arbitrary")),
        name="mlstm_mixer",
    )(*args)


def kernel(x_prompt, x_sample, state_s5_re, state_s5_im, state_gla, state_mlstm_c, state_mlstm_n,
           state_mlstm_m, norm_mix, norm_mlp, norm_final, w_in_even, s5_a_re, s5_a_im, s5_log_step,
           s5_b_re, s5_b_im, s5_c_re, s5_c_im, s5_d, s5_glu_w, s5_glu_b, gla_gk_up, gla_gk_b,
           gla_norm, w_out_even, w_in_odd, mlstm_b_i, mlstm_b_f, mlstm_norm, w_out_odd,
           w_mlp_up, w_mlp_down):
    bp, lp, d = x_prompt.shape
    bs, ls, _ = x_sample.shape
    mp = bp * lp
    msamp = bs * ls
    xp2 = x_prompt.reshape(mp, d)
    xs2 = x_sample.reshape(msamp, d)

    n_even = S5_WIDTH + 2 * GLA_HEADS * GLA_HK + 2 * GLA_HEADS * GLA_HV
    projm, projt = _norm_mm(xp2, norm_mix[0], w_in_even, 0, n_even, tm=1024, tn=1024, out_dtype=BF16)
    projm_s, projt_s = _norm_mm(xs2, norm_mix[0], w_in_even, 0, n_even, tm=msamp, tn=1024, out_dtype=BF16)

    abar_re, abar_im, bre, bim, cre, cim = _s5_consts(s5_a_re[0], s5_a_im[0], s5_log_step[0], s5_b_re[0],
                                                      s5_b_im[0], s5_c_re[0], s5_c_im[0])
    s5_consts = (bre, bim, cre, cim, abar_re, abar_im, s5_d[0].reshape(1, S5_WIDTH),
                 s5_glu_w[0].astype(BF16), s5_glu_b[0].reshape(1, S5_WIDTH))

    s5_tl = 64
    ys5_p, hr_p, hi_p = _s5_call(projm, *s5_consts, None, row0=0, nsrc=bp, rs=s5_tl, seq_len=lp,
                                 nb=bp, tl=s5_tl, permute=True)
    ys5_p = ys5_p.reshape(mp, S5_WIDTH)
    hr_p = hr_p[SUBLANES - bp:]
    hi_p = hi_p[SUBLANES - bp:]
    u_s = jnp.transpose(projm_s[:, :S5_WIDTH].reshape(bs, ls, S5_WIDTH), (1, 0, 2))
    h0 = (state_s5_re[0].reshape(bs, S5_CH), state_s5_im[0].reshape(bs, S5_CH))
    ys5_s, hr_s, hi_s = _s5_call(u_s.reshape(msamp, S5_WIDTH), *s5_consts, h0, row0=0, nsrc=1, rs=msamp,
                                 seq_len=msamp, nb=bs, tl=ls, permute=False)
    ys5_s = jnp.transpose(ys5_s.reshape(ls, bs, S5_WIDTH), (1, 0, 2)).reshape(msamp, S5_WIDTH)

    up_pad = jnp.pad(gla_gk_up[0], ((0, LANES - gla_gk_up.shape[1]), (0, 0)))
    gkb = gla_gk_b[0].reshape(1, -1)
    gnw = gla_norm[0].reshape(1, GLA_HV)
    ptile = 256
    ygla_p, sg_p = _gla_call(projm, projt, up_pad, gkb, gnw, None, row0=0, nseq_blocks=bp,
                             ntiles=lp // ptile, nbb=1, nch=ptile // GLA_CHUNK, cs=GLA_CHUNK, batch=bp,
                             hps=GLA_HEADS)
    gnbb = 32
    ygla_s, sg_s = _gla_call(projm_s, projt_s, up_pad, gkb, gnw, state_gla[0], row0=0,
                             nseq_blocks=bs // gnbb, ntiles=1, nbb=gnbb, nch=1, cs=ls, batch=bs, hps=1)

    h = _mm_out([ys5_p, ygla_p], [ys5_s, ygla_s], w_out_even, 0, (xp2, 0), (xs2, 0))
    hid = _norm_mm(h, norm_mlp[0], w_mlp_up, 0, D_FF, act="relu2", out_dtype=BF16, tn=1024)
    h = _mm_res(hid, w_mlp_down, 0, h, tn=D_MODEL, tk=512)

    n_odd = 4 * ML_HEADS * ML_DH
    projm, projt = _norm_mm(h, norm_mix[1], w_in_odd, 0, n_odd, out_dtype=BF16, tn=1024)
    gate_b = jnp.pad(jnp.concatenate([mlstm_b_i[0], mlstm_b_f[0]]), (0, LANES - 2 * ML_HEADS)).reshape(1, LANES)
    mnw = mlstm_norm[0].reshape(1, ML_DH)
    yml_p, c_p, n_p, m_p = _mlstm_call(projm, projt, gate_b, mnw, None, row0=0, nseq_blocks=bp,
                                       ntiles=lp // ML_CHUNK, nbb=1, cl=ML_CHUNK, rp=ML_CHUNK, batch=bp,
                                       hps=ML_HEADS)
    mnbb = 8
    init = (state_mlstm_c[0], state_mlstm_n[0].reshape(bs, ML_HEADS, 1, ML_DH),
            jnp.broadcast_to(state_mlstm_m[0][:, :, None, None], (bs, ML_HEADS, 1, LANES)))
    yml_s, c_s, n_s, m_s = _mlstm_call(projm, projt, gate_b, mnw, init, row0=mp, nseq_blocks=bs // mnbb,
                                       ntiles=1, nbb=mnbb, cl=ls, rp=LANES, batch=bs, hps=1)
    h = _mm_out([yml_p], [yml_s], w_out_odd, 0, (h, 0), (h, mp // 512))
    hid = _norm_mm(h, norm_mlp[1], w_mlp_up, 1, D_FF, act="relu2", out_dtype=BF16, tn=1024)
    h = _mm_res(hid, w_mlp_down, 1, h, tn=D_MODEL, tk=512)

    y_p, y_s = _rmsnorm(h, norm_final, mp)
    g, p = S5_GROUPS, S5_STATE
    return (y_p.reshape(bp, lp, d), y_s.reshape(bs, ls, d),
            hr_p.reshape(1, bp, g, p), hi_p.reshape(1, bp, g, p), sg_p[None],
            c_p[None], n_p.reshape(1, bp, ML_HEADS, ML_DH), m_p[:, :, 0, 0][None],
            hr_s.reshape(1, bs, g, p), hi_s.reshape(1, bs, g, p), sg_s[None],
            c_s[None], n_s.reshape(1, bs, ML_HEADS, ML_DH), m_s[:, :, 0, 0][None])
```

```python
import functools
import math

import jax
import jax.numpy as jnp
import numpy as np
from jax import lax
from jax.experimental import pallas as pl
from jax.experimental.pallas import tpu as pltpu

F32 = jnp.float32
BF16 = jnp.bfloat16
EPS = 1e-6

D_MODEL = 2048
D_FF = 4 * D_MODEL
S5_WIDTH = 1024
S5_GROUPS = 64
S5_GROUP = 16
S5_STATE = 64
S5_CH = S5_GROUPS * S5_STATE
S5_NBLK = 4
GLA_HEADS = 4
GLA_HK = 128
GLA_HV = 256
GLA_TAU = 16.0
GLA_CHUNK = 64
ML_HEADS = 4
ML_DH = 512
ML_CHUNK = 256

SUBLANES = 8
LANES = 128
VMEM_LIMIT = 56 * 1024 * 1024


def _cparams(sem):
    return pltpu.CompilerParams(dimension_semantics=sem, vmem_limit_bytes=VMEM_LIMIT)


def _dot(a, b):
    return jnp.dot(a, b, preferred_element_type=F32)


def _dot_nt(a, b):
    return lax.dot_general(a, b, (((1,), (1,)), ((), ())), preferred_element_type=F32)


def _split3(x):
    p1 = x.astype(BF16)
    r1 = x - p1.astype(F32)
    p2 = r1.astype(BF16)
    r2 = r1 - p2.astype(F32)
    p3 = r2.astype(BF16)
    return p1, p2, p3


def _dot_exact_lhs(m, x):
    p1, p2, p3 = _split3(x)
    return _dot(m, p1) + _dot(m, p2) + _dot(m, p3)


def _dot_nt_exact_lhs(m, x):
    p1, p2, p3 = _split3(x)
    return _dot_nt(m, p1) + _dot_nt(m, p2) + _dot_nt(m, p3)


def _log_sigmoid(x):
    return jnp.minimum(x, 0.0) - jnp.log1p(jnp.exp(-jnp.abs(x)))


def _sigmoid(x):
    return 1.0 / (1.0 + jnp.exp(-x))


def _norm_mm_body(*refs, act, n_tail, tm, rchunk):
    has_tail = n_tail > 0
    if has_tail:
        x_ref, g_ref, w_ref, wt_ref, o_ref, ot_ref, xn_ref = refs
    else:
        x_ref, g_ref, w_ref, o_ref, xn_ref = refs

    @pl.when(pl.program_id(1) == 0)
    def _():
        g = g_ref[...]
        for r in range(0, tm, rchunk):
            x = x_ref[r:r + rchunk, :]
            ms = jnp.mean(x * x, axis=-1, keepdims=True)
            xn_ref[r:r + rchunk, :] = (x * lax.rsqrt(ms + EPS) * g).astype(BF16)
        if has_tail:
            col = lax.broadcasted_iota(jnp.int32, wt_ref.shape, 0)
            wt = jnp.where(col < n_tail, wt_ref[...], 0.0)
            ot_ref[...] = _dot_nt(xn_ref[...], wt.astype(BF16))

    mm = _dot_nt if has_tail else _dot
    acc = mm(xn_ref[...], w_ref[...].astype(BF16))
    if act == "relu2":
        acc = jnp.square(jnp.maximum(acc, 0.0))
    o_ref[...] = acc.astype(o_ref.dtype)


def _norm_mm(x, g, w3, layer, n_main, *, act=None, out_dtype=F32, tm=1088, tn=512):
    m, kdim = x.shape
    grid = (m // tm, n_main // tn)
    n_tail = w3.shape[2] - n_main
    in_specs = [pl.BlockSpec((tm, kdim), lambda i, j: (i, 0)),
                pl.BlockSpec((1, kdim), lambda i, j: (0, 0))]
    args = [x, g.reshape(1, kdim)]
    out_shape = [jax.ShapeDtypeStruct((m, n_main), out_dtype)]
    out_specs = [pl.BlockSpec((tm, tn), lambda i, j: (i, j))]
    if n_tail > 0:
        w3t = jnp.transpose(w3, (0, 2, 1))
        in_specs += [pl.BlockSpec((None, tn, kdim), lambda i, j: (layer, j, 0)),
                     pl.BlockSpec((None, LANES, kdim), lambda i, j: (layer, n_main // LANES, 0))]
        args += [w3t, w3t]
        out_shape.append(jax.ShapeDtypeStruct((m, LANES), F32))
        out_specs.append(pl.BlockSpec((tm, LANES), lambda i, j: (i, 0)))
    else:
        in_specs.append(pl.BlockSpec((None, kdim, tn), lambda i, j: (layer, 0, j)))
        args.append(w3)
    body = functools.partial(_norm_mm_body, act=act, n_tail=n_tail, tm=tm, rchunk=tm // 4)
    res = pl.pallas_call(
        body, grid=grid, in_specs=in_specs, out_specs=out_specs, out_shape=out_shape,
        scratch_shapes=[pltpu.VMEM((tm, kdim), BF16)],
        compiler_params=_cparams(("arbitrary", "arbitrary")),
        name="norm_mm",
    )(*args)
    return res if n_tail > 0 else res[0]


def _mm_res_body(a_ref, w_ref, r_ref, o_ref):
    @pl.when(pl.program_id(2) == 0)
    def _():
        o_ref[...] = r_ref[...] + _dot(a_ref[...].astype(BF16), w_ref[...].astype(BF16))

    @pl.when(pl.program_id(2) > 0)
    def _():
        o_ref[...] = o_ref[...] + _dot(a_ref[...].astype(BF16), w_ref[...].astype(BF16))


def _mm_res(a, w3, layer, res, *, tm=1088, tn=1024, tk=1024):
    m, kdim = a.shape
    n = w3.shape[2]
    grid = (m // tm, n // tn, kdim // tk)
    return pl.pallas_call(
        _mm_res_body, grid=grid,
        in_specs=[
            pl.BlockSpec((tm, tk), lambda i, j, k: (i, k)),
            pl.BlockSpec((None, tk, tn), lambda i, j, k: (layer, k, j)),
            pl.BlockSpec((tm, tn), lambda i, j, k: (i, j)),
        ],
        out_specs=pl.BlockSpec((tm, tn), lambda i, j, k: (i, j)),
        out_shape=jax.ShapeDtypeStruct((m, n), F32),
        compiler_params=_cparams(("arbitrary", "arbitrary", "arbitrary")),
        name="mm_res",
    )(a, w3, res)


def _mm_out_body(*refs, kcs, n_ptiles):
    n = len(kcs)
    ap, asm = refs[:n], refs[n:2 * n]
    w_ref, rp_ref, rs_ref, o_ref, wb_ref = refs[2 * n:]
    i = pl.program_id(1)

    @pl.when(i == 0)
    def _():
        wb_ref[...] = w_ref[...].astype(BF16)

    def compute(srcs, r_ref):
        acc = r_ref[...]
        off = 0
        for a_ref, kc in zip(srcs, kcs):
            acc = acc + _dot(a_ref[...], wb_ref[off:off + kc, :])
            off += kc
        o_ref[...] = acc

    @pl.when(i < n_ptiles)
    def _():
        compute(ap, rp_ref)

    @pl.when(i >= n_ptiles)
    def _():
        compute(asm, rs_ref)


def _mm_out(a_prompt, a_sample, w3, layer, res_prompt, res_sample, *, tm=512, tn=1024):
    kcs = tuple(a.shape[1] for a in a_prompt)
    kdim, n = sum(kcs), w3.shape[2]
    n_ptiles = a_prompt[0].shape[0] // tm
    m = a_prompt[0].shape[0] + a_sample[0].shape[0]
    (rp, rp0), (rs, rs0) = res_prompt, res_sample
    pidx = lambda i: jnp.minimum(i, n_ptiles - 1)
    in_specs = ([pl.BlockSpec((tm, kc), lambda j, i: (pidx(i), 0)) for kc in kcs]
                + [pl.BlockSpec((tm, kc), lambda j, i: (0, 0)) for kc in kcs]
                + [pl.BlockSpec((None, kdim, tn), lambda j, i: (layer, 0, j)),
                   pl.BlockSpec((tm, tn), lambda j, i: (rp0 + pidx(i), j)),
                   pl.BlockSpec((tm, tn), lambda j, i: (rs0, j))])
    body = functools.partial(_mm_out_body, kcs=kcs, n_ptiles=n_ptiles)
    return pl.pallas_call(
        body, grid=(n // tn, n_ptiles + 1), in_specs=in_specs,
        out_specs=pl.BlockSpec((tm, tn), lambda j, i: (i, j)),
        out_shape=jax.ShapeDtypeStruct((m, n), F32),
        scratch_shapes=[pltpu.VMEM((kdim, tn), BF16)],
        compiler_params=_cparams(("arbitrary", "arbitrary")),
        name="mm_out",
    )(*a_prompt, *a_sample, w3, rp, rs)


def _rmsnorm_body(x_ref, g_ref, op_ref, os_ref, *, n_ptiles):
    x = x_ref[...]
    ms = jnp.mean(x * x, axis=-1, keepdims=True)
    y = x * lax.rsqrt(ms + EPS) * g_ref[...]

    @pl.when(pl.program_id(0) < n_ptiles)
    def _():
        op_ref[...] = y

    @pl.when(pl.program_id(0) >= n_ptiles)
    def _():
        os_ref[...] = y


def _rmsnorm(x, g, m_prompt, *, tm=512):
    m, d = x.shape
    n_ptiles = m_prompt // tm
    return pl.pallas_call(
        functools.partial(_rmsnorm_body, n_ptiles=n_ptiles), grid=(n_ptiles + 1,),
        in_specs=[pl.BlockSpec((tm, d), lambda i: (i, 0)), pl.BlockSpec((1, d), lambda i: (0, 0))],
        out_specs=[pl.BlockSpec((tm, d), lambda i: (jnp.minimum(i, n_ptiles - 1), 0)),
                   pl.BlockSpec((tm, d), lambda i: (0, 0))],
        out_shape=[jax.ShapeDtypeStruct((m_prompt, d), F32),
                   jax.ShapeDtypeStruct((m - m_prompt, d), F32)],
        compiler_params=_cparams(("arbitrary",)),
        name="final_norm",
    )(x, g.reshape(1, d))


def _s5_consts_body(ar_ref, ai_ref, ls_ref, br_ref, bi_ref, cr_ref, ci_ref,
                    abr_ref, abi_ref, bre_ref, bim_ref, cre_ref, cim_ref):
    g, p, q = S5_GROUPS, S5_STATE, S5_GROUP
    gb = g // S5_NBLK
    gq, wid = g * q, gb * p
    ar = ar_ref[...]
    ai = ai_ref[...]
    dt = jnp.exp(ls_ref[...])
    mag = jnp.exp(ar * dt)
    abr = mag * jnp.cos(ai * dt)
    abi = mag * jnp.sin(ai * dt)
    abr_ref[...] = abr
    abi_ref[...] = abi
    lam2 = ar * ar + ai * ai
    zr = abr - 1.0
    cr = (zr * ar + abi * ai) / lam2
    ci = (abi * ar - zr * ai) / lam2

    rep = jnp.where((lax.broadcasted_iota(jnp.int32, (gq, g), 0) // q)
                    == lax.broadcasted_iota(jnp.int32, (gq, g), 1), 1.0, 0.0).astype(BF16)
    cr_r = _dot_exact_lhs(rep, cr)
    ci_r = _dot_exact_lhs(rep, ci)
    br = br_ref[...]
    bi = bi_ref[...]
    bbr = cr_r * br - ci_r * bi
    bbi = cr_r * bi + ci_r * br

    tile = jnp.where(lax.broadcasted_iota(jnp.int32, (p, wid), 0)
                     == lax.broadcasted_iota(jnp.int32, (p, wid), 1) % p, 1.0, 0.0).astype(BF16)
    keep = ((lax.broadcasted_iota(jnp.int32, (gq, wid), 0) // q) % gb
            == lax.broadcasted_iota(jnp.int32, (gq, wid), 1) // p)

    def spread(t):
        return jnp.where(keep, _dot(t.astype(BF16), tile), 0.0)

    bre_ref[...] = spread(bbr).astype(BF16).reshape(bre_ref.shape)
    bim_ref[...] = spread(bbi).astype(BF16).reshape(bim_ref.shape)
    dre = spread(cr_ref[...])
    dim = spread(ci_ref[...])
    rows = gb * q
    for k in range(S5_NBLK):
        cre_ref[k] = jnp.transpose(dre[k * rows:(k + 1) * rows, :]).astype(BF16)
        cim_ref[k] = jnp.transpose(dim[k * rows:(k + 1) * rows, :]).astype(BF16)


def _s5_consts(a_re, a_im, log_step, b_re, b_im, c_re, c_im):
    g, p, q = S5_GROUPS, S5_STATE, S5_GROUP
    gb = g // S5_NBLK
    brt = jnp.transpose(b_re, (0, 2, 1)).reshape(g * q, p)
    bit = jnp.transpose(b_im, (0, 2, 1)).reshape(g * q, p)
    full = lambda shape: pl.BlockSpec(shape, lambda: (0,) * len(shape))
    ins = [(g, p), (g, p), (g, 1)] + [(g * q, p)] * 4
    outs = [((g, p), F32)] * 2 + [((S5_NBLK, gb * q, gb * p), BF16)] * 2 + [((S5_NBLK, gb * p, gb * q), BF16)] * 2
    abr, abi, bre, bim, cre, cim = pl.pallas_call(
        _s5_consts_body, in_specs=[full(s) for s in ins], out_specs=[full(s) for s, _ in outs],
        out_shape=[jax.ShapeDtypeStruct(s, d) for s, d in outs],
        compiler_params=pltpu.CompilerParams(vmem_limit_bytes=VMEM_LIMIT),
        name="s5_consts",
    )(a_re, a_im, log_step.reshape(g, 1), brt, bit, c_re.reshape(g * q, p), c_im.reshape(g * q, p))
    return abr.reshape(1, S5_CH), abi.reshape(1, S5_CH), bre, bim, cre, cim


def _s5_perm(nb, tl):
    r = np.arange(nb * tl)
    p = np.zeros((nb * tl, nb * tl), np.float32)
    p[r, (r % nb) * tl + r // nb] = 1.0
    return p


def _s5_body(*refs, nsrc, nb, has_init, permute):
    u_refs, refs = refs[:nsrc], refs[nsrc:]
    if permute:
        pm_ref, pmt_ref = refs[:2]
        refs = refs[2:]
    bre_ref, bim_ref, cre_ref, cim_ref, ar_ref, ai_ref, d_ref, gw_ref, gb_ref = refs[:9]
    refs = refs[9:]
    if has_init:
        h0r_ref, h0i_ref, y_ref, hr_ref, hi_ref, sre, sim, ysc = refs
    else:
        y_ref, hr_ref, hi_ref, sre, sim, ysc = refs
    rows = sre.shape[0]
    tl = rows // nb
    cb = S5_CH // S5_NBLK
    ub = S5_WIDTH // S5_NBLK
    hw = cb // 2

    @pl.when(pl.program_id(0) == 0)
    def _():
        if has_init:
            hr_ref[...] = h0r_ref[...]
            hi_ref[...] = h0i_ref[...]
        else:
            hr_ref[...] = jnp.zeros_like(hr_ref)
            hi_ref[...] = jnp.zeros_like(hi_ref)

    u_b = u_refs[0][...] if nsrc == 1 else jnp.concatenate([r[...] for r in u_refs], axis=0)
    if permute:
        u_tm = _dot(pm_ref[...], u_b)
        u_b = u_tm.astype(BF16)
    else:
        u_tm = u_b.astype(F32)
    low_rows = lax.broadcasted_iota(jnp.int32, (SUBLANES, hw), 0) < nb

    for k in range(S5_NBLK):
        uk = u_b[:, k * ub:(k + 1) * ub]
        sre[...] = _dot(uk, bre_ref[k])
        sim[...] = _dot(uk, bim_ref[k])
        for half in range(2):
            c0 = k * cb + half * hw
            l0 = half * hw
            ar8 = jnp.broadcast_to(ar_ref[:, c0:c0 + hw], (SUBLANES, hw))
            ai8 = jnp.broadcast_to(ai_ref[:, c0:c0 + hw], (SUBLANES, hw))

            if nb < SUBLANES:
                def j_body(j, c, l0=l0, ar8=ar8, ai8=ai8):
                    sr, si = c
                    row = pl.multiple_of(j * SUBLANES, SUBLANES)
                    dr = sre[pl.ds(row, SUBLANES), l0:l0 + hw]
                    di = sim[pl.ds(row, SUBLANES), l0:l0 + hw]
                    pr = pltpu.roll(sr, nb, 0)
                    pi = pltpu.roll(si, nb, 0)
                    yr = ar8 * pr - ai8 * pi + dr
                    yi = ar8 * pi + ai8 * pr + di
                    qr = pltpu.roll(yr, nb, 0)
                    qi = pltpu.roll(yi, nb, 0)
                    zr = ar8 * qr - ai8 * qi + dr
                    zi = ar8 * qi + ai8 * qr + di
                    sre[pl.ds(row, SUBLANES), l0:l0 + hw] = jnp.where(low_rows, yr, zr)
                    sim[pl.ds(row, SUBLANES), l0:l0 + hw] = jnp.where(low_rows, yi, zi)
                    return zr, zi

                sr, si = lax.fori_loop(0, rows // SUBLANES, j_body,
                                       (hr_ref[:, c0:c0 + hw], hi_ref[:, c0:c0 + hw]), unroll=True)
                hr_ref[:, c0:c0 + hw] = sr
                hi_ref[:, c0:c0 + hw] = si
            else:
                def rg_body(rg, carry, c0=c0, l0=l0, ar8=ar8, ai8=ai8):
                    r0 = pl.multiple_of(rg * SUBLANES, SUBLANES)
                    xr = hr_ref[pl.ds(r0, SUBLANES), c0:c0 + hw]
                    xi = hi_ref[pl.ds(r0, SUBLANES), c0:c0 + hw]
                    for t in range(tl):
                        row = pl.multiple_of(t * nb + r0, SUBLANES)
                        nr = ar8 * xr - ai8 * xi + sre[pl.ds(row, SUBLANES), l0:l0 + hw]
                        ni = ar8 * xi + ai8 * xr + sim[pl.ds(row, SUBLANES), l0:l0 + hw]
                        sre[pl.ds(row, SUBLANES), l0:l0 + hw] = nr
                        sim[pl.ds(row, SUBLANES), l0:l0 + hw] = ni
                        xr, xi = nr, ni
                    hr_ref[pl.ds(r0, SUBLANES), c0:c0 + hw] = xr
                    hi_ref[pl.ds(r0, SUBLANES), c0:c0 + hw] = xi
                    return carry

                lax.fori_loop(0, nb // SUBLANES, rg_body, 0)
        ysc[:, k * ub:(k + 1) * ub] = (_dot(sre[...].astype(BF16), cre_ref[k])
                                       - _dot(sim[...].astype(BF16), cim_ref[k]))

    y = ysc[...] + d_ref[...] * u_tm
    y = 0.5 * y * (1.0 + jnp.tanh(math.sqrt(2.0 / math.pi) * (y + 0.044715 * (y * y * y))))
    z = _dot(y.astype(BF16), gw_ref[...]) + gb_ref[...]
    out = (y * _sigmoid(z)).astype(BF16)
    if permute:
        out = _dot(pmt_ref[...], out).astype(BF16)
    y_ref[...] = out.reshape(y_ref.shape)


def _s5_call(src, bre, bim, cre, cim, abar_re, abar_im, d_skip, glu_w, glu_b, h0, *,
             row0, nsrc, rs, seq_len, nb, tl, permute):
    rows = nsrc * rs
    ntiles = seq_len // rs if nsrc > 1 else 1
    has_init = h0 is not None
    srows = max(nb, SUBLANES)
    const = lambda shape: pl.BlockSpec(shape, lambda t: (0,) * len(shape))
    in_specs = [pl.BlockSpec((rs, S5_WIDTH), lambda t, s=s: ((row0 + s * seq_len) // rs + t, 0))
                for s in range(nsrc)]
    args = [src] * nsrc
    if permute:
        pm = _s5_perm(nb, tl)
        in_specs += [const((rows, rows)), const((rows, rows))]
        args += [jnp.asarray(pm, BF16), jnp.asarray(pm.T, BF16)]
    in_specs += [
        const(bre.shape), const(bim.shape), const(cre.shape), const(cim.shape),
        const((1, S5_CH)), const((1, S5_CH)), const((1, S5_WIDTH)),
        const((S5_WIDTH, S5_WIDTH)), const((1, S5_WIDTH)),
    ]
    args += [bre, bim, cre, cim, abar_re, abar_im, d_skip, glu_w, glu_b]
    if has_init:
        in_specs += [const((srows, S5_CH)), const((srows, S5_CH))]
        args += [h0[0], h0[1]]
    body = functools.partial(_s5_body, nsrc=nsrc, nb=nb, has_init=has_init, permute=permute)
    return pl.pallas_call(
        body, grid=(ntiles,), in_specs=in_specs,
        out_specs=[pl.BlockSpec((nsrc, rs, S5_WIDTH), lambda t: (0, t, 0)),
                   const((srows, S5_CH)), const((srows, S5_CH))],
        out_shape=[jax.ShapeDtypeStruct((nsrc, ntiles * rs, S5_WIDTH), BF16),
                   jax.ShapeDtypeStruct((srows, S5_CH), F32),
                   jax.ShapeDtypeStruct((srows, S5_CH), F32)],
        scratch_shapes=[pltpu.VMEM((rows, S5_CH // S5_NBLK), F32),
                        pltpu.VMEM((rows, S5_CH // S5_NBLK), F32),
                        pltpu.VMEM((rows, S5_WIDTH), F32)],
        compiler_params=_cparams(("arbitrary",)),
        name="s5_mixer",
    )(*args)


def _gla_body(*refs, nbb, nch, cs, has_init, hps):
    if has_init:
        (q_ref, k_ref, v_ref, g_ref, gl_ref, up_ref, gb_ref, nw_ref, s0_ref, y_ref, s_ref) = refs
    else:
        (q_ref, k_ref, v_ref, g_ref, gl_ref, up_ref, gb_ref, nw_ref, y_ref, s_ref) = refs
    rows = nbb * nch * cs
    sh = int(math.log2(cs))

    @pl.when(pl.program_id(2) == 0)
    def _():
        if has_init:
            s_ref[...] = s0_ref[...]
        else:
            s_ref[...] = jnp.zeros_like(s_ref)

    ri = lax.broadcasted_iota(jnp.int32, (rows, rows), 0)
    ci = lax.broadcasted_iota(jnp.int32, (rows, rows), 1)
    same = (ri >> sh) == (ci >> sh)
    causal = jnp.logical_and(same, ci <= ri)
    lmat = jnp.where(causal, 1.0, 0.0).astype(BF16)
    tmat = jnp.where(same, 1.0, 0.0).astype(BF16)
    rchunk = lax.broadcasted_iota(jnp.int32, (rows, 1), 0) >> sh
    cchunk = lax.broadcasted_iota(jnp.int32, (1, rows), 1) >> sh
    single = nbb * nch == 1
    row_sliced = not single and cs % 16 == 0
    z = _dot(gl_ref[...].astype(BF16), up_ref[...].astype(BF16)) + gb_ref[...]
    la = _log_sigmoid(z) * (1.0 / GLA_TAU)
    p1, p2, p3 = _split3(la)
    bcum_all = _dot(lmat, p1) + _dot(lmat, p2) + _dot(lmat, p3)
    tot_all = _dot(tmat, p1) + _dot(tmat, p2) + _dot(tmat, p3)

    for hh in range(hps):
        ks = slice(hh * GLA_HK, (hh + 1) * GLA_HK)
        vs = slice(hh * GLA_HV, (hh + 1) * GLA_HV)
        q = q_ref[:, ks].astype(F32) * (GLA_HK ** -0.5)
        k = k_ref[:, ks].astype(F32)
        vb = v_ref[:, vs]
        bcum = bcum_all[:, ks]
        tot = tot_all[:, ks]

        qd = q * jnp.exp(bcum)
        kd = k * jnp.exp(-bcum)
        kdec = k * jnp.exp(tot - bcum)
        qdb = qd.astype(BF16)
        att = jnp.where(causal, _dot_nt(qdb, kd.astype(BF16)), 0.0)
        o = _dot(att.astype(BF16), vb)

        kdec_t = jnp.transpose(kdec)
        dec_t = jnp.transpose(jnp.exp(tot))
        inter = []
        for b in range(nbb):
            s = s_ref[b, hh]
            for c in range(nch):
                idx = b * nch + c
                if row_sliced:
                    inter.append(_dot(qdb[idx * cs:(idx + 1) * cs], s.astype(BF16)))
                else:
                    oi = _dot(qdb, s.astype(BF16))
                    o = o + (oi if single else jnp.where(rchunk == idx, oi, 0.0))
                kt = kdec_t if single else jnp.where(cchunk == idx, kdec_t, 0.0)
                s = dec_t[:, idx * cs:idx * cs + 1] * s + _dot(kt.astype(BF16), vb)
            s_ref[b, hh] = s
        if row_sliced:
            o = o + jnp.concatenate(inter, axis=0)

        o = o * lax.rsqrt(jnp.mean(o * o, axis=-1, keepdims=True) + EPS) * nw_ref[...]
        gg = g_ref[:, vs].astype(F32)
        y_ref[:, vs] = (o * (gg * _sigmoid(gg))).astype(y_ref.dtype)


def _gla_call(projm, projt, up_pad, gkb, nw, s0, *, row0, nseq_blocks, ntiles, nbb, nch, cs, batch, hps):
    rows = nbb * nch * cs
    has_init = s0 is not None
    rb0 = row0 // rows
    rowblk = lambda s, h, t: rb0 + s * ntiles + t
    wk, wv = hps * GLA_HK, hps * GLA_HV
    qoff = S5_WIDTH // wk
    koff = qoff + GLA_HEADS // hps
    voff = (S5_WIDTH + 2 * GLA_HEADS * GLA_HK) // wv
    goff = voff + GLA_HEADS // hps
    in_specs = [
        pl.BlockSpec((rows, wk), lambda s, h, t: (rowblk(s, h, t), qoff + h)),
        pl.BlockSpec((rows, wk), lambda s, h, t: (rowblk(s, h, t), koff + h)),
        pl.BlockSpec((rows, wv), lambda s, h, t: (rowblk(s, h, t), voff + h)),
        pl.BlockSpec((rows, wv), lambda s, h, t: (rowblk(s, h, t), goff + h)),
        pl.BlockSpec((rows, LANES), lambda s, h, t: (rowblk(s, h, t), 0)),
        pl.BlockSpec((LANES, wk), lambda s, h, t: (0, h)),
        pl.BlockSpec((1, wk), lambda s, h, t: (0, h)),
        pl.BlockSpec((1, GLA_HV), lambda s, h, t: (0, 0)),
    ]
    args = [projm, projm, projm, projm, projt, up_pad, gkb, nw]
    if has_init:
        in_specs.append(pl.BlockSpec((nbb, hps, GLA_HK, GLA_HV), lambda s, h, t: (s, h, 0, 0)))
        args.append(s0)
    body = functools.partial(_gla_body, nbb=nbb, nch=nch, cs=cs, has_init=has_init, hps=hps)
    return pl.pallas_call(
        body, grid=(nseq_blocks, GLA_HEADS // hps, ntiles), in_specs=in_specs,
        out_specs=[pl.BlockSpec((rows, wv), lambda s, h, t: (s * ntiles + t, h)),
                   pl.BlockSpec((nbb, hps, GLA_HK, GLA_HV), lambda s, h, t: (s, h, 0, 0))],
        out_shape=[jax.ShapeDtypeStruct((nseq_blocks * ntiles * rows, GLA_HEADS * GLA_HV), BF16),
                   jax.ShapeDtypeStruct((batch, GLA_HEADS, GLA_HK, GLA_HV), F32)],
        compiler_params=_cparams(("arbitrary", "arbitrary", "arbitrary")),
        name="gla_mixer",
    )(*args)


def _mlstm_body(*refs, nbb, cl, rp, has_init, hps):
    if has_init:
        (q_ref, k_ref, v_ref, og_ref, gt_ref, gb_ref, nw_ref, c0_ref, n0_ref, m0_ref,
         y_ref, c_ref, n_ref, m_ref) = refs
    else:
        (q_ref, k_ref, v_ref, og_ref, gt_ref, gb_ref, nw_ref, y_ref, c_ref, n_ref, m_ref) = refs
    rows = nbb * cl
    sh = int(math.log2(cl))
    head0 = pl.program_id(1) * hps

    @pl.when(pl.program_id(2) == 0)
    def _():
        if has_init:
            c_ref[...] = c0_ref[...]
            n_ref[...] = n0_ref[...]
            m_ref[...] = m0_ref[...]
        else:
            c_ref[...] = jnp.zeros_like(c_ref)
            n_ref[...] = jnp.zeros_like(n_ref)
            m_ref[...] = jnp.full(m_ref.shape, -jnp.inf, F32)

    def pad(x):
        if rp == rows:
            return x
        return jnp.concatenate([x, jnp.zeros((rp - rows, x.shape[1]), x.dtype)], axis=0)

    ksc = ML_DH ** -0.5
    gates = pad(gt_ref[...] + gb_ref[...])
    lane = lax.broadcasted_iota(jnp.int32, (rp, LANES), 1)
    ri = lax.broadcasted_iota(jnp.int32, (rp, rp), 0)
    ci = lax.broadcasted_iota(jnp.int32, (rp, rp), 1)
    causal = jnp.logical_and((ri >> sh) == (ci >> sh), ci <= ri)
    lmat = jnp.where(causal, 1.0, 0.0).astype(BF16)
    neg = jnp.where(causal, 0.0, -jnp.inf)
    er = lax.broadcasted_iota(jnp.int32, (SUBLANES, LANES), 0)
    ec = lax.broadcasted_iota(jnp.int32, (SUBLANES, LANES), 1)
    pick = jnp.where(er == ec, 1.0, 0.0).astype(BF16)
    rbatch = lax.broadcasted_iota(jnp.int32, (rp, 1), 0) >> sh
    cbatch = lax.broadcasted_iota(jnp.int32, (1, rp), 1) >> sh
    fc_all = _dot_exact_lhs(lmat, _log_sigmoid(gates))
    rows_all = _dot_nt_exact_lhs(pick, jnp.where(lane < ML_HEADS, gates, fc_all))
    sub = lax.broadcasted_iota(jnp.int32, (SUBLANES, rp), 0)

    for hh in range(hps):
        head = head0 + hh
        hs = slice(hh * ML_DH, (hh + 1) * ML_DH)
        qb = pad(q_ref[:, hs])
        kb = pad(k_ref[:, hs])
        vb = pad(v_ref[:, hs])
        og = pad(og_ref[:, hs]).astype(F32)
        ig = jnp.sum(jnp.where(lane == head, gates, 0.0), axis=1, keepdims=True)
        fcum = jnp.sum(jnp.where(lane == head + ML_HEADS, fc_all, 0.0), axis=1, keepdims=True)
        ig_row = jnp.sum(jnp.where(sub == head, rows_all, 0.0), axis=0, keepdims=True)
        fcum_row = jnp.sum(jnp.where(sub == head + ML_HEADS, rows_all, 0.0), axis=0, keepdims=True)

        mm = jnp.zeros((rp, 1), F32)
        for b in range(nbb):
            mb = m_ref[b, hh][:, 0:1]
            mm = jnp.where(rbatch == b, mb, mm) if nbb > 1 else jnp.broadcast_to(mb, (rp, 1))

        dmat = fcum - fcum_row + ig_row + neg
        dprev = fcum + mm
        m = jnp.maximum(jnp.max(dmat, axis=1, keepdims=True), dprev)
        w = jnp.exp(dmat - m + math.log(ksc))
        wp = jnp.exp(dprev - m)
        sc = _dot_nt(qb, kb) * w
        num = _dot(sc.astype(BF16), vb)
        den = jnp.sum(sc, axis=1, keepdims=True)

        k_t = jnp.transpose(kb.astype(F32))
        for b in range(nbb):
            cm = c_ref[b, hh]
            nm = n_ref[b, hh]
            mb = m_ref[b, hh][:, 0:1]
            qc = wp * _dot(qb, cm.astype(BF16))
            nm8 = jnp.broadcast_to(nm, (SUBLANES, ML_DH)).astype(BF16)
            qn = wp * _dot_nt(qb, nm8)[:, 0:1]
            if nbb > 1:
                rsel = rbatch == b
                qc = jnp.where(rsel, qc, 0.0)
                qn = jnp.where(rsel, qn, 0.0)
            num = num + qc
            den = den + qn
            last = b * cl + cl - 1
            m_new = m[last:last + 1, :]
            fl = fcum[last:last + 1, :]
            decay = jnp.exp(fl + mb - m_new)
            wk_row = ksc * jnp.exp(fl - fcum_row + ig_row - m_new)
            if nbb > 1:
                wk_row = jnp.where(cbatch == b, wk_row, 0.0)
            c_ref[b, hh] = decay * cm + _dot((k_t * wk_row).astype(BF16), vb)
            wk8 = jnp.broadcast_to(wk_row, (SUBLANES, rp)).astype(BF16)
            n_ref[b, hh] = decay * nm + _dot(wk8, kb)[0:1, :]
            m_ref[b, hh] = jnp.broadcast_to(m_new, (1, LANES))

        hv = num / jnp.maximum(jnp.abs(den), jnp.exp(-m))
        hv = hv * _sigmoid(og)
        hv = hv * lax.rsqrt(jnp.mean(hv * hv, axis=-1, keepdims=True) + EPS) * nw_ref[...]
        y_ref[:, hs] = hv[:rows].astype(y_ref.dtype)


def _mlstm_call(projm, projt, gate_b, nw, init, *, row0, nseq_blocks, ntiles, nbb, cl, rp, batch, hps):
    rows = nbb * cl
    has_init = init is not None
    rb0 = row0 // rows
    rowblk = lambda s, h, t: rb0 + s * ntiles + t
    hd = ML_HEADS // hps
    wd = hps * ML_DH
    in_specs = [
        pl.BlockSpec((rows, wd), lambda s, h, t: (rowblk(s, h, t), h)),
        pl.BlockSpec((rows, wd), lambda s, h, t: (rowblk(s, h, t), hd + h)),
        pl.BlockSpec((rows, wd), lambda s, h, t: (rowblk(s, h, t), 2 * hd + h)),
        pl.BlockSpec((rows, wd), lambda s, h, t: (rowblk(s, h, t), 3 * hd + h)),
        pl.BlockSpec((rows, LANES), lambda s, h, t: (rowblk(s, h, t), 0)),
        pl.BlockSpec((1, LANES), lambda s, h, t: (0, 0)),
        pl.BlockSpec((1, ML_DH), lambda s, h, t: (0, 0)),
    ]
    args = [projm, projm, projm, projm, projt, gate_b, nw]
    c_spec = pl.BlockSpec((nbb, hps, ML_DH, ML_DH), lambda s, h, t: (s, h, 0, 0))
    n_spec = pl.BlockSpec((nbb, hps, 1, ML_DH), lambda s, h, t: (s, h, 0, 0))
    m_spec = pl.BlockSpec((nbb, hps, 1, LANES), lambda s, h, t: (s, h, 0, 0))
    if has_init:
        in_specs += [c_spec, n_spec, m_spec]
        args += list(init)
    body = functools.partial(_mlstm_body, nbb=nbb, cl=cl, rp=rp, has_init=has_init, hps=hps)
    return pl.pallas_call(
        body, grid=(nseq_blocks, hd, ntiles), in_specs=in_specs,
        out_specs=[pl.BlockSpec((rows, wd), lambda s, h, t: (s * ntiles + t, h)),
                   c_spec, n_spec, m_spec],
        out_shape=[jax.ShapeDtypeStruct((nseq_blocks * ntiles * rows, ML_HEADS * ML_DH), BF16),
                   jax.ShapeDtypeStruct((batch, ML_HEADS, ML_DH, ML_DH), F32),
                   jax.ShapeDtypeStruct((batch, ML_HEADS, 1, ML_DH), F32),
                   jax.ShapeDtypeStruct((batch, ML_HEADS, 1, LANES), F32)],
        compiler_params=_cparams(("arbitrary", "arbitrary", "arbitrary")),
        name="mlstm_mixer",
    )(*args)


def kernel(x_prompt, x_sample, state_s5_re, state_s5_im, state_gla, state_mlstm_c, state_mlstm_n,
           state_mlstm_m, norm_mix, norm_mlp, norm_final, w_in_even, s5_a_re, s5_a_im, s5_log_step,
           s5_b_re, s5_b_im, s5_c_re, s5_c_im, s5_d, s5_glu_w, s5_glu_b, gla_gk_up, gla_gk_b,
           gla_norm, w_out_even, w_in_odd, mlstm_b_i, mlstm_b_f, mlstm_norm, w_out_odd,
           w_mlp_up, w_mlp_down):
    bp, lp, d = x_prompt.shape
    bs, ls, _ = x_sample.shape
    mp = bp * lp
    msamp = bs * ls
    xp2 = x_prompt.reshape(mp, d)
    xs2 = x_sample.reshape(msamp, d)

    n_even = S5_WIDTH + 2 * GLA_HEADS * GLA_HK + 2 * GLA_HEADS * GLA_HV
    projm, projt = _norm_mm(xp2, norm_mix[0], w_in_even, 0, n_even, tm=1024, tn=1024, out_dtype=BF16)
    projm_s, projt_s = _norm_mm(xs2, norm_mix[0], w_in_even, 0, n_even, tm=msamp, tn=1024, out_dtype=BF16)

    abar_re, abar_im, bre, bim, cre, cim = _s5_consts(s5_a_re[0], s5_a_im[0], s5_log_step[0], s5_b_re[0],
                                                      s5_b_im[0], s5_c_re[0], s5_c_im[0])
    s5_consts = (bre, bim, cre, cim, abar_re, abar_im, s5_d[0].reshape(1, S5_WIDTH),
                 s5_glu_w[0].astype(BF16), s5_glu_b[0].reshape(1, S5_WIDTH))

    s5_tl = 64
    ys5_p, hr_p, hi_p = _s5_call(projm, *s5_consts, None, row0=0, nsrc=bp, rs=s5_tl, seq_len=lp,
                                 nb=bp, tl=s5_tl, permute=True)
    ys5_p = ys5_p.reshape(mp, S5_WIDTH)
    hr_p = hr_p[SUBLANES - bp:]
    hi_p = hi_p[SUBLANES - bp:]
    u_s = jnp.transpose(projm_s[:, :S5_WIDTH].reshape(bs, ls, S5_WIDTH), (1, 0, 2))
    h0 = (state_s5_re[0].reshape(bs, S5_CH), state_s5_im[0].reshape(bs, S5_CH))
    ys5_s, hr_s, hi_s = _s5_call(u_s.reshape(msamp, S5_WIDTH), *s5_consts, h0, row0=0, nsrc=1, rs=msamp,
                                 seq_len=msamp, nb=bs, tl=ls, permute=False)
    ys5_s = jnp.transpose(ys5_s.reshape(ls, bs, S5_WIDTH), (1, 0, 2)).reshape(msamp, S5_WIDTH)

    up_pad = jnp.pad(gla_gk_up[0], ((0, LANES - gla_gk_up.shape[1]), (0, 0)))
    gkb = gla_gk_b[0].reshape(1, -1)
    gnw = gla_norm[0].reshape(1, GLA_HV)
    ptile = 256
    ygla_p, sg_p = _gla_call(projm, projt, up_pad, gkb, gnw, None, row0=0, nseq_blocks=bp,
                             ntiles=lp // ptile, nbb=1, nch=ptile // GLA_CHUNK, cs=GLA_CHUNK, batch=bp,
                             hps=GLA_HEADS)
    gnbb = 32
    ygla_s, sg_s = _gla_call(projm_s, projt_s, up_pad, gkb, gnw, state_gla[0], row0=0,
                             nseq_blocks=bs // gnbb, ntiles=1, nbb=gnbb, nch=1, cs=ls, batch=bs, hps=1)

    h = _mm_out([ys5_p, ygla_p], [ys5_s, ygla_s], w_out_even, 0, (xp2, 0), (xs2, 0))
    hid = _norm_mm(h, norm_mlp[0], w_mlp_up, 0, D_FF, act="relu2", out_dtype=BF16, tn=1024)
    h = _mm_res(hid, w_mlp_down, 0, h, tn=D_MODEL, tk=512)

    n_odd = 4 * ML_HEADS * ML_DH
    projm, projt = _norm_mm(h, norm_mix[1], w_in_odd, 0, n_odd, out_dtype=BF16, tn=1024)
    gate_b = jnp.pad(jnp.concatenate([mlstm_b_i[0], mlstm_b_f[0]]), (0, LANES - 2 * ML_HEADS)).reshape(1, LANES)
    mnw = mlstm_norm[0].reshape(1, ML_DH)
    yml_p, c_p, n_p, m_p = _mlstm_call(projm, projt, gate_b, mnw, None, row0=0, nseq_blocks=bp,
                                       ntiles=lp // ML_CHUNK, nbb=1, cl=ML_CHUNK, rp=ML_CHUNK, batch=bp,
                                       hps=ML_HEADS)
    mnbb = 8
    init = (state_mlstm_c[0], state_mlstm_n[0].reshape(bs, ML_HEADS, 1, ML_DH),
            jnp.broadcast_to(state_mlstm_m[0][:, :, None, None], (bs, ML_HEADS, 1, LANES)))
    yml_s, c_s, n_s, m_s = _mlstm_call(projm, projt, gate_b, mnw, init, row0=mp, nseq_blocks=bs // mnbb,
                                       ntiles=1, nbb=mnbb, cl=ls, rp=LANES, batch=bs, hps=1)
    h = _mm_out([yml_p], [yml_s], w_out_odd, 0, (h, 0), (h, mp // 512))
    hid = _norm_mm(h, norm_mlp[1], w_mlp_up, 1, D_FF, act="relu2", out_dtype=BF16, tn=1024)
    h = _mm_res(hid, w_mlp_down, 1, h, tn=D_MODEL, tk=512)

    y_p, y_s = _rmsnorm(h, norm_final, mp)
    g, p = S5_GROUPS, S5_STATE
    return (y_p.reshape(bp, lp, d), y_s.reshape(bs, ls, d),
            hr_p.reshape(1, bp, g, p), hi_p.reshape(1, bp, g, p), sg_p[None],
            c_p[None], n_p.reshape(1, bp, ML_HEADS, ML_DH), m_p[:, :, 0, 0][None],
            hr_s.reshape(1, bs, g, p), hi_s.reshape(1, bs, g, p), sg_s[None],
            c_s[None], n_s.reshape(1, bs, ML_HEADS, ML_DH), m_s[:, :, 0, 0][None])
```

```python
import functools
import math

import jax
import jax.numpy as jnp
import numpy as np
from jax import lax
from jax.experimental import pallas as pl
from jax.experimental.pallas import tpu as pltpu

F32 = jnp.float32
BF16 = jnp.bfloat16
EPS = 1e-6

D_MODEL = 2048
D_FF = 4 * D_MODEL
S5_WIDTH = 1024
S5_GROUPS = 64
S5_GROUP = 16
S5_STATE = 64
S5_CH = S5_GROUPS * S5_STATE
S5_NBLK = 4
GLA_HEADS = 4
GLA_HK = 128
GLA_HV = 256
GLA_TAU = 16.0
GLA_CHUNK = 64
ML_HEADS = 4
ML_DH = 512
ML_CHUNK = 256

SUBLANES = 8
LANES = 128
VMEM_LIMIT = 56 * 1024 * 1024


def _cparams(sem):
    return pltpu.CompilerParams(dimension_semantics=sem, vmem_limit_bytes=VMEM_LIMIT)


def _dot(a, b):
    return jnp.dot(a, b, preferred_element_type=F32)


def _dot_nt(a, b):
    return lax.dot_general(a, b, (((1,), (1,)), ((), ())), preferred_element_type=F32)


def _split3(x):
    p1 = x.astype(BF16)
    r1 = x - p1.astype(F32)
    p2 = r1.astype(BF16)
    r2 = r1 - p2.astype(F32)
    p3 = r2.astype(BF16)
    return p1, p2, p3


def _dot_exact_lhs(m, x):
    p1, p2, p3 = _split3(x)
    return _dot(m, p1) + _dot(m, p2) + _dot(m, p3)


def _dot_nt_exact_lhs(m, x):
    p1, p2, p3 = _split3(x)
    return _dot_nt(m, p1) + _dot_nt(m, p2) + _dot_nt(m, p3)


def _log_sigmoid(x):
    return jnp.minimum(x, 0.0) - jnp.log1p(jnp.exp(-jnp.abs(x)))


def _sigmoid(x):
    return 1.0 / (1.0 + jnp.exp(-x))


def _norm_mm_body(*refs, act, n_tail, tm, rchunk):
    has_tail = n_tail > 0
    if has_tail:
        x_ref, g_ref, w_ref, wt_ref, o_ref, ot_ref, xn_ref = refs
    else:
        x_ref, g_ref, w_ref, o_ref, xn_ref = refs

    @pl.when(pl.program_id(1) == 0)
    def _():
        g = g_ref[...]
        for r in range(0, tm, rchunk):
            x = x_ref[r:r + rchunk, :]
            ms = jnp.mean(x * x, axis=-1, keepdims=True)
            xn_ref[r:r + rchunk, :] = (x * lax.rsqrt(ms + EPS) * g).astype(BF16)
        if has_tail:
            col = lax.broadcasted_iota(jnp.int32, wt_ref.shape, 0)
            wt = jnp.where(col < n_tail, wt_ref[...], 0.0)
            ot_ref[...] = _dot_nt(xn_ref[...], wt.astype(BF16))

    mm = _dot_nt if has_tail else _dot
    acc = mm(xn_ref[...], w_ref[...].astype(BF16))
    if act == "relu2":
        acc = jnp.square(jnp.maximum(acc, 0.0))
    o_ref[...] = acc.astype(o_ref.dtype)


def _norm_mm(x, g, w3, layer, n_main, *, act=None, out_dtype=F32, tm=1088, tn=512):
    m, kdim = x.shape
    grid = (m // tm, n_main // tn)
    n_tail = w3.shape[2] - n_main
    in_specs = [pl.BlockSpec((tm, kdim), lambda i, j: (i, 0)),
                pl.BlockSpec((1, kdim), lambda i, j: (0, 0))]
    args = [x, g.reshape(1, kdim)]
    out_shape = [jax.ShapeDtypeStruct((m, n_main), out_dtype)]
    out_specs = [pl.BlockSpec((tm, tn), lambda i, j: (i, j))]
    if n_tail > 0:
        w3t = jnp.transpose(w3, (0, 2, 1))
        in_specs += [pl.BlockSpec((None, tn, kdim), lambda i, j: (layer, j, 0)),
                     pl.BlockSpec((None, LANES, kdim), lambda i, j: (layer, n_main // LANES, 0))]
        args += [w3t, w3t]
        out_shape.append(jax.ShapeDtypeStruct((m, LANES), F32))
        out_specs.append(pl.BlockSpec((tm, LANES), lambda i, j: (i, 0)))
    else:
        in_specs.append(pl.BlockSpec((None, kdim, tn), lambda i, j: (layer, 0, j)))
        args.append(w3)
    body = functools.partial(_norm_mm_body, act=act, n_tail=n_tail, tm=tm, rchunk=tm // 4)
    res = pl.pallas_call(
        body, grid=grid, in_specs=in_specs, out_specs=out_specs, out_shape=out_shape,
        scratch_shapes=[pltpu.VMEM((tm, kdim), BF16)],
        compiler_params=_cparams(("arbitrary", "arbitrary")),
        name="norm_mm",
    )(*args)
    return res if n_tail > 0 else res[0]


def _norm_mm_lookahead(x, g, w3, layer, n_main, *, act=None, out_dtype=F32, tm=1088, tn=1024):
    m, kdim = x.shape
    ni, nj = m // tm, n_main // tn
    rchunk = tm // 4

    def outer(x_hbm, g_hbm, w_hbm, o_hbm, xn_ref, cnt_ref):
        cnt_ref[0] = 0

        def body(x_ref, g_ref, w_ref, o_ref):
            step = cnt_ref[0]
            cnt_ref[0] = step + 1

            @pl.when(step % nj == 0)
            def _():
                gv = g_ref[...]
                for r in range(0, tm, rchunk):
                    xr = x_ref[r:r + rchunk, :]
                    ms = jnp.mean(xr * xr, axis=-1, keepdims=True)
                    xn_ref[r:r + rchunk, :] = (xr * lax.rsqrt(ms + EPS) * gv).astype(BF16)

            acc = _dot(xn_ref[...], w_ref[...].astype(BF16))
            if act == "relu2":
                acc = jnp.square(jnp.maximum(acc, 0.0))
            o_ref[...] = acc.astype(o_ref.dtype)

        pltpu.emit_pipeline(
            body, grid=(ni, nj),
            in_specs=[
                pl.BlockSpec((tm, kdim), lambda i, j: (i, 0),
                             pipeline_mode=pl.Buffered(2, use_lookahead=True)),
                pl.BlockSpec((1, kdim), lambda i, j: (0, 0)),
                pl.BlockSpec((None, kdim, tn), lambda i, j: (layer, 0, j)),
            ],
            out_specs=[pl.BlockSpec((tm, tn), lambda i, j: (i, j))],
        )(x_hbm, g_hbm, w_hbm, o_hbm)

    return pl.pallas_call(
        outer,
        in_specs=[pl.BlockSpec(memory_space=pl.ANY)] * 3,
        out_specs=pl.BlockSpec(memory_space=pl.ANY),
        out_shape=jax.ShapeDtypeStruct((m, n_main), out_dtype),
        scratch_shapes=[pltpu.VMEM((tm, kdim), BF16), pltpu.SMEM((1,), jnp.int32)],
        compiler_params=pltpu.CompilerParams(vmem_limit_bytes=VMEM_LIMIT),
        name="norm_mm_la",
    )(x, g.reshape(1, kdim), w3)


def _mm_res_body(a_ref, w_ref, r_ref, o_ref):
    @pl.when(pl.program_id(2) == 0)
    def _():
        o_ref[...] = r_ref[...] + _dot(a_ref[...].astype(BF16), w_ref[...].astype(BF16))

    @pl.when(pl.program_id(2) > 0)
    def _():
        o_ref[...] = o_ref[...] + _dot(a_ref[...].astype(BF16), w_ref[...].astype(BF16))


def _mm_res(a, w3, layer, res, *, tm=1088, tn=1024, tk=1024):
    m, kdim = a.shape
    n = w3.shape[2]
    grid = (m // tm, n // tn, kdim // tk)
    return pl.pallas_call(
        _mm_res_body, grid=grid,
        in_specs=[
            pl.BlockSpec((tm, tk), lambda i, j, k: (i, k)),
            pl.BlockSpec((None, tk, tn), lambda i, j, k: (layer, k, j)),
            pl.BlockSpec((tm, tn), lambda i, j, k: (i, j)),
        ],
        out_specs=pl.BlockSpec((tm, tn), lambda i, j, k: (i, j)),
        out_shape=jax.ShapeDtypeStruct((m, n), F32),
        compiler_params=_cparams(("arbitrary", "arbitrary", "arbitrary")),
        name="mm_res",
    )(a, w3, res)


def _mm_out_body(*refs, kcs, n_ptiles):
    n = len(kcs)
    ap, asm = refs[:n], refs[n:2 * n]
    w_ref, rp_ref, rs_ref, o_ref, wb_ref = refs[2 * n:]
    i = pl.program_id(1)

    @pl.when(i == 0)
    def _():
        wb_ref[...] = w_ref[...].astype(BF16)

    def compute(srcs, r_ref):
        acc = r_ref[...]
        off = 0
        for a_ref, kc in zip(srcs, kcs):
            acc = acc + _dot(a_ref[...], wb_ref[off:off + kc, :])
            off += kc
        o_ref[...] = acc

    @pl.when(i < n_ptiles)
    def _():
        compute(ap, rp_ref)

    @pl.when(i >= n_ptiles)
    def _():
        compute(asm, rs_ref)


def _mm_out(a_prompt, a_sample, w3, layer, res_prompt, res_sample, *, tm=512, tn=1024):
    kcs = tuple(a.shape[1] for a in a_prompt)
    kdim, n = sum(kcs), w3.shape[2]
    n_ptiles = a_prompt[0].shape[0] // tm
    m = a_prompt[0].shape[0] + a_sample[0].shape[0]
    (rp, rp0), (rs, rs0) = res_prompt, res_sample
    pidx = lambda i: jnp.minimum(i, n_ptiles - 1)
    in_specs = ([pl.BlockSpec((tm, kc), lambda j, i: (pidx(i), 0)) for kc in kcs]
                + [pl.BlockSpec((tm, kc), lambda j, i: (0, 0)) for kc in kcs]
                + [pl.BlockSpec((None, kdim, tn), lambda j, i: (layer, 0, j)),
                   pl.BlockSpec((tm, tn), lambda j, i: (rp0 + pidx(i), j)),
                   pl.BlockSpec((tm, tn), lambda j, i: (rs0, j))])
    body = functools.partial(_mm_out_body, kcs=kcs, n_ptiles=n_ptiles)
    return pl.pallas_call(
        body, grid=(n // tn, n_ptiles + 1), in_specs=in_specs,
        out_specs=pl.BlockSpec((tm, tn), lambda j, i: (i, j)),
        out_shape=jax.ShapeDtypeStruct((m, n), F32),
        scratch_shapes=[pltpu.VMEM((kdim, tn), BF16)],
        compiler_params=_cparams(("arbitrary", "arbitrary")),
        name="mm_out",
    )(*a_prompt, *a_sample, w3, rp, rs)


def _rmsnorm_body(x_ref, g_ref, op_ref, os_ref, *, n_ptiles):
    x = x_ref[...]
    ms = jnp.mean(x * x, axis=-1, keepdims=True)
    y = x * lax.rsqrt(ms + EPS) * g_ref[...]

    @pl.when(pl.program_id(0) < n_ptiles)
    def _():
        op_ref[...] = y

    @pl.when(pl.program_id(0) >= n_ptiles)
    def _():
        os_ref[...] = y


def _rmsnorm(x, g, m_prompt, *, tm=512):
    m, d = x.shape
    n_ptiles = m_prompt // tm
    return pl.pallas_call(
        functools.partial(_rmsnorm_body, n_ptiles=n_ptiles), grid=(n_ptiles + 1,),
        in_specs=[pl.BlockSpec((tm, d), lambda i: (i, 0)), pl.BlockSpec((1, d), lambda i: (0, 0))],
        out_specs=[pl.BlockSpec((tm, d), lambda i: (jnp.minimum(i, n_ptiles - 1), 0)),
                   pl.BlockSpec((tm, d), lambda i: (0, 0))],
        out_shape=[jax.ShapeDtypeStruct((m_prompt, d), F32),
                   jax.ShapeDtypeStruct((m - m_prompt, d), F32)],
        compiler_params=_cparams(("arbitrary",)),
        name="final_norm",
    )(x, g.reshape(1, d))


def _s5_consts_body(ar_ref, ai_ref, ls_ref, br_ref, bi_ref, cr_ref, ci_ref,
                    abr_ref, abi_ref, bre_ref, bim_ref, cre_ref, cim_ref):
    g, p, q = S5_GROUPS, S5_STATE, S5_GROUP
    gb = g // S5_NBLK
    gq, wid = g * q, gb * p
    ar = ar_ref[...]
    ai = ai_ref[...]
    dt = jnp.exp(ls_ref[...])
    mag = jnp.exp(ar * dt)
    abr = mag * jnp.cos(ai * dt)
    abi = mag * jnp.sin(ai * dt)
    abr_ref[...] = abr
    abi_ref[...] = abi
    lam2 = ar * ar + ai * ai
    zr = abr - 1.0
    cr = (zr * ar + abi * ai) / lam2
    ci = (abi * ar - zr * ai) / lam2

    rep = jnp.where((lax.broadcasted_iota(jnp.int32, (gq, g), 0) // q)
                    == lax.broadcasted_iota(jnp.int32, (gq, g), 1), 1.0, 0.0).astype(BF16)
    cr_r = _dot_exact_lhs(rep, cr)
    ci_r = _dot_exact_lhs(rep, ci)
    br = br_ref[...]
    bi = bi_ref[...]
    bbr = cr_r * br - ci_r * bi
    bbi = cr_r * bi + ci_r * br

    tile = jnp.where(lax.broadcasted_iota(jnp.int32, (p, wid), 0)
                     == lax.broadcasted_iota(jnp.int32, (p, wid), 1) % p, 1.0, 0.0).astype(BF16)
    keep = ((lax.broadcasted_iota(jnp.int32, (gq, wid), 0) // q) % gb
            == lax.broadcasted_iota(jnp.int32, (gq, wid), 1) // p)

    def spread(t):
        return jnp.where(keep, _dot(t.astype(BF16), tile), 0.0)

    bre_ref[...] = spread(bbr).astype(BF16).reshape(bre_ref.shape)
    bim_ref[...] = spread(bbi).astype(BF16).reshape(bim_ref.shape)
    dre = spread(cr_ref[...])
    dim = spread(ci_ref[...])
    rows = gb * q
    for k in range(S5_NBLK):
        cre_ref[k] = jnp.transpose(dre[k * rows:(k + 1) * rows, :]).astype(BF16)
        cim_ref[k] = jnp.transpose(dim[k * rows:(k + 1) * rows, :]).astype(BF16)


def _s5_consts(a_re, a_im, log_step, b_re, b_im, c_re, c_im):
    g, p, q = S5_GROUPS, S5_STATE, S5_GROUP
    gb = g // S5_NBLK
    brt = jnp.transpose(b_re, (0, 2, 1)).reshape(g * q, p)
    bit = jnp.transpose(b_im, (0, 2, 1)).reshape(g * q, p)
    full = lambda shape: pl.BlockSpec(shape, lambda: (0,) * len(shape))
    ins = [(g, p), (g, p), (g, 1)] + [(g * q, p)] * 4
    outs = [((g, p), F32)] * 2 + [((S5_NBLK, gb * q, gb * p), BF16)] * 2 + [((S5_NBLK, gb * p, gb * q), BF16)] * 2
    abr, abi, bre, bim, cre, cim = pl.pallas_call(
        _s5_consts_body, in_specs=[full(s) for s in ins], out_specs=[full(s) for s, _ in outs],
        out_shape=[jax.ShapeDtypeStruct(s, d) for s, d in outs],
        compiler_params=pltpu.CompilerParams(vmem_limit_bytes=VMEM_LIMIT),
        name="s5_consts",
    )(a_re, a_im, log_step.reshape(g, 1), brt, bit, c_re.reshape(g * q, p), c_im.reshape(g * q, p))
    return abr.reshape(1, S5_CH), abi.reshape(1, S5_CH), bre, bim, cre, cim


def _s5_perm(nb, tl):
    r = np.arange(nb * tl)
    p = np.zeros((nb * tl, nb * tl), np.float32)
    p[r, (r % nb) * tl + r // nb] = 1.0
    return p


def _s5_body(*refs, nsrc, nb, has_init, permute):
    u_refs, refs = refs[:nsrc], refs[nsrc:]
    if permute:
        pm_ref, pmt_ref = refs[:2]
        refs = refs[2:]
    bre_ref, bim_ref, cre_ref, cim_ref, ar_ref, ai_ref, d_ref, gw_ref, gb_ref = refs[:9]
    refs = refs[9:]
    if has_init:
        h0r_ref, h0i_ref, y_ref, hr_ref, hi_ref, sre, sim, ysc = refs
    else:
        y_ref, hr_ref, hi_ref, sre, sim, ysc = refs
    rows = sre.shape[0]
    tl = rows // nb
    cb = S5_CH // S5_NBLK
    ub = S5_WIDTH // S5_NBLK
    hw = cb // 2

    @pl.when(pl.program_id(0) == 0)
    def _():
        if has_init:
            hr_ref[...] = h0r_ref[...]
            hi_ref[...] = h0i_ref[...]
        else:
            hr_ref[...] = jnp.zeros_like(hr_ref)
            hi_ref[...] = jnp.zeros_like(hi_ref)

    u_b = u_refs[0][...] if nsrc == 1 else jnp.concatenate([r[...] for r in u_refs], axis=0)
    if permute:
        u_tm = _dot(pm_ref[...], u_b)
        u_b = u_tm.astype(BF16)
    else:
        u_tm = u_b.astype(F32)
    low_rows = lax.broadcasted_iota(jnp.int32, (SUBLANES, hw), 0) < nb

    for k in range(S5_NBLK):
        uk = u_b[:, k * ub:(k + 1) * ub]
        sre[...] = _dot(uk, bre_ref[k])
        sim[...] = _dot(uk, bim_ref[k])
        for half in range(2):
            c0 = k * cb + half * hw
            l0 = half * hw
            ar8 = jnp.broadcast_to(ar_ref[:, c0:c0 + hw], (SUBLANES, hw))
            ai8 = jnp.broadcast_to(ai_ref[:, c0:c0 + hw], (SUBLANES, hw))

            if nb < SUBLANES:
                def j_body(j, c, l0=l0, ar8=ar8, ai8=ai8):
                    sr, si = c
                    row = pl.multiple_of(j * SUBLANES, SUBLANES)
                    dr = sre[pl.ds(row, SUBLANES), l0:l0 + hw]
                    di = sim[pl.ds(row, SUBLANES), l0:l0 + hw]
                    pr = pltpu.roll(sr, nb, 0)
                    pi = pltpu.roll(si, nb, 0)
                    yr = ar8 * pr - ai8 * pi + dr
                    yi = ar8 * pi + ai8 * pr + di
                    qr = pltpu.roll(yr, nb, 0)
                    qi = pltpu.roll(yi, nb, 0)
                    zr = ar8 * qr - ai8 * qi + dr
                    zi = ar8 * qi + ai8 * qr + di
                    sre[pl.ds(row, SUBLANES), l0:l0 + hw] = jnp.where(low_rows, yr, zr)
                    sim[pl.ds(row, SUBLANES), l0:l0 + hw] = jnp.where(low_rows, yi, zi)
                    return zr, zi

                sr, si = lax.fori_loop(0, rows // SUBLANES, j_body,
                                       (hr_ref[:, c0:c0 + hw], hi_ref[:, c0:c0 + hw]), unroll=True)
                hr_ref[:, c0:c0 + hw] = sr
                hi_ref[:, c0:c0 + hw] = si
            else:
                def rg_body(rg, carry, c0=c0, l0=l0, ar8=ar8, ai8=ai8):
                    r0 = pl.multiple_of(rg * SUBLANES, SUBLANES)
                    xr = hr_ref[pl.ds(r0, SUBLANES), c0:c0 + hw]
                    xi = hi_ref[pl.ds(r0, SUBLANES), c0:c0 + hw]
                    for t in range(tl):
                        row = pl.multiple_of(t * nb + r0, SUBLANES)
                        nr = ar8 * xr - ai8 * xi + sre[pl.ds(row, SUBLANES), l0:l0 + hw]
                        ni = ar8 * xi + ai8 * xr + sim[pl.ds(row, SUBLANES), l0:l0 + hw]
                        sre[pl.ds(row, SUBLANES), l0:l0 + hw] = nr
                        sim[pl.ds(row, SUBLANES), l0:l0 + hw] = ni
                        xr, xi = nr, ni
                    hr_ref[pl.ds(r0, SUBLANES), c0:c0 + hw] = xr
                    hi_ref[pl.ds(r0, SUBLANES), c0:c0 + hw] = xi
                    return carry

                lax.fori_loop(0, nb // SUBLANES, rg_body, 0)
        ysc[:, k * ub:(k + 1) * ub] = (_dot(sre[...].astype(BF16), cre_ref[k])
                                       - _dot(sim[...].astype(BF16), cim_ref[k]))

    y = ysc[...] + d_ref[...] * u_tm
    y = 0.5 * y * (1.0 + jnp.tanh(math.sqrt(2.0 / math.pi) * (y + 0.044715 * (y * y * y))))
    z = _dot(y.astype(BF16), gw_ref[...]) + gb_ref[...]
    out = (y * _sigmoid(z)).astype(BF16)
    if permute:
        out = _dot(pmt_ref[...], out).astype(BF16)
    y_ref[...] = out.reshape(y_ref.shape)


def _s5_call(src, bre, bim, cre, cim, abar_re, abar_im, d_skip, glu_w, glu_b, h0, *,
             row0, nsrc, rs, seq_len, nb, tl, permute):
    rows = nsrc * rs
    ntiles = seq_len // rs if nsrc > 1 else 1
    has_init = h0 is not None
    srows = max(nb, SUBLANES)
    const = lambda shape: pl.BlockSpec(shape, lambda t: (0,) * len(shape))
    in_specs = [pl.BlockSpec((rs, S5_WIDTH), lambda t, s=s: ((row0 + s * seq_len) // rs + t, 0))
                for s in range(nsrc)]
    args = [src] * nsrc
    if permute:
        pm = _s5_perm(nb, tl)
        in_specs += [const((rows, rows)), const((rows, rows))]
        args += [jnp.asarray(pm, BF16), jnp.asarray(pm.T, BF16)]
    in_specs += [
        const(bre.shape), const(bim.shape), const(cre.shape), const(cim.shape),
        const((1, S5_CH)), const((1, S5_CH)), const((1, S5_WIDTH)),
        const((S5_WIDTH, S5_WIDTH)), const((1, S5_WIDTH)),
    ]
    args += [bre, bim, cre, cim, abar_re, abar_im, d_skip, glu_w, glu_b]
    if has_init:
        in_specs += [const((srows, S5_CH)), const((srows, S5_CH))]
        args += [h0[0], h0[1]]
    body = functools.partial(_s5_body, nsrc=nsrc, nb=nb, has_init=has_init, permute=permute)
    return pl.pallas_call(
        body, grid=(ntiles,), in_specs=in_specs,
        out_specs=[pl.BlockSpec((nsrc, rs, S5_WIDTH), lambda t: (0, t, 0)),
                   const((srows, S5_CH)), const((srows, S5_CH))],
        out_shape=[jax.ShapeDtypeStruct((nsrc, ntiles * rs, S5_WIDTH), BF16),
                   jax.ShapeDtypeStruct((srows, S5_CH), F32),
                   jax.ShapeDtypeStruct((srows, S5_CH), F32)],
        scratch_shapes=[pltpu.VMEM((rows, S5_CH // S5_NBLK), F32),
                        pltpu.VMEM((rows, S5_CH // S5_NBLK), F32),
                        pltpu.VMEM((rows, S5_WIDTH), F32)],
        compiler_params=_cparams(("arbitrary",)),
        name="s5_mixer",
    )(*args)


def _gla_body(*refs, nbb, nch, cs, has_init, hps):
    if has_init:
        (q_ref, k_ref, v_ref, g_ref, gl_ref, up_ref, gb_ref, nw_ref, s0_ref, y_ref, s_ref) = refs
    else:
        (q_ref, k_ref, v_ref, g_ref, gl_ref, up_ref, gb_ref, nw_ref, y_ref, s_ref) = refs
    rows = nbb * nch * cs
    sh = int(math.log2(cs))

    @pl.when(pl.program_id(2) == 0)
    def _():
        if has_init:
            s_ref[...] = s0_ref[...]
        else:
            s_ref[...] = jnp.zeros_like(s_ref)

    ri = lax.broadcasted_iota(jnp.int32, (rows, rows), 0)
    ci = lax.broadcasted_iota(jnp.int32, (rows, rows), 1)
    same = (ri >> sh) == (ci >> sh)
    causal = jnp.logical_and(same, ci <= ri)
    lmat = jnp.where(causal, 1.0, 0.0).astype(BF16)
    tmat = jnp.where(same, 1.0, 0.0).astype(BF16)
    rchunk = lax.broadcasted_iota(jnp.int32, (rows, 1), 0) >> sh
    cchunk = lax.broadcasted_iota(jnp.int32, (1, rows), 1) >> sh
    single = nbb * nch == 1
    z = _dot(gl_ref[...].astype(BF16), up_ref[...].astype(BF16)) + gb_ref[...]
    la = _log_sigmoid(z) * (1.0 / GLA_TAU)
    p1, p2, p3 = _split3(la)
    bcum_all = _dot(lmat, p1) + _dot(lmat, p2) + _dot(lmat, p3)
    tot_all = _dot(tmat, p1) + _dot(tmat, p2) + _dot(tmat, p3)

    for hh in range(hps):
        ks = slice(hh * GLA_HK, (hh + 1) * GLA_HK)
        vs = slice(hh * GLA_HV, (hh + 1) * GLA_HV)
        q = q_ref[:, ks].astype(F32) * (GLA_HK ** -0.5)
        k = k_ref[:, ks].astype(F32)
        vb = v_ref[:, vs]
        bcum = bcum_all[:, ks]
        tot = tot_all[:, ks]

        qd = q * jnp.exp(bcum)
        kd = k * jnp.exp(-bcum)
        kdec = k * jnp.exp(tot - bcum)
        qdb = qd.astype(BF16)
        att = jnp.where(causal, _dot_nt(qdb, kd.astype(BF16)), 0.0)
        o = _dot(att.astype(BF16), vb)

        kdec_t = jnp.transpose(kdec)
        dec_t = jnp.transpose(jnp.exp(tot))
        for b in range(nbb):
            s = s_ref[b, hh]
            for c in range(nch):
                idx = b * nch + c
                oi = _dot(qdb, s.astype(BF16))
                o = o + (oi if single else jnp.where(rchunk == idx, oi, 0.0))
                kt = kdec_t if single else jnp.where(cchunk == idx, kdec_t, 0.0)
                s = dec_t[:, idx * cs:idx * cs + 1] * s + _dot(kt.astype(BF16), vb)
            s_ref[b, hh] = s

        o = o * lax.rsqrt(jnp.mean(o * o, axis=-1, keepdims=True) + EPS) * nw_ref[...]
        gg = g_ref[:, vs].astype(F32)
        y_ref[:, vs] = (o * (gg * _sigmoid(gg))).astype(y_ref.dtype)


def _gla_call(projm, projt, up_pad, gkb, nw, s0, *, row0, nseq_blocks, ntiles, nbb, nch, cs, batch, hps):
    rows = nbb * nch * cs
    has_init = s0 is not None
    rb0 = row0 // rows
    rowblk = lambda s, h, t: rb0 + s * ntiles + t
    wk, wv = hps * GLA_HK, hps * GLA_HV
    qoff = S5_WIDTH // wk
    koff = qoff + GLA_HEADS // hps
    voff = (S5_WIDTH + 2 * GLA_HEADS * GLA_HK) // wv
    goff = voff + GLA_HEADS // hps
    in_specs = [
        pl.BlockSpec((rows, wk), lambda s, h, t: (rowblk(s, h, t), qoff + h)),
        pl.BlockSpec((rows, wk), lambda s, h, t: (rowblk(s, h, t), koff + h)),
        pl.BlockSpec((rows, wv), lambda s, h, t: (rowblk(s, h, t), voff + h)),
        pl.BlockSpec((rows, wv), lambda s, h, t: (rowblk(s, h, t), goff + h)),
        pl.BlockSpec((rows, LANES), lambda s, h, t: (rowblk(s, h, t), 0)),
        pl.BlockSpec((LANES, wk), lambda s, h, t: (0, h)),
        pl.BlockSpec((1, wk), lambda s, h, t: (0, h)),
        pl.BlockSpec((1, GLA_HV), lambda s, h, t: (0, 0)),
    ]
    args = [projm, projm, projm, projm, projt, up_pad, gkb, nw]
    if has_init:
        in_specs.append(pl.BlockSpec((nbb, hps, GLA_HK, GLA_HV), lambda s, h, t: (s, h, 0, 0)))
        args.append(s0)
    body = functools.partial(_gla_body, nbb=nbb, nch=nch, cs=cs, has_init=has_init, hps=hps)
    return pl.pallas_call(
        body, grid=(nseq_blocks, GLA_HEADS // hps, ntiles), in_specs=in_specs,
        out_specs=[pl.BlockSpec((rows, wv), lambda s, h, t: (s * ntiles + t, h)),
                   pl.BlockSpec((nbb, hps, GLA_HK, GLA_HV), lambda s, h, t: (s, h, 0, 0))],
        out_shape=[jax.ShapeDtypeStruct((nseq_blocks * ntiles * rows, GLA_HEADS * GLA_HV), BF16),
                   jax.ShapeDtypeStruct((batch, GLA_HEADS, GLA_HK, GLA_HV), F32)],
        compiler_params=_cparams(("arbitrary", "arbitrary", "arbitrary")),
        name="gla_mixer",
    )(*args)


def _mlstm_body(*refs, nbb, cl, rp, has_init, hps):
    if has_init:
        (q_ref, k_ref, v_ref, og_ref, gt_ref, gb_ref, nw_ref, c0_ref, n0_ref, m0_ref,
         y_ref, c_ref, n_ref, m_ref) = refs
    else:
        (q_ref, k_ref, v_ref, og_ref, gt_ref, gb_ref, nw_ref, y_ref, c_ref, n_ref, m_ref) = refs
    rows = nbb * cl
    sh = int(math.log2(cl))
    head0 = pl.program_id(1) * hps

    @pl.when(pl.program_id(2) == 0)
    def _():
        if has_init:
            c_ref[...] = c0_ref[...]
            n_ref[...] = n0_ref[...]
            m_ref[...] = m0_ref[...]
        else:
            c_ref[...] = jnp.zeros_like(c_ref)
            n_ref[...] = jnp.zeros_like(n_ref)
            m_ref[...] = jnp.full(m_ref.shape, -jnp.inf, F32)

    def pad(x):
        if rp == rows:
            return x
        return jnp.concatenate([x, jnp.zeros((rp - rows, x.shape[1]), x.dtype)], axis=0)

    ksc = ML_DH ** -0.5
    gates = pad(gt_ref[...] + gb_ref[...])
    lane = lax.broadcasted_iota(jnp.int32, (rp, LANES), 1)
    ri = lax.broadcasted_iota(jnp.int32, (rp, rp), 0)
    ci = lax.broadcasted_iota(jnp.int32, (rp, rp), 1)
    causal = jnp.logical_and((ri >> sh) == (ci >> sh), ci <= ri)
    lmat = jnp.where(causal, 1.0, 0.0).astype(BF16)
    neg = jnp.where(causal, 0.0, -jnp.inf)
    er = lax.broadcasted_iota(jnp.int32, (SUBLANES, LANES), 0)
    ec = lax.broadcasted_iota(jnp.int32, (SUBLANES, LANES), 1)
    pick = jnp.where(er == ec, 1.0, 0.0).astype(BF16)
    rbatch = lax.broadcasted_iota(jnp.int32, (rp, 1), 0) >> sh
    cbatch = lax.broadcasted_iota(jnp.int32, (1, rp), 1) >> sh
    fc_all = _dot_exact_lhs(lmat, _log_sigmoid(gates))
    rows_all = _dot_nt_exact_lhs(pick, jnp.where(lane < ML_HEADS, gates, fc_all))
    sub = lax.broadcasted_iota(jnp.int32, (SUBLANES, rp), 0)

    for hh in range(hps):
        head = head0 + hh
        hs = slice(hh * ML_DH, (hh + 1) * ML_DH)
        qb = pad(q_ref[:, hs])
        kb = pad(k_ref[:, hs])
        vb = pad(v_ref[:, hs])
        og = pad(og_ref[:, hs]).astype(F32)
        ig = jnp.sum(jnp.where(lane == head, gates, 0.0), axis=1, keepdims=True)
        fcum = jnp.sum(jnp.where(lane == head + ML_HEADS, fc_all, 0.0), axis=1, keepdims=True)
        ig_row = jnp.sum(jnp.where(sub == head, rows_all, 0.0), axis=0, keepdims=True)
        fcum_row = jnp.sum(jnp.where(sub == head + ML_HEADS, rows_all, 0.0), axis=0, keepdims=True)

        mm = jnp.zeros((rp, 1), F32)
        for b in range(nbb):
            mb = m_ref[b, hh][:, 0:1]
            mm = jnp.where(rbatch == b, mb, mm) if nbb > 1 else jnp.broadcast_to(mb, (rp, 1))

        dmat = fcum - fcum_row + ig_row + neg
        dprev = fcum + mm
        m = jnp.maximum(jnp.max(dmat, axis=1, keepdims=True), dprev)
        w = jnp.exp(dmat - m + math.log(ksc))
        wp = jnp.exp(dprev - m)
        sc = _dot_nt(qb, kb) * w
        num = _dot(sc.astype(BF16), vb)
        den = jnp.sum(sc, axis=1, keepdims=True)

        k_t = jnp.transpose(kb.astype(F32))
        for b in range(nbb):
            cm = c_ref[b, hh]
            nm = n_ref[b, hh]
            mb = m_ref[b, hh][:, 0:1]
            qc = wp * _dot(qb, cm.astype(BF16))
            nm8 = jnp.broadcast_to(nm, (SUBLANES, ML_DH)).astype(BF16)
            qn = wp * _dot_nt(qb, nm8)[:, 0:1]
            if nbb > 1:
                rsel = rbatch == b
                qc = jnp.where(rsel, qc, 0.0)
                qn = jnp.where(rsel, qn, 0.0)
            num = num + qc
            den = den + qn
            last = b * cl + cl - 1
            m_new = m[last:last + 1, :]
            fl = fcum[last:last + 1, :]
            decay = jnp.exp(fl + mb - m_new)
            wk_row = ksc * jnp.exp(fl - fcum_row + ig_row - m_new)
            if nbb > 1:
                wk_row = jnp.where(cbatch == b, wk_row, 0.0)
            c_ref[b, hh] = decay * cm + _dot((k_t * wk_row).astype(BF16), vb)
            wk8 = jnp.broadcast_to(wk_row, (SUBLANES, rp)).astype(BF16)
            n_ref[b, hh] = decay * nm + _dot(wk8, kb)[0:1, :]
            m_ref[b, hh] = jnp.broadcast_to(m_new, (1, LANES))

        hv = num / jnp.maximum(jnp.abs(den), jnp.exp(-m))
        hv = hv * _sigmoid(og)
        hv = hv * lax.rsqrt(jnp.mean(hv * hv, axis=-1, keepdims=True) + EPS) * nw_ref[...]
        y_ref[:, hs] = hv[:rows].astype(y_ref.dtype)


def _mlstm_call(projm, projt, gate_b, nw, init, *, row0, nseq_blocks, ntiles, nbb, cl, rp, batch, hps):
    rows = nbb * cl
    has_init = init is not None
    rb0 = row0 // rows
    rowblk = lambda s, h, t: rb0 + s * ntiles + t
    hd = ML_HEADS // hps
    wd = hps * ML_DH
    in_specs = [
        pl.BlockSpec((rows, wd), lambda s, h, t: (rowblk(s, h, t), h)),
        pl.BlockSpec((rows, wd), lambda s, h, t: (rowblk(s, h, t), hd + h)),
        pl.BlockSpec((rows, wd), lambda s, h, t: (rowblk(s, h, t), 2 * hd + h)),
        pl.BlockSpec((rows, wd), lambda s, h, t: (rowblk(s, h, t), 3 * hd + h)),
        pl.BlockSpec((rows, LANES), lambda s, h, t: (rowblk(s, h, t), 0)),
        pl.BlockSpec((1, LANES), lambda s, h, t: (0, 0)),
        pl.BlockSpec((1, ML_DH), lambda s, h, t: (0, 0)),
    ]
    args = [projm, projm, projm, projm, projt, gate_b, nw]
    c_spec = pl.BlockSpec((nbb, hps, ML_DH, ML_DH), lambda s, h, t: (s, h, 0, 0))
    n_spec = pl.BlockSpec((nbb, hps, 1, ML_DH), lambda s, h, t: (s, h, 0, 0))
    m_spec = pl.BlockSpec((nbb, hps, 1, LANES), lambda s, h, t: (s, h, 0, 0))
    if has_init:
        in_specs += [c_spec, n_spec, m_spec]
        args += list(init)
    body = functools.partial(_mlstm_body, nbb=nbb, cl=cl, rp=rp, has_init=has_init, hps=hps)
    return pl.pallas_call(
        body, grid=(nseq_blocks, hd, ntiles), in_specs=in_specs,
        out_specs=[pl.BlockSpec((rows, wd), lambda s, h, t: (s * ntiles + t, h)),
                   c_spec, n_spec, m_spec],
        out_shape=[jax.ShapeDtypeStruct((nseq_blocks * ntiles * rows, ML_HEADS * ML_DH), BF16),
                   jax.ShapeDtypeStruct((batch, ML_HEADS, ML_DH, ML_DH), F32),
                   jax.ShapeDtypeStruct((batch, ML_HEADS, 1, ML_DH), F32),
                   jax.ShapeDtypeStruct((batch, ML_HEADS, 1, LANES), F32)],
        compiler_params=_cparams(("arbitrary", "arbitrary", "arbitrary")),
        name="mlstm_mixer",
    )(*args)


def kernel(x_prompt, x_sample, state_s5_re, state_s5_im, state_gla, state_mlstm_c, state_mlstm_n,
           state_mlstm_m, norm_mix, norm_mlp, norm_final, w_in_even, s5_a_re, s5_a_im, s5_log_step,
           s5_b_re, s5_b_im, s5_c_re, s5_c_im, s5_d, s5_glu_w, s5_glu_b, gla_gk_up, gla_gk_b,
           gla_norm, w_out_even, w_in_odd, mlstm_b_i, mlstm_b_f, mlstm_norm, w_out_odd,
           w_mlp_up, w_mlp_down):
    bp, lp, d = x_prompt.shape
    bs, ls, _ = x_sample.shape
    mp = bp * lp
    msamp = bs * ls
    xp2 = x_prompt.reshape(mp, d)
    xs2 = x_sample.reshape(msamp, d)

    n_even = S5_WIDTH + 2 * GLA_HEADS * GLA_HK + 2 * GLA_HEADS * GLA_HV
    projm, projt = _norm_mm(xp2, norm_mix[0], w_in_even, 0, n_even, tm=1024, tn=1024, out_dtype=BF16)
    projm_s, projt_s = _norm_mm(xs2, norm_mix[0], w_in_even, 0, n_even, tm=msamp, tn=1024, out_dtype=BF16)

    abar_re, abar_im, bre, bim, cre, cim = _s5_consts(s5_a_re[0], s5_a_im[0], s5_log_step[0], s5_b_re[0],
                                                      s5_b_im[0], s5_c_re[0], s5_c_im[0])
    s5_consts = (bre, bim, cre, cim, abar_re, abar_im, s5_d[0].reshape(1, S5_WIDTH),
                 s5_glu_w[0].astype(BF16), s5_glu_b[0].reshape(1, S5_WIDTH))

    s5_tl = 64
    ys5_p, hr_p, hi_p = _s5_call(projm, *s5_consts, None, row0=0, nsrc=bp, rs=s5_tl, seq_len=lp,
                                 nb=bp, tl=s5_tl, permute=True)
    ys5_p = ys5_p.reshape(mp, S5_WIDTH)
    hr_p = hr_p[SUBLANES - bp:]
    hi_p = hi_p[SUBLANES - bp:]
    u_s = jnp.transpose(projm_s[:, :S5_WIDTH].reshape(bs, ls, S5_WIDTH), (1, 0, 2))
    h0 = (state_s5_re[0].reshape(bs, S5_CH), state_s5_im[0].reshape(bs, S5_CH))
    ys5_s, hr_s, hi_s = _s5_call(u_s.reshape(msamp, S5_WIDTH), *s5_consts, h0, row0=0, nsrc=1, rs=msamp,
                                 seq_len=msamp, nb=bs, tl=ls, permute=False)
    ys5_s = jnp.transpose(ys5_s.reshape(ls, bs, S5_WIDTH), (1, 0, 2)).reshape(msamp, S5_WIDTH)

    up_pad = jnp.pad(gla_gk_up[0], ((0, LANES - gla_gk_up.shape[1]), (0, 0)))
    gkb = gla_gk_b[0].reshape(1, -1)
    gnw = gla_norm[0].reshape(1, GLA_HV)
    ptile = 256
    ygla_p, sg_p = _gla_call(projm, projt, up_pad, gkb, gnw, None, row0=0, nseq_blocks=bp,
                             ntiles=lp // ptile, nbb=1, nch=ptile // GLA_CHUNK, cs=GLA_CHUNK, batch=bp,
                             hps=GLA_HEADS)
    gnbb = 32
    ygla_s, sg_s = _gla_call(projm_s, projt_s, up_pad, gkb, gnw, state_gla[0], row0=0,
                             nseq_blocks=bs // gnbb, ntiles=1, nbb=gnbb, nch=1, cs=ls, batch=bs, hps=1)

    h = _mm_out([ys5_p, ygla_p], [ys5_s, ygla_s], w_out_even, 0, (xp2, 0), (xs2, 0))
    hid = _norm_mm_lookahead(h, norm_mlp[0], w_mlp_up, 0, D_FF, act="relu2", out_dtype=BF16)
    h = _mm_res(hid, w_mlp_down, 0, h, tn=D_MODEL, tk=512)

    n_odd = 4 * ML_HEADS * ML_DH
    projm, projt = _norm_mm(h, norm_mix[1], w_in_odd, 0, n_odd, out_dtype=BF16, tn=1024)
    gate_b = jnp.pad(jnp.concatenate([mlstm_b_i[0], mlstm_b_f[0]]), (0, LANES - 2 * ML_HEADS)).reshape(1, LANES)
    mnw = mlstm_norm[0].reshape(1, ML_DH)
    yml_p, c_p, n_p, m_p = _mlstm_call(projm, projt, gate_b, mnw, None, row0=0, nseq_blocks=bp,
                                       ntiles=lp // ML_CHUNK, nbb=1, cl=ML_CHUNK, rp=ML_CHUNK, batch=bp,
                                       hps=ML_HEADS)
    mnbb = 8
    init = (state_mlstm_c[0], state_mlstm_n[0].reshape(bs, ML_HEADS, 1, ML_DH),
            jnp.broadcast_to(state_mlstm_m[0][:, :, None, None], (bs, ML_HEADS, 1, LANES)))
    yml_s, c_s, n_s, m_s = _mlstm_call(projm, projt, gate_b, mnw, init, row0=mp, nseq_blocks=bs // mnbb,
                                       ntiles=1, nbb=mnbb, cl=ls, rp=LANES, batch=bs, hps=1)
    h = _mm_out([yml_p], [yml_s], w_out_odd, 0, (h, 0), (h, mp // 512))
    hid = _norm_mm_lookahead(h, norm_mlp[1], w_mlp_up, 1, D_FF, act="relu2", out_dtype=BF16)
    h = _mm_res(hid, w_mlp_down, 1, h, tn=D_MODEL, tk=512)

    y_p, y_s = _rmsnorm(h, norm_final, mp)
    g, p = S5_GROUPS, S5_STATE
    return (y_p.reshape(bp, lp, d), y_s.reshape(bs, ls, d),
            hr_p.reshape(1, bp, g, p), hi_p.reshape(1, bp, g, p), sg_p[None],
            c_p[None], n_p.reshape(1, bp, ML_HEADS, ML_DH), m_p[:, :, 0, 0][None],
            hr_s.reshape(1, bs, g, p), hi_s.reshape(1, bs, g, p), sg_s[None],
            c_s[None], n_s.reshape(1, bs, ML_HEADS, ML_DH), m_s[:, :, 0, 0][None])
```

```python
import functools
import math

import jax
import jax.numpy as jnp
import numpy as np
from jax import lax
from jax.experimental import pallas as pl
from jax.experimental.pallas import tpu as pltpu

F32 = jnp.float32
BF16 = jnp.bfloat16
EPS = 1e-6

D_MODEL = 2048
D_FF = 4 * D_MODEL
S5_WIDTH = 1024
S5_GROUPS = 64
S5_GROUP = 16
S5_STATE = 64
S5_CH = S5_GROUPS * S5_STATE
S5_NBLK = 4
GLA_HEADS = 4
GLA_HK = 128
GLA_HV = 256
GLA_TAU = 16.0
GLA_CHUNK = 64
ML_HEADS = 4
ML_DH = 512
ML_CHUNK = 256

SUBLANES = 8
LANES = 128
VMEM_LIMIT = 56 * 1024 * 1024


def _cparams(sem):
    return pltpu.CompilerParams(dimension_semantics=sem, vmem_limit_bytes=VMEM_LIMIT)


def _dot(a, b):
    return jnp.dot(a, b, preferred_element_type=F32)


def _dot_nt(a, b):
    return lax.dot_general(a, b, (((1,), (1,)), ((), ())), preferred_element_type=F32)


def _split3(x):
    p1 = x.astype(BF16)
    r1 = x - p1.astype(F32)
    p2 = r1.astype(BF16)
    r2 = r1 - p2.astype(F32)
    p3 = r2.astype(BF16)
    return p1, p2, p3


def _dot_exact_lhs(m, x):
    p1, p2, p3 = _split3(x)
    return _dot(m, p1) + _dot(m, p2) + _dot(m, p3)


def _dot_nt_exact_lhs(m, x):
    p1, p2, p3 = _split3(x)
    return _dot_nt(m, p1) + _dot_nt(m, p2) + _dot_nt(m, p3)


def _log_sigmoid(x):
    return jnp.minimum(x, 0.0) - jnp.log1p(jnp.exp(-jnp.abs(x)))


def _sigmoid(x):
    return 1.0 / (1.0 + jnp.exp(-x))


def _norm_mm_body(*refs, act, n_tail, tm, rchunk):
    has_tail = n_tail > 0
    if has_tail:
        x_ref, g_ref, w_ref, wt_ref, o_ref, ot_ref, xn_ref = refs
    else:
        x_ref, g_ref, w_ref, o_ref, xn_ref = refs

    @pl.when(pl.program_id(1) == 0)
    def _():
        g = g_ref[...]
        for r in range(0, tm, rchunk):
            x = x_ref[r:r + rchunk, :]
            ms = jnp.mean(x * x, axis=-1, keepdims=True)
            xn_ref[r:r + rchunk, :] = (x * lax.rsqrt(ms + EPS) * g).astype(BF16)
        if has_tail:
            col = lax.broadcasted_iota(jnp.int32, wt_ref.shape, 0)
            wt = jnp.where(col < n_tail, wt_ref[...], 0.0)
            ot_ref[...] = _dot_nt(xn_ref[...], wt.astype(BF16))

    mm = _dot_nt if has_tail else _dot
    acc = mm(xn_ref[...], w_ref[...].astype(BF16))
    if act == "relu2":
        acc = jnp.square(jnp.maximum(acc, 0.0))
    o_ref[...] = acc.astype(o_ref.dtype)


def _norm_mm(x, g, w3, layer, n_main, *, act=None, out_dtype=F32, tm=1088, tn=512):
    m, kdim = x.shape
    grid = (m // tm, n_main // tn)
    n_tail = w3.shape[2] - n_main
    in_specs = [pl.BlockSpec((tm, kdim), lambda i, j: (i, 0)),
                pl.BlockSpec((1, kdim), lambda i, j: (0, 0))]
    args = [x, g.reshape(1, kdim)]
    out_shape = [jax.ShapeDtypeStruct((m, n_main), out_dtype)]
    out_specs = [pl.BlockSpec((tm, tn), lambda i, j: (i, j))]
    if n_tail > 0:
        w3t = jnp.transpose(w3, (0, 2, 1))
        in_specs += [pl.BlockSpec((None, tn, kdim), lambda i, j: (layer, j, 0)),
                     pl.BlockSpec((None, LANES, kdim), lambda i, j: (layer, n_main // LANES, 0))]
        args += [w3t, w3t]
        out_shape.append(jax.ShapeDtypeStruct((m, LANES), F32))
        out_specs.append(pl.BlockSpec((tm, LANES), lambda i, j: (i, 0)))
    else:
        in_specs.append(pl.BlockSpec((None, kdim, tn), lambda i, j: (layer, 0, j)))
        args.append(w3)
    body = functools.partial(_norm_mm_body, act=act, n_tail=n_tail, tm=tm, rchunk=tm // 4)
    res = pl.pallas_call(
        body, grid=grid, in_specs=in_specs, out_specs=out_specs, out_shape=out_shape,
        scratch_shapes=[pltpu.VMEM((tm, kdim), BF16)],
        compiler_params=_cparams(("arbitrary", "arbitrary")),
        name="norm_mm",
    )(*args)
    return res if n_tail > 0 else res[0]


def _norm_mm_lookahead(x, g, w3, layer, n_main, *, act=None, out_dtype=F32, tm=1088, tn=1024):
    m, kdim = x.shape
    ni, nj = m // tm, n_main // tn
    rchunk = tm // 4

    def outer(x_hbm, g_hbm, w_hbm, o_hbm, xn_ref, cnt_ref):
        cnt_ref[0] = 0

        def body(x_ref, g_ref, w_ref, o_ref):
            step = cnt_ref[0]
            cnt_ref[0] = step + 1

            @pl.when(step % nj == 0)
            def _():
                gv = g_ref[...]
                for r in range(0, tm, rchunk):
                    xr = x_ref[r:r + rchunk, :]
                    ms = jnp.mean(xr * xr, axis=-1, keepdims=True)
                    xn_ref[r:r + rchunk, :] = (xr * lax.rsqrt(ms + EPS) * gv).astype(BF16)

            acc = _dot(xn_ref[...], w_ref[...].astype(BF16))
            if act == "relu2":
                acc = jnp.square(jnp.maximum(acc, 0.0))
            o_ref[...] = acc.astype(o_ref.dtype)

        pltpu.emit_pipeline(
            body, grid=(ni, nj),
            in_specs=[
                pl.BlockSpec((tm, kdim), lambda i, j: (i, 0),
                             pipeline_mode=pl.Buffered(2, use_lookahead=True)),
                pl.BlockSpec((1, kdim), lambda i, j: (0, 0)),
                pl.BlockSpec((None, kdim, tn), lambda i, j: (layer, 0, j)),
            ],
            out_specs=[pl.BlockSpec((tm, tn), lambda i, j: (i, j))],
        )(x_hbm, g_hbm, w_hbm, o_hbm)

    return pl.pallas_call(
        outer,
        in_specs=[pl.BlockSpec(memory_space=pl.ANY)] * 3,
        out_specs=pl.BlockSpec(memory_space=pl.ANY),
        out_shape=jax.ShapeDtypeStruct((m, n_main), out_dtype),
        scratch_shapes=[pltpu.VMEM((tm, kdim), BF16), pltpu.SMEM((1,), jnp.int32)],
        compiler_params=pltpu.CompilerParams(vmem_limit_bytes=VMEM_LIMIT),
        name="norm_mm_la",
    )(x, g.reshape(1, kdim), w3)


def _mm_res_lookahead(a, w3, layer, res, *, tm=1088, tk=512):
    m, kdim = a.shape
    n = w3.shape[2]
    ni, nk = m // tm, kdim // tk

    def outer(a_hbm, w_hbm, r_hbm, o_hbm, cnt_ref):
        cnt_ref[0] = 0

        def body(a_ref, w_ref, r_ref, o_ref):
            step = cnt_ref[0]
            cnt_ref[0] = step + 1

            @pl.when(step % nk == 0)
            def _():
                o_ref[...] = r_ref[...] + _dot(a_ref[...].astype(BF16), w_ref[...].astype(BF16))

            @pl.when(step % nk != 0)
            def _():
                o_ref[...] = o_ref[...] + _dot(a_ref[...].astype(BF16), w_ref[...].astype(BF16))

        pltpu.emit_pipeline(
            body, grid=(ni, nk),
            in_specs=[
                pl.BlockSpec((tm, tk), lambda i, k: (i, k)),
                pl.BlockSpec((None, tk, n), lambda i, k: (layer, k, 0)),
                pl.BlockSpec((tm, n), lambda i, k: (i, 0),
                             pipeline_mode=pl.Buffered(2, use_lookahead=True)),
            ],
            out_specs=[pl.BlockSpec((tm, n), lambda i, k: (i, 0))],
        )(a_hbm, w_hbm, r_hbm, o_hbm)

    return pl.pallas_call(
        outer,
        in_specs=[pl.BlockSpec(memory_space=pl.ANY)] * 3,
        out_specs=pl.BlockSpec(memory_space=pl.ANY),
        out_shape=jax.ShapeDtypeStruct((m, n), F32),
        scratch_shapes=[pltpu.SMEM((1,), jnp.int32)],
        compiler_params=pltpu.CompilerParams(vmem_limit_bytes=VMEM_LIMIT),
        name="mm_res_la",
    )(a, w3, res)


def _mm_out_body(*refs, kcs, n_ptiles):
    n = len(kcs)
    ap, asm = refs[:n], refs[n:2 * n]
    w_ref, rp_ref, rs_ref, o_ref, wb_ref = refs[2 * n:]
    i = pl.program_id(1)

    @pl.when(i == 0)
    def _():
        wb_ref[...] = w_ref[...].astype(BF16)

    def compute(srcs, r_ref):
        acc = r_ref[...]
        off = 0
        for a_ref, kc in zip(srcs, kcs):
            acc = acc + _dot(a_ref[...], wb_ref[off:off + kc, :])
            off += kc
        o_ref[...] = acc

    @pl.when(i < n_ptiles)
    def _():
        compute(ap, rp_ref)

    @pl.when(i >= n_ptiles)
    def _():
        compute(asm, rs_ref)


def _mm_out(a_prompt, a_sample, w3, layer, res_prompt, res_sample, *, tm=512, tn=1024):
    kcs = tuple(a.shape[1] for a in a_prompt)
    kdim, n = sum(kcs), w3.shape[2]
    n_ptiles = a_prompt[0].shape[0] // tm
    m = a_prompt[0].shape[0] + a_sample[0].shape[0]
    (rp, rp0), (rs, rs0) = res_prompt, res_sample
    pidx = lambda i: jnp.minimum(i, n_ptiles - 1)
    in_specs = ([pl.BlockSpec((tm, kc), lambda j, i: (pidx(i), 0)) for kc in kcs]
                + [pl.BlockSpec((tm, kc), lambda j, i: (0, 0)) for kc in kcs]
                + [pl.BlockSpec((None, kdim, tn), lambda j, i: (layer, 0, j)),
                   pl.BlockSpec((tm, tn), lambda j, i: (rp0 + pidx(i), j)),
                   pl.BlockSpec((tm, tn), lambda j, i: (rs0, j))])
    body = functools.partial(_mm_out_body, kcs=kcs, n_ptiles=n_ptiles)
    return pl.pallas_call(
        body, grid=(n // tn, n_ptiles + 1), in_specs=in_specs,
        out_specs=pl.BlockSpec((tm, tn), lambda j, i: (i, j)),
        out_shape=jax.ShapeDtypeStruct((m, n), F32),
        scratch_shapes=[pltpu.VMEM((kdim, tn), BF16)],
        compiler_params=_cparams(("arbitrary", "arbitrary")),
        name="mm_out",
    )(*a_prompt, *a_sample, w3, rp, rs)


def _rmsnorm_body(x_ref, g_ref, op_ref, os_ref, *, n_ptiles):
    x = x_ref[...]
    ms = jnp.mean(x * x, axis=-1, keepdims=True)
    y = x * lax.rsqrt(ms + EPS) * g_ref[...]

    @pl.when(pl.program_id(0) < n_ptiles)
    def _():
        op_ref[...] = y

    @pl.when(pl.program_id(0) >= n_ptiles)
    def _():
        os_ref[...] = y


def _rmsnorm(x, g, m_prompt, *, tm=512):
    m, d = x.shape
    n_ptiles = m_prompt // tm
    return pl.pallas_call(
        functools.partial(_rmsnorm_body, n_ptiles=n_ptiles), grid=(n_ptiles + 1,),
        in_specs=[pl.BlockSpec((tm, d), lambda i: (i, 0)), pl.BlockSpec((1, d), lambda i: (0, 0))],
        out_specs=[pl.BlockSpec((tm, d), lambda i: (jnp.minimum(i, n_ptiles - 1), 0)),
                   pl.BlockSpec((tm, d), lambda i: (0, 0))],
        out_shape=[jax.ShapeDtypeStruct((m_prompt, d), F32),
                   jax.ShapeDtypeStruct((m - m_prompt, d), F32)],
        compiler_params=_cparams(("arbitrary",)),
        name="final_norm",
    )(x, g.reshape(1, d))


def _s5_consts_body(ar_ref, ai_ref, ls_ref, br_ref, bi_ref, cr_ref, ci_ref,
                    abr_ref, abi_ref, bre_ref, bim_ref, cre_ref, cim_ref):
    g, p, q = S5_GROUPS, S5_STATE, S5_GROUP
    gb = g // S5_NBLK
    gq, wid = g * q, gb * p
    ar = ar_ref[...]
    ai = ai_ref[...]
    dt = jnp.exp(ls_ref[...])
    mag = jnp.exp(ar * dt)
    abr = mag * jnp.cos(ai * dt)
    abi = mag * jnp.sin(ai * dt)
    abr_ref[...] = abr
    abi_ref[...] = abi
    lam2 = ar * ar + ai * ai
    zr = abr - 1.0
    cr = (zr * ar + abi * ai) / lam2
    ci = (abi * ar - zr * ai) / lam2

    rep = jnp.where((lax.broadcasted_iota(jnp.int32, (gq, g), 0) // q)
                    == lax.broadcasted_iota(jnp.int32, (gq, g), 1), 1.0, 0.0).astype(BF16)
    cr_r = _dot_exact_lhs(rep, cr)
    ci_r = _dot_exact_lhs(rep, ci)
    br = br_ref[...]
    bi = bi_ref[...]
    bbr = cr_r * br - ci_r * bi
    bbi = cr_r * bi + ci_r * br

    tile = jnp.where(lax.broadcasted_iota(jnp.int32, (p, wid), 0)
                     == lax.broadcasted_iota(jnp.int32, (p, wid), 1) % p, 1.0, 0.0).astype(BF16)
    keep = ((lax.broadcasted_iota(jnp.int32, (gq, wid), 0) // q) % gb
            == lax.broadcasted_iota(jnp.int32, (gq, wid), 1) // p)

    def spread(t):
        return jnp.where(keep, _dot(t.astype(BF16), tile), 0.0)

    bre_ref[...] = spread(bbr).astype(BF16).reshape(bre_ref.shape)
    bim_ref[...] = spread(bbi).astype(BF16).reshape(bim_ref.shape)
    dre = spread(cr_ref[...])
    dim = spread(ci_ref[...])
    rows = gb * q
    for k in range(S5_NBLK):
        cre_ref[k] = jnp.transpose(dre[k * rows:(k + 1) * rows, :]).astype(BF16)
        cim_ref[k] = jnp.transpose(dim[k * rows:(k + 1) * rows, :]).astype(BF16)


def _s5_consts(a_re, a_im, log_step, b_re, b_im, c_re, c_im):
    g, p, q = S5_GROUPS, S5_STATE, S5_GROUP
    gb = g // S5_NBLK
    brt = jnp.transpose(b_re, (0, 2, 1)).reshape(g * q, p)
    bit = jnp.transpose(b_im, (0, 2, 1)).reshape(g * q, p)
    full = lambda shape: pl.BlockSpec(shape, lambda: (0,) * len(shape))
    ins = [(g, p), (g, p), (g, 1)] + [(g * q, p)] * 4
    outs = [((g, p), F32)] * 2 + [((S5_NBLK, gb * q, gb * p), BF16)] * 2 + [((S5_NBLK, gb * p, gb * q), BF16)] * 2
    abr, abi, bre, bim, cre, cim = pl.pallas_call(
        _s5_consts_body, in_specs=[full(s) for s in ins], out_specs=[full(s) for s, _ in outs],
        out_shape=[jax.ShapeDtypeStruct(s, d) for s, d in outs],
        compiler_params=pltpu.CompilerParams(vmem_limit_bytes=VMEM_LIMIT),
        name="s5_consts",
    )(a_re, a_im, log_step.reshape(g, 1), brt, bit, c_re.reshape(g * q, p), c_im.reshape(g * q, p))
    return abr.reshape(1, S5_CH), abi.reshape(1, S5_CH), bre, bim, cre, cim


def _s5_perm(nb, tl):
    r = np.arange(nb * tl)
    p = np.zeros((nb * tl, nb * tl), np.float32)
    p[r, (r % nb) * tl + r // nb] = 1.0
    return p


def _s5_body(*refs, nsrc, nb, has_init, permute):
    u_refs, refs = refs[:nsrc], refs[nsrc:]
    if permute:
        pm_ref, pmt_ref = refs[:2]
        refs = refs[2:]
    bre_ref, bim_ref, cre_ref, cim_ref, ar_ref, ai_ref, d_ref, gw_ref, gb_ref = refs[:9]
    refs = refs[9:]
    if has_init:
        h0r_ref, h0i_ref, y_ref, hr_ref, hi_ref, sre, sim, ysc = refs
    else:
        y_ref, hr_ref, hi_ref, sre, sim, ysc = refs
    rows = sre.shape[0]
    tl = rows // nb
    cb = S5_CH // S5_NBLK
    ub = S5_WIDTH // S5_NBLK
    hw = cb // 2

    @pl.when(pl.program_id(0) == 0)
    def _():
        if has_init:
            hr_ref[...] = h0r_ref[...]
            hi_ref[...] = h0i_ref[...]
        else:
            hr_ref[...] = jnp.zeros_like(hr_ref)
            hi_ref[...] = jnp.zeros_like(hi_ref)

    u_b = u_refs[0][...] if nsrc == 1 else jnp.concatenate([r[...] for r in u_refs], axis=0)
    if permute:
        u_tm = _dot(pm_ref[...], u_b)
        u_b = u_tm.astype(BF16)
    else:
        u_tm = u_b.astype(F32)
    low_rows = lax.broadcasted_iota(jnp.int32, (SUBLANES, hw), 0) < nb

    for k in range(S5_NBLK):
        uk = u_b[:, k * ub:(k + 1) * ub]
        sre[...] = _dot(uk, bre_ref[k])
        sim[...] = _dot(uk, bim_ref[k])
        for half in range(2):
            c0 = k * cb + half * hw
            l0 = half * hw
            ar8 = jnp.broadcast_to(ar_ref[:, c0:c0 + hw], (SUBLANES, hw))
            ai8 = jnp.broadcast_to(ai_ref[:, c0:c0 + hw], (SUBLANES, hw))

            if nb < SUBLANES:
                def j_body(j, c, l0=l0, ar8=ar8, ai8=ai8):
                    sr, si = c
                    row = pl.multiple_of(j * SUBLANES, SUBLANES)
                    dr = sre[pl.ds(row, SUBLANES), l0:l0 + hw]
                    di = sim[pl.ds(row, SUBLANES), l0:l0 + hw]
                    pr = pltpu.roll(sr, nb, 0)
                    pi = pltpu.roll(si, nb, 0)
                    yr = ar8 * pr - ai8 * pi + dr
                    yi = ar8 * pi + ai8 * pr + di
                    qr = pltpu.roll(yr, nb, 0)
                    qi = pltpu.roll(yi, nb, 0)
                    zr = ar8 * qr - ai8 * qi + dr
                    zi = ar8 * qi + ai8 * qr + di
                    sre[pl.ds(row, SUBLANES), l0:l0 + hw] = jnp.where(low_rows, yr, zr)
                    sim[pl.ds(row, SUBLANES), l0:l0 + hw] = jnp.where(low_rows, yi, zi)
                    return zr, zi

                sr, si = lax.fori_loop(0, rows // SUBLANES, j_body,
                                       (hr_ref[:, c0:c0 + hw], hi_ref[:, c0:c0 + hw]), unroll=True)
                hr_ref[:, c0:c0 + hw] = sr
                hi_ref[:, c0:c0 + hw] = si
            else:
                def rg_body(rg, carry, c0=c0, l0=l0, ar8=ar8, ai8=ai8):
                    r0 = pl.multiple_of(rg * SUBLANES, SUBLANES)
                    xr = hr_ref[pl.ds(r0, SUBLANES), c0:c0 + hw]
                    xi = hi_ref[pl.ds(r0, SUBLANES), c0:c0 + hw]
                    for t in range(tl):
                        row = pl.multiple_of(t * nb + r0, SUBLANES)
                        nr = ar8 * xr - ai8 * xi + sre[pl.ds(row, SUBLANES), l0:l0 + hw]
                        ni = ar8 * xi + ai8 * xr + sim[pl.ds(row, SUBLANES), l0:l0 + hw]
                        sre[pl.ds(row, SUBLANES), l0:l0 + hw] = nr
                        sim[pl.ds(row, SUBLANES), l0:l0 + hw] = ni
                        xr, xi = nr, ni
                    hr_ref[pl.ds(r0, SUBLANES), c0:c0 + hw] = xr
                    hi_ref[pl.ds(r0, SUBLANES), c0:c0 + hw] = xi
                    return carry

                lax.fori_loop(0, nb // SUBLANES, rg_body, 0)
        ysc[:, k * ub:(k + 1) * ub] = (_dot(sre[...].astype(BF16), cre_ref[k])
                                       - _dot(sim[...].astype(BF16), cim_ref[k]))

    y = ysc[...] + d_ref[...] * u_tm
    y = 0.5 * y * (1.0 + jnp.tanh(math.sqrt(2.0 / math.pi) * (y + 0.044715 * (y * y * y))))
    z = _dot(y.astype(BF16), gw_ref[...]) + gb_ref[...]
    out = (y * _sigmoid(z)).astype(BF16)
    if permute:
        out = _dot(pmt_ref[...], out).astype(BF16)
    y_ref[...] = out.reshape(y_ref.shape)


def _s5_call(src, bre, bim, cre, cim, abar_re, abar_im, d_skip, glu_w, glu_b, h0, *,
             row0, nsrc, rs, seq_len, nb, tl, permute):
    rows = nsrc * rs
    ntiles = seq_len // rs if nsrc > 1 else 1
    has_init = h0 is not None
    srows = max(nb, SUBLANES)
    const = lambda shape: pl.BlockSpec(shape, lambda t: (0,) * len(shape))
    in_specs = [pl.BlockSpec((rs, S5_WIDTH), lambda t, s=s: ((row0 + s * seq_len) // rs + t, 0))
                for s in range(nsrc)]
    args = [src] * nsrc
    if permute:
        pm = _s5_perm(nb, tl)
        in_specs += [const((rows, rows)), const((rows, rows))]
        args += [jnp.asarray(pm, BF16), jnp.asarray(pm.T, BF16)]
    in_specs += [
        const(bre.shape), const(bim.shape), const(cre.shape), const(cim.shape),
        const((1, S5_CH)), const((1, S5_CH)), const((1, S5_WIDTH)),
        const((S5_WIDTH, S5_WIDTH)), const((1, S5_WIDTH)),
    ]
    args += [bre, bim, cre, cim, abar_re, abar_im, d_skip, glu_w, glu_b]
    if has_init:
        in_specs += [const((srows, S5_CH)), const((srows, S5_CH))]
        args += [h0[0], h0[1]]
    body = functools.partial(_s5_body, nsrc=nsrc, nb=nb, has_init=has_init, permute=permute)
    return pl.pallas_call(
        body, grid=(ntiles,), in_specs=in_specs,
        out_specs=[pl.BlockSpec((nsrc, rs, S5_WIDTH), lambda t: (0, t, 0)),
                   const((srows, S5_CH)), const((srows, S5_CH))],
        out_shape=[jax.ShapeDtypeStruct((nsrc, ntiles * rs, S5_WIDTH), BF16),
                   jax.ShapeDtypeStruct((srows, S5_CH), F32),
                   jax.ShapeDtypeStruct((srows, S5_CH), F32)],
        scratch_shapes=[pltpu.VMEM((rows, S5_CH // S5_NBLK), F32),
                        pltpu.VMEM((rows, S5_CH // S5_NBLK), F32),
                        pltpu.VMEM((rows, S5_WIDTH), F32)],
        compiler_params=_cparams(("arbitrary",)),
        name="s5_mixer",
    )(*args)


def _gla_body(*refs, nbb, nch, cs, has_init, hps):
    if has_init:
        (q_ref, k_ref, v_ref, g_ref, gl_ref, up_ref, gb_ref, nw_ref, s0_ref, y_ref, s_ref) = refs
    else:
        (q_ref, k_ref, v_ref, g_ref, gl_ref, up_ref, gb_ref, nw_ref, y_ref, s_ref) = refs
    rows = nbb * nch * cs
    sh = int(math.log2(cs))

    @pl.when(pl.program_id(2) == 0)
    def _():
        if has_init:
            s_ref[...] = s0_ref[...]
        else:
            s_ref[...] = jnp.zeros_like(s_ref)

    ri = lax.broadcasted_iota(jnp.int32, (rows, rows), 0)
    ci = lax.broadcasted_iota(jnp.int32, (rows, rows), 1)
    same = (ri >> sh) == (ci >> sh)
    causal = jnp.logical_and(same, ci <= ri)
    lmat = jnp.where(causal, 1.0, 0.0).astype(BF16)
    tmat = jnp.where(same, 1.0, 0.0).astype(BF16)
    rchunk = lax.broadcasted_iota(jnp.int32, (rows, 1), 0) >> sh
    cchunk = lax.broadcasted_iota(jnp.int32, (1, rows), 1) >> sh
    single = nbb * nch == 1
    z = _dot(gl_ref[...].astype(BF16), up_ref[...].astype(BF16)) + gb_ref[...]
    la = _log_sigmoid(z) * (1.0 / GLA_TAU)
    p1, p2, p3 = _split3(la)
    bcum_all = _dot(lmat, p1) + _dot(lmat, p2) + _dot(lmat, p3)
    tot_all = _dot(tmat, p1) + _dot(tmat, p2) + _dot(tmat, p3)

    for hh in range(hps):
        ks = slice(hh * GLA_HK, (hh + 1) * GLA_HK)
        vs = slice(hh * GLA_HV, (hh + 1) * GLA_HV)
        q = q_ref[:, ks].astype(F32) * (GLA_HK ** -0.5)
        k = k_ref[:, ks].astype(F32)
        vb = v_ref[:, vs]
        bcum = bcum_all[:, ks]
        tot = tot_all[:, ks]

        qd = q * jnp.exp(bcum)
        kd = k * jnp.exp(-bcum)
        kdec = k * jnp.exp(tot - bcum)
        qdb = qd.astype(BF16)
        att = jnp.where(causal, _dot_nt(qdb, kd.astype(BF16)), 0.0)
        o = _dot(att.astype(BF16), vb)

        kdec_t = jnp.transpose(kdec)
        dec_t = jnp.transpose(jnp.exp(tot))
        for b in range(nbb):
            s = s_ref[b, hh]
            for c in range(nch):
                idx = b * nch + c
                oi = _dot(qdb, s.astype(BF16))
                o = o + (oi if single else jnp.where(rchunk == idx, oi, 0.0))
                kt = kdec_t if single else jnp.where(cchunk == idx, kdec_t, 0.0)
                s = dec_t[:, idx * cs:idx * cs + 1] * s + _dot(kt.astype(BF16), vb)
            s_ref[b, hh] = s

        o = o * lax.rsqrt(jnp.mean(o * o, axis=-1, keepdims=True) + EPS) * nw_ref[...]
        gg = g_ref[:, vs].astype(F32)
        y_ref[:, vs] = (o * (gg * _sigmoid(gg))).astype(y_ref.dtype)


def _gla_call(projm, projt, up_pad, gkb, nw, s0, *, row0, nseq_blocks, ntiles, nbb, nch, cs, batch, hps):
    rows = nbb * nch * cs
    has_init = s0 is not None
    rb0 = row0 // rows
    rowblk = lambda s, h, t: rb0 + s * ntiles + t
    wk, wv = hps * GLA_HK, hps * GLA_HV
    qoff = S5_WIDTH // wk
    koff = qoff + GLA_HEADS // hps
    voff = (S5_WIDTH + 2 * GLA_HEADS * GLA_HK) // wv
    goff = voff + GLA_HEADS // hps
    in_specs = [
        pl.BlockSpec((rows, wk), lambda s, h, t: (rowblk(s, h, t), qoff + h)),
        pl.BlockSpec((rows, wk), lambda s, h, t: (rowblk(s, h, t), koff + h)),
        pl.BlockSpec((rows, wv), lambda s, h, t: (rowblk(s, h, t), voff + h)),
        pl.BlockSpec((rows, wv), lambda s, h, t: (rowblk(s, h, t), goff + h)),
        pl.BlockSpec((rows, LANES), lambda s, h, t: (rowblk(s, h, t), 0)),
        pl.BlockSpec((LANES, wk), lambda s, h, t: (0, h)),
        pl.BlockSpec((1, wk), lambda s, h, t: (0, h)),
        pl.BlockSpec((1, GLA_HV), lambda s, h, t: (0, 0)),
    ]
    args = [projm, projm, projm, projm, projt, up_pad, gkb, nw]
    if has_init:
        in_specs.append(pl.BlockSpec((nbb, hps, GLA_HK, GLA_HV), lambda s, h, t: (s, h, 0, 0)))
        args.append(s0)
    body = functools.partial(_gla_body, nbb=nbb, nch=nch, cs=cs, has_init=has_init, hps=hps)
    return pl.pallas_call(
        body, grid=(nseq_blocks, GLA_HEADS // hps, ntiles), in_specs=in_specs,
        out_specs=[pl.BlockSpec((rows, wv), lambda s, h, t: (s * ntiles + t, h)),
                   pl.BlockSpec((nbb, hps, GLA_HK, GLA_HV), lambda s, h, t: (s, h, 0, 0))],
        out_shape=[jax.ShapeDtypeStruct((nseq_blocks * ntiles * rows, GLA_HEADS * GLA_HV), BF16),
                   jax.ShapeDtypeStruct((batch, GLA_HEADS, GLA_HK, GLA_HV), F32)],
        compiler_params=_cparams(("arbitrary", "arbitrary", "arbitrary")),
        name="gla_mixer",
    )(*args)


def _mlstm_body(*refs, nbb, cl, rp, has_init, hps):
    if has_init:
        (q_ref, k_ref, v_ref, og_ref, gt_ref, gb_ref, nw_ref, c0_ref, n0_ref, m0_ref,
         y_ref, c_ref, n_ref, m_ref) = refs
    else:
        (q_ref, k_ref, v_ref, og_ref, gt_ref, gb_ref, nw_ref, y_ref, c_ref, n_ref, m_ref) = refs
    rows = nbb * cl
    sh = int(math.log2(cl))
    head0 = pl.program_id(1) * hps

    @pl.when(pl.program_id(2) == 0)
    def _():
        if has_init:
            c_ref[...] = c0_ref[...]
            n_ref[...] = n0_ref[...]
            m_ref[...] = m0_ref[...]
        else:
            c_ref[...] = jnp.zeros_like(c_ref)
            n_ref[...] = jnp.zeros_like(n_ref)
            m_ref[...] = jnp.full(m_ref.shape, -jnp.inf, F32)

    def pad(x):
        if rp == rows:
            return x
        return jnp.concatenate([x, jnp.zeros((rp - rows, x.shape[1]), x.dtype)], axis=0)

    ksc = ML_DH ** -0.5
    gates = pad(gt_ref[...] + gb_ref[...])
    lane = lax.broadcasted_iota(jnp.int32, (rp, LANES), 1)
    ri = lax.broadcasted_iota(jnp.int32, (rp, rp), 0)
    ci = lax.broadcasted_iota(jnp.int32, (rp, rp), 1)
    causal = jnp.logical_and((ri >> sh) == (ci >> sh), ci <= ri)
    lmat = jnp.where(causal, 1.0, 0.0).astype(BF16)
    neg = jnp.where(causal, 0.0, -jnp.inf)
    er = lax.broadcasted_iota(jnp.int32, (SUBLANES, LANES), 0)
    ec = lax.broadcasted_iota(jnp.int32, (SUBLANES, LANES), 1)
    pick = jnp.where(er == ec, 1.0, 0.0).astype(BF16)
    rbatch = lax.broadcasted_iota(jnp.int32, (rp, 1), 0) >> sh
    cbatch = lax.broadcasted_iota(jnp.int32, (1, rp), 1) >> sh
    fc_all = _dot_exact_lhs(lmat, _log_sigmoid(gates))
    rows_all = _dot_nt_exact_lhs(pick, jnp.where(lane < ML_HEADS, gates, fc_all))
    sub = lax.broadcasted_iota(jnp.int32, (SUBLANES, rp), 0)

    for hh in range(hps):
        head = head0 + hh
        hs = slice(hh * ML_DH, (hh + 1) * ML_DH)
        qb = pad(q_ref[:, hs])
        kb = pad(k_ref[:, hs])
        vb = pad(v_ref[:, hs])
        og = pad(og_ref[:, hs]).astype(F32)
        ig = jnp.sum(jnp.where(lane == head, gates, 0.0), axis=1, keepdims=True)
        fcum = jnp.sum(jnp.where(lane == head + ML_HEADS, fc_all, 0.0), axis=1, keepdims=True)
        ig_row = jnp.sum(jnp.where(sub == head, rows_all, 0.0), axis=0, keepdims=True)
        fcum_row = jnp.sum(jnp.where(sub == head + ML_HEADS, rows_all, 0.0), axis=0, keepdims=True)

        mm = jnp.zeros((rp, 1), F32)
        for b in range(nbb):
            mb = m_ref[b, hh][:, 0:1]
            mm = jnp.where(rbatch == b, mb, mm) if nbb > 1 else jnp.broadcast_to(mb, (rp, 1))

        dmat = fcum - fcum_row + ig_row + neg
        dprev = fcum + mm
        m = jnp.maximum(jnp.max(dmat, axis=1, keepdims=True), dprev)
        w = jnp.exp(dmat - m + math.log(ksc))
        wp = jnp.exp(dprev - m)
        sc = _dot_nt(qb, kb) * w
        num = _dot(sc.astype(BF16), vb)
        den = jnp.sum(sc, axis=1, keepdims=True)

        k_t = jnp.transpose(kb.astype(F32))
        for b in range(nbb):
            cm = c_ref[b, hh]
            nm = n_ref[b, hh]
            mb = m_ref[b, hh][:, 0:1]
            qc = wp * _dot(qb, cm.astype(BF16))
            nm8 = jnp.broadcast_to(nm, (SUBLANES, ML_DH)).astype(BF16)
            qn = wp * _dot_nt(qb, nm8)[:, 0:1]
            if nbb > 1:
                rsel = rbatch == b
                qc = jnp.where(rsel, qc, 0.0)
                qn = jnp.where(rsel, qn, 0.0)
            num = num + qc
            den = den + qn
            last = b * cl + cl - 1
            m_new = m[last:last + 1, :]
            fl = fcum[last:last + 1, :]
            decay = jnp.exp(fl + mb - m_new)
            wk_row = ksc * jnp.exp(fl - fcum_row + ig_row - m_new)
            if nbb > 1:
                wk_row = jnp.where(cbatch == b, wk_row, 0.0)
            c_ref[b, hh] = decay * cm + _dot((k_t * wk_row).astype(BF16), vb)
            wk8 = jnp.broadcast_to(wk_row, (SUBLANES, rp)).astype(BF16)
            n_ref[b, hh] = decay * nm + _dot(wk8, kb)[0:1, :]
            m_ref[b, hh] = jnp.broadcast_to(m_new, (1, LANES))

        hv = num / jnp.maximum(jnp.abs(den), jnp.exp(-m))
        hv = hv * _sigmoid(og)
        hv = hv * lax.rsqrt(jnp.mean(hv * hv, axis=-1, keepdims=True) + EPS) * nw_ref[...]
        y_ref[:, hs] = hv[:rows].astype(y_ref.dtype)


def _mlstm_call(projm, projt, gate_b, nw, init, *, row0, nseq_blocks, ntiles, nbb, cl, rp, batch, hps):
    rows = nbb * cl
    has_init = init is not None
    rb0 = row0 // rows
    rowblk = lambda s, h, t: rb0 + s * ntiles + t
    hd = ML_HEADS // hps
    wd = hps * ML_DH
    in_specs = [
        pl.BlockSpec((rows, wd), lambda s, h, t: (rowblk(s, h, t), h)),
        pl.BlockSpec((rows, wd), lambda s, h, t: (rowblk(s, h, t), hd + h)),
        pl.BlockSpec((rows, wd), lambda s, h, t: (rowblk(s, h, t), 2 * hd + h)),
        pl.BlockSpec((rows, wd), lambda s, h, t: (rowblk(s, h, t), 3 * hd + h)),
        pl.BlockSpec((rows, LANES), lambda s, h, t: (rowblk(s, h, t), 0)),
        pl.BlockSpec((1, LANES), lambda s, h, t: (0, 0)),
        pl.BlockSpec((1, ML_DH), lambda s, h, t: (0, 0)),
    ]
    args = [projm, projm, projm, projm, projt, gate_b, nw]
    c_spec = pl.BlockSpec((nbb, hps, ML_DH, ML_DH), lambda s, h, t: (s, h, 0, 0))
    n_spec = pl.BlockSpec((nbb, hps, 1, ML_DH), lambda s, h, t: (s, h, 0, 0))
    m_spec = pl.BlockSpec((nbb, hps, 1, LANES), lambda s, h, t: (s, h, 0, 0))
    if has_init:
        in_specs += [c_spec, n_spec, m_spec]
        args += list(init)
    body = functools.partial(_mlstm_body, nbb=nbb, cl=cl, rp=rp, has_init=has_init, hps=hps)
    return pl.pallas_call(
        body, grid=(nseq_blocks, hd, ntiles), in_specs=in_specs,
        out_specs=[pl.BlockSpec((rows, wd), lambda s, h, t: (s * ntiles + t, h)),
                   c_spec, n_spec, m_spec],
        out_shape=[jax.ShapeDtypeStruct((nseq_blocks * ntiles * rows, ML_HEADS * ML_DH), BF16),
                   jax.ShapeDtypeStruct((batch, ML_HEADS, ML_DH, ML_DH), F32),
                   jax.ShapeDtypeStruct((batch, ML_HEADS, 1, ML_DH), F32),
                   jax.ShapeDtypeStruct((batch, ML_HEADS, 1, LANES), F32)],
        compiler_params=_cparams(("arbitrary", "arbitrary", "arbitrary")),
        name="mlstm_mixer",
    )(*args)


def kernel(x_prompt, x_sample, state_s5_re, state_s5_im, state_gla, state_mlstm_c, state_mlstm_n,
           state_mlstm_m, norm_mix, norm_mlp, norm_final, w_in_even, s5_a_re, s5_a_im, s5_log_step,
           s5_b_re, s5_b_im, s5_c_re, s5_c_im, s5_d, s5_glu_w, s5_glu_b, gla_gk_up, gla_gk_b,
           gla_norm, w_out_even, w_in_odd, mlstm_b_i, mlstm_b_f, mlstm_norm, w_out_odd,
           w_mlp_up, w_mlp_down):
    bp, lp, d = x_prompt.shape
    bs, ls, _ = x_sample.shape
    mp = bp * lp
    msamp = bs * ls
    xp2 = x_prompt.reshape(mp, d)
    xs2 = x_sample.reshape(msamp, d)

    n_even = S5_WIDTH + 2 * GLA_HEADS * GLA_HK + 2 * GLA_HEADS * GLA_HV
    projm, projt = _norm_mm(xp2, norm_mix[0], w_in_even, 0, n_even, tm=1024, tn=1024, out_dtype=BF16)
    projm_s, projt_s = _norm_mm(xs2, norm_mix[0], w_in_even, 0, n_even, tm=msamp, tn=1024, out_dtype=BF16)

    abar_re, abar_im, bre, bim, cre, cim = _s5_consts(s5_a_re[0], s5_a_im[0], s5_log_step[0], s5_b_re[0],
                                                      s5_b_im[0], s5_c_re[0], s5_c_im[0])
    s5_consts = (bre, bim, cre, cim, abar_re, abar_im, s5_d[0].reshape(1, S5_WIDTH),
                 s5_glu_w[0].astype(BF16), s5_glu_b[0].reshape(1, S5_WIDTH))

    s5_tl = 64
    ys5_p, hr_p, hi_p = _s5_call(projm, *s5_consts, None, row0=0, nsrc=bp, rs=s5_tl, seq_len=lp,
                                 nb=bp, tl=s5_tl, permute=True)
    ys5_p = ys5_p.reshape(mp, S5_WIDTH)
    hr_p = hr_p[SUBLANES - bp:]
    hi_p = hi_p[SUBLANES - bp:]
    u_s = jnp.transpose(projm_s[:, :S5_WIDTH].reshape(bs, ls, S5_WIDTH), (1, 0, 2))
    h0 = (state_s5_re[0].reshape(bs, S5_CH), state_s5_im[0].reshape(bs, S5_CH))
    ys5_s, hr_s, hi_s = _s5_call(u_s.reshape(msamp, S5_WIDTH), *s5_consts, h0, row0=0, nsrc=1, rs=msamp,
                                 seq_len=msamp, nb=bs, tl=ls, permute=False)
    ys5_s = jnp.transpose(ys5_s.reshape(ls, bs, S5_WIDTH), (1, 0, 2)).reshape(msamp, S5_WIDTH)

    up_pad = jnp.pad(gla_gk_up[0], ((0, LANES - gla_gk_up.shape[1]), (0, 0)))
    gkb = gla_gk_b[0].reshape(1, -1)
    gnw = gla_norm[0].reshape(1, GLA_HV)
    ptile = 256
    ygla_p, sg_p = _gla_call(projm, projt, up_pad, gkb, gnw, None, row0=0, nseq_blocks=bp,
                             ntiles=lp // ptile, nbb=1, nch=ptile // GLA_CHUNK, cs=GLA_CHUNK, batch=bp,
                             hps=GLA_HEADS)
    gnbb = 32
    ygla_s, sg_s = _gla_call(projm_s, projt_s, up_pad, gkb, gnw, state_gla[0], row0=0,
                             nseq_blocks=bs // gnbb, ntiles=1, nbb=gnbb, nch=1, cs=ls, batch=bs, hps=1)

    h = _mm_out([ys5_p, ygla_p], [ys5_s, ygla_s], w_out_even, 0, (xp2, 0), (xs2, 0))
    hid = _norm_mm_lookahead(h, norm_mlp[0], w_mlp_up, 0, D_FF, act="relu2", out_dtype=BF16)
    h = _mm_res_lookahead(hid, w_mlp_down, 0, h)

    n_odd = 4 * ML_HEADS * ML_DH
    projm, projt = _norm_mm(h, norm_mix[1], w_in_odd, 0, n_odd, out_dtype=BF16, tn=1024)
    gate_b = jnp.pad(jnp.concatenate([mlstm_b_i[0], mlstm_b_f[0]]), (0, LANES - 2 * ML_HEADS)).reshape(1, LANES)
    mnw = mlstm_norm[0].reshape(1, ML_DH)
    yml_p, c_p, n_p, m_p = _mlstm_call(projm, projt, gate_b, mnw, None, row0=0, nseq_blocks=bp,
                                       ntiles=lp // ML_CHUNK, nbb=1, cl=ML_CHUNK, rp=ML_CHUNK, batch=bp,
                                       hps=ML_HEADS)
    mnbb = 8
    init = (state_mlstm_c[0], state_mlstm_n[0].reshape(bs, ML_HEADS, 1, ML_DH),
            jnp.broadcast_to(state_mlstm_m[0][:, :, None, None], (bs, ML_HEADS, 1, LANES)))
    yml_s, c_s, n_s, m_s = _mlstm_call(projm, projt, gate_b, mnw, init, row0=mp, nseq_blocks=bs // mnbb,
                                       ntiles=1, nbb=mnbb, cl=ls, rp=LANES, batch=bs, hps=1)
    h = _mm_out([yml_p], [yml_s], w_out_odd, 0, (h, 0), (h, mp // 512))
    hid = _norm_mm_lookahead(h, norm_mlp[1], w_mlp_up, 1, D_FF, act="relu2", out_dtype=BF16)
    h = _mm_res_lookahead(hid, w_mlp_down, 1, h)

    y_p, y_s = _rmsnorm(h, norm_final, mp)
    g, p = S5_GROUPS, S5_STATE
    return (y_p.reshape(bp, lp, d), y_s.reshape(bs, ls, d),
            hr_p.reshape(1, bp, g, p), hi_p.reshape(1, bp, g, p), sg_p[None],
            c_p[None], n_p.reshape(1, bp, ML_HEADS, ML_DH), m_p[:, :, 0, 0][None],
            hr_s.reshape(1, bs, g, p), hi_s.reshape(1, bs, g, p), sg_s[None],
            c_s[None], n_s.reshape(1, bs, ML_HEADS, ML_DH), m_s[:, :, 0, 0][None])
```

```python
import functools
import math

import jax
import jax.numpy as jnp
import numpy as np
from jax import lax
from jax.experimental import pallas as pl
from jax.experimental.pallas import tpu as pltpu

F32 = jnp.float32
BF16 = jnp.bfloat16
EPS = 1e-6

D_MODEL = 2048
D_FF = 4 * D_MODEL
S5_WIDTH = 1024
S5_GROUPS = 64
S5_GROUP = 16
S5_STATE = 64
S5_CH = S5_GROUPS * S5_STATE
S5_NBLK = 4
GLA_HEADS = 4
GLA_HK = 128
GLA_HV = 256
GLA_TAU = 16.0
GLA_CHUNK = 64
ML_HEADS = 4
ML_DH = 512
ML_CHUNK = 256

SUBLANES = 8
LANES = 128
VMEM_LIMIT = 56 * 1024 * 1024


def _cparams(sem):
    return pltpu.CompilerParams(dimension_semantics=sem, vmem_limit_bytes=VMEM_LIMIT)


def _dot(a, b):
    return jnp.dot(a, b, preferred_element_type=F32)


def _dot_nt(a, b):
    return lax.dot_general(a, b, (((1,), (1,)), ((), ())), preferred_element_type=F32)


def _split3(x):
    p1 = x.astype(BF16)
    r1 = x - p1.astype(F32)
    p2 = r1.astype(BF16)
    r2 = r1 - p2.astype(F32)
    p3 = r2.astype(BF16)
    return p1, p2, p3


def _dot_exact_lhs(m, x):
    p1, p2, p3 = _split3(x)
    return _dot(m, p1) + _dot(m, p2) + _dot(m, p3)


def _dot_nt_exact_lhs(m, x):
    p1, p2, p3 = _split3(x)
    return _dot_nt(m, p1) + _dot_nt(m, p2) + _dot_nt(m, p3)


def _log_sigmoid(x):
    return jnp.minimum(x, 0.0) - jnp.log1p(jnp.exp(-jnp.abs(x)))


def _sigmoid(x):
    return 1.0 / (1.0 + jnp.exp(-x))


def _norm_mm_body(*refs, act, n_tail, tm, rchunk):
    has_tail = n_tail > 0
    if has_tail:
        x_ref, g_ref, w_ref, wt_ref, o_ref, ot_ref, xn_ref = refs
    else:
        x_ref, g_ref, w_ref, o_ref, xn_ref = refs

    @pl.when(pl.program_id(1) == 0)
    def _():
        g = g_ref[...]
        for r in range(0, tm, rchunk):
            x = x_ref[r:r + rchunk, :]
            ms = jnp.mean(x * x, axis=-1, keepdims=True)
            xn_ref[r:r + rchunk, :] = (x * lax.rsqrt(ms + EPS) * g).astype(BF16)
        if has_tail:
            col = lax.broadcasted_iota(jnp.int32, wt_ref.shape, 0)
            wt = jnp.where(col < n_tail, wt_ref[...], 0.0)
            ot_ref[...] = _dot_nt(xn_ref[...], wt.astype(BF16))

    mm = _dot_nt if has_tail else _dot
    acc = mm(xn_ref[...], w_ref[...].astype(BF16))
    if act == "relu2":
        acc = jnp.square(jnp.maximum(acc, 0.0))
    o_ref[...] = acc.astype(o_ref.dtype)


def _norm_mm(x, g, w3, layer, n_main, *, act=None, out_dtype=F32, tm=1088, tn=512):
    m, kdim = x.shape
    grid = (m // tm, n_main // tn)
    n_tail = w3.shape[2] - n_main
    in_specs = [pl.BlockSpec((tm, kdim), lambda i, j: (i, 0)),
                pl.BlockSpec((1, kdim), lambda i, j: (0, 0))]
    args = [x, g.reshape(1, kdim)]
    out_shape = [jax.ShapeDtypeStruct((m, n_main), out_dtype)]
    out_specs = [pl.BlockSpec((tm, tn), lambda i, j: (i, j))]
    if n_tail > 0:
        w3t = jnp.transpose(w3, (0, 2, 1))
        in_specs += [pl.BlockSpec((None, tn, kdim), lambda i, j: (layer, j, 0)),
                     pl.BlockSpec((None, LANES, kdim), lambda i, j: (layer, n_main // LANES, 0))]
        args += [w3t, w3t]
        out_shape.append(jax.ShapeDtypeStruct((m, LANES), F32))
        out_specs.append(pl.BlockSpec((tm, LANES), lambda i, j: (i, 0)))
    else:
        in_specs.append(pl.BlockSpec((None, kdim, tn), lambda i, j: (layer, 0, j)))
        args.append(w3)
    body = functools.partial(_norm_mm_body, act=act, n_tail=n_tail, tm=tm, rchunk=tm // 4)
    res = pl.pallas_call(
        body, grid=grid, in_specs=in_specs, out_specs=out_specs, out_shape=out_shape,
        scratch_shapes=[pltpu.VMEM((tm, kdim), BF16)],
        compiler_params=_cparams(("arbitrary", "arbitrary")),
        name="norm_mm",
    )(*args)
    return res if n_tail > 0 else res[0]


def _norm_mm_lookahead(x, g, w3, layer, n_main, *, act=None, out_dtype=F32, tm=1088, tn=1024):
    m, kdim = x.shape
    ni, nj = m // tm, n_main // tn
    rchunk = tm // 4

    def outer(x_hbm, g_hbm, w_hbm, o_hbm, xn_ref, cnt_ref):
        cnt_ref[0] = 0

        def body(x_ref, g_ref, w_ref, o_ref):
            step = cnt_ref[0]
            cnt_ref[0] = step + 1

            @pl.when(step % nj == 0)
            def _():
                gv = g_ref[...]
                for r in range(0, tm, rchunk):
                    xr = x_ref[r:r + rchunk, :]
                    ms = jnp.mean(xr * xr, axis=-1, keepdims=True)
                    xn_ref[r:r + rchunk, :] = (xr * lax.rsqrt(ms + EPS) * gv).astype(BF16)

            acc = _dot(xn_ref[...], w_ref[...].astype(BF16))
            if act == "relu2":
                acc = jnp.square(jnp.maximum(acc, 0.0))
            o_ref[...] = acc.astype(o_ref.dtype)

        pltpu.emit_pipeline(
            body, grid=(ni, nj),
            in_specs=[
                pl.BlockSpec((tm, kdim), lambda i, j: (i, 0),
                             pipeline_mode=pl.Buffered(2, use_lookahead=True)),
                pl.BlockSpec((1, kdim), lambda i, j: (0, 0)),
                pl.BlockSpec((None, kdim, tn), lambda i, j: (layer, 0, j)),
            ],
            out_specs=[pl.BlockSpec((tm, tn), lambda i, j: (i, j))],
        )(x_hbm, g_hbm, w_hbm, o_hbm)

    return pl.pallas_call(
        outer,
        in_specs=[pl.BlockSpec(memory_space=pl.ANY)] * 3,
        out_specs=pl.BlockSpec(memory_space=pl.ANY),
        out_shape=jax.ShapeDtypeStruct((m, n_main), out_dtype),
        scratch_shapes=[pltpu.VMEM((tm, kdim), BF16), pltpu.SMEM((1,), jnp.int32)],
        compiler_params=pltpu.CompilerParams(vmem_limit_bytes=VMEM_LIMIT),
        name="norm_mm_la",
    )(x, g.reshape(1, kdim), w3)


def _mm_res_lookahead(a, w3, layer, res, *, tm=1088, tk=512):
    m, kdim = a.shape
    n = w3.shape[2]
    ni, nk = m // tm, kdim // tk

    def outer(a_hbm, w_hbm, r_hbm, o_hbm, cnt_ref):
        cnt_ref[0] = 0

        def body(a_ref, w_ref, r_ref, o_ref):
            step = cnt_ref[0]
            cnt_ref[0] = step + 1

            @pl.when(step % nk == 0)
            def _():
                o_ref[...] = r_ref[...] + _dot(a_ref[...].astype(BF16), w_ref[...].astype(BF16))

            @pl.when(step % nk != 0)
            def _():
                o_ref[...] = o_ref[...] + _dot(a_ref[...].astype(BF16), w_ref[...].astype(BF16))

        pltpu.emit_pipeline(
            body, grid=(ni, nk),
            in_specs=[
                pl.BlockSpec((tm, tk), lambda i, k: (i, k)),
                pl.BlockSpec((None, tk, n), lambda i, k: (layer, k, 0)),
                pl.BlockSpec((tm, n), lambda i, k: (i, 0),
                             pipeline_mode=pl.Buffered(2, use_lookahead=True)),
            ],
            out_specs=[pl.BlockSpec((tm, n), lambda i, k: (i, 0))],
        )(a_hbm, w_hbm, r_hbm, o_hbm)

    return pl.pallas_call(
        outer,
        in_specs=[pl.BlockSpec(memory_space=pl.ANY)] * 3,
        out_specs=pl.BlockSpec(memory_space=pl.ANY),
        out_shape=jax.ShapeDtypeStruct((m, n), F32),
        scratch_shapes=[pltpu.SMEM((1,), jnp.int32)],
        compiler_params=pltpu.CompilerParams(vmem_limit_bytes=VMEM_LIMIT),
        name="mm_res_la",
    )(a, w3, res)


def _mm_out_body(*refs, kcs, n_ptiles):
    n = len(kcs)
    ap, asm = refs[:n], refs[n:2 * n]
    w_ref, rp_ref, rs_ref, o_ref, wb_ref = refs[2 * n:]
    i = pl.program_id(1)

    @pl.when(i == 0)
    def _():
        wb_ref[...] = w_ref[...].astype(BF16)

    def compute(srcs, r_ref):
        acc = r_ref[...]
        off = 0
        for a_ref, kc in zip(srcs, kcs):
            acc = acc + _dot(a_ref[...], wb_ref[off:off + kc, :])
            off += kc
        o_ref[...] = acc

    @pl.when(i < n_ptiles)
    def _():
        compute(ap, rp_ref)

    @pl.when(i >= n_ptiles)
    def _():
        compute(asm, rs_ref)


def _mm_out(a_prompt, a_sample, w3, layer, res_prompt, res_sample, *, tm=512, tn=1024):
    kcs = tuple(a.shape[1] for a in a_prompt)
    kdim, n = sum(kcs), w3.shape[2]
    n_ptiles = a_prompt[0].shape[0] // tm
    m = a_prompt[0].shape[0] + a_sample[0].shape[0]
    (rp, rp0), (rs, rs0) = res_prompt, res_sample
    pidx = lambda i: jnp.minimum(i, n_ptiles - 1)
    in_specs = ([pl.BlockSpec((tm, kc), lambda j, i: (pidx(i), 0)) for kc in kcs]
                + [pl.BlockSpec((tm, kc), lambda j, i: (0, 0)) for kc in kcs]
                + [pl.BlockSpec((None, kdim, tn), lambda j, i: (layer, 0, j)),
                   pl.BlockSpec((tm, tn), lambda j, i: (rp0 + pidx(i), j)),
                   pl.BlockSpec((tm, tn), lambda j, i: (rs0, j))])
    body = functools.partial(_mm_out_body, kcs=kcs, n_ptiles=n_ptiles)
    return pl.pallas_call(
        body, grid=(n // tn, n_ptiles + 1), in_specs=in_specs,
        out_specs=pl.BlockSpec((tm, tn), lambda j, i: (i, j)),
        out_shape=jax.ShapeDtypeStruct((m, n), F32),
        scratch_shapes=[pltpu.VMEM((kdim, tn), BF16)],
        compiler_params=_cparams(("arbitrary", "arbitrary")),
        name="mm_out",
    )(*a_prompt, *a_sample, w3, rp, rs)


def _rmsnorm_body(x_ref, g_ref, op_ref, os_ref, *, n_ptiles):
    x = x_ref[...]
    ms = jnp.mean(x * x, axis=-1, keepdims=True)
    y = x * lax.rsqrt(ms + EPS) * g_ref[...]

    @pl.when(pl.program_id(0) < n_ptiles)
    def _():
        op_ref[...] = y

    @pl.when(pl.program_id(0) >= n_ptiles)
    def _():
        os_ref[...] = y


def _rmsnorm(x, g, m_prompt, *, tm=512):
    m, d = x.shape
    n_ptiles = m_prompt // tm
    return pl.pallas_call(
        functools.partial(_rmsnorm_body, n_ptiles=n_ptiles), grid=(n_ptiles + 1,),
        in_specs=[pl.BlockSpec((tm, d), lambda i: (i, 0)), pl.BlockSpec((1, d), lambda i: (0, 0))],
        out_specs=[pl.BlockSpec((tm, d), lambda i: (jnp.minimum(i, n_ptiles - 1), 0)),
                   pl.BlockSpec((tm, d), lambda i: (0, 0))],
        out_shape=[jax.ShapeDtypeStruct((m_prompt, d), F32),
                   jax.ShapeDtypeStruct((m - m_prompt, d), F32)],
        compiler_params=_cparams(("arbitrary",)),
        name="final_norm",
    )(x, g.reshape(1, d))


def _s5_consts_body(ar_ref, ai_ref, ls_ref, br_ref, bi_ref, cr_ref, ci_ref,
                    abr_ref, abi_ref, bre_ref, bim_ref, cre_ref, cim_ref):
    g, p, q = S5_GROUPS, S5_STATE, S5_GROUP
    gb = g // S5_NBLK
    gq, wid = g * q, gb * p
    ar = ar_ref[...]
    ai = ai_ref[...]
    dt = jnp.exp(ls_ref[...])
    mag = jnp.exp(ar * dt)
    abr = mag * jnp.cos(ai * dt)
    abi = mag * jnp.sin(ai * dt)
    abr_ref[...] = abr
    abi_ref[...] = abi
    lam2 = ar * ar + ai * ai
    zr = abr - 1.0
    cr = (zr * ar + abi * ai) / lam2
    ci = (abi * ar - zr * ai) / lam2

    rep = jnp.where((lax.broadcasted_iota(jnp.int32, (gq, g), 0) // q)
                    == lax.broadcasted_iota(jnp.int32, (gq, g), 1), 1.0, 0.0).astype(BF16)
    cr_r = _dot_exact_lhs(rep, cr)
    ci_r = _dot_exact_lhs(rep, ci)
    br = br_ref[...]
    bi = bi_ref[...]
    bbr = cr_r * br - ci_r * bi
    bbi = cr_r * bi + ci_r * br

    tile = jnp.where(lax.broadcasted_iota(jnp.int32, (p, wid), 0)
                     == lax.broadcasted_iota(jnp.int32, (p, wid), 1) % p, 1.0, 0.0).astype(BF16)
    keep = ((lax.broadcasted_iota(jnp.int32, (gq, wid), 0) // q) % gb
            == lax.broadcasted_iota(jnp.int32, (gq, wid), 1) // p)

    def spread(t):
        return jnp.where(keep, _dot(t.astype(BF16), tile), 0.0)

    bre_ref[...] = spread(bbr).astype(BF16).reshape(bre_ref.shape)
    bim_ref[...] = spread(bbi).astype(BF16).reshape(bim_ref.shape)
    dre = spread(cr_ref[...])
    dim = spread(ci_ref[...])
    rows = gb * q
    for k in range(S5_NBLK):
        cre_ref[k] = jnp.transpose(dre[k * rows:(k + 1) * rows, :]).astype(BF16)
        cim_ref[k] = jnp.transpose(dim[k * rows:(k + 1) * rows, :]).astype(BF16)


def _s5_consts(a_re, a_im, log_step, b_re, b_im, c_re, c_im):
    g, p, q = S5_GROUPS, S5_STATE, S5_GROUP
    gb = g // S5_NBLK
    brt = jnp.transpose(b_re, (0, 2, 1)).reshape(g * q, p)
    bit = jnp.transpose(b_im, (0, 2, 1)).reshape(g * q, p)
    full = lambda shape: pl.BlockSpec(shape, lambda: (0,) * len(shape))
    ins = [(g, p), (g, p), (g, 1)] + [(g * q, p)] * 4
    outs = [((g, p), F32)] * 2 + [((S5_NBLK, gb * q, gb * p), BF16)] * 2 + [((S5_NBLK, gb * p, gb * q), BF16)] * 2
    abr, abi, bre, bim, cre, cim = pl.pallas_call(
        _s5_consts_body, in_specs=[full(s) for s in ins], out_specs=[full(s) for s, _ in outs],
        out_shape=[jax.ShapeDtypeStruct(s, d) for s, d in outs],
        compiler_params=pltpu.CompilerParams(vmem_limit_bytes=VMEM_LIMIT),
        name="s5_consts",
    )(a_re, a_im, log_step.reshape(g, 1), brt, bit, c_re.reshape(g * q, p), c_im.reshape(g * q, p))
    return abr.reshape(1, S5_CH), abi.reshape(1, S5_CH), bre, bim, cre, cim


def _s5_perm(nb, tl):
    r = np.arange(nb * tl)
    p = np.zeros((nb * tl, nb * tl), np.float32)
    p[r, (r % nb) * tl + r // nb] = 1.0
    return p


def _s5_body(*refs, nsrc, nb, has_init, permute):
    u_refs, refs = refs[:nsrc], refs[nsrc:]
    if permute:
        pm_ref, pmt_ref = refs[:2]
        refs = refs[2:]
    bre_ref, bim_ref, cre_ref, cim_ref, ar_ref, ai_ref, d_ref, gw_ref, gb_ref = refs[:9]
    refs = refs[9:]
    if has_init:
        h0r_ref, h0i_ref, y_ref, hr_ref, hi_ref, sre, sim, ysc = refs
    else:
        y_ref, hr_ref, hi_ref, sre, sim, ysc = refs
    rows = sre.shape[0]
    tl = rows // nb
    cb = S5_CH // S5_NBLK
    ub = S5_WIDTH // S5_NBLK
    hw = cb // 2

    @pl.when(pl.program_id(0) == 0)
    def _():
        if has_init:
            hr_ref[...] = h0r_ref[...]
            hi_ref[...] = h0i_ref[...]
        else:
            hr_ref[...] = jnp.zeros_like(hr_ref)
            hi_ref[...] = jnp.zeros_like(hi_ref)

    u_b = u_refs[0][...] if nsrc == 1 else jnp.concatenate([r[...] for r in u_refs], axis=0)
    if permute:
        u_tm = _dot(pm_ref[...], u_b)
        u_b = u_tm.astype(BF16)
    else:
        u_tm = u_b.astype(F32)
    low_rows = lax.broadcasted_iota(jnp.int32, (SUBLANES, hw), 0) < nb

    for k in range(S5_NBLK):
        uk = u_b[:, k * ub:(k + 1) * ub]
        sre[...] = _dot(uk, bre_ref[k])
        sim[...] = _dot(uk, bim_ref[k])
        for half in range(2):
            c0 = k * cb + half * hw
            l0 = half * hw
            ar8 = jnp.broadcast_to(ar_ref[:, c0:c0 + hw], (SUBLANES, hw))
            ai8 = jnp.broadcast_to(ai_ref[:, c0:c0 + hw], (SUBLANES, hw))

            if nb < SUBLANES:
                def j_body(j, c, l0=l0, ar8=ar8, ai8=ai8):
                    sr, si = c
                    row = pl.multiple_of(j * SUBLANES, SUBLANES)
                    dr = sre[pl.ds(row, SUBLANES), l0:l0 + hw]
                    di = sim[pl.ds(row, SUBLANES), l0:l0 + hw]
                    pr = pltpu.roll(sr, nb, 0)
                    pi = pltpu.roll(si, nb, 0)
                    yr = ar8 * pr - ai8 * pi + dr
                    yi = ar8 * pi + ai8 * pr + di
                    qr = pltpu.roll(yr, nb, 0)
                    qi = pltpu.roll(yi, nb, 0)
                    zr = ar8 * qr - ai8 * qi + dr
                    zi = ar8 * qi + ai8 * qr + di
                    sre[pl.ds(row, SUBLANES), l0:l0 + hw] = jnp.where(low_rows, yr, zr)
                    sim[pl.ds(row, SUBLANES), l0:l0 + hw] = jnp.where(low_rows, yi, zi)
                    return zr, zi

                sr, si = lax.fori_loop(0, rows // SUBLANES, j_body,
                                       (hr_ref[:, c0:c0 + hw], hi_ref[:, c0:c0 + hw]), unroll=True)
                hr_ref[:, c0:c0 + hw] = sr
                hi_ref[:, c0:c0 + hw] = si
            else:
                def rg_body(rg, carry, c0=c0, l0=l0, ar8=ar8, ai8=ai8):
                    r0 = pl.multiple_of(rg * SUBLANES, SUBLANES)
                    xr = hr_ref[pl.ds(r0, SUBLANES), c0:c0 + hw]
                    xi = hi_ref[pl.ds(r0, SUBLANES), c0:c0 + hw]
                    for t in range(tl):
                        row = pl.multiple_of(t * nb + r0, SUBLANES)
                        nr = ar8 * xr - ai8 * xi + sre[pl.ds(row, SUBLANES), l0:l0 + hw]
                        ni = ar8 * xi + ai8 * xr + sim[pl.ds(row, SUBLANES), l0:l0 + hw]
                        sre[pl.ds(row, SUBLANES), l0:l0 + hw] = nr
                        sim[pl.ds(row, SUBLANES), l0:l0 + hw] = ni
                        xr, xi = nr, ni
                    hr_ref[pl.ds(r0, SUBLANES), c0:c0 + hw] = xr
                    hi_ref[pl.ds(r0, SUBLANES), c0:c0 + hw] = xi
                    return carry

                lax.fori_loop(0, nb // SUBLANES, rg_body, 0)
        ysc[:, k * ub:(k + 1) * ub] = (_dot(sre[...].astype(BF16), cre_ref[k])
                                       - _dot(sim[...].astype(BF16), cim_ref[k]))

    y = ysc[...] + d_ref[...] * u_tm
    y = 0.5 * y * (1.0 + jnp.tanh(math.sqrt(2.0 / math.pi) * (y + 0.044715 * (y * y * y))))
    z = _dot(y.astype(BF16), gw_ref[...]) + gb_ref[...]
    out = (y * _sigmoid(z)).astype(BF16)
    if permute:
        out = _dot(pmt_ref[...], out).astype(BF16)
    y_ref[...] = out.reshape(y_ref.shape)


def _s5_call(src, bre, bim, cre, cim, abar_re, abar_im, d_skip, glu_w, glu_b, h0, *,
             row0, nsrc, rs, seq_len, nb, tl, permute):
    rows = nsrc * rs
    ntiles = seq_len // rs if nsrc > 1 else 1
    has_init = h0 is not None
    srows = max(nb, SUBLANES)
    const = lambda shape: pl.BlockSpec(shape, lambda t: (0,) * len(shape))
    in_specs = [pl.BlockSpec((rs, S5_WIDTH), lambda t, s=s: ((row0 + s * seq_len) // rs + t, 0))
                for s in range(nsrc)]
    args = [src] * nsrc
    if permute:
        pm = _s5_perm(nb, tl)
        in_specs += [const((rows, rows)), const((rows, rows))]
        args += [jnp.asarray(pm, BF16), jnp.asarray(pm.T, BF16)]
    in_specs += [
        const(bre.shape), const(bim.shape), const(cre.shape), const(cim.shape),
        const((1, S5_CH)), const((1, S5_CH)), const((1, S5_WIDTH)),
        const((S5_WIDTH, S5_WIDTH)), const((1, S5_WIDTH)),
    ]
    args += [bre, bim, cre, cim, abar_re, abar_im, d_skip, glu_w, glu_b]
    if has_init:
        in_specs += [const((srows, S5_CH)), const((srows, S5_CH))]
        args += [h0[0], h0[1]]
    body = functools.partial(_s5_body, nsrc=nsrc, nb=nb, has_init=has_init, permute=permute)
    return pl.pallas_call(
        body, grid=(ntiles,), in_specs=in_specs,
        out_specs=[pl.BlockSpec((nsrc, rs, S5_WIDTH), lambda t: (0, t, 0)),
                   const((srows, S5_CH)), const((srows, S5_CH))],
        out_shape=[jax.ShapeDtypeStruct((nsrc, ntiles * rs, S5_WIDTH), BF16),
                   jax.ShapeDtypeStruct((srows, S5_CH), F32),
                   jax.ShapeDtypeStruct((srows, S5_CH), F32)],
        scratch_shapes=[pltpu.VMEM((rows, S5_CH // S5_NBLK), F32),
                        pltpu.VMEM((rows, S5_CH // S5_NBLK), F32),
                        pltpu.VMEM((rows, S5_WIDTH), F32)],
        compiler_params=_cparams(("arbitrary",)),
        name="s5_mixer",
    )(*args)


def _gla_body(*refs, nbb, nch, cs, has_init, hps):
    if has_init:
        (q_ref, k_ref, v_ref, g_ref, gl_ref, up_ref, gb_ref, nw_ref, s0_ref, y_ref, s_ref) = refs
    else:
        (q_ref, k_ref, v_ref, g_ref, gl_ref, up_ref, gb_ref, nw_ref, y_ref, s_ref) = refs
    rows = nbb * nch * cs
    sh = int(math.log2(cs))

    @pl.when(pl.program_id(2) == 0)
    def _():
        if has_init:
            s_ref[...] = s0_ref[...]
        else:
            s_ref[...] = jnp.zeros_like(s_ref)

    ri = lax.broadcasted_iota(jnp.int32, (rows, rows), 0)
    ci = lax.broadcasted_iota(jnp.int32, (rows, rows), 1)
    same = (ri >> sh) == (ci >> sh)
    causal = jnp.logical_and(same, ci <= ri)
    lmat = jnp.where(causal, 1.0, 0.0).astype(BF16)
    tmat = jnp.where(same, 1.0, 0.0).astype(BF16)
    rchunk = lax.broadcasted_iota(jnp.int32, (rows, 1), 0) >> sh
    cchunk = lax.broadcasted_iota(jnp.int32, (1, rows), 1) >> sh
    single = nbb * nch == 1
    row_sliced = not single and cs % 16 == 0
    z = _dot(gl_ref[...].astype(BF16), up_ref[...].astype(BF16)) + gb_ref[...]
    la = _log_sigmoid(z) * (1.0 / GLA_TAU)
    p1, p2, p3 = _split3(la)
    bcum_all = _dot(lmat, p1) + _dot(lmat, p2) + _dot(lmat, p3)
    tot_all = _dot(tmat, p1) + _dot(tmat, p2) + _dot(tmat, p3)

    for hh in range(hps):
        ks = slice(hh * GLA_HK, (hh + 1) * GLA_HK)
        vs = slice(hh * GLA_HV, (hh + 1) * GLA_HV)
        q = q_ref[:, ks].astype(F32) * (GLA_HK ** -0.5)
        k = k_ref[:, ks].astype(F32)
        vb = v_ref[:, vs]
        bcum = bcum_all[:, ks]
        tot = tot_all[:, ks]

        qd = q * jnp.exp(bcum)
        kd = k * jnp.exp(-bcum)
        kdec = k * jnp.exp(tot - bcum)
        qdb = qd.astype(BF16)
        att = jnp.where(causal, _dot_nt(qdb, kd.astype(BF16)), 0.0)
        o = _dot(att.astype(BF16), vb)

        kdec_t = jnp.transpose(kdec)
        dec_t = jnp.transpose(jnp.exp(tot))
        inter = []
        for b in range(nbb):
            s = s_ref[b, hh]
            for c in range(nch):
                idx = b * nch + c
                if row_sliced:
                    inter.append(_dot(qdb[idx * cs:(idx + 1) * cs], s.astype(BF16)))
                else:
                    oi = _dot(qdb, s.astype(BF16))
                    o = o + (oi if single else jnp.where(rchunk == idx, oi, 0.0))
                kt = kdec_t if single else jnp.where(cchunk == idx, kdec_t, 0.0)
                s = dec_t[:, idx * cs:idx * cs + 1] * s + _dot(kt.astype(BF16), vb)
            s_ref[b, hh] = s
        if row_sliced:
            o = o + jnp.concatenate(inter, axis=0)

        o = o * lax.rsqrt(jnp.mean(o * o, axis=-1, keepdims=True) + EPS) * nw_ref[...]
        gg = g_ref[:, vs].astype(F32)
        y_ref[:, vs] = (o * (gg * _sigmoid(gg))).astype(y_ref.dtype)


def _gla_call(projm, projt, up_pad, gkb, nw, s0, *, row0, nseq_blocks, ntiles, nbb, nch, cs, batch, hps):
    rows = nbb * nch * cs
    has_init = s0 is not None
    rb0 = row0 // rows
    rowblk = lambda s, h, t: rb0 + s * ntiles + t
    wk, wv = hps * GLA_HK, hps * GLA_HV
    qoff = S5_WIDTH // wk
    koff = qoff + GLA_HEADS // hps
    voff = (S5_WIDTH + 2 * GLA_HEADS * GLA_HK) // wv
    goff = voff + GLA_HEADS // hps
    in_specs = [
        pl.BlockSpec((rows, wk), lambda s, h, t: (rowblk(s, h, t), qoff + h)),
        pl.BlockSpec((rows, wk), lambda s, h, t: (rowblk(s, h, t), koff + h)),
        pl.BlockSpec((rows, wv), lambda s, h, t: (rowblk(s, h, t), voff + h)),
        pl.BlockSpec((rows, wv), lambda s, h, t: (rowblk(s, h, t), goff + h)),
        pl.BlockSpec((rows, LANES), lambda s, h, t: (rowblk(s, h, t), 0)),
        pl.BlockSpec((LANES, wk), lambda s, h, t: (0, h)),
        pl.BlockSpec((1, wk), lambda s, h, t: (0, h)),
        pl.BlockSpec((1, GLA_HV), lambda s, h, t: (0, 0)),
    ]
    args = [projm, projm, projm, projm, projt, up_pad, gkb, nw]
    if has_init:
        in_specs.append(pl.BlockSpec((nbb, hps, GLA_HK, GLA_HV), lambda s, h, t: (s, h, 0, 0)))
        args.append(s0)
    body = functools.partial(_gla_body, nbb=nbb, nch=nch, cs=cs, has_init=has_init, hps=hps)
    return pl.pallas_call(
        body, grid=(nseq_blocks, GLA_HEADS // hps, ntiles), in_specs=in_specs,
        out_specs=[pl.BlockSpec((rows, wv), lambda s, h, t: (s * ntiles + t, h)),
                   pl.BlockSpec((nbb, hps, GLA_HK, GLA_HV), lambda s, h, t: (s, h, 0, 0))],
        out_shape=[jax.ShapeDtypeStruct((nseq_blocks * ntiles * rows, GLA_HEADS * GLA_HV), BF16),
                   jax.ShapeDtypeStruct((batch, GLA_HEADS, GLA_HK, GLA_HV), F32)],
        compiler_params=_cparams(("arbitrary", "arbitrary", "arbitrary")),
        name="gla_mixer",
    )(*args)


def _mlstm_body(*refs, nbb, cl, rp, has_init, hps):
    if has_init:
        (q_ref, k_ref, v_ref, og_ref, gt_ref, gb_ref, nw_ref, c0_ref, n0_ref, m0_ref,
         y_ref, c_ref, n_ref, m_ref) = refs
    else:
        (q_ref, k_ref, v_ref, og_ref, gt_ref, gb_ref, nw_ref, y_ref, c_ref, n_ref, m_ref) = refs
    rows = nbb * cl
    sh = int(math.log2(cl))
    head0 = pl.program_id(1) * hps

    @pl.when(pl.program_id(2) == 0)
    def _():
        if has_init:
            c_ref[...] = c0_ref[...]
            n_ref[...] = n0_ref[...]
            m_ref[...] = m0_ref[...]
        else:
            c_ref[...] = jnp.zeros_like(c_ref)
            n_ref[...] = jnp.zeros_like(n_ref)
            m_ref[...] = jnp.full(m_ref.shape, -jnp.inf, F32)

    def pad(x):
        if rp == rows:
            return x
        return jnp.concatenate([x, jnp.zeros((rp - rows, x.shape[1]), x.dtype)], axis=0)

    ksc = ML_DH ** -0.5
    gates = pad(gt_ref[...] + gb_ref[...])
    lane = lax.broadcasted_iota(jnp.int32, (rp, LANES), 1)
    ri = lax.broadcasted_iota(jnp.int32, (rp, rp), 0)
    ci = lax.broadcasted_iota(jnp.int32, (rp, rp), 1)
    causal = jnp.logical_and((ri >> sh) == (ci >> sh), ci <= ri)
    lmat = jnp.where(causal, 1.0, 0.0).astype(BF16)
    neg = jnp.where(causal, 0.0, -jnp.inf)
    er = lax.broadcasted_iota(jnp.int32, (SUBLANES, LANES), 0)
    ec = lax.broadcasted_iota(jnp.int32, (SUBLANES, LANES), 1)
    pick = jnp.where(er == ec, 1.0, 0.0).astype(BF16)
    rbatch = lax.broadcasted_iota(jnp.int32, (rp, 1), 0) >> sh
    cbatch = lax.broadcasted_iota(jnp.int32, (1, rp), 1) >> sh
    fc_all = _dot_exact_lhs(lmat, _log_sigmoid(gates))
    rows_all = _dot_nt_exact_lhs(pick, jnp.where(lane < ML_HEADS, gates, fc_all))
    sub = lax.broadcasted_iota(jnp.int32, (SUBLANES, rp), 0)

    for hh in range(hps):
        head = head0 + hh
        hs = slice(hh * ML_DH, (hh + 1) * ML_DH)
        qb = pad(q_ref[:, hs])
        kb = pad(k_ref[:, hs])
        vb = pad(v_ref[:, hs])
        og = pad(og_ref[:, hs]).astype(F32)
        ig = jnp.sum(jnp.where(lane == head, gates, 0.0), axis=1, keepdims=True)
        fcum = jnp.sum(jnp.where(lane == head + ML_HEADS, fc_all, 0.0), axis=1, keepdims=True)
        ig_row = jnp.sum(jnp.where(sub == head, rows_all, 0.0), axis=0, keepdims=True)
        fcum_row = jnp.sum(jnp.where(sub == head + ML_HEADS, rows_all, 0.0), axis=0, keepdims=True)

        mm = jnp.zeros((rp, 1), F32)
        for b in range(nbb):
            mb = m_ref[b, hh][:, 0:1]
            mm = jnp.where(rbatch == b, mb, mm) if nbb > 1 else jnp.broadcast_to(mb, (rp, 1))

        dmat = fcum - fcum_row + ig_row + neg
        dprev = fcum + mm
        m = jnp.maximum(jnp.max(dmat, axis=1, keepdims=True), dprev)
        w = jnp.exp(dmat - m + math.log(ksc))
        wp = jnp.exp(dprev - m)
        sc = _dot_nt(qb, kb) * w
        num = _dot(sc.astype(BF16), vb)
        den = jnp.sum(sc, axis=1, keepdims=True)

        k_t = jnp.transpose(kb.astype(F32))
        for b in range(nbb):
            cm = c_ref[b, hh]
            nm = n_ref[b, hh]
            mb = m_ref[b, hh][:, 0:1]
            qc = wp * _dot(qb, cm.astype(BF16))
            nm8 = jnp.broadcast_to(nm, (SUBLANES, ML_DH)).astype(BF16)
            qn = wp * _dot_nt(qb, nm8)[:, 0:1]
            if nbb > 1:
                rsel = rbatch == b
                qc = jnp.where(rsel, qc, 0.0)
                qn = jnp.where(rsel, qn, 0.0)
            num = num + qc
            den = den + qn
            last = b * cl + cl - 1
            m_new = m[last:last + 1, :]
            fl = fcum[last:last + 1, :]
            decay = jnp.exp(fl + mb - m_new)
            wk_row = ksc * jnp.exp(fl - fcum_row + ig_row - m_new)
            if nbb > 1:
                wk_row = jnp.where(cbatch == b, wk_row, 0.0)
            c_ref[b, hh] = decay * cm + _dot((k_t * wk_row).astype(BF16), vb)
            wk8 = jnp.broadcast_to(wk_row, (SUBLANES, rp)).astype(BF16)
            n_ref[b, hh] = decay * nm + _dot(wk8, kb)[0:1, :]
            m_ref[b, hh] = jnp.broadcast_to(m_new, (1, LANES))

        hv = num / jnp.maximum(jnp.abs(den), jnp.exp(-m))
        hv = hv * _sigmoid(og)
        hv = hv * lax.rsqrt(jnp.mean(hv * hv, axis=-1, keepdims=True) + EPS) * nw_ref[...]
        y_ref[:, hs] = hv[:rows].astype(y_ref.dtype)


def _mlstm_call(projm, projt, gate_b, nw, init, *, row0, nseq_blocks, ntiles, nbb, cl, rp, batch, hps):
    rows = nbb * cl
    has_init = init is not None
    rb0 = row0 // rows
    rowblk = lambda s, h, t: rb0 + s * ntiles + t
    hd = ML_HEADS // hps
    wd = hps * ML_DH
    in_specs = [
        pl.BlockSpec((rows, wd), lambda s, h, t: (rowblk(s, h, t), h)),
        pl.BlockSpec((rows, wd), lambda s, h, t: (rowblk(s, h, t), hd + h)),
        pl.BlockSpec((rows, wd), lambda s, h, t: (rowblk(s, h, t), 2 * hd + h)),
        pl.BlockSpec((rows, wd), lambda s, h, t: (rowblk(s, h, t), 3 * hd + h)),
        pl.BlockSpec((rows, LANES), lambda s, h, t: (rowblk(s, h, t), 0)),
        pl.BlockSpec((1, LANES), lambda s, h, t: (0, 0)),
        pl.BlockSpec((1, ML_DH), lambda s, h, t: (0, 0)),
    ]
    args = [projm, projm, projm, projm, projt, gate_b, nw]
    c_spec = pl.BlockSpec((nbb, hps, ML_DH, ML_DH), lambda s, h, t: (s, h, 0, 0))
    n_spec = pl.BlockSpec((nbb, hps, 1, ML_DH), lambda s, h, t: (s, h, 0, 0))
    m_spec = pl.BlockSpec((nbb, hps, 1, LANES), lambda s, h, t: (s, h, 0, 0))
    if has_init:
        in_specs += [c_spec, n_spec, m_spec]
        args += list(init)
    body = functools.partial(_mlstm_body, nbb=nbb, cl=cl, rp=rp, has_init=has_init, hps=hps)
    return pl.pallas_call(
        body, grid=(nseq_blocks, hd, ntiles), in_specs=in_specs,
        out_specs=[pl.BlockSpec((rows, wd), lambda s, h, t: (s * ntiles + t, h)),
                   c_spec, n_spec, m_spec],
        out_shape=[jax.ShapeDtypeStruct((nseq_blocks * ntiles * rows, ML_HEADS * ML_DH), BF16),
                   jax.ShapeDtypeStruct((batch, ML_HEADS, ML_DH, ML_DH), F32),
                   jax.ShapeDtypeStruct((batch, ML_HEADS, 1, ML_DH), F32),
                   jax.ShapeDtypeStruct((batch, ML_HEADS, 1, LANES), F32)],
        compiler_params=_cparams(("arbitrary", "arbitrary", "arbitrary")),
        name="mlstm_mixer",
    )(*args)


def kernel(x_prompt, x_sample, state_s5_re, state_s5_im, state_gla, state_mlstm_c, state_mlstm_n,
           state_mlstm_m, norm_mix, norm_mlp, norm_final, w_in_even, s5_a_re, s5_a_im, s5_log_step,
           s5_b_re, s5_b_im, s5_c_re, s5_c_im, s5_d, s5_glu_w, s5_glu_b, gla_gk_up, gla_gk_b,
           gla_norm, w_out_even, w_in_odd, mlstm_b_i, mlstm_b_f, mlstm_norm, w_out_odd,
           w_mlp_up, w_mlp_down):
    bp, lp, d = x_prompt.shape
    bs, ls, _ = x_sample.shape
    mp = bp * lp
    msamp = bs * ls
    xp2 = x_prompt.reshape(mp, d)
    xs2 = x_sample.reshape(msamp, d)

    n_even = S5_WIDTH + 2 * GLA_HEADS * GLA_HK + 2 * GLA_HEADS * GLA_HV
    projm, projt = _norm_mm(xp2, norm_mix[0], w_in_even, 0, n_even, tm=1024, tn=1024, out_dtype=BF16)
    projm_s, projt_s = _norm_mm(xs2, norm_mix[0], w_in_even, 0, n_even, tm=msamp, tn=1024, out_dtype=BF16)

    abar_re, abar_im, bre, bim, cre, cim = _s5_consts(s5_a_re[0], s5_a_im[0], s5_log_step[0], s5_b_re[0],
                                                      s5_b_im[0], s5_c_re[0], s5_c_im[0])
    s5_consts = (bre, bim, cre, cim, abar_re, abar_im, s5_d[0].reshape(1, S5_WIDTH),
                 s5_glu_w[0].astype(BF16), s5_glu_b[0].reshape(1, S5_WIDTH))

    s5_tl = 64
    ys5_p, hr_p, hi_p = _s5_call(projm, *s5_consts, None, row0=0, nsrc=bp, rs=s5_tl, seq_len=lp,
                                 nb=bp, tl=s5_tl, permute=True)
    ys5_p = ys5_p.reshape(mp, S5_WIDTH)
    hr_p = hr_p[SUBLANES - bp:]
    hi_p = hi_p[SUBLANES - bp:]
    u_s = jnp.transpose(projm_s[:, :S5_WIDTH].reshape(bs, ls, S5_WIDTH), (1, 0, 2))
    h0 = (state_s5_re[0].reshape(bs, S5_CH), state_s5_im[0].reshape(bs, S5_CH))
    ys5_s, hr_s, hi_s = _s5_call(u_s.reshape(msamp, S5_WIDTH), *s5_consts, h0, row0=0, nsrc=1, rs=msamp,
                                 seq_len=msamp, nb=bs, tl=ls, permute=False)
    ys5_s = jnp.transpose(ys5_s.reshape(ls, bs, S5_WIDTH), (1, 0, 2)).reshape(msamp, S5_WIDTH)

    up_pad = jnp.pad(gla_gk_up[0], ((0, LANES - gla_gk_up.shape[1]), (0, 0)))
    gkb = gla_gk_b[0].reshape(1, -1)
    gnw = gla_norm[0].reshape(1, GLA_HV)
    ptile = 256
    ygla_p, sg_p = _gla_call(projm, projt, up_pad, gkb, gnw, None, row0=0, nseq_blocks=bp,
                             ntiles=lp // ptile, nbb=1, nch=ptile // GLA_CHUNK, cs=GLA_CHUNK, batch=bp,
                             hps=GLA_HEADS)
    gnbb = 32
    ygla_s, sg_s = _gla_call(projm_s, projt_s, up_pad, gkb, gnw, state_gla[0], row0=0,
                             nseq_blocks=bs // gnbb, ntiles=1, nbb=gnbb, nch=1, cs=ls, batch=bs, hps=1)

    h = _mm_out([ys5_p, ygla_p], [ys5_s, ygla_s], w_out_even, 0, (xp2, 0), (xs2, 0))
    hid = _norm_mm_lookahead(h, norm_mlp[0], w_mlp_up, 0, D_FF, act="relu2", out_dtype=BF16)
    h = _mm_res_lookahead(hid, w_mlp_down, 0, h)

    n_odd = 4 * ML_HEADS * ML_DH
    projm, projt = _norm_mm(h, norm_mix[1], w_in_odd, 0, n_odd, out_dtype=BF16, tn=1024)
    gate_b = jnp.pad(jnp.concatenate([mlstm_b_i[0], mlstm_b_f[0]]), (0, LANES - 2 * ML_HEADS)).reshape(1, LANES)
    mnw = mlstm_norm[0].reshape(1, ML_DH)
    yml_p, c_p, n_p, m_p = _mlstm_call(projm, projt, gate_b, mnw, None, row0=0, nseq_blocks=bp,
                                       ntiles=lp // ML_CHUNK, nbb=1, cl=ML_CHUNK, rp=ML_CHUNK, batch=bp,
                                       hps=ML_HEADS)
    mnbb = 8
    init = (state_mlstm_c[0], state_mlstm_n[0].reshape(bs, ML_HEADS, 1, ML_DH),
            jnp.broadcast_to(state_mlstm_m[0][:, :, None, None], (bs, ML_HEADS, 1, LANES)))
    yml_s, c_s, n_s, m_s = _mlstm_call(projm, projt, gate_b, mnw, init, row0=mp, nseq_blocks=bs // mnbb,
                                       ntiles=1, nbb=mnbb, cl=ls, rp=LANES, batch=bs, hps=1)
    h = _mm_out([yml_p], [yml_s], w_out_odd, 0, (h, 0), (h, mp // 512))
    hid = _norm_mm_lookahead(h, norm_mlp[1], w_mlp_up, 1, D_FF, act="relu2", out_dtype=BF16)
    h = _mm_res_lookahead(hid, w_mlp_down, 1, h)

    y_p, y_s = _rmsnorm(h, norm_final, mp)
    g, p = S5_GROUPS, S5_STATE
    return (y_p.reshape(bp, lp, d), y_s.reshape(bs, ls, d),
            hr_p.reshape(1, bp, g, p), hi_p.reshape(1, bp, g, p), sg_p[None],
            c_p[None], n_p.reshape(1, bp, ML_HEADS, ML_DH), m_p[:, :, 0, 0][None],
            hr_s.reshape(1, bs, g, p), hi_s.reshape(1, bs, g, p), sg_s[None],
            c_s[None], n_s.reshape(1, bs, ML_HEADS, ML_DH), m_s[:, :, 0, 0][None])
```
